```python
import jax, jax.numpy as jnp
from jax import lax
import numpy as np

D_MODEL = 1024
BATCH = 32
SEQ = 2048
DEPTH = 4

GRID_W = 64
CTX_LEN = 256
HEAD_DIM = 64
N_Q_HEADS = 8
N_KV_HEADS = 2
Q_GROUP = N_Q_HEADS // N_KV_HEADS
ATTN_WIDTH = N_Q_HEADS * HEAD_DIM
KV_WIDTH = N_KV_HEADS * HEAD_DIM
AXIS_DIM = HEAD_DIM // 2
ROPE_THETA = 10000.0
Q_BLOCK = 128
ATTN_SCALE = HEAD_DIM ** -0.5
CONF_WIDTH = D_MODEL // 2
CONF_KERNEL = 31
SC_WIDTH = D_MODEL // 2
SC_KERNEL = 3
N_BRANCHES = 3
N_MOD = 6
FFN_HIDDEN = -(-8 * D_MODEL // (3 * 256)) * 256
EPS = 1e-6

OFF_K = ATTN_WIDTH
OFF_V = OFF_K + KV_WIDTH
OFF_CONF = OFF_V + KV_WIDTH
OFF_SC = OFF_CONF + 2 * CONF_WIDTH
OFF_GATE = OFF_SC + 3 * SC_WIDTH
IN_WIDTH = OFF_GATE + N_BRANCHES * D_MODEL

kernel_name = 'hybrid_gqa_conformer_shortconv_dit_block'


def rms_norm(x):
    xf = x.astype(jnp.float32)
    return (xf * lax.rsqrt(jnp.mean(xf * xf, axis=-1, keepdims=True) + EPS)).astype(x.dtype)


def head_rms_norm(x, g):
    return rms_norm(x) * g


def layer_norm(x, g, b):
    xf = x.astype(jnp.float32)
    mu = jnp.mean(xf, axis=-1, keepdims=True)
    var = jnp.mean(jnp.square(xf - mu), axis=-1, keepdims=True)
    return ((xf - mu) * lax.rsqrt(var + EPS)).astype(x.dtype) * g + b


def rope_tables(seq_len):
    rows = seq_len // GRID_W
    r_ids, c_ids = jnp.meshgrid(jnp.arange(rows), jnp.arange(GRID_W), indexing='ij')
    r_ids = r_ids.reshape(-1).astype(jnp.float32)
    c_ids = c_ids.reshape(-1).astype(jnp.float32)
    freqs = ROPE_THETA ** (-jnp.arange(0, AXIS_DIM, 2, dtype=jnp.float32) / AXIS_DIM)
    ang_r = r_ids[:, None] * freqs
    ang_c = c_ids[:, None] * freqs
    return (jnp.cos(ang_r)[:, None, :], jnp.sin(ang_r)[:, None, :],
            jnp.cos(ang_c)[:, None, :], jnp.sin(ang_c)[:, None, :])


def _rotate_half(xp, cos, sin):
    x1, x2 = jnp.split(xp, 2, axis=-1)
    return jnp.concatenate([x1 * cos - x2 * sin, x1 * sin + x2 * cos], axis=-1)


def apply_rope_2d(x, tabs):
    cos_r, sin_r, cos_c, sin_c = tabs
    xf = x.astype(jnp.float32)
    out = jnp.concatenate([_rotate_half(xf[..., :AXIS_DIM], cos_r, sin_r),
                           _rotate_half(xf[..., AXIS_DIM:], cos_c, sin_c)], axis=-1)
    return out.astype(x.dtype)


def depthwise_conv(x, w):
    k = w.shape[0]
    return lax.conv_general_dilated(x, w[:, None, :].astype(x.dtype), window_strides=(1,),
                                    padding=[(k // 2, k // 2)],
                                    dimension_numbers=('NWC', 'WIO', 'NWC'),
                                    feature_group_count=x.shape[-1])


def latent_attention(q, k_all, v_all):
    b, s = q.shape[0], q.shape[1]
    n_blk = s // Q_BLOCK
    qb = q.reshape(b, n_blk, Q_BLOCK, N_KV_HEADS, Q_GROUP, HEAD_DIM).transpose(1, 0, 2, 3, 4, 5)

    def one_block(qi):
        sc = jnp.einsum('bqhgd,bkhd->bhgqk', qi, k_all).astype(jnp.float32) * ATTN_SCALE
        p = jax.nn.softmax(sc, axis=-1).astype(v_all.dtype)
        return jnp.einsum('bhgqk,bkhd->bqhgd', p, v_all)

    o = lax.map(one_block, qb)
    return o.transpose(1, 0, 2, 3, 4, 5).reshape(b, s, ATTN_WIDTH)


def context_attention(qc, kc, vc):
    b, l = qc.shape[0], qc.shape[1]
    qg = qc.reshape(b, l, N_KV_HEADS, Q_GROUP, HEAD_DIM)
    sc = jnp.einsum('bqhgd,bkhd->bhgqk', qg, kc).astype(jnp.float32) * ATTN_SCALE
    p = jax.nn.softmax(sc, axis=-1).astype(vc.dtype)
    return jnp.einsum('bhgqk,bkhd->bqhgd', p, vc).reshape(b, l, ATTN_WIDTH)


def conformer_branch(u, dw_w, dw_b, ln_g, ln_b, w_out):
    a, g = jnp.split(u, 2, axis=-1)
    h = a * jax.nn.sigmoid(g)
    h = depthwise_conv(h, dw_w) + dw_b
    h = jax.nn.silu(layer_norm(h, ln_g, ln_b))
    return h @ w_out


def shortconv_branch(u, dw_w, w_out):
    bg, cg, xs = jnp.split(u, 3, axis=-1)
    return (bg * depthwise_conv(cg * xs, dw_w)) @ w_out


def merge_branches(attn_heads, conf_u, sc_u, gate_logits, w_attn_o, conf_dw_w, conf_dw_b,
                   conf_ln_g, conf_ln_b, w_conf_out, sc_dw_w, w_sc_out, w_mix_out):
    y_attn = attn_heads @ w_attn_o
    y_conf = conformer_branch(conf_u, conf_dw_w, conf_dw_b, conf_ln_g, conf_ln_b, w_conf_out)
    y_sc = shortconv_branch(sc_u, sc_dw_w, w_sc_out)
    g = jax.nn.sigmoid(gate_logits.reshape(gate_logits.shape[:-1] + (N_BRANCHES, D_MODEL)))
    merged = g[..., 0, :] * y_attn + g[..., 1, :] * y_conf + g[..., 2, :] * y_sc
    return merged @ w_mix_out


def swiglu(h, w_in, w_out):
    a, b = jnp.split(h @ w_in, 2, axis=-1)
    return (jax.nn.silu(a) * b) @ w_out


def _fwd_setup_inputs(seed: int = 0) -> dict:
    key = jax.random.key(seed)
    ks = jax.random.split(key, 24)
    f32 = jnp.float32
    L, D = DEPTH, D_MODEL

    def nrm(k, shape, scale):
        return jax.random.normal(k, shape, f32) * scale

    return {
        'x': nrm(ks[0], (BATCH, SEQ, D), 1.0),
        'c': nrm(ks[1], (BATCH, D), 1.0),
        'ctx': nrm(ks[2], (BATCH, CTX_LEN, D), 1.0),
        'c_ctx': nrm(ks[3], (D,), 1.0),
        'w_ada': nrm(ks[4], (L, D, N_MOD * D), 0.5 * D ** -0.5),
        'b_ada': nrm(ks[5], (L, N_MOD * D), 0.02),
        'w_in': nrm(ks[6], (L, D, IN_WIDTH), D ** -0.5),
        'q_norm': 1.0 + nrm(ks[7], (L, HEAD_DIM), 0.02),
        'k_norm': 1.0 + nrm(ks[8], (L, HEAD_DIM), 0.02),
        'w_attn_o': nrm(ks[9], (L, ATTN_WIDTH, D), ATTN_WIDTH ** -0.5),
        'conf_dw_w': nrm(ks[10], (L, CONF_KERNEL, CONF_WIDTH), CONF_KERNEL ** -0.5),
        'conf_dw_b': nrm(ks[11], (L, CONF_WIDTH), 0.02),
        'conf_ln_g': 1.0 + nrm(ks[12], (L, CONF_WIDTH), 0.02),
        'conf_ln_b': nrm(ks[13], (L, CONF_WIDTH), 0.02),
        'w_conf_out': nrm(ks[14], (L, CONF_WIDTH, D), CONF_WIDTH ** -0.5),
        'sc_dw_w': nrm(ks[15], (L, SC_KERNEL, SC_WIDTH), SC_KERNEL ** -0.5),
        'w_sc_out': nrm(ks[16], (L, SC_WIDTH, D), SC_WIDTH ** -0.5),
        'w_mix_out': nrm(ks[17], (L, D, D), D ** -0.5),
        'w_ffn_in': nrm(ks[18], (L, D, 2 * FFN_HIDDEN), D ** -0.5),
        'w_ffn_out': nrm(ks[19], (L, FFN_HIDDEN, D), FFN_HIDDEN ** -0.5),
    }


def _fwd_reference(x, c, ctx, c_ctx, w_ada, b_ada, w_in, q_norm, k_norm, w_attn_o, conf_dw_w, conf_dw_b,
              conf_ln_g, conf_ln_b, w_conf_out, sc_dw_w, w_sc_out, w_mix_out, w_ffn_in, w_ffn_out):
    b, s = x.shape[0], x.shape[1]
    lc = ctx.shape[1]
    rope = rope_tables(s)
    for i in range(DEPTH):
        last = i == DEPTH - 1
        wi = w_in[i]
        branch_w = (w_attn_o[i], conf_dw_w[i], conf_dw_b[i], conf_ln_g[i], conf_ln_b[i],
                    w_conf_out[i], sc_dw_w[i], w_sc_out[i], w_mix_out[i])
        mod = (jax.nn.silu(c) @ w_ada[i] + b_ada[i]).reshape(b, N_MOD, 1, D_MODEL)
        modc = (jax.nn.silu(c_ctx) @ w_ada[i] + b_ada[i]).reshape(N_MOD, D_MODEL)

        hc = rms_norm(ctx) * (1 + modc[1]) + modc[0]
        kvc = hc @ wi[:, OFF_K:OFF_CONF]
        kc = head_rms_norm(kvc[..., :KV_WIDTH].reshape(b, lc, N_KV_HEADS, HEAD_DIM), k_norm[i])
        vc = kvc[..., KV_WIDTH:].reshape(b, lc, N_KV_HEADS, HEAD_DIM)

        h = rms_norm(x) * (1 + mod[:, 1]) + mod[:, 0]
        p = h @ wi
        q = apply_rope_2d(head_rms_norm(p[..., :OFF_K].reshape(b, s, N_Q_HEADS, HEAD_DIM), q_norm[i]), rope)
        k = apply_rope_2d(head_rms_norm(p[..., OFF_K:OFF_V].reshape(b, s, N_KV_HEADS, HEAD_DIM), k_norm[i]), rope)
        v = p[..., OFF_V:OFF_CONF].reshape(b, s, N_KV_HEADS, HEAD_DIM)
        attn = latent_attention(q, jnp.concatenate([kc, k], axis=1), jnp.concatenate([vc, v], axis=1))
        mixed = merge_branches(attn, p[..., OFF_CONF:OFF_SC], p[..., OFF_SC:OFF_GATE], p[..., OFF_GATE:], *branch_w)
        x_new = x + mod[:, 2] * mixed
        h2 = rms_norm(x_new) * (1 + mod[:, 4]) + mod[:, 3]
        x_new = x_new + mod[:, 5] * swiglu(h2, w_ffn_in[i], w_ffn_out[i])

        if not last:
            qc = head_rms_norm((hc @ wi[:, :OFF_K]).reshape(b, lc, N_Q_HEADS, HEAD_DIM), q_norm[i])
            pc = hc @ wi[:, OFF_CONF:]
            attn_c = context_attention(qc, kc, vc)
            mixed_c = merge_branches(attn_c, pc[..., :OFF_SC - OFF_CONF], pc[..., OFF_SC - OFF_CONF:OFF_GATE - OFF_CONF],
                                     pc[..., OFF_GATE - OFF_CONF:], *branch_w)
            ctx = ctx + modc[2] * mixed_c
            h2c = rms_norm(ctx) * (1 + modc[4]) + modc[3]
            ctx = ctx + modc[5] * swiglu(h2c, w_ffn_in[i], w_ffn_out[i])
        x = x_new
    return x


import jax as _jax
import jax.numpy as _jnp

TWIN_FORMAT = 'train_step'
FWD_PARAMS = ['x', 'c', 'ctx', 'c_ctx', 'w_ada', 'b_ada', 'w_in', 'q_norm', 'k_norm', 'w_attn_o', 'conf_dw_w', 'conf_dw_b', 'conf_ln_g', 'conf_ln_b', 'w_conf_out', 'sc_dw_w', 'w_sc_out', 'w_mix_out', 'w_ffn_in', 'w_ffn_out']
TWIN_WEIGHTS = ['c_ctx', 'w_ada', 'b_ada', 'w_in', 'q_norm', 'k_norm', 'w_attn_o', 'conf_dw_w', 'conf_dw_b', 'conf_ln_g', 'conf_ln_b', 'w_conf_out', 'sc_dw_w', 'w_sc_out', 'w_mix_out', 'w_ffn_in', 'w_ffn_out']
TWIN_DIFF_INPUT = 'x'
TWIN_INPUTS = ['x', 'c', 'ctx', 'c_ctx', 'w_ada', 'b_ada', 'w_in', 'q_norm', 'k_norm', 'w_attn_o', 'conf_dw_w', 'conf_dw_b', 'conf_ln_g', 'conf_ln_b', 'w_conf_out', 'sc_dw_w', 'w_sc_out', 'w_mix_out', 'w_ffn_in', 'w_ffn_out', 'loss_target', 'm_c_ctx', 'm_w_ada', 'm_b_ada', 'm_w_in', 'm_q_norm', 'm_k_norm', 'm_w_attn_o', 'm_conf_dw_w', 'm_conf_dw_b', 'm_conf_ln_g', 'm_conf_ln_b', 'm_w_conf_out', 'm_sc_dw_w', 'm_w_sc_out', 'm_w_mix_out', 'm_w_ffn_in', 'm_w_ffn_out', 'v_c_ctx', 'v_w_ada', 'v_b_ada', 'v_w_in', 'v_q_norm', 'v_k_norm', 'v_w_attn_o', 'v_conf_dw_w', 'v_conf_dw_b', 'v_conf_ln_g', 'v_conf_ln_b', 'v_w_conf_out', 'v_sc_dw_w', 'v_w_sc_out', 'v_w_mix_out', 'v_w_ffn_in', 'v_w_ffn_out']
TWIN_OUTPUTS = ['loss', 'grad_x', 'grad_c_ctx', 'grad_w_ada', 'grad_b_ada', 'grad_w_in', 'grad_q_norm', 'grad_k_norm', 'grad_w_attn_o', 'grad_conf_dw_w', 'grad_conf_dw_b', 'grad_conf_ln_g', 'grad_conf_ln_b', 'grad_w_conf_out', 'grad_sc_dw_w', 'grad_w_sc_out', 'grad_w_mix_out', 'grad_w_ffn_in', 'grad_w_ffn_out', 'delta_c_ctx', 'delta_w_ada', 'delta_b_ada', 'delta_w_in', 'delta_q_norm', 'delta_k_norm', 'delta_w_attn_o', 'delta_conf_dw_w', 'delta_conf_dw_b', 'delta_conf_ln_g', 'delta_conf_ln_b', 'delta_w_conf_out', 'delta_sc_dw_w', 'delta_w_sc_out', 'delta_w_mix_out', 'delta_w_ffn_in', 'delta_w_ffn_out', 'new_m_c_ctx', 'new_m_w_ada', 'new_m_b_ada', 'new_m_w_in', 'new_m_q_norm', 'new_m_k_norm', 'new_m_w_attn_o', 'new_m_conf_dw_w', 'new_m_conf_dw_b', 'new_m_conf_ln_g', 'new_m_conf_ln_b', 'new_m_w_conf_out', 'new_m_sc_dw_w', 'new_m_w_sc_out', 'new_m_w_mix_out', 'new_m_w_ffn_in', 'new_m_w_ffn_out', 'new_v_c_ctx', 'new_v_w_ada', 'new_v_b_ada', 'new_v_w_in', 'new_v_q_norm', 'new_v_k_norm', 'new_v_w_attn_o', 'new_v_conf_dw_w', 'new_v_conf_dw_b', 'new_v_conf_ln_g', 'new_v_conf_ln_b', 'new_v_w_conf_out', 'new_v_sc_dw_w', 'new_v_w_sc_out', 'new_v_w_mix_out', 'new_v_w_ffn_in', 'new_v_w_ffn_out']
TWIN_LEAF_KINDS = {'loss': 'loss', 'grad_x': 'grad_x', 'grad_c_ctx': 'grad_w', 'grad_w_ada': 'grad_w', 'grad_b_ada': 'grad_w', 'grad_w_in': 'grad_w', 'grad_q_norm': 'grad_w', 'grad_k_norm': 'grad_w', 'grad_w_attn_o': 'grad_w', 'grad_conf_dw_w': 'grad_w', 'grad_conf_dw_b': 'grad_w', 'grad_conf_ln_g': 'grad_w', 'grad_conf_ln_b': 'grad_w', 'grad_w_conf_out': 'grad_w', 'grad_sc_dw_w': 'grad_w', 'grad_w_sc_out': 'grad_w', 'grad_w_mix_out': 'grad_w', 'grad_w_ffn_in': 'grad_w', 'grad_w_ffn_out': 'grad_w', 'delta_c_ctx': 'delta_w', 'delta_w_ada': 'delta_w', 'delta_b_ada': 'delta_w', 'delta_w_in': 'delta_w', 'delta_q_norm': 'delta_w', 'delta_k_norm': 'delta_w', 'delta_w_attn_o': 'delta_w', 'delta_conf_dw_w': 'delta_w', 'delta_conf_dw_b': 'delta_w', 'delta_conf_ln_g': 'delta_w', 'delta_conf_ln_b': 'delta_w', 'delta_w_conf_out': 'delta_w', 'delta_sc_dw_w': 'delta_w', 'delta_w_sc_out': 'delta_w', 'delta_w_mix_out': 'delta_w', 'delta_w_ffn_in': 'delta_w', 'delta_w_ffn_out': 'delta_w', 'new_m_c_ctx': 'new_m', 'new_m_w_ada': 'new_m', 'new_m_b_ada': 'new_m', 'new_m_w_in': 'new_m', 'new_m_q_norm': 'new_m', 'new_m_k_norm': 'new_m', 'new_m_w_attn_o': 'new_m', 'new_m_conf_dw_w': 'new_m', 'new_m_conf_dw_b': 'new_m', 'new_m_conf_ln_g': 'new_m', 'new_m_conf_ln_b': 'new_m', 'new_m_w_conf_out': 'new_m', 'new_m_sc_dw_w': 'new_m', 'new_m_w_sc_out': 'new_m', 'new_m_w_mix_out': 'new_m', 'new_m_w_ffn_in': 'new_m', 'new_m_w_ffn_out': 'new_m', 'new_v_c_ctx': 'new_v', 'new_v_w_ada': 'new_v', 'new_v_b_ada': 'new_v', 'new_v_w_in': 'new_v', 'new_v_q_norm': 'new_v', 'new_v_k_norm': 'new_v', 'new_v_w_attn_o': 'new_v', 'new_v_conf_dw_w': 'new_v', 'new_v_conf_dw_b': 'new_v', 'new_v_conf_ln_g': 'new_v', 'new_v_conf_ln_b': 'new_v', 'new_v_w_conf_out': 'new_v', 'new_v_sc_dw_w': 'new_v', 'new_v_w_sc_out': 'new_v', 'new_v_w_mix_out': 'new_v', 'new_v_w_ffn_in': 'new_v', 'new_v_w_ffn_out': 'new_v'}


def _forward(args):
    return _fwd_reference(*[args[k] for k in FWD_PARAMS])


def _output_shape():
    out = _jax.eval_shape(lambda: _forward(_fwd_setup_inputs(0)))
    return out.shape, out.dtype

N_MICROBATCH = 1
ADAM_LR = 0.001
ADAM_B1 = 0.9
ADAM_B2 = 0.999
ADAM_EPS = 1e-08
ADAM_WD = 0.01
ADAM_STEP = 10
PER_EXAMPLE_BATCH_AXIS = {'x': 0, 'c': 0, 'ctx': 0, 'loss_target': 0}
SHARED_INPUTS = []
_WEIGHT_DTYPES = {'c_ctx': _jnp.float32, 'w_ada': _jnp.float32, 'b_ada': _jnp.float32, 'w_in': _jnp.float32, 'q_norm': _jnp.float32, 'k_norm': _jnp.float32, 'w_attn_o': _jnp.float32, 'conf_dw_w': _jnp.float32, 'conf_dw_b': _jnp.float32, 'conf_ln_g': _jnp.float32, 'conf_ln_b': _jnp.float32, 'w_conf_out': _jnp.float32, 'sc_dw_w': _jnp.float32, 'w_sc_out': _jnp.float32, 'w_mix_out': _jnp.float32, 'w_ffn_in': _jnp.float32, 'w_ffn_out': _jnp.float32}
MOMENT_SCALE = {'c_ctx': 8.674151e-02, 'w_ada': 2.469135e+00, 'b_ada': 5.575368e+00, 'w_in': 1.611897e-01, 'q_norm': 3.898954e-02, 'k_norm': 3.903040e-02, 'w_attn_o': 2.417811e-01, 'conf_dw_w': 1.384317e-01, 'conf_dw_b': 1.010164e+00, 'conf_ln_g': 1.608024e+00, 'conf_ln_b': 1.126621e+00, 'w_conf_out': 1.626406e-01, 'sc_dw_w': 3.339186e+00, 'w_sc_out': 1.621205e-01, 'w_mix_out': 2.439330e-01, 'w_ffn_in': 1.041633e-01, 'w_ffn_out': 1.462103e-01}


def _to_microbatches(a, axis):
    t = _jnp.moveaxis(a, axis, 0)
    t = t.reshape((N_MICROBATCH, t.shape[0] // N_MICROBATCH) + t.shape[1:])
    return _jnp.moveaxis(t, 1, axis + 1)


def setup_inputs(seed: int = 0) -> dict:
    inp = _fwd_setup_inputs(seed)
    key = _jax.random.fold_in(_jax.random.key(seed), 7919)
    shape, _ = _output_shape()
    out = dict(inp)
    out["loss_target"] = _jax.random.normal(_jax.random.fold_in(key, 0), shape, _jnp.float32)
    for i, name in enumerate(TWIN_WEIGHTS):
        w = inp[name].astype(_jnp.float32)
        if MOMENT_SCALE is None:
            s = _jnp.sqrt(_jnp.mean(_jnp.square(w)) + 1e-30)
        else:
            s = MOMENT_SCALE[name]
        km, kv = _jax.random.split(_jax.random.fold_in(key, i + 1))
        out[name] = w
        out["m_" + name] = s * _jax.random.normal(km, w.shape, _jnp.float32)
        out["v_" + name] = (s * s) * _jax.random.uniform(kv, w.shape, _jnp.float32, 0.5, 1.5)
    if N_MICROBATCH > 1:
        for name, axis in PER_EXAMPLE_BATCH_AXIS.items():
            out[name] = _to_microbatches(out[name], axis)
    return {'x': out['x'], 'c': out['c'], 'ctx': out['ctx'], 'c_ctx': out['c_ctx'], 'w_ada': out['w_ada'], 'b_ada': out['b_ada'], 'w_in': out['w_in'], 'q_norm': out['q_norm'], 'k_norm': out['k_norm'], 'w_attn_o': out['w_attn_o'], 'conf_dw_w': out['conf_dw_w'], 'conf_dw_b': out['conf_dw_b'], 'conf_ln_g': out['conf_ln_g'], 'conf_ln_b': out['conf_ln_b'], 'w_conf_out': out['w_conf_out'], 'sc_dw_w': out['sc_dw_w'], 'w_sc_out': out['w_sc_out'], 'w_mix_out': out['w_mix_out'], 'w_ffn_in': out['w_ffn_in'], 'w_ffn_out': out['w_ffn_out'], 'loss_target': out['loss_target'], 'm_c_ctx': out['m_c_ctx'], 'm_w_ada': out['m_w_ada'], 'm_b_ada': out['m_b_ada'], 'm_w_in': out['m_w_in'], 'm_q_norm': out['m_q_norm'], 'm_k_norm': out['m_k_norm'], 'm_w_attn_o': out['m_w_attn_o'], 'm_conf_dw_w': out['m_conf_dw_w'], 'm_conf_dw_b': out['m_conf_dw_b'], 'm_conf_ln_g': out['m_conf_ln_g'], 'm_conf_ln_b': out['m_conf_ln_b'], 'm_w_conf_out': out['m_w_conf_out'], 'm_sc_dw_w': out['m_sc_dw_w'], 'm_w_sc_out': out['m_w_sc_out'], 'm_w_mix_out': out['m_w_mix_out'], 'm_w_ffn_in': out['m_w_ffn_in'], 'm_w_ffn_out': out['m_w_ffn_out'], 'v_c_ctx': out['v_c_ctx'], 'v_w_ada': out['v_w_ada'], 'v_b_ada': out['v_b_ada'], 'v_w_in': out['v_w_in'], 'v_q_norm': out['v_q_norm'], 'v_k_norm': out['v_k_norm'], 'v_w_attn_o': out['v_w_attn_o'], 'v_conf_dw_w': out['v_conf_dw_w'], 'v_conf_dw_b': out['v_conf_dw_b'], 'v_conf_ln_g': out['v_conf_ln_g'], 'v_conf_ln_b': out['v_conf_ln_b'], 'v_w_conf_out': out['v_w_conf_out'], 'v_sc_dw_w': out['v_sc_dw_w'], 'v_w_sc_out': out['v_w_sc_out'], 'v_w_mix_out': out['v_w_mix_out'], 'v_w_ffn_in': out['v_w_ffn_in'], 'v_w_ffn_out': out['v_w_ffn_out']}


def _loss(weights, diff, rest, loss_target):
    with _jax.named_scope("forward"):
        args = {**rest, TWIN_DIFF_INPUT: diff, **{k: w.astype(_WEIGHT_DTYPES[k]) for k, w in weights.items()}}
        y = _forward(args)
    with _jax.named_scope("loss_head"):
        err = _jnp.square(y.astype(_jnp.float32) - loss_target)
        return 0.5 * _jnp.sum(_jnp.mean(err, axis=-1)) if err.ndim else 0.5 * err


def _adamw(w, g, m, v):
    m = ADAM_B1 * m + (1.0 - ADAM_B1) * g
    v = ADAM_B2 * v + (1.0 - ADAM_B2) * _jnp.square(g)
    m_hat = m / (1.0 - ADAM_B1 ** ADAM_STEP)
    v_hat = v / (1.0 - ADAM_B2 ** ADAM_STEP)
    delta = -ADAM_LR * (m_hat / (_jnp.sqrt(v_hat) + ADAM_EPS) + ADAM_WD * w)
    return delta, m, v


def reference(x, c, ctx, c_ctx, w_ada, b_ada, w_in, q_norm, k_norm, w_attn_o, conf_dw_w, conf_dw_b, conf_ln_g, conf_ln_b, w_conf_out, sc_dw_w, w_sc_out, w_mix_out, w_ffn_in, w_ffn_out, loss_target, m_c_ctx, m_w_ada, m_b_ada, m_w_in, m_q_norm, m_k_norm, m_w_attn_o, m_conf_dw_w, m_conf_dw_b, m_conf_ln_g, m_conf_ln_b, m_w_conf_out, m_sc_dw_w, m_w_sc_out, m_w_mix_out, m_w_ffn_in, m_w_ffn_out, v_c_ctx, v_w_ada, v_b_ada, v_w_in, v_q_norm, v_k_norm, v_w_attn_o, v_conf_dw_w, v_conf_dw_b, v_conf_ln_g, v_conf_ln_b, v_w_conf_out, v_sc_dw_w, v_w_sc_out, v_w_mix_out, v_w_ffn_in, v_w_ffn_out):
    given = dict(x=x, c=c, ctx=ctx, c_ctx=c_ctx, w_ada=w_ada, b_ada=b_ada, w_in=w_in, q_norm=q_norm, k_norm=k_norm, w_attn_o=w_attn_o, conf_dw_w=conf_dw_w, conf_dw_b=conf_dw_b, conf_ln_g=conf_ln_g, conf_ln_b=conf_ln_b, w_conf_out=w_conf_out, sc_dw_w=sc_dw_w, w_sc_out=w_sc_out, w_mix_out=w_mix_out, w_ffn_in=w_ffn_in, w_ffn_out=w_ffn_out, loss_target=loss_target, m_c_ctx=m_c_ctx, m_w_ada=m_w_ada, m_b_ada=m_b_ada, m_w_in=m_w_in, m_q_norm=m_q_norm, m_k_norm=m_k_norm, m_w_attn_o=m_w_attn_o, m_conf_dw_w=m_conf_dw_w, m_conf_dw_b=m_conf_dw_b, m_conf_ln_g=m_conf_ln_g, m_conf_ln_b=m_conf_ln_b, m_w_conf_out=m_w_conf_out, m_sc_dw_w=m_sc_dw_w, m_w_sc_out=m_w_sc_out, m_w_mix_out=m_w_mix_out, m_w_ffn_in=m_w_ffn_in, m_w_ffn_out=m_w_ffn_out, v_c_ctx=v_c_ctx, v_w_ada=v_w_ada, v_b_ada=v_b_ada, v_w_in=v_w_in, v_q_norm=v_q_norm, v_k_norm=v_k_norm, v_w_attn_o=v_w_attn_o, v_conf_dw_w=v_conf_dw_w, v_conf_dw_b=v_conf_dw_b, v_conf_ln_g=v_conf_ln_g, v_conf_ln_b=v_conf_ln_b, v_w_conf_out=v_w_conf_out, v_sc_dw_w=v_sc_dw_w, v_w_sc_out=v_w_sc_out, v_w_mix_out=v_w_mix_out, v_w_ffn_in=v_w_ffn_in, v_w_ffn_out=v_w_ffn_out)
    weights = {n: given[n] for n in TWIN_WEIGHTS}
    shared = {n: given[n] for n in SHARED_INPUTS}
    per_example = {n: given[n] for n in ['x', 'c', 'ctx']}
    grad_fn = _jax.value_and_grad(_loss, argnums=(0, 1))

    def one_microbatch(ex, loss_target):
        ex = dict(ex)
        diff = ex.pop(TWIN_DIFF_INPUT)
        return grad_fn(weights, diff, {**shared, **ex}, loss_target)

    if N_MICROBATCH == 1:
        loss, (grad_w, grad_x) = one_microbatch(per_example, given["loss_target"])
    else:
        def body(carry, xs):
            loss_sum, grad_sum = carry
            l_k, (gw_k, gx_k) = one_microbatch(xs[0], xs[1])
            with _jax.named_scope("update"):
                return (loss_sum + l_k, _jax.tree.map(_jnp.add, grad_sum, gw_k)), gx_k

        init = (_jnp.zeros((), _jnp.float32), _jax.tree.map(_jnp.zeros_like, weights))
        (loss, grad_w), grad_x = _jax.lax.scan(body, init, (per_example, given["loss_target"]))
    with _jax.named_scope("update"):
        delta_w, new_m, new_v = {}, {}, {}
        for n in TWIN_WEIGHTS:
            delta_w[n], new_m[n], new_v[n] = _adamw(weights[n], grad_w[n], given["m_" + n], given["v_" + n])
    return (loss, grad_x, *[grad_w[n] for n in TWIN_WEIGHTS], *[delta_w[n] for n in TWIN_WEIGHTS],
            *[new_m[n] for n in TWIN_WEIGHTS], *[new_v[n] for n in TWIN_WEIGHTS])
```

```python
import functools
import math

import numpy as np
import jax
import jax.numpy as jnp
from jax import lax
from jax.experimental import pallas as pl
from jax.experimental.pallas import tpu as pltpu

f32 = jnp.float32
bf16 = jnp.bfloat16

NDEV = 8
AXES = ("x", "y", "c")
HEAD_DIM = 64
N_Q_HEADS = 8
N_KV_HEADS = 2
Q_GROUP = N_Q_HEADS // N_KV_HEADS
AW = N_Q_HEADS * HEAD_DIM
KVW = N_KV_HEADS * HEAD_DIM
GW = Q_GROUP * HEAD_DIM
QKW = AW + KVW
QKVW = AW + 2 * KVW
GRID_W = 64
AXIS_DIM = HEAD_DIM // 2
ROPE_THETA = 10000.0
ATTN_SCALE = HEAD_DIM ** -0.5
EPS = 1e-6
N_MOD = 6
MOD_ROWS = 8
CONV_PAD_ROWS = 16
SC_PAD_ROWS = 8
LANE = 128
SUBLANE = 8
VMEM_LIMIT = 56 * 1024 * 1024

ADAM_LR = 0.001
ADAM_B1 = 0.9
ADAM_B2 = 0.999
ADAM_EPS = 1e-08
ADAM_WD = 0.01
ADAM_STEP = 10

HI = lax.Precision.HIGHEST
NN = (((1,), (0,)), ((), ()))
NT_ = (((1,), (1,)), ((), ()))
TN = (((0,), (0,)), ((), ()))


def _params(ndims, vmem=None):
    return pltpu.CompilerParams(dimension_semantics=("arbitrary",) * ndims, vmem_limit_bytes=vmem)


def _divisor(n, cap, mult):
    best = None
    for d in range(mult, min(n, cap) + 1, mult):
        if n % d == 0:
            best = d
    return n if best is None else best


def _const_spec(a):
    nd = a.ndim
    return pl.BlockSpec(a.shape, lambda *idx: (0,) * nd)


class _Cfg:
    pass


def _mm(a, b, mode, out_dtype, name):
    if mode == "tn":
        m, ka = a.shape
        nb = b.shape[1]
        tka = _divisor(ka, 640, LANE)
        tnb = _divisor(nb, 1024, LANE)
        tmr = _divisor(m, 1024, SUBLANE)
        nsteps = m // tmr

        def body(a_ref, b_ref, o_ref, acc_ref):
            k = pl.program_id(2)

            @pl.when(k == 0)
            def _():
                acc_ref[...] = jnp.zeros_like(acc_ref)

            acc_ref[...] += lax.dot_general(a_ref[...].astype(bf16), b_ref[...].astype(bf16), TN,
                                            preferred_element_type=f32)

            @pl.when(k == nsteps - 1)
            def _():
                o_ref[...] = acc_ref[...].astype(out_dtype)

        return pl.pallas_call(
            body, out_shape=jax.ShapeDtypeStruct((ka, nb), out_dtype),
            grid=(ka // tka, nb // tnb, nsteps),
            in_specs=[pl.BlockSpec((tmr, tka), lambda i, j, k: (k, i)),
                      pl.BlockSpec((tmr, tnb), lambda i, j, k: (k, j))],
            out_specs=pl.BlockSpec((tka, tnb), lambda i, j, k: (i, j)),
            scratch_shapes=[pltpu.VMEM((tka, tnb), f32)],
            compiler_params=_params(3, VMEM_LIMIT), name=name)(a, b)

    m, kdim = a.shape
    n = b.shape[1] if mode == "nn" else b.shape[0]
    tm = _divisor(m, 1024 if a.dtype == bf16 else 512, SUBLANE)
    tn = _divisor(n, 1408, LANE)
    tk = _divisor(kdim, 1408, LANE)
    nsteps = kdim // tk
    dims = NN if mode == "nn" else NT_

    def body(a_ref, b_ref, o_ref, acc_ref):
        k = pl.program_id(2)

        @pl.when(k == 0)
        def _():
            acc_ref[...] = jnp.zeros_like(acc_ref)

        acc_ref[...] += lax.dot_general(a_ref[...].astype(bf16), b_ref[...].astype(bf16), dims,
                                        preferred_element_type=f32)

        @pl.when(k == nsteps - 1)
        def _():
            o_ref[...] = acc_ref[...].astype(out_dtype)

    if mode == "nn":
        b_spec = pl.BlockSpec((tk, tn), lambda i, j, k: (k, j))
    else:
        b_spec = pl.BlockSpec((tn, tk), lambda i, j, k: (j, k))
    return pl.pallas_call(
        body, out_shape=jax.ShapeDtypeStruct((m, n), out_dtype),
        grid=(m // tm, n // tn, nsteps),
        in_specs=[pl.BlockSpec((tm, tk), lambda i, j, k: (i, k)), b_spec],
        out_specs=pl.BlockSpec((tm, tn), lambda i, j, k: (i, j)),
        scratch_shapes=[pltpu.VMEM((tm, tn), f32)],
        compiler_params=_params(3, VMEM_LIMIT), name=name)(a, b)


def _row_spec(cfg, width, col=0):
    return pl.BlockSpec((cfg.TR, width), lambda i: (i, col))


def _mod_spec(cfg):
    nt, nc = cfg.NT, cfg.NC
    return pl.BlockSpec((1, 1, MOD_ROWS, cfg.D),
                        lambda i: (i // nt, ((i % nt) >= nc).astype(jnp.int32), 0, 0))


def _segment_start(cfg, i):
    j = i % cfg.NT
    return jnp.logical_or(j == 0, j == cfg.NC)


def _norm_fwd(cfg, name, xin, o, modg, gk, modn, sk):
    has_o = o is not None
    r_, d = xin.shape

    def body(*refs):
        if has_o:
            x_ref, o_ref, mg_ref, mn_ref, xn_ref, h_ref = refs
            x = x_ref[...] + mg_ref[0, 0, gk:gk + 1, :] * o_ref[...]
            xn_ref[...] = x
        else:
            x_ref, mn_ref, h_ref = refs
            x = x_ref[...]
        r = lax.rsqrt(jnp.mean(x * x, axis=-1, keepdims=True) + EPS)
        h = (x * r) * (1.0 + mn_ref[0, 0, sk + 1:sk + 2, :]) + mn_ref[0, 0, sk:sk + 1, :]
        h_ref[...] = h.astype(bf16)

    row = _row_spec(cfg, d)
    if has_o:
        ins, in_specs = (xin, o, modg, modn), [row, row, _mod_spec(cfg), _mod_spec(cfg)]
        out_shape = (jax.ShapeDtypeStruct((r_, d), f32), jax.ShapeDtypeStruct((r_, d), bf16))
        out_specs = (row, row)
    else:
        ins, in_specs = (xin, modn), [row, _mod_spec(cfg)]
        out_shape = jax.ShapeDtypeStruct((r_, d), bf16)
        out_specs = row
    res = pl.pallas_call(body, out_shape=out_shape, grid=(r_ // cfg.TR,), in_specs=in_specs,
                         out_specs=out_specs, compiler_params=_params(1), name=name)(*ins)
    return res if has_o else (xin, res)


def _norm_bwd(cfg, name, xnew, o, dh, dres, modg, gk, modn, sk):
    has_o = o is not None
    r_, d = xnew.shape

    def body(*refs):
        if has_o:
            xn_ref, o_ref, dh_ref, dr_ref, mg_ref, mn_ref, dx_ref, do_ref, dm_ref = refs
        else:
            xn_ref, dh_ref, dr_ref, mn_ref, dx_ref, dm_ref = refs
        i = pl.program_id(0)
        x = xn_ref[...]
        dhv = dh_ref[...].astype(f32)
        r = lax.rsqrt(jnp.mean(x * x, axis=-1, keepdims=True) + EPS)
        xh = x * r
        dxh = dhv * (1.0 + mn_ref[0, 0, sk + 1:sk + 2, :])
        dx = r * (dxh - xh * jnp.mean(dxh * xh, axis=-1, keepdims=True)) + dr_ref[...]
        dx_ref[...] = dx

        @pl.when(_segment_start(cfg, i))
        def _():
            dm_ref[...] = jnp.zeros_like(dm_ref)

        dm_ref[0, 0, 0:1, :] += jnp.sum(dhv, axis=0, keepdims=True)
        dm_ref[0, 0, 1:2, :] += jnp.sum(dhv * xh, axis=0, keepdims=True)
        if has_o:
            ov = o_ref[...]
            dm_ref[0, 0, 2:3, :] += jnp.sum(dx * ov, axis=0, keepdims=True)
            do_ref[...] = (dx * mg_ref[0, 0, gk:gk + 1, :]).astype(bf16)

    row = _row_spec(cfg, d)
    dm_shape = jax.ShapeDtypeStruct((cfg.B, 2, MOD_ROWS, d), f32)
    if has_o:
        ins = (xnew, o, dh, dres, modg, modn)
        in_specs = [row, row, row, row, _mod_spec(cfg), _mod_spec(cfg)]
        out_shape = (jax.ShapeDtypeStruct((r_, d), f32), jax.ShapeDtypeStruct((r_, d), bf16), dm_shape)
        out_specs = (row, row, _mod_spec(cfg))
    else:
        ins = (xnew, dh, dres, modn)
        in_specs = [row, row, row, _mod_spec(cfg)]
        out_shape = (jax.ShapeDtypeStruct((r_, d), f32), dm_shape)
        out_specs = (row, _mod_spec(cfg))
    res = pl.pallas_call(body, out_shape=out_shape, grid=(r_ // cfg.TR,), in_specs=in_specs,
                         out_specs=out_specs, compiler_params=_params(1), name=name)(*ins)
    if has_o:
        return res
    return res[0], None, res[1]


def _loss_head(cfg, name, x1, o, modg, gk, target):
    r_, d = x1.shape
    nt, nc = cfg.NT, cfg.NC
    nlat = nt - nc

    def body(x_ref, o_ref, mg_ref, t_ref, ls_ref, dy_ref, do_ref, dm_ref):
        i = pl.program_id(0)
        lat = (i % nt) >= nc
        gate = mg_ref[0, 0, gk:gk + 1, :]
        ov = o_ref[...]
        err = jnp.where(lat, x_ref[...] + gate * ov - t_ref[...], 0.0)
        ls_ref[...] = jnp.zeros_like(ls_ref)
        ls_ref[0, 0:1, :] = jnp.sum(err * err, axis=0, keepdims=True)
        dy = err * (1.0 / d)
        dy_ref[...] = dy
        do_ref[...] = (dy * gate).astype(bf16)

        @pl.when(_segment_start(cfg, i))
        def _():
            dm_ref[...] = jnp.zeros_like(dm_ref)

        dm_ref[0, 0, 2:3, :] += jnp.sum(dy * ov, axis=0, keepdims=True)

    row = _row_spec(cfg, d)
    t_spec = pl.BlockSpec((cfg.TR, d), lambda i: ((i // nt) * nlat + jnp.maximum((i % nt) - nc, 0), 0))
    return pl.pallas_call(
        body,
        out_shape=(jax.ShapeDtypeStruct((r_ // cfg.TR, SUBLANE, d), f32), jax.ShapeDtypeStruct((r_, d), f32),
                   jax.ShapeDtypeStruct((r_, d), bf16), jax.ShapeDtypeStruct((cfg.B, 2, MOD_ROWS, d), f32)),
        grid=(r_ // cfg.TR,),
        in_specs=[row, row, _mod_spec(cfg), t_spec],
        out_specs=(pl.BlockSpec((1, SUBLANE, d), lambda i: (i, 0, 0)), row, row, _mod_spec(cfg)),
        compiler_params=_params(1), name=name)(x1, o, modg, target)


def _swap16(y):
    w = y.shape[1]
    lane = lax.broadcasted_iota(jnp.int32, (1, w), 1)
    lo = jnp.bitwise_and(lane, 31) < 16
    return jnp.where(lo, pltpu.roll(y, w - 16, 1), pltpu.roll(y, 16, 1))


def _head_sum(v, bd):
    return jnp.dot(v, bd, precision=HI, preferred_element_type=f32)


def _qkv_fwd(cfg, name, p, cs, gain, bd, rep):
    r_ = p.shape[0]
    nt = cfg.NT

    def body(p_ref, cs_ref, g_ref, bd_ref, rep_ref, q_ref, k_ref, v_ref):
        x = p_ref[:, 0:QKW]
        r = lax.rsqrt(_head_sum(x * x, bd_ref[...]) * (1.0 / HEAD_DIM) + EPS)
        y = (x * r) * g_ref[...]
        out = y * cs_ref[0] + _swap16(y) * cs_ref[1]
        q_ref[...] = out[:, 0:AW].astype(bf16)
        kr = out[:, AW:QKW].astype(bf16)
        k_ref[...] = jnp.dot(kr, rep_ref[...], preferred_element_type=f32).astype(bf16)
        vv = p_ref[:, QKW:QKVW].astype(bf16)
        v_ref[...] = jnp.dot(vv, rep_ref[...], preferred_element_type=f32).astype(bf16)

    row = _row_spec(cfg, AW)
    return pl.pallas_call(
        body, out_shape=(jax.ShapeDtypeStruct((r_, AW), bf16),) * 3, grid=(r_ // cfg.TR,),
        in_specs=[_row_spec(cfg, QKVW), pl.BlockSpec((2, cfg.TR, QKW), lambda i: (0, i % nt, 0)),
                  _const_spec(gain), _const_spec(bd), _const_spec(rep)],
        out_specs=(row, row, row), compiler_params=_params(1), name=name)(p, cs, gain, bd, rep)


def _qkv_bwd(cfg, name, p, dq, dkr, dvr, cs, gain, bd, rept):
    r_ = p.shape[0]
    nt = cfg.NT

    def body(p_ref, dq_ref, dk_ref, dv_ref, cs_ref, g_ref, bd_ref, rt_ref, dp_ref, dg_ref):
        i = pl.program_id(0)
        x = p_ref[:, 0:QKW]
        rt = rt_ref[...]
        dk = jnp.dot(dk_ref[...], rt, precision=HI, preferred_element_type=f32)
        g = jnp.concatenate([dq_ref[...], dk], axis=1)
        gy = g * cs_ref[0] + _swap16(g * cs_ref[1])
        bdv = bd_ref[...]
        r = lax.rsqrt(_head_sum(x * x, bdv) * (1.0 / HEAD_DIM) + EPS)
        xh = x * r
        dxh = gy * g_ref[...]
        dx = r * (dxh - xh * (_head_sum(dxh * xh, bdv) * (1.0 / HEAD_DIM)))
        dp_ref[:, 0:QKW] = dx.astype(bf16)
        dp_ref[:, QKW:QKVW] = jnp.dot(dv_ref[...], rt, precision=HI, preferred_element_type=f32).astype(bf16)

        @pl.when(i == 0)
        def _():
            dg_ref[...] = jnp.zeros_like(dg_ref)

        dg_ref[0:1, :] += jnp.sum(gy * xh, axis=0, keepdims=True)

    row = _row_spec(cfg, AW)
    return pl.pallas_call(
        body, out_shape=(jax.ShapeDtypeStruct((r_, QKVW), bf16), jax.ShapeDtypeStruct((SUBLANE, QKW), f32)),
        grid=(r_ // cfg.TR,),
        in_specs=[_row_spec(cfg, QKVW), row, row, row,
                  pl.BlockSpec((2, cfg.TR, QKW), lambda i: (0, i % nt, 0)),
                  _const_spec(gain), _const_spec(bd), _const_spec(rept)],
        out_specs=(_row_spec(cfg, QKVW), pl.BlockSpec((SUBLANE, QKW), lambda i: (0, 0))),
        compiler_params=_params(1), name=name)(p, dq, dkr, dvr, cs, gain, bd, rept)


def _head_masks():
    lane = lax.broadcasted_iota(jnp.int32, (1, GW), 1)
    return [jnp.logical_and(lane >= HEAD_DIM * h, lane < HEAD_DIM * (h + 1)) for h in range(Q_GROUP)]


def _attn_fwd(cfg, name, q, krep, vrep):
    r_ = q.shape[0]
    tr, t, lc, nt, nc = cfg.TR, cfg.T, cfg.LC, cfg.NT, cfg.NC

    def body(q_ref, k_ref, v_ref, o_ref, l_ref):
        j = pl.program_id(2)
        masks = _head_masks()
        lane = lax.broadcasted_iota(jnp.int32, (1, LANE), 1)

        def run(nk):
            qv = q_ref[...]
            kv = k_ref[0:nk, :]
            vv = v_ref[0:nk, :]
            acc = jnp.zeros((tr, GW), f32)
            lse = jnp.zeros((tr, LANE), f32)
            for h in range(Q_GROUP):
                qh = jnp.where(masks[h], qv, jnp.zeros_like(qv))
                s = lax.dot_general(qh, kv, NT_, preferred_element_type=f32) * ATTN_SCALE
                m = jnp.max(s, axis=1, keepdims=True)
                pr = jnp.exp(s - m)
                l = jnp.sum(pr, axis=1, keepdims=True)
                oh = jnp.dot(pr.astype(bf16), vv, preferred_element_type=f32) / l
                acc = acc + jnp.where(masks[h], oh, 0.0)
                lse = lse + jnp.where(lane == h, m + jnp.log(l), 0.0)
            o_ref[...] = acc
            l_ref[...] = lse

        @pl.when(j < nc)
        def _():
            run(lc)

        @pl.when(j >= nc)
        def _():
            run(t)

    qspec = pl.BlockSpec((tr, GW), lambda e, g, j: (e * nt + j, g))
    kspec = pl.BlockSpec((t, GW), lambda e, g, j: (e, g))
    return pl.pallas_call(
        body, out_shape=(jax.ShapeDtypeStruct((r_, AW), f32), jax.ShapeDtypeStruct((r_, N_KV_HEADS * LANE), f32)),
        grid=(cfg.B, N_KV_HEADS, nt), in_specs=[qspec, kspec, kspec],
        out_specs=(qspec, pl.BlockSpec((tr, LANE), lambda e, g, j: (e * nt + j, g))),
        compiler_params=_params(3, VMEM_LIMIT), name=name)(q, krep, vrep)


def _attn_bwd(cfg, name, q, krep, vrep, o, lse, do):
    r_ = q.shape[0]
    tr, t, lc, nt, nc = cfg.TR, cfg.T, cfg.LC, cfg.NT, cfg.NC

    def body(q_ref, k_ref, v_ref, o_ref, l_ref, do_ref, dq_ref, dk_ref, dv_ref):
        j = pl.program_id(2)
        masks = _head_masks()
        lane = lax.broadcasted_iota(jnp.int32, (1, LANE), 1)

        @pl.when(j == 0)
        def _():
            dk_ref[...] = jnp.zeros_like(dk_ref)
            dv_ref[...] = jnp.zeros_like(dv_ref)

        def run(nk):
            qv = q_ref[...]
            kv = k_ref[0:nk, :]
            vv = v_ref[0:nk, :]
            ov = o_ref[...]
            dov = do_ref[...]
            lv = l_ref[...]
            dq = jnp.zeros((tr, GW), f32)
            dk = jnp.zeros((nk, GW), f32)
            dv = jnp.zeros((nk, GW), f32)
            for h in range(Q_GROUP):
                qh = jnp.where(masks[h], qv, jnp.zeros_like(qv))
                doh = jnp.where(masks[h], dov, 0.0)
                dohb = doh.astype(bf16)
                delta = jnp.sum(doh * ov, axis=1, keepdims=True)
                lse_h = jnp.sum(jnp.where(lane == h, lv, 0.0), axis=1, keepdims=True)
                s = lax.dot_general(qh, kv, NT_, preferred_element_type=f32) * ATTN_SCALE
                pr = jnp.exp(s - lse_h)
                dpr = lax.dot_general(dohb, vv, NT_, preferred_element_type=f32)
                ds = (pr * (dpr - delta) * ATTN_SCALE).astype(bf16)
                dq = dq + jnp.where(masks[h], jnp.dot(ds, kv, preferred_element_type=f32), 0.0)
                dk = dk + lax.dot_general(ds, qh, TN, preferred_element_type=f32)
                dv = dv + lax.dot_general(pr.astype(bf16), dohb, TN, preferred_element_type=f32)
            dq_ref[...] = dq
            dk_ref[0:nk, :] += dk
            dv_ref[0:nk, :] += dv

        @pl.when(j < nc)
        def _():
            run(lc)

        @pl.when(j >= nc)
        def _():
            run(t)

    qspec = pl.BlockSpec((tr, GW), lambda e, g, j: (e * nt + j, g))
    kspec = pl.BlockSpec((t, GW), lambda e, g, j: (e, g))
    lspec = pl.BlockSpec((tr, LANE), lambda e, g, j: (e * nt + j, g))
    return pl.pallas_call(
        body, out_shape=(jax.ShapeDtypeStruct((r_, AW), f32),) * 3,
        grid=(cfg.B, N_KV_HEADS, nt), in_specs=[qspec, kspec, kspec, qspec, lspec, qspec],
        out_specs=(qspec, kspec, kspec),
        compiler_params=_params(3, VMEM_LIMIT), name=name)(q, krep, vrep, o, lse, do)


def _padded(x, pad_rows):
    z = jnp.zeros((pad_rows, x.shape[1]), f32)
    return jnp.concatenate([z, x, z], axis=0)


def _shifted(buf, shift):
    n = buf.shape[0]
    shift = shift % n
    return buf if shift == 0 else pltpu.roll(buf, shift, 0)


def _dwconv(x, w_ref, taps, pad_rows):
    n = x.shape[0]
    buf = _padded(x, pad_rows)
    acc = jnp.zeros_like(buf)
    for k in range(taps):
        acc = acc + w_ref[k:k + 1, :] * _shifted(buf, taps // 2 - k)
    return acc[pad_rows:pad_rows + n]


def _dwconv_t(dy, w_ref, taps, pad_rows):
    n = dy.shape[0]
    buf = _padded(dy, pad_rows)
    acc = jnp.zeros_like(buf)
    for k in range(taps):
        acc = acc + w_ref[k:k + 1, :] * _shifted(buf, k - taps // 2)
    return acc[pad_rows:pad_rows + n]


def _dwconv_dw(x, dy, dw_ref, taps, pad_rows):
    n = x.shape[0]
    buf = _padded(x, pad_rows)
    for k in range(taps):
        xs = _shifted(buf, taps // 2 - k)[pad_rows:pad_rows + n]
        dw_ref[k:k + 1, :] += jnp.sum(xs * dy, axis=0, keepdims=True)


def _conv_specs(cfg):
    t, cb, cw = cfg.T, cfg.CB, cfg.CW
    offs = (cfg.OFF_CONF, cfg.OFF_CONF + cw, cfg.OFF_SC, cfg.OFF_SC + cw, cfg.OFF_SC + 2 * cw)
    return [pl.BlockSpec((t, cb), lambda c, e, off=off: (e, off // cb + c)) for off in offs]


def _conv_fwd(cfg, name, p, w31, w3):
    r_ = p.shape[0]
    t, lc, cb, cw = cfg.T, cfg.LC, cfg.CB, cfg.CW
    k31, k3 = cfg.K31, cfg.K3

    def body(a_ref, g_ref, bg_ref, cg_ref, xs_ref, w31_ref, w3_ref, hc_ref, z_ref):
        for lo, hi in ((0, lc), (lc, t)):
            hh = a_ref[lo:hi, :] * jax.nn.sigmoid(g_ref[lo:hi, :])
            hc_ref[lo:hi, :] = _dwconv(hh, w31_ref, k31, CONV_PAD_ROWS) + w31_ref[k31:k31 + 1, :]
            u = cg_ref[lo:hi, :] * xs_ref[lo:hi, :]
            z_ref[lo:hi, :] = (bg_ref[lo:hi, :] * _dwconv(u, w3_ref, k3, SC_PAD_ROWS)).astype(bf16)

    ospec = pl.BlockSpec((t, cb), lambda c, e: (e, c))
    return pl.pallas_call(
        body, out_shape=(jax.ShapeDtypeStruct((r_, cw), f32), jax.ShapeDtypeStruct((r_, cw), bf16)),
        grid=(cw // cb, cfg.B),
        in_specs=_conv_specs(cfg) + [pl.BlockSpec((w31.shape[0], cb), lambda c, e: (0, c)),
                                     pl.BlockSpec((w3.shape[0], cb), lambda c, e: (0, c))],
        out_specs=(ospec, ospec), compiler_params=_params(2, VMEM_LIMIT), name=name)(p, p, p, p, p, w31, w3)


def _conv_bwd(cfg, name, p, dhc, dz, w31, w3):
    r_ = p.shape[0]
    t, lc, cb, cw = cfg.T, cfg.LC, cfg.CB, cfg.CW
    k31, k3 = cfg.K31, cfg.K3

    def body(a_ref, g_ref, bg_ref, cg_ref, xs_ref, dhc_ref, dz_ref, w31_ref, w3_ref,
             da_ref, dg_ref, dbg_ref, dcg_ref, dxs_ref, dw31_ref, dw3_ref):
        e = pl.program_id(1)

        @pl.when(e == 0)
        def _():
            dw31_ref[...] = jnp.zeros_like(dw31_ref)
            dw3_ref[...] = jnp.zeros_like(dw3_ref)

        for lo, hi in ((0, lc), (lc, t)):
            a = a_ref[lo:hi, :]
            sg = jax.nn.sigmoid(g_ref[lo:hi, :])
            hh = a * sg
            dy = dhc_ref[lo:hi, :]
            dhh = _dwconv_t(dy, w31_ref, k31, CONV_PAD_ROWS)
            _dwconv_dw(hh, dy, dw31_ref, k31, CONV_PAD_ROWS)
            dw31_ref[k31:k31 + 1, :] += jnp.sum(dy, axis=0, keepdims=True)
            da_ref[lo:hi, :] = (dhh * sg).astype(bf16)
            dg_ref[lo:hi, :] = (dhh * hh * (1.0 - sg)).astype(bf16)

            cg = cg_ref[lo:hi, :]
            xs = xs_ref[lo:hi, :]
            u = cg * xs
            dzv = dz_ref[lo:hi, :]
            dbg_ref[lo:hi, :] = (dzv * _dwconv(u, w3_ref, k3, SC_PAD_ROWS)).astype(bf16)
            duc = dzv * bg_ref[lo:hi, :]
            du = _dwconv_t(duc, w3_ref, k3, SC_PAD_ROWS)
            _dwconv_dw(u, duc, dw3_ref, k3, SC_PAD_ROWS)
            dcg_ref[lo:hi, :] = (du * xs).astype(bf16)
            dxs_ref[lo:hi, :] = (du * cg).astype(bf16)

    ospec = pl.BlockSpec((t, cb), lambda c, e: (e, c))
    w31_spec = pl.BlockSpec((w31.shape[0], cb), lambda c, e: (0, c))
    w3_spec = pl.BlockSpec((w3.shape[0], cb), lambda c, e: (0, c))
    return pl.pallas_call(
        body,
        out_shape=(jax.ShapeDtypeStruct((r_, cw), bf16),) * 5
        + (jax.ShapeDtypeStruct(w31.shape, f32), jax.ShapeDtypeStruct(w3.shape, f32)),
        grid=(cw // cb, cfg.B),
        in_specs=_conv_specs(cfg) + [ospec, ospec, w31_spec, w3_spec],
        out_specs=(ospec,) * 5 + (w31_spec, w3_spec),
        compiler_params=_params(2, VMEM_LIMIT), name=name)(p, p, p, p, p, dhc, dz, w31, w3)


def _ln_silu_fwd(cfg, name, hc, lnp):
    r_, cw = hc.shape

    def body(x_ref, p_ref, o_ref):
        x = x_ref[...]
        mu = jnp.mean(x, axis=-1, keepdims=True)
        xc = x - mu
        rs = lax.rsqrt(jnp.mean(xc * xc, axis=-1, keepdims=True) + EPS)
        hn = (xc * rs) * p_ref[0:1, :] + p_ref[1:2, :]
        o_ref[...] = (hn * jax.nn.sigmoid(hn)).astype(bf16)

    row = _row_spec(cfg, cw)
    return pl.pallas_call(body, out_shape=jax.ShapeDtypeStruct((r_, cw), bf16), grid=(r_ // cfg.TR,),
                          in_specs=[row, _const_spec(lnp)], out_specs=row,
                          compiler_params=_params(1), name=name)(hc, lnp)


def _ln_silu_bwd(cfg, name, hc, dhs, lnp):
    r_, cw = hc.shape

    def body(x_ref, d_ref, p_ref, dx_ref, dp_ref):
        i = pl.program_id(0)
        x = x_ref[...]
        mu = jnp.mean(x, axis=-1, keepdims=True)
        xc = x - mu
        rs = lax.rsqrt(jnp.mean(xc * xc, axis=-1, keepdims=True) + EPS)
        xh = xc * rs
        gain = p_ref[0:1, :]
        hn = xh * gain + p_ref[1:2, :]
        sg = jax.nn.sigmoid(hn)
        dhn = d_ref[...] * (sg * (1.0 + hn * (1.0 - sg)))
        dxh = dhn * gain
        dx_ref[...] = rs * (dxh - jnp.mean(dxh, axis=-1, keepdims=True)
                            - xh * jnp.mean(dxh * xh, axis=-1, keepdims=True))

        @pl.when(i == 0)
        def _():
            dp_ref[...] = jnp.zeros_like(dp_ref)

        dp_ref[0:1, :] += jnp.sum(dhn * xh, axis=0, keepdims=True)
        dp_ref[1:2, :] += jnp.sum(dhn, axis=0, keepdims=True)

    row = _row_spec(cfg, cw)
    return pl.pallas_call(
        body, out_shape=(jax.ShapeDtypeStruct((r_, cw), f32), jax.ShapeDtypeStruct((SUBLANE, cw), f32)),
        grid=(r_ // cfg.TR,), in_specs=[row, row, _const_spec(lnp)],
        out_specs=(row, pl.BlockSpec((SUBLANE, cw), lambda i: (0, 0))),
        compiler_params=_params(1), name=name)(hc, dhs, lnp)


def _gate_fwd(cfg, name, p, ya, yc, ys):
    r_, d = ya.shape
    gb = cfg.GB
    base = cfg.OFF_GATE // gb
    per = d // gb

    def body(g0_ref, g1_ref, g2_ref, ya_ref, yc_ref, ys_ref, o_ref):
        m = (jax.nn.sigmoid(g0_ref[...]) * ya_ref[...] + jax.nn.sigmoid(g1_ref[...]) * yc_ref[...]
             + jax.nn.sigmoid(g2_ref[...]) * ys_ref[...])
        o_ref[...] = m.astype(bf16)

    yspec = pl.BlockSpec((cfg.TR, gb), lambda i, c: (i, c))
    gspecs = [pl.BlockSpec((cfg.TR, gb), lambda i, c, k=k: (i, base + k * per + c)) for k in range(3)]
    return pl.pallas_call(body, out_shape=jax.ShapeDtypeStruct((r_, d), bf16), grid=(r_ // cfg.TR, per),
                          in_specs=gspecs + [yspec] * 3, out_specs=yspec,
                          compiler_params=_params(2), name=name)(p, p, p, ya, yc, ys)


def _gate_bwd(cfg, name, p, ya, yc, ys, dm):
    r_, d = ya.shape
    gb = cfg.GB
    base = cfg.OFF_GATE // gb
    per = d // gb

    def body(g0_ref, g1_ref, g2_ref, ya_ref, yc_ref, ys_ref, dm_ref, da_ref, dc_ref, ds_ref, d0_ref, d1_ref, d2_ref):
        dmv = dm_ref[...]
        for g_ref, y_ref, dy_ref, dg_ref in ((g0_ref, ya_ref, da_ref, d0_ref), (g1_ref, yc_ref, dc_ref, d1_ref),
                                             (g2_ref, ys_ref, ds_ref, d2_ref)):
            sg = jax.nn.sigmoid(g_ref[...])
            dy_ref[...] = (dmv * sg).astype(bf16)
            dg_ref[...] = (dmv * y_ref[...] * sg * (1.0 - sg)).astype(bf16)

    yspec = pl.BlockSpec((cfg.TR, gb), lambda i, c: (i, c))
    gspecs = [pl.BlockSpec((cfg.TR, gb), lambda i, c, k=k: (i, base + k * per + c)) for k in range(3)]
    return pl.pallas_call(body, out_shape=(jax.ShapeDtypeStruct((r_, d), bf16),) * 6, grid=(r_ // cfg.TR, per),
                          in_specs=gspecs + [yspec] * 4, out_specs=(yspec,) * 6,
                          compiler_params=_params(2), name=name)(p, p, p, ya, yc, ys, dm)


def _swiglu_fwd(cfg, name, ff):
    r_, f2 = ff.shape
    fh = f2 // 2

    def body(a_ref, b_ref, o_ref):
        a = a_ref[...]
        o_ref[...] = (a * jax.nn.sigmoid(a) * b_ref[...]).astype(bf16)

    return pl.pallas_call(body, out_shape=jax.ShapeDtypeStruct((r_, fh), bf16), grid=(r_ // cfg.TR,),
                          in_specs=[_row_spec(cfg, fh, 0), _row_spec(cfg, fh, 1)], out_specs=_row_spec(cfg, fh),
                          compiler_params=_params(1, VMEM_LIMIT), name=name)(ff, ff)


def _swiglu_bwd(cfg, name, ff, ds):
    r_, f2 = ff.shape
    fh = f2 // 2

    def body(a_ref, b_ref, d_ref, o_ref):
        a = a_ref[...]
        sg = jax.nn.sigmoid(a)
        dsv = d_ref[...]
        o_ref[:, 0:fh] = (dsv * b_ref[...] * (sg * (1.0 + a * (1.0 - sg)))).astype(bf16)
        o_ref[:, fh:f2] = (dsv * a * sg).astype(bf16)

    return pl.pallas_call(body, out_shape=jax.ShapeDtypeStruct((r_, f2), bf16), grid=(r_ // cfg.TR,),
                          in_specs=[_row_spec(cfg, fh, 0), _row_spec(cfg, fh, 1), _row_spec(cfg, fh)],
                          out_specs=_row_spec(cfg, f2), compiler_params=_params(1, VMEM_LIMIT), name=name)(ff, ff, ds)


def _ada_fwd(name, call, w_ada, b_loc):
    nl, d, na = w_ada.shape
    nbp = call.shape[0]

    def body(c_ref, w_ref, b_ref, o_ref):
        cv = c_ref[...]
        a = (cv * jax.nn.sigmoid(cv)).astype(bf16)
        o_ref[0] = jnp.dot(a, w_ref[0].astype(bf16), preferred_element_type=f32) + b_ref[0]

    return pl.pallas_call(
        body, out_shape=jax.ShapeDtypeStruct((nl, nbp, na), f32), grid=(nl,),
        in_specs=[_const_spec(call), pl.BlockSpec((1, d, na), lambda l: (l, 0, 0)),
                  pl.BlockSpec((1, 1, na), lambda l: (l, 0, 0))],
        out_specs=pl.BlockSpec((1, nbp, na), lambda l: (l, 0, 0)),
        compiler_params=_params(1, VMEM_LIMIT), name=name)(call, w_ada, b_loc)


def _ada_bwd(name, call, w_ada, dmod, dmodc, cctx_row):
    nl, d, na = w_ada.shape
    nbp = call.shape[0]

    def body(c_ref, w_ref, dm_ref, dc_ref, gw_ref, gb_ref, gc_ref):
        l = pl.program_id(0)
        cv = c_ref[...]
        sg = jax.nn.sigmoid(cv)
        a = (cv * sg).astype(bf16)
        dctx = jnp.sum(dc_ref[0], axis=0, keepdims=True)
        rows = lax.broadcasted_iota(jnp.int32, (nbp, 1), 0)
        dm = jnp.where(rows == cctx_row, dctx, dm_ref[0])
        gw_ref[0] = lax.dot_general(a, dm.astype(bf16), TN, preferred_element_type=f32)
        gb_ref[0] = jnp.zeros((SUBLANE, na), f32)
        gb_ref[0, 0:1, :] = jnp.sum(dm, axis=0, keepdims=True)
        dc8 = jnp.broadcast_to(dctx, (SUBLANE, na)).astype(bf16)
        part = lax.dot_general(dc8, w_ref[0].astype(bf16), NT_, preferred_element_type=f32)
        cc = c_ref[cctx_row:cctx_row + 1, :]
        sc = jax.nn.sigmoid(cc)
        part = part * (sc * (1.0 + cc * (1.0 - sc)))

        @pl.when(l == 0)
        def _():
            gc_ref[...] = jnp.zeros_like(gc_ref)

        gc_ref[...] += part

    return pl.pallas_call(
        body,
        out_shape=(jax.ShapeDtypeStruct((nl, d, na), f32), jax.ShapeDtypeStruct((nl, SUBLANE, na), f32),
                   jax.ShapeDtypeStruct((SUBLANE, d), f32)),
        grid=(nl,),
        in_specs=[_const_spec(call), pl.BlockSpec((1, d, na), lambda l: (l, 0, 0)),
                  pl.BlockSpec((1, nbp, na), lambda l: (l, 0, 0)),
                  pl.BlockSpec((1, NDEV, na), lambda l: (l, 0, 0))],
        out_specs=(pl.BlockSpec((1, d, na), lambda l: (l, 0, 0)), pl.BlockSpec((1, SUBLANE, na), lambda l: (l, 0, 0)),
                   pl.BlockSpec((SUBLANE, d), lambda l: (0, 0))),
        compiler_params=_params(1, VMEM_LIMIT), name=name)(call, w_ada, dmod, dmodc)


def _adamw(name, w, g, m, v):
    rows, cols = w.shape
    tr = _divisor(rows, max(SUBLANE, (1 << 19) // cols), SUBLANE)
    c1 = 1.0 / (1.0 - ADAM_B1 ** ADAM_STEP)
    c2 = 1.0 / (1.0 - ADAM_B2 ** ADAM_STEP)

    def body(w_ref, g_ref, m_ref, v_ref, d_ref, nm_ref, nv_ref):
        gv = g_ref[...]
        nm = ADAM_B1 * m_ref[...] + (1.0 - ADAM_B1) * gv
        nv = ADAM_B2 * v_ref[...] + (1.0 - ADAM_B2) * (gv * gv)
        nm_ref[...] = nm
        nv_ref[...] = nv
        d_ref[...] = -ADAM_LR * ((nm * c1) / (jnp.sqrt(nv * c2) + ADAM_EPS) + ADAM_WD * w_ref[...])

    spec = pl.BlockSpec((tr, cols), lambda i: (i, 0))
    return pl.pallas_call(body, out_shape=(jax.ShapeDtypeStruct((rows, cols), f32),) * 3, grid=(rows // tr,),
                          in_specs=[spec] * 4, out_specs=(spec,) * 3,
                          compiler_params=_params(1, VMEM_LIMIT), name=name)(w, g, m, v)


def _sum_slots(name, x):
    nd, rows, cols = x.shape
    tr = _divisor(rows, max(SUBLANE, (1 << 18) // cols), SUBLANE)

    def body(x_ref, o_ref):
        acc = x_ref[0]
        for s in range(1, nd):
            acc = acc + x_ref[s]
        o_ref[...] = acc

    return pl.pallas_call(body, out_shape=jax.ShapeDtypeStruct((rows, cols), x.dtype), grid=(rows // tr,),
                          in_specs=[pl.BlockSpec((nd, tr, cols), lambda i: (0, i, 0))],
                          out_specs=pl.BlockSpec((tr, cols), lambda i: (i, 0)),
                          compiler_params=_params(1, VMEM_LIMIT), name=name)(x)


def _peer(k):
    x, y, c = (lax.axis_index(a) for a in AXES)
    px = 1 - x if k & 4 else x
    py = 1 - y if k & 2 else y
    pc = 1 - c if k & 1 else c
    return (px, py, pc), 4 * px + 2 * py + pc


def _exchange(name, xs, scatter):
    n = len(xs)
    nrel = NDEV - 1

    def body(*refs):
        ins, outs = refs[:n], refs[n:2 * n]
        send_sems, recv_sems, local_sems = refs[2 * n:]
        x, y, c = (lax.axis_index(a) for a in AXES)
        me = 4 * x + 2 * y + c
        started = []
        for a in range(n):
            src = ins[a].at[me] if scatter else ins[a]
            loc = pltpu.make_async_copy(src, outs[a].at[me], local_sems.at[a])
            loc.start()
            started.append(loc)
        for a in range(n):
            for k in range(1, NDEV):
                peer, pidx = _peer(k)
                src = ins[a].at[pidx] if scatter else ins[a]
                cp = pltpu.make_async_remote_copy(
                    src_ref=src, dst_ref=outs[a].at[me], send_sem=send_sems.at[a * nrel + k - 1],
                    recv_sem=recv_sems.at[a * nrel + k - 1], device_id=peer, device_id_type=pl.DeviceIdType.MESH)
                cp.start()
        for a in range(n):
            for k in range(1, NDEV):
                peer, pidx = _peer(k)
                src = ins[a].at[pidx] if scatter else ins[a]
                cp = pltpu.make_async_remote_copy(
                    src_ref=src, dst_ref=outs[a].at[pidx], send_sem=send_sems.at[a * nrel + k - 1],
                    recv_sem=recv_sems.at[a * nrel + k - 1], device_id=peer, device_id_type=pl.DeviceIdType.MESH)
                cp.wait_recv()
                cp.wait_send()
        for loc in started:
            loc.wait()

    out_shape = []
    for xa in xs:
        shp = xa.shape if scatter else (NDEV,) + xa.shape
        out_shape.append(jax.ShapeDtypeStruct(shp, xa.dtype))
    anyspec = pl.BlockSpec(memory_space=pl.ANY)
    outs = pl.pallas_call(
        body, out_shape=tuple(out_shape), in_specs=[anyspec] * n, out_specs=(anyspec,) * n,
        scratch_shapes=[pltpu.SemaphoreType.DMA((n * nrel,)), pltpu.SemaphoreType.DMA((n * nrel,)),
                        pltpu.SemaphoreType.DMA((n,))],
        name=name)(*xs)
    return list(outs)


def _all_gather(name, xs):
    return _exchange(name, xs, False)


def _all_to_all(name, xs):
    return _exchange(name, xs, True)


def _rope_table(cfg):
    s, lc = cfg.S, cfg.LC
    rows = s // GRID_W
    r_ids, c_ids = jnp.meshgrid(jnp.arange(rows), jnp.arange(GRID_W), indexing="ij")
    r_ids = r_ids.reshape(-1).astype(f32)
    c_ids = c_ids.reshape(-1).astype(f32)
    freqs = ROPE_THETA ** (-jnp.arange(0, AXIS_DIM, 2, dtype=f32) / AXIS_DIM)
    ang_r = r_ids[:, None] * freqs
    ang_c = c_ids[:, None] * freqs
    cos = jnp.concatenate([jnp.cos(ang_r), jnp.cos(ang_r), jnp.cos(ang_c), jnp.cos(ang_c)], axis=1)
    sin = jnp.concatenate([-jnp.sin(ang_r), jnp.sin(ang_r), -jnp.sin(ang_c), jnp.sin(ang_c)], axis=1)
    cos = jnp.concatenate([jnp.ones((lc, HEAD_DIM), f32), cos], axis=0)
    sin = jnp.concatenate([jnp.zeros((lc, HEAD_DIM), f32), sin], axis=0)
    reps = QKW // HEAD_DIM
    return jnp.stack([jnp.tile(cos, (1, reps)), jnp.tile(sin, (1, reps))])


def _block_diag_ones():
    idx = np.arange(QKW) // HEAD_DIM
    return jnp.asarray((idx[:, None] == idx[None, :]).astype(np.float32))


def _replicate_matrix():
    src = np.arange(KVW)
    dst = np.arange(AW)
    m = (src[:, None] // HEAD_DIM == dst[None, :] // GW) & (src[:, None] % HEAD_DIM == dst[None, :] % HEAD_DIM)
    return m.astype(np.float32)


def _pack_flat(parts):
    flat = jnp.concatenate([p.reshape(-1) for p in parts])
    n = flat.shape[0]
    unit = SUBLANE * LANE
    total = -(-n // unit) * unit
    flat = jnp.pad(flat, (0, total - n))
    return flat.reshape(total // LANE, LANE)


def _unpack_flat(flat, shapes):
    flat = flat.reshape(flat.shape[:-2] + (-1,))
    out, off = [], 0
    for shp in shapes:
        size = int(np.prod(shp))
        out.append(flat[..., off:off + size].reshape(flat.shape[:-1] + tuple(shp)))
        off += size
    return out


def kernel(x, c, ctx, c_ctx, w_ada, b_ada, w_in, q_norm, k_norm, w_attn_o, conf_dw_w, conf_dw_b, conf_ln_g, conf_ln_b, w_conf_out, sc_dw_w, w_sc_out, w_mix_out, w_ffn_in, w_ffn_out, loss_target, m_c_ctx, m_w_ada, m_b_ada, m_w_in, m_q_norm, m_k_norm, m_w_attn_o, m_conf_dw_w, m_conf_dw_b, m_conf_ln_g, m_conf_ln_b, m_w_conf_out, m_sc_dw_w, m_w_sc_out, m_w_mix_out, m_w_ffn_in, m_w_ffn_out, v_c_ctx, v_w_ada, v_b_ada, v_w_in, v_q_norm, v_k_norm, v_w_attn_o, v_conf_dw_w, v_conf_dw_b, v_conf_ln_g, v_conf_ln_b, v_w_conf_out, v_sc_dw_w, v_w_sc_out, v_w_mix_out, v_w_ffn_in, v_w_ffn_out):
    cfg = _Cfg()
    cfg.B, cfg.S, cfg.D = x.shape
    cfg.LC = ctx.shape[1]
    cfg.T = cfg.LC + cfg.S
    cfg.TR = min(256, cfg.LC)
    assert cfg.LC % cfg.TR == 0 and cfg.S % cfg.TR == 0 and cfg.S % GRID_W == 0
    cfg.NT, cfg.NC = cfg.T // cfg.TR, cfg.LC // cfg.TR
    cfg.R = cfg.B * cfg.T
    nl = w_in.shape[0]
    b, d = cfg.B, cfg.D
    cfg.CW = conf_dw_b.shape[1]
    cfg.K31, cfg.K3 = conf_dw_w.shape[1], sc_dw_w.shape[1]
    assert w_sc_out.shape[1] == cfg.CW and cfg.CW % LANE == 0
    assert cfg.K31 // 2 <= CONV_PAD_ROWS and cfg.K3 // 2 <= SC_PAD_ROWS
    cfg.CB = LANE
    cfg.OFF_CONF = QKVW
    cfg.OFF_SC = cfg.OFF_CONF + 2 * cfg.CW
    cfg.OFF_GATE = cfg.OFF_SC + 3 * cfg.CW
    n_in = w_in.shape[2] * NDEV
    assert n_in == cfg.OFF_GATE + 3 * d and w_attn_o.shape[1] == AW
    cfg.GB = math.gcd(cfg.OFF_GATE, d)
    assert cfg.GB % LANE == 0
    fh = w_ffn_out.shape[1] * NDEV
    na = w_ada.shape[2]
    cw8 = cfg.CW // NDEV

    xi, yi, ci = (lax.axis_index(a) for a in AXES)
    me = 4 * xi + 2 * yi + ci

    small_shapes = [c.shape, conf_dw_w.shape, sc_dw_w.shape]
    (g0,) = _all_gather("gather_small", [_pack_flat([c, conf_dw_w, sc_dw_w])])
    c_all, cw_all, sw_all = _unpack_flat(g0, small_shapes)
    nb = NDEV * b
    nbp = -(-(nb + 1) // SUBLANE) * SUBLANE
    call = jnp.concatenate([c_all.reshape(nb, d), c_ctx[None, :], jnp.zeros((nbp - nb - 1, d), f32)], axis=0)
    w31_full = jnp.moveaxis(cw_all, 0, 2).reshape(nl, cfg.K31, cfg.CW)
    w3_full = jnp.moveaxis(sw_all, 0, 2).reshape(nl, cfg.K3, cfg.CW)
    k31p = -(-(cfg.K31 + 1) // SUBLANE) * SUBLANE
    w31b = jnp.concatenate([w31_full, conf_dw_b[:, None, :], jnp.zeros((nl, k31p - cfg.K31 - 1, cfg.CW), f32)], axis=1)
    w3p = jnp.concatenate([w3_full, jnp.zeros((nl, SUBLANE - cfg.K3, cfg.CW), f32)], axis=1)

    b_loc = lax.dynamic_slice(b_ada, (0, me * na), (nl, na))[:, None, :]
    mod_loc = _ada_fwd("ada_fwd", call, w_ada, b_loc)
    (mod_g,) = _all_gather("gather_mod", [mod_loc.reshape(nl * nbp, na)])
    mod_full = jnp.transpose(mod_g.reshape(NDEV, nl, nbp, na), (1, 2, 0, 3)).reshape(nl, nbp, N_MOD, d)
    mod_lat = lax.dynamic_slice_in_dim(mod_full, me * b, b, axis=1)
    mod_ctx = jnp.broadcast_to(mod_full[:, nb][:, None], (nl, b, N_MOD, d))
    mod = jnp.stack([mod_ctx, mod_lat], axis=2)
    mod = jnp.pad(mod, ((0, 0), (0, 0), (0, 0), (0, MOD_ROWS - N_MOD), (0, 0)))

    cs = _rope_table(cfg)
    bd = _block_diag_ones()
    rep_np = _replicate_matrix()
    rep = jnp.asarray(rep_np, dtype=bf16)
    rept = jnp.asarray(rep_np.T)
    reps_q, reps_k = AW // HEAD_DIM, KVW // HEAD_DIM

    xin = jnp.concatenate([ctx, x], axis=1).reshape(cfg.R, d)
    target = loss_target.reshape(b * cfg.S, d)
    saved = []
    xcur, ocur = xin, None
    for l in range(nl):
        wl = _all_gather(f"gather_w{l}", [
            jnp.transpose(w_in[l]).astype(bf16), jnp.transpose(w_attn_o[l]).astype(bf16),
            jnp.transpose(w_conf_out[l]).astype(bf16), jnp.transpose(w_sc_out[l]).astype(bf16),
            w_mix_out[l].astype(bf16), jnp.transpose(w_ffn_in[l]).astype(bf16), w_ffn_out[l].astype(bf16)])
        win_t, wao_t, wco_t, wso_t, wmix, wfi_t, wfo = [g.reshape(NDEV * g.shape[1], g.shape[2]) for g in wl]
        gain = jnp.concatenate([jnp.tile(q_norm[l], reps_q), jnp.tile(k_norm[l], reps_k)])[None, :]
        lnp = jnp.concatenate([conf_ln_g[l][None], conf_ln_b[l][None], jnp.zeros((SUBLANE - 2, cfg.CW), f32)], axis=0)

        if l == 0:
            x0, h = _norm_fwd(cfg, f"norm_a{l}", xcur, None, None, 0, mod[l], 0)
        else:
            x0, h = _norm_fwd(cfg, f"norm_a{l}", xcur, ocur, mod[l - 1], 5, mod[l], 0)
        p = _mm(h, win_t, "nt", f32, f"mm_in{l}")
        q, krep, vrep = _qkv_fwd(cfg, f"qkv{l}", p, cs, gain, bd, rep)
        attn, lse = _attn_fwd(cfg, f"attn{l}", q, krep, vrep)
        hc, z = _conv_fwd(cfg, f"conv{l}", p, w31b[l], w3p[l])
        hs = _ln_silu_fwd(cfg, f"lnsilu{l}", hc, lnp)
        ya = _mm(attn, wao_t, "nt", f32, f"mm_ao{l}")
        yc = _mm(hs, wco_t, "nt", f32, f"mm_co{l}")
        ys = _mm(z, wso_t, "nt", f32, f"mm_so{l}")
        merged = _gate_fwd(cfg, f"gate{l}", p, ya, yc, ys)
        mixed = _mm(merged, wmix, "nn", f32, f"mm_mix{l}")
        x1, h2 = _norm_fwd(cfg, f"norm_b{l}", x0, mixed, mod[l], 2, mod[l], 3)
        ff = _mm(h2, wfi_t, "nt", f32, f"mm_fi{l}")
        sw = _swiglu_fwd(cfg, f"swiglu{l}", ff)
        o = _mm(sw, wfo, "nn", f32, f"mm_fo{l}")
        saved.append(dict(x0=x0, h=h, p=p, q=q, krep=krep, vrep=vrep, attn=attn, lse=lse, hc=hc, z=z, hs=hs,
                          ya=ya, yc=yc, ys=ys, merged=merged, mixed=mixed, x1=x1, h2=h2, ff=ff, sw=sw, o=o,
                          oprev=ocur, gain=gain, lnp=lnp,
                          w=(win_t, wao_t, wco_t, wso_t, wmix, wfi_t, wfo)))
        xcur, ocur = x1, o

    lsum, dres, do, dm_loss = _loss_head(cfg, "loss", xcur, ocur, mod[nl - 1], 5, target)
    loss = lax.psum((0.5 / d) * jnp.sum(lsum), AXES)

    dmod_rows = [[None] * N_MOD for _ in range(nl)]
    dmod_rows[nl - 1][5] = dm_loss[:, :, 2]
    g_qn, g_kn, g_w31, g_w3, g_ln = [None] * nl, [None] * nl, [None] * nl, [None] * nl, [None] * nl
    gbig = [None] * nl
    for l in reversed(range(nl)):
        sv = saved[l]
        win_t, wao_t, wco_t, wso_t, wmix, wfi_t, wfo = sv["w"]
        ds = _mm(do, wfo, "nt", f32, f"mm_dsw{l}")
        g_wfo = _mm(sv["sw"], do, "tn", f32, f"mm_gfo{l}")
        dff = _swiglu_bwd(cfg, f"swiglu_b{l}", sv["ff"], ds)
        dh2 = _mm(dff, wfi_t, "nn", f32, f"mm_dh2{l}")
        g_wfi = _mm(dff, sv["h2"], "tn", f32, f"mm_gfi{l}")
        dx0p, dmixed, dm_b = _norm_bwd(cfg, f"norm_b_b{l}", sv["x1"], sv["mixed"], dh2, dres, mod[l], 2, mod[l], 3)
        dmod_rows[l][3], dmod_rows[l][4], dmod_rows[l][2] = dm_b[:, :, 0], dm_b[:, :, 1], dm_b[:, :, 2]
        dmerged = _mm(dmixed, wmix, "nt", f32, f"mm_dmg{l}")
        g_wmix = _mm(sv["merged"], dmixed, "tn", f32, f"mm_gmix{l}")
        dya, dyc, dys, dg0, dg1, dg2 = _gate_bwd(cfg, f"gate_b{l}", sv["p"], sv["ya"], sv["yc"], sv["ys"], dmerged)
        dattn = _mm(dya, wao_t, "nn", f32, f"mm_dat{l}")
        g_wao = _mm(dya, sv["attn"], "tn", f32, f"mm_gao{l}")
        dhs = _mm(dyc, wco_t, "nn", f32, f"mm_dhs{l}")
        g_wco = _mm(dyc, sv["hs"], "tn", f32, f"mm_gco{l}")
        dz = _mm(dys, wso_t, "nn", f32, f"mm_dz{l}")
        g_wso = _mm(dys, sv["z"], "tn", f32, f"mm_gso{l}")
        dhc, g_ln[l] = _ln_silu_bwd(cfg, f"lnsilu_b{l}", sv["hc"], dhs, sv["lnp"])
        da, dg, dbg, dcg, dxs, g_w31[l], g_w3[l] = _conv_bwd(cfg, f"conv_b{l}", sv["p"], dhc, dz, w31b[l], w3p[l])
        dq, dkr, dvr = _attn_bwd(cfg, f"attn_b{l}", sv["q"], sv["krep"], sv["vrep"], sv["attn"], sv["lse"], dattn)
        dpq, dgain = _qkv_bwd(cfg, f"qkv_b{l}", sv["p"], dq, dkr, dvr, cs, sv["gain"], bd, rept)
        g_qn[l] = jnp.sum(dgain[0, :AW].reshape(reps_q, HEAD_DIM), axis=0)
        g_kn[l] = jnp.sum(dgain[0, AW:].reshape(reps_k, HEAD_DIM), axis=0)
        dp = jnp.concatenate([dpq, da, dg, dbg, dcg, dxs, dg0, dg1, dg2], axis=1)
        dh = _mm(dp, win_t, "nn", f32, f"mm_dh{l}")
        g_win = _mm(dp, sv["h"], "tn", f32, f"mm_gin{l}")
        if l == 0:
            dres, _, dm_a = _norm_bwd(cfg, f"norm_a_b{l}", sv["x0"], None, dh, dx0p, None, 0, mod[l], 0)
        else:
            dres, do, dm_a = _norm_bwd(cfg, f"norm_a_b{l}", sv["x0"], sv["oprev"], dh, dx0p, mod[l - 1], 5, mod[l], 0)
            dmod_rows[l - 1][5] = dm_a[:, :, 2]
        dmod_rows[l][0], dmod_rows[l][1] = dm_a[:, :, 0], dm_a[:, :, 1]

        parts = [g_win, g_wao, g_wco, g_wso, g_wmix, g_wfi, g_wfo]
        recv = _all_to_all(f"scatter_g{l}", [g.reshape(NDEV, g.shape[0] // NDEV, g.shape[1]) for g in parts])
        gbig[l] = [_sum_slots(f"sum_g{l}_{i}", r) for i, r in enumerate(recv)]

    grad_x = dres.reshape(b, cfg.T, d)[:, cfg.LC:, :]

    rb = -(-(b + 1) // SUBLANE) * SUBLANE
    dmod_l = jnp.stack([jnp.stack(rows, axis=2) for rows in dmod_rows])
    d_lat = dmod_l[:, :, 1].reshape(nl, b, NDEV, na)
    d_ctx = jnp.sum(dmod_l[:, :, 0], axis=1).reshape(nl, 1, NDEV, na)
    send = jnp.concatenate([d_lat, d_ctx, jnp.zeros((nl, rb - b - 1, NDEV, na), f32)], axis=1)
    send = jnp.transpose(send, (2, 0, 1, 3)).reshape(NDEV, nl * rb, na)
    (recv,) = _all_to_all("scatter_dmod", [send])
    recv = recv.reshape(NDEV, nl, rb, na)
    dmod_ex = jnp.transpose(recv[:, :, :b], (1, 0, 2, 3)).reshape(nl, nb, na)
    dmod_ex = jnp.pad(dmod_ex, ((0, 0), (0, nbp - nb), (0, 0)))
    dmodc = jnp.transpose(recv[:, :, b], (1, 0, 2))
    g_wada, g_bloc, g_cctx = _ada_bwd("ada_bwd", call, w_ada, dmod_ex, dmodc, nb)

    g_bada = lax.dynamic_update_slice(jnp.zeros((nl, N_MOD * d), f32), g_bloc[:, 0], (0, me * na))
    g_w31s = jnp.stack(g_w31)
    small_parts = [g_cctx[0], g_bada, jnp.stack(g_qn), jnp.stack(g_kn), g_w31s[:, :cfg.K31], g_w31s[:, cfg.K31],
                   jnp.stack([g[0] for g in g_ln]), jnp.stack([g[1] for g in g_ln]),
                   jnp.stack(g_w3)[:, :cfg.K3]]
    small_part_shapes = [p_.shape for p_ in small_parts]
    (gs,) = _all_gather("gather_gsmall", [_pack_flat(small_parts)])
    gsum = _sum_slots("sum_gsmall", gs)
    (gr_cctx, gr_bada, gr_qn, gr_kn, gr_w31, gr_b31, gr_lng, gr_lnb, gr_w3) = _unpack_flat(gsum, small_part_shapes)
    gr_w31 = lax.dynamic_slice_in_dim(gr_w31, me * cw8, cw8, axis=2)
    gr_w3 = lax.dynamic_slice_in_dim(gr_w3, me * cw8, cw8, axis=2)

    def big(i, transposed):
        g = jnp.stack([gbig[l][i] for l in range(nl)])
        return jnp.transpose(g, (0, 2, 1)) if transposed else g

    grads = {
        "c_ctx": gr_cctx, "w_ada": g_wada, "b_ada": gr_bada, "w_in": big(0, True), "q_norm": gr_qn, "k_norm": gr_kn,
        "w_attn_o": big(1, True), "conf_dw_w": gr_w31, "conf_dw_b": gr_b31, "conf_ln_g": gr_lng, "conf_ln_b": gr_lnb,
        "w_conf_out": big(2, True), "sc_dw_w": gr_w3, "w_sc_out": big(3, True), "w_mix_out": big(4, False),
        "w_ffn_in": big(5, True), "w_ffn_out": big(6, False)}
    weights = dict(c_ctx=c_ctx, w_ada=w_ada, b_ada=b_ada, w_in=w_in, q_norm=q_norm, k_norm=k_norm, w_attn_o=w_attn_o,
                   conf_dw_w=conf_dw_w, conf_dw_b=conf_dw_b, conf_ln_g=conf_ln_g, conf_ln_b=conf_ln_b,
                   w_conf_out=w_conf_out, sc_dw_w=sc_dw_w, w_sc_out=w_sc_out, w_mix_out=w_mix_out,
                   w_ffn_in=w_ffn_in, w_ffn_out=w_ffn_out)
    m_in = dict(c_ctx=m_c_ctx, w_ada=m_w_ada, b_ada=m_b_ada, w_in=m_w_in, q_norm=m_q_norm, k_norm=m_k_norm,
                w_attn_o=m_w_attn_o, conf_dw_w=m_conf_dw_w, conf_dw_b=m_conf_dw_b, conf_ln_g=m_conf_ln_g,
                conf_ln_b=m_conf_ln_b, w_conf_out=m_w_conf_out, sc_dw_w=m_sc_dw_w, w_sc_out=m_w_sc_out,
                w_mix_out=m_w_mix_out, w_ffn_in=m_w_ffn_in, w_ffn_out=m_w_ffn_out)
    v_in = dict(c_ctx=v_c_ctx, w_ada=v_w_ada, b_ada=v_b_ada, w_in=v_w_in, q_norm=v_q_norm, k_norm=v_k_norm,
                w_attn_o=v_w_attn_o, conf_dw_w=v_conf_dw_w, conf_dw_b=v_conf_dw_b, conf_ln_g=v_conf_ln_g,
                conf_ln_b=v_conf_ln_b, w_conf_out=v_w_conf_out, sc_dw_w=v_sc_dw_w, w_sc_out=v_w_sc_out,
                w_mix_out=v_w_mix_out, w_ffn_in=v_w_ffn_in, w_ffn_out=v_w_ffn_out)
    names = list(weights)
    big_names = ("w_ada", "w_in", "w_attn_o", "w_conf_out", "w_sc_out", "w_mix_out", "w_ffn_in", "w_ffn_out")
    small_names = [n for n in names if n not in big_names]
    delta, new_m, new_v = {}, {}, {}
    for n in big_names:
        shp = weights[n].shape
        two_d = (shp[0] * shp[1], shp[2])
        dl, nm, nv = _adamw(f"adamw_{n}", weights[n].reshape(two_d), grads[n].reshape(two_d),
                            m_in[n].reshape(two_d), v_in[n].reshape(two_d))
        delta[n], new_m[n], new_v[n] = dl.reshape(shp), nm.reshape(shp), nv.reshape(shp)
    sshapes = [weights[n].shape for n in small_names]
    dl, nm, nv = _adamw("adamw_small", _pack_flat([weights[n] for n in small_names]),
                        _pack_flat([grads[n] for n in small_names]), _pack_flat([m_in[n] for n in small_names]),
                        _pack_flat([v_in[n] for n in small_names]))
    for n, a_, b_, c_ in zip(small_names, _unpack_flat(dl, sshapes), _unpack_flat(nm, sshapes), _unpack_flat(nv, sshapes)):
        delta[n], new_m[n], new_v[n] = a_, b_, c_

    return (loss, grad_x, *[grads[n] for n in names], *[delta[n] for n in names],
            *[new_m[n] for n in names], *[new_v[n] for n in names])
```

```python
import functools
import math

import numpy as np
import jax
import jax.numpy as jnp
from jax import lax
from jax.experimental import pallas as pl
from jax.experimental.pallas import tpu as pltpu

f32 = jnp.float32
bf16 = jnp.bfloat16

NDEV = 8
AXES = ("x", "y", "c")
HEAD_DIM = 64
N_Q_HEADS = 8
N_KV_HEADS = 2
Q_GROUP = N_Q_HEADS // N_KV_HEADS
AW = N_Q_HEADS * HEAD_DIM
KVW = N_KV_HEADS * HEAD_DIM
GW = Q_GROUP * HEAD_DIM
QKW = AW + KVW
QKVW = AW + 2 * KVW
GRID_W = 64
AXIS_DIM = HEAD_DIM // 2
ROPE_THETA = 10000.0
ATTN_SCALE = HEAD_DIM ** -0.5
EPS = 1e-6
N_MOD = 6
MOD_ROWS = 8
CONV_PAD_ROWS = 16
SC_PAD_ROWS = 8
LANE = 128
SUBLANE = 8
VMEM_LIMIT = 56 * 1024 * 1024

ADAM_LR = 0.001
ADAM_B1 = 0.9
ADAM_B2 = 0.999
ADAM_EPS = 1e-08
ADAM_WD = 0.01
ADAM_STEP = 10

HI = lax.Precision.HIGHEST
NN = (((1,), (0,)), ((), ()))
NT_ = (((1,), (1,)), ((), ()))
TN = (((0,), (0,)), ((), ()))


def _params(ndims, vmem=None):
    return pltpu.CompilerParams(dimension_semantics=("arbitrary",) * ndims, vmem_limit_bytes=vmem)


def _divisor(n, cap, mult):
    best = None
    for d in range(mult, min(n, cap) + 1, mult):
        if n % d == 0:
            best = d
    return n if best is None else best


def _const_spec(a):
    nd = a.ndim
    return pl.BlockSpec(a.shape, lambda *idx: (0,) * nd)


class _Cfg:
    pass


def _mm(a, b, mode, out_dtype, name):
    if mode == "tn":
        m, ka = a.shape
        nb = b.shape[1]
        tka = _divisor(ka, 640, LANE)
        tnb = _divisor(nb, 1024, LANE)
        tmr = _divisor(m, 1024, SUBLANE)
        nsteps = m // tmr

        def body(a_ref, b_ref, o_ref, acc_ref):
            k = pl.program_id(2)

            @pl.when(k == 0)
            def _():
                acc_ref[...] = jnp.zeros_like(acc_ref)

            acc_ref[...] += lax.dot_general(a_ref[...].astype(bf16), b_ref[...].astype(bf16), TN,
                                            preferred_element_type=f32)

            @pl.when(k == nsteps - 1)
            def _():
                o_ref[...] = acc_ref[...].astype(out_dtype)

        return pl.pallas_call(
            body, out_shape=jax.ShapeDtypeStruct((ka, nb), out_dtype),
            grid=(ka // tka, nb // tnb, nsteps),
            in_specs=[pl.BlockSpec((tmr, tka), lambda i, j, k: (k, i)),
                      pl.BlockSpec((tmr, tnb), lambda i, j, k: (k, j))],
            out_specs=pl.BlockSpec((tka, tnb), lambda i, j, k: (i, j)),
            scratch_shapes=[pltpu.VMEM((tka, tnb), f32)],
            compiler_params=_params(3, VMEM_LIMIT), name=name)(a, b)

    m, kdim = a.shape
    n = b.shape[1] if mode == "nn" else b.shape[0]
    tm = _divisor(m, 1024 if a.dtype == bf16 else 512, SUBLANE)
    tn = _divisor(n, 1408, LANE)
    tk = _divisor(kdim, 1408, LANE)
    nsteps = kdim // tk
    dims = NN if mode == "nn" else NT_

    def body(a_ref, b_ref, o_ref, acc_ref):
        k = pl.program_id(2)

        @pl.when(k == 0)
        def _():
            acc_ref[...] = jnp.zeros_like(acc_ref)

        acc_ref[...] += lax.dot_general(a_ref[...].astype(bf16), b_ref[...].astype(bf16), dims,
                                        preferred_element_type=f32)

        @pl.when(k == nsteps - 1)
        def _():
            o_ref[...] = acc_ref[...].astype(out_dtype)

    if mode == "nn":
        b_spec = pl.BlockSpec((tk, tn), lambda i, j, k: (k, j))
    else:
        b_spec = pl.BlockSpec((tn, tk), lambda i, j, k: (j, k))
    return pl.pallas_call(
        body, out_shape=jax.ShapeDtypeStruct((m, n), out_dtype),
        grid=(m // tm, n // tn, nsteps),
        in_specs=[pl.BlockSpec((tm, tk), lambda i, j, k: (i, k)), b_spec],
        out_specs=pl.BlockSpec((tm, tn), lambda i, j, k: (i, j)),
        scratch_shapes=[pltpu.VMEM((tm, tn), f32)],
        compiler_params=_params(3, VMEM_LIMIT), name=name)(a, b)


def _row_spec(cfg, width, col=0):
    return pl.BlockSpec((cfg.TR, width), lambda i: (i, col))


def _mod_spec(cfg):
    nt, nc = cfg.NT, cfg.NC
    return pl.BlockSpec((1, 1, MOD_ROWS, cfg.D),
                        lambda i: (i // nt, ((i % nt) >= nc).astype(jnp.int32), 0, 0))


def _segment_start(cfg, i):
    j = i % cfg.NT
    return jnp.logical_or(j == 0, j == cfg.NC)


def _norm_fwd(cfg, name, xin, o, modg, gk, modn, sk):
    has_o = o is not None
    r_, d = xin.shape

    def body(*refs):
        if has_o:
            x_ref, o_ref, mg_ref, mn_ref, xn_ref, h_ref = refs
            x = x_ref[...] + mg_ref[0, 0, gk:gk + 1, :] * o_ref[...]
            xn_ref[...] = x
        else:
            x_ref, mn_ref, h_ref = refs
            x = x_ref[...]
        r = lax.rsqrt(jnp.mean(x * x, axis=-1, keepdims=True) + EPS)
        h = (x * r) * (1.0 + mn_ref[0, 0, sk + 1:sk + 2, :]) + mn_ref[0, 0, sk:sk + 1, :]
        h_ref[...] = h.astype(bf16)

    row = _row_spec(cfg, d)
    if has_o:
        ins, in_specs = (xin, o, modg, modn), [row, row, _mod_spec(cfg), _mod_spec(cfg)]
        out_shape = (jax.ShapeDtypeStruct((r_, d), f32), jax.ShapeDtypeStruct((r_, d), bf16))
        out_specs = (row, row)
    else:
        ins, in_specs = (xin, modn), [row, _mod_spec(cfg)]
        out_shape = jax.ShapeDtypeStruct((r_, d), bf16)
        out_specs = row
    res = pl.pallas_call(body, out_shape=out_shape, grid=(r_ // cfg.TR,), in_specs=in_specs,
                         out_specs=out_specs, compiler_params=_params(1), name=name)(*ins)
    return res if has_o else (xin, res)


def _norm_bwd(cfg, name, xnew, o, dh, dres, modg, gk, modn, sk):
    has_o = o is not None
    r_, d = xnew.shape

    def body(*refs):
        if has_o:
            xn_ref, o_ref, dh_ref, dr_ref, mg_ref, mn_ref, dx_ref, do_ref, dm_ref = refs
        else:
            xn_ref, dh_ref, dr_ref, mn_ref, dx_ref, dm_ref = refs
        i = pl.program_id(0)
        x = xn_ref[...]
        dhv = dh_ref[...].astype(f32)
        r = lax.rsqrt(jnp.mean(x * x, axis=-1, keepdims=True) + EPS)
        xh = x * r
        dxh = dhv * (1.0 + mn_ref[0, 0, sk + 1:sk + 2, :])
        dx = r * (dxh - xh * jnp.mean(dxh * xh, axis=-1, keepdims=True)) + dr_ref[...]
        dx_ref[...] = dx

        @pl.when(_segment_start(cfg, i))
        def _():
            dm_ref[...] = jnp.zeros_like(dm_ref)

        dm_ref[0, 0, 0:1, :] += jnp.sum(dhv, axis=0, keepdims=True)
        dm_ref[0, 0, 1:2, :] += jnp.sum(dhv * xh, axis=0, keepdims=True)
        if has_o:
            ov = o_ref[...]
            dm_ref[0, 0, 2:3, :] += jnp.sum(dx * ov, axis=0, keepdims=True)
            do_ref[...] = (dx * mg_ref[0, 0, gk:gk + 1, :]).astype(bf16)

    row = _row_spec(cfg, d)
    dm_shape = jax.ShapeDtypeStruct((cfg.B, 2, MOD_ROWS, d), f32)
    if has_o:
        ins = (xnew, o, dh, dres, modg, modn)
        in_specs = [row, row, row, row, _mod_spec(cfg), _mod_spec(cfg)]
        out_shape = (jax.ShapeDtypeStruct((r_, d), f32), jax.ShapeDtypeStruct((r_, d), bf16), dm_shape)
        out_specs = (row, row, _mod_spec(cfg))
    else:
        ins = (xnew, dh, dres, modn)
        in_specs = [row, row, row, _mod_spec(cfg)]
        out_shape = (jax.ShapeDtypeStruct((r_, d), f32), dm_shape)
        out_specs = (row, _mod_spec(cfg))
    res = pl.pallas_call(body, out_shape=out_shape, grid=(r_ // cfg.TR,), in_specs=in_specs,
                         out_specs=out_specs, compiler_params=_params(1), name=name)(*ins)
    if has_o:
        return res
    return res[0], None, res[1]


def _loss_head(cfg, name, x1, o, modg, gk, target):
    r_, d = x1.shape
    nt, nc = cfg.NT, cfg.NC
    nlat = nt - nc

    def body(x_ref, o_ref, mg_ref, t_ref, ls_ref, dy_ref, do_ref, dm_ref):
        i = pl.program_id(0)
        lat = (i % nt) >= nc
        gate = mg_ref[0, 0, gk:gk + 1, :]
        ov = o_ref[...]
        err = jnp.where(lat, x_ref[...] + gate * ov - t_ref[...], 0.0)
        ls_ref[...] = jnp.zeros_like(ls_ref)
        ls_ref[0, 0:1, :] = jnp.sum(err * err, axis=0, keepdims=True)
        dy = err * (1.0 / d)
        dy_ref[...] = dy
        do_ref[...] = (dy * gate).astype(bf16)

        @pl.when(_segment_start(cfg, i))
        def _():
            dm_ref[...] = jnp.zeros_like(dm_ref)

        dm_ref[0, 0, 2:3, :] += jnp.sum(dy * ov, axis=0, keepdims=True)

    row = _row_spec(cfg, d)
    t_spec = pl.BlockSpec((cfg.TR, d), lambda i: ((i // nt) * nlat + jnp.maximum((i % nt) - nc, 0), 0))
    return pl.pallas_call(
        body,
        out_shape=(jax.ShapeDtypeStruct((r_ // cfg.TR, SUBLANE, d), f32), jax.ShapeDtypeStruct((r_, d), f32),
                   jax.ShapeDtypeStruct((r_, d), bf16), jax.ShapeDtypeStruct((cfg.B, 2, MOD_ROWS, d), f32)),
        grid=(r_ // cfg.TR,),
        in_specs=[row, row, _mod_spec(cfg), t_spec],
        out_specs=(pl.BlockSpec((1, SUBLANE, d), lambda i: (i, 0, 0)), row, row, _mod_spec(cfg)),
        compiler_params=_params(1), name=name)(x1, o, modg, target)


def _swap16(y):
    w = y.shape[1]
    lane = lax.broadcasted_iota(jnp.int32, (1, w), 1)
    lo = jnp.bitwise_and(lane, 31) < 16
    return jnp.where(lo, pltpu.roll(y, w - 16, 1), pltpu.roll(y, 16, 1))


def _head_sum(v, bd):
    return jnp.dot(v, bd, precision=HI, preferred_element_type=f32)


def _qkv_fwd(cfg, name, p, cs, gain, bd, rep):
    r_ = p.shape[0]
    nt = cfg.NT

    def body(p_ref, cs_ref, g_ref, bd_ref, rep_ref, q_ref, k_ref, v_ref):
        x = p_ref[:, 0:QKW]
        r = lax.rsqrt(_head_sum(x * x, bd_ref[...]) * (1.0 / HEAD_DIM) + EPS)
        y = (x * r) * g_ref[...]
        out = y * cs_ref[0] + _swap16(y) * cs_ref[1]
        q_ref[...] = out[:, 0:AW].astype(bf16)
        kr = out[:, AW:QKW].astype(bf16)
        k_ref[...] = jnp.dot(kr, rep_ref[...], preferred_element_type=f32).astype(bf16)
        vv = p_ref[:, QKW:QKVW].astype(bf16)
        v_ref[...] = jnp.dot(vv, rep_ref[...], preferred_element_type=f32).astype(bf16)

    row = _row_spec(cfg, AW)
    return pl.pallas_call(
        body, out_shape=(jax.ShapeDtypeStruct((r_, AW), bf16),) * 3, grid=(r_ // cfg.TR,),
        in_specs=[_row_spec(cfg, QKVW), pl.BlockSpec((2, cfg.TR, QKW), lambda i: (0, i % nt, 0)),
                  _const_spec(gain), _const_spec(bd), _const_spec(rep)],
        out_specs=(row, row, row), compiler_params=_params(1), name=name)(p, cs, gain, bd, rep)


def _qkv_bwd(cfg, name, p, dq, dkr, dvr, cs, gain, bd, rept):
    r_ = p.shape[0]
    nt = cfg.NT

    def body(p_ref, dq_ref, dk_ref, dv_ref, cs_ref, g_ref, bd_ref, rt_ref, dp_ref, dg_ref):
        i = pl.program_id(0)
        x = p_ref[:, 0:QKW]
        rt = rt_ref[...]
        dk = jnp.dot(dk_ref[...], rt, precision=HI, preferred_element_type=f32)
        g = jnp.concatenate([dq_ref[...], dk], axis=1)
        gy = g * cs_ref[0] + _swap16(g * cs_ref[1])
        bdv = bd_ref[...]
        r = lax.rsqrt(_head_sum(x * x, bdv) * (1.0 / HEAD_DIM) + EPS)
        xh = x * r
        dxh = gy * g_ref[...]
        dx = r * (dxh - xh * (_head_sum(dxh * xh, bdv) * (1.0 / HEAD_DIM)))
        dp_ref[:, 0:QKW] = dx.astype(bf16)
        dp_ref[:, QKW:QKVW] = jnp.dot(dv_ref[...], rt, precision=HI, preferred_element_type=f32).astype(bf16)

        @pl.when(i == 0)
        def _():
            dg_ref[...] = jnp.zeros_like(dg_ref)

        dg_ref[0:1, :] += jnp.sum(gy * xh, axis=0, keepdims=True)

    row = _row_spec(cfg, AW)
    return pl.pallas_call(
        body, out_shape=(jax.ShapeDtypeStruct((r_, QKVW), bf16), jax.ShapeDtypeStruct((SUBLANE, QKW), f32)),
        grid=(r_ // cfg.TR,),
        in_specs=[_row_spec(cfg, QKVW), row, row, row,
                  pl.BlockSpec((2, cfg.TR, QKW), lambda i: (0, i % nt, 0)),
                  _const_spec(gain), _const_spec(bd), _const_spec(rept)],
        out_specs=(_row_spec(cfg, QKVW), pl.BlockSpec((SUBLANE, QKW), lambda i: (0, 0))),
        compiler_params=_params(1), name=name)(p, dq, dkr, dvr, cs, gain, bd, rept)


def _head_masks():
    lane = lax.broadcasted_iota(jnp.int32, (1, GW), 1)
    return [jnp.logical_and(lane >= HEAD_DIM * h, lane < HEAD_DIM * (h + 1)) for h in range(Q_GROUP)]


def _attn_fwd(cfg, name, q, krep, vrep, comm=None):
    r_ = q.shape[0]
    tr, t, lc, nt, nc = cfg.TR, cfg.T, cfg.LC, cfg.NT, cfg.NC

    def body(q_ref, k_ref, v_ref, o_ref, l_ref):
        j = pl.program_id(2)
        masks = _head_masks()
        lane = lax.broadcasted_iota(jnp.int32, (1, LANE), 1)

        def run(nk):
            qv = q_ref[...]
            kv = k_ref[0:nk, :]
            vv = v_ref[0:nk, :]
            acc = jnp.zeros((tr, GW), f32)
            lse = jnp.zeros((tr, LANE), f32)
            for h in range(Q_GROUP):
                qh = jnp.where(masks[h], qv, jnp.zeros_like(qv))
                s = lax.dot_general(qh, kv, NT_, preferred_element_type=f32) * ATTN_SCALE
                m = jnp.max(s, axis=1, keepdims=True)
                pr = jnp.exp(s - m)
                l = jnp.sum(pr, axis=1, keepdims=True)
                oh = jnp.dot(pr.astype(bf16), vv, preferred_element_type=f32) / l
                acc = acc + jnp.where(masks[h], oh, 0.0)
                lse = lse + jnp.where(lane == h, m + jnp.log(l), 0.0)
            o_ref[...] = acc
            l_ref[...] = lse

        @pl.when(j < nc)
        def _():
            run(lc)

        @pl.when(j >= nc)
        def _():
            run(t)

    qspec = pl.BlockSpec((tr, GW), lambda e, g, j: (e * nt + j, g))
    kspec = pl.BlockSpec((t, GW), lambda e, g, j: (e, g))
    return _hosting_call(
        body, comm, out_shape=(jax.ShapeDtypeStruct((r_, AW), f32), jax.ShapeDtypeStruct((r_, N_KV_HEADS * LANE), f32)),
        grid=(cfg.B, N_KV_HEADS, nt), in_specs=[qspec, kspec, kspec],
        out_specs=(qspec, pl.BlockSpec((tr, LANE), lambda e, g, j: (e * nt + j, g))),
        compiler_params=_params(3, VMEM_LIMIT), name=name, args=(q, krep, vrep))


def _hosting_call(body, comm, *, out_shape, grid, in_specs, out_specs, compiler_params, name, args):
    if comm is None:
        return pl.pallas_call(body, out_shape=out_shape, grid=grid, in_specs=in_specs, out_specs=out_specs,
                              compiler_params=compiler_params, name=name)(*args), None
    xs, scatter = comm
    n, n_in, n_out = len(xs), len(args), len(out_shape)
    last = [g - 1 for g in grid]

    def hosted(*refs):
        ins, cins = refs[:n_in], refs[n_in:n_in + n]
        outs, couts = refs[n_in + n:n_in + n + n_out], refs[n_in + n + n_out:n_in + 2 * n + n_out]
        sems = refs[n_in + 2 * n + n_out:]
        ids = [pl.program_id(a) for a in range(len(grid))]
        is_first = functools.reduce(jnp.logical_and, [i == 0 for i in ids])
        is_last = functools.reduce(jnp.logical_and, [i == l for i, l in zip(ids, last)])

        @pl.when(is_first)
        def _():
            _exchange_start(cins, couts, sems, scatter)

        body(*ins, *outs)

        @pl.when(is_last)
        def _():
            _exchange_wait(cins, couts, sems, scatter)

    anyspec = pl.BlockSpec(memory_space=pl.ANY)
    res = pl.pallas_call(
        hosted, out_shape=tuple(out_shape) + tuple(_exchange_out_shapes(xs, scatter)), grid=grid,
        in_specs=list(in_specs) + [anyspec] * n, out_specs=tuple(out_specs) + (anyspec,) * n,
        scratch_shapes=_exchange_scratch(n), compiler_params=compiler_params, name=name)(*args, *xs)
    return tuple(res[:n_out]), list(res[n_out:])


def _attn_bwd(cfg, name, q, krep, vrep, o, lse, do, comm=None):
    r_ = q.shape[0]
    tr, t, lc, nt, nc = cfg.TR, cfg.T, cfg.LC, cfg.NT, cfg.NC

    def body(q_ref, k_ref, v_ref, o_ref, l_ref, do_ref, dq_ref, dk_ref, dv_ref):
        j = pl.program_id(2)
        masks = _head_masks()
        lane = lax.broadcasted_iota(jnp.int32, (1, LANE), 1)

        @pl.when(j == 0)
        def _():
            dk_ref[...] = jnp.zeros_like(dk_ref)
            dv_ref[...] = jnp.zeros_like(dv_ref)

        def run(nk):
            qv = q_ref[...]
            kv = k_ref[0:nk, :]
            vv = v_ref[0:nk, :]
            ov = o_ref[...]
            dov = do_ref[...]
            lv = l_ref[...]
            dq = jnp.zeros((tr, GW), f32)
            dk = jnp.zeros((nk, GW), f32)
            dv = jnp.zeros((nk, GW), f32)
            for h in range(Q_GROUP):
                qh = jnp.where(masks[h], qv, jnp.zeros_like(qv))
                doh = jnp.where(masks[h], dov, 0.0)
                dohb = doh.astype(bf16)
                delta = jnp.sum(doh * ov, axis=1, keepdims=True)
                lse_h = jnp.sum(jnp.where(lane == h, lv, 0.0), axis=1, keepdims=True)
                s = lax.dot_general(qh, kv, NT_, preferred_element_type=f32) * ATTN_SCALE
                pr = jnp.exp(s - lse_h)
                dpr = lax.dot_general(dohb, vv, NT_, preferred_element_type=f32)
                ds = (pr * (dpr - delta) * ATTN_SCALE).astype(bf16)
                dq = dq + jnp.where(masks[h], jnp.dot(ds, kv, preferred_element_type=f32), 0.0)
                dk = dk + lax.dot_general(ds, qh, TN, preferred_element_type=f32)
                dv = dv + lax.dot_general(pr.astype(bf16), dohb, TN, preferred_element_type=f32)
            dq_ref[...] = dq
            dk_ref[0:nk, :] += dk
            dv_ref[0:nk, :] += dv

        @pl.when(j < nc)
        def _():
            run(lc)

        @pl.when(j >= nc)
        def _():
            run(t)

    qspec = pl.BlockSpec((tr, GW), lambda e, g, j: (e * nt + j, g))
    kspec = pl.BlockSpec((t, GW), lambda e, g, j: (e, g))
    lspec = pl.BlockSpec((tr, LANE), lambda e, g, j: (e * nt + j, g))
    return _hosting_call(
        body, comm, out_shape=(jax.ShapeDtypeStruct((r_, AW), f32),) * 3,
        grid=(cfg.B, N_KV_HEADS, nt), in_specs=[qspec, kspec, kspec, qspec, lspec, qspec],
        out_specs=(qspec, kspec, kspec),
        compiler_params=_params(3, VMEM_LIMIT), name=name, args=(q, krep, vrep, o, lse, do))


def _padded(x, pad_rows):
    z = jnp.zeros((pad_rows, x.shape[1]), f32)
    return jnp.concatenate([z, x, z], axis=0)


def _shifted(buf, shift):
    n = buf.shape[0]
    shift = shift % n
    return buf if shift == 0 else pltpu.roll(buf, shift, 0)


def _dwconv(x, w_ref, taps, pad_rows):
    n = x.shape[0]
    buf = _padded(x, pad_rows)
    acc = jnp.zeros_like(buf)
    for k in range(taps):
        acc = acc + w_ref[k:k + 1, :] * _shifted(buf, taps // 2 - k)
    return acc[pad_rows:pad_rows + n]


def _dwconv_t(dy, w_ref, taps, pad_rows):
    n = dy.shape[0]
    buf = _padded(dy, pad_rows)
    acc = jnp.zeros_like(buf)
    for k in range(taps):
        acc = acc + w_ref[k:k + 1, :] * _shifted(buf, k - taps // 2)
    return acc[pad_rows:pad_rows + n]


def _dwconv_dw(x, dy, dw_ref, taps, pad_rows):
    n = x.shape[0]
    buf = _padded(x, pad_rows)
    for k in range(taps):
        xs = _shifted(buf, taps // 2 - k)[pad_rows:pad_rows + n]
        dw_ref[k:k + 1, :] += jnp.sum(xs * dy, axis=0, keepdims=True)


def _conv_specs(cfg):
    t, cb, cw = cfg.T, cfg.CB, cfg.CW
    offs = (cfg.OFF_CONF, cfg.OFF_CONF + cw, cfg.OFF_SC, cfg.OFF_SC + cw, cfg.OFF_SC + 2 * cw)
    return [pl.BlockSpec((t, cb), lambda c, e, off=off: (e, off // cb + c)) for off in offs]


def _conv_fwd(cfg, name, p, w31, w3):
    r_ = p.shape[0]
    t, lc, cb, cw = cfg.T, cfg.LC, cfg.CB, cfg.CW
    k31, k3 = cfg.K31, cfg.K3

    def body(a_ref, g_ref, bg_ref, cg_ref, xs_ref, w31_ref, w3_ref, hc_ref, z_ref):
        for lo, hi in ((0, lc), (lc, t)):
            hh = a_ref[lo:hi, :] * jax.nn.sigmoid(g_ref[lo:hi, :])
            hc_ref[lo:hi, :] = _dwconv(hh, w31_ref, k31, CONV_PAD_ROWS) + w31_ref[k31:k31 + 1, :]
            u = cg_ref[lo:hi, :] * xs_ref[lo:hi, :]
            z_ref[lo:hi, :] = (bg_ref[lo:hi, :] * _dwconv(u, w3_ref, k3, SC_PAD_ROWS)).astype(bf16)

    ospec = pl.BlockSpec((t, cb), lambda c, e: (e, c))
    return pl.pallas_call(
        body, out_shape=(jax.ShapeDtypeStruct((r_, cw), f32), jax.ShapeDtypeStruct((r_, cw), bf16)),
        grid=(cw // cb, cfg.B),
        in_specs=_conv_specs(cfg) + [pl.BlockSpec((w31.shape[0], cb), lambda c, e: (0, c)),
                                     pl.BlockSpec((w3.shape[0], cb), lambda c, e: (0, c))],
        out_specs=(ospec, ospec), compiler_params=_params(2, VMEM_LIMIT), name=name)(p, p, p, p, p, w31, w3)


def _conv_bwd(cfg, name, p, dhc, dz, w31, w3):
    r_ = p.shape[0]
    t, lc, cb, cw = cfg.T, cfg.LC, cfg.CB, cfg.CW
    k31, k3 = cfg.K31, cfg.K3

    def body(a_ref, g_ref, bg_ref, cg_ref, xs_ref, dhc_ref, dz_ref, w31_ref, w3_ref,
             da_ref, dg_ref, dbg_ref, dcg_ref, dxs_ref, dw31_ref, dw3_ref):
        e = pl.program_id(1)

        @pl.when(e == 0)
        def _():
            dw31_ref[...] = jnp.zeros_like(dw31_ref)
            dw3_ref[...] = jnp.zeros_like(dw3_ref)

        for lo, hi in ((0, lc), (lc, t)):
            a = a_ref[lo:hi, :]
            sg = jax.nn.sigmoid(g_ref[lo:hi, :])
            hh = a * sg
            dy = dhc_ref[lo:hi, :]
            dhh = _dwconv_t(dy, w31_ref, k31, CONV_PAD_ROWS)
            _dwconv_dw(hh, dy, dw31_ref, k31, CONV_PAD_ROWS)
            dw31_ref[k31:k31 + 1, :] += jnp.sum(dy, axis=0, keepdims=True)
            da_ref[lo:hi, :] = (dhh * sg).astype(bf16)
            dg_ref[lo:hi, :] = (dhh * hh * (1.0 - sg)).astype(bf16)

            cg = cg_ref[lo:hi, :]
            xs = xs_ref[lo:hi, :]
            u = cg * xs
            dzv = dz_ref[lo:hi, :]
            dbg_ref[lo:hi, :] = (dzv * _dwconv(u, w3_ref, k3, SC_PAD_ROWS)).astype(bf16)
            duc = dzv * bg_ref[lo:hi, :]
            du = _dwconv_t(duc, w3_ref, k3, SC_PAD_ROWS)
            _dwconv_dw(u, duc, dw3_ref, k3, SC_PAD_ROWS)
            dcg_ref[lo:hi, :] = (du * xs).astype(bf16)
            dxs_ref[lo:hi, :] = (du * cg).astype(bf16)

    ospec = pl.BlockSpec((t, cb), lambda c, e: (e, c))
    w31_spec = pl.BlockSpec((w31.shape[0], cb), lambda c, e: (0, c))
    w3_spec = pl.BlockSpec((w3.shape[0], cb), lambda c, e: (0, c))
    return pl.pallas_call(
        body,
        out_shape=(jax.ShapeDtypeStruct((r_, cw), bf16),) * 5
        + (jax.ShapeDtypeStruct(w31.shape, f32), jax.ShapeDtypeStruct(w3.shape, f32)),
        grid=(cw // cb, cfg.B),
        in_specs=_conv_specs(cfg) + [ospec, ospec, w31_spec, w3_spec],
        out_specs=(ospec,) * 5 + (w31_spec, w3_spec),
        compiler_params=_params(2, VMEM_LIMIT), name=name)(p, p, p, p, p, dhc, dz, w31, w3)


def _ln_silu_fwd(cfg, name, hc, lnp):
    r_, cw = hc.shape

    def body(x_ref, p_ref, o_ref):
        x = x_ref[...]
        mu = jnp.mean(x, axis=-1, keepdims=True)
        xc = x - mu
        rs = lax.rsqrt(jnp.mean(xc * xc, axis=-1, keepdims=True) + EPS)
        hn = (xc * rs) * p_ref[0:1, :] + p_ref[1:2, :]
        o_ref[...] = (hn * jax.nn.sigmoid(hn)).astype(bf16)

    row = _row_spec(cfg, cw)
    return pl.pallas_call(body, out_shape=jax.ShapeDtypeStruct((r_, cw), bf16), grid=(r_ // cfg.TR,),
                          in_specs=[row, _const_spec(lnp)], out_specs=row,
                          compiler_params=_params(1), name=name)(hc, lnp)


def _ln_silu_bwd(cfg, name, hc, dhs, lnp):
    r_, cw = hc.shape

    def body(x_ref, d_ref, p_ref, dx_ref, dp_ref):
        i = pl.program_id(0)
        x = x_ref[...]
        mu = jnp.mean(x, axis=-1, keepdims=True)
        xc = x - mu
        rs = lax.rsqrt(jnp.mean(xc * xc, axis=-1, keepdims=True) + EPS)
        xh = xc * rs
        gain = p_ref[0:1, :]
        hn = xh * gain + p_ref[1:2, :]
        sg = jax.nn.sigmoid(hn)
        dhn = d_ref[...] * (sg * (1.0 + hn * (1.0 - sg)))
        dxh = dhn * gain
        dx_ref[...] = rs * (dxh - jnp.mean(dxh, axis=-1, keepdims=True)
                            - xh * jnp.mean(dxh * xh, axis=-1, keepdims=True))

        @pl.when(i == 0)
        def _():
            dp_ref[...] = jnp.zeros_like(dp_ref)

        dp_ref[0:1, :] += jnp.sum(dhn * xh, axis=0, keepdims=True)
        dp_ref[1:2, :] += jnp.sum(dhn, axis=0, keepdims=True)

    row = _row_spec(cfg, cw)
    return pl.pallas_call(
        body, out_shape=(jax.ShapeDtypeStruct((r_, cw), f32), jax.ShapeDtypeStruct((SUBLANE, cw), f32)),
        grid=(r_ // cfg.TR,), in_specs=[row, row, _const_spec(lnp)],
        out_specs=(row, pl.BlockSpec((SUBLANE, cw), lambda i: (0, 0))),
        compiler_params=_params(1), name=name)(hc, dhs, lnp)


def _gate_fwd(cfg, name, p, ya, yc, ys):
    r_, d = ya.shape
    gb = cfg.GB
    base = cfg.OFF_GATE // gb
    per = d // gb

    def body(g0_ref, g1_ref, g2_ref, ya_ref, yc_ref, ys_ref, o_ref):
        m = (jax.nn.sigmoid(g0_ref[...]) * ya_ref[...] + jax.nn.sigmoid(g1_ref[...]) * yc_ref[...]
             + jax.nn.sigmoid(g2_ref[...]) * ys_ref[...])
        o_ref[...] = m.astype(bf16)

    yspec = pl.BlockSpec((cfg.TR, gb), lambda i, c: (i, c))
    gspecs = [pl.BlockSpec((cfg.TR, gb), lambda i, c, k=k: (i, base + k * per + c)) for k in range(3)]
    return pl.pallas_call(body, out_shape=jax.ShapeDtypeStruct((r_, d), bf16), grid=(r_ // cfg.TR, per),
                          in_specs=gspecs + [yspec] * 3, out_specs=yspec,
                          compiler_params=_params(2), name=name)(p, p, p, ya, yc, ys)


def _gate_bwd(cfg, name, p, ya, yc, ys, dm):
    r_, d = ya.shape
    gb = cfg.GB
    base = cfg.OFF_GATE // gb
    per = d // gb

    def body(g0_ref, g1_ref, g2_ref, ya_ref, yc_ref, ys_ref, dm_ref, da_ref, dc_ref, ds_ref, d0_ref, d1_ref, d2_ref):
        dmv = dm_ref[...]
        for g_ref, y_ref, dy_ref, dg_ref in ((g0_ref, ya_ref, da_ref, d0_ref), (g1_ref, yc_ref, dc_ref, d1_ref),
                                             (g2_ref, ys_ref, ds_ref, d2_ref)):
            sg = jax.nn.sigmoid(g_ref[...])
            dy_ref[...] = (dmv * sg).astype(bf16)
            dg_ref[...] = (dmv * y_ref[...] * sg * (1.0 - sg)).astype(bf16)

    yspec = pl.BlockSpec((cfg.TR, gb), lambda i, c: (i, c))
    gspecs = [pl.BlockSpec((cfg.TR, gb), lambda i, c, k=k: (i, base + k * per + c)) for k in range(3)]
    return pl.pallas_call(body, out_shape=(jax.ShapeDtypeStruct((r_, d), bf16),) * 6, grid=(r_ // cfg.TR, per),
                          in_specs=gspecs + [yspec] * 4, out_specs=(yspec,) * 6,
                          compiler_params=_params(2), name=name)(p, p, p, ya, yc, ys, dm)


def _swiglu_fwd(cfg, name, ff):
    r_, f2 = ff.shape
    fh = f2 // 2

    def body(a_ref, b_ref, o_ref):
        a = a_ref[...]
        o_ref[...] = (a * jax.nn.sigmoid(a) * b_ref[...]).astype(bf16)

    return pl.pallas_call(body, out_shape=jax.ShapeDtypeStruct((r_, fh), bf16), grid=(r_ // cfg.TR,),
                          in_specs=[_row_spec(cfg, fh, 0), _row_spec(cfg, fh, 1)], out_specs=_row_spec(cfg, fh),
                          compiler_params=_params(1, VMEM_LIMIT), name=name)(ff, ff)


def _swiglu_bwd(cfg, name, ff, ds):
    r_, f2 = ff.shape
    fh = f2 // 2

    def body(a_ref, b_ref, d_ref, o_ref):
        a = a_ref[...]
        sg = jax.nn.sigmoid(a)
        dsv = d_ref[...]
        o_ref[:, 0:fh] = (dsv * b_ref[...] * (sg * (1.0 + a * (1.0 - sg)))).astype(bf16)
        o_ref[:, fh:f2] = (dsv * a * sg).astype(bf16)

    return pl.pallas_call(body, out_shape=jax.ShapeDtypeStruct((r_, f2), bf16), grid=(r_ // cfg.TR,),
                          in_specs=[_row_spec(cfg, fh, 0), _row_spec(cfg, fh, 1), _row_spec(cfg, fh)],
                          out_specs=_row_spec(cfg, f2), compiler_params=_params(1, VMEM_LIMIT), name=name)(ff, ff, ds)


def _ada_fwd(name, call, w_ada, b_loc):
    nl, d, na = w_ada.shape
    nbp = call.shape[0]

    def body(c_ref, w_ref, b_ref, o_ref):
        cv = c_ref[...]
        a = (cv * jax.nn.sigmoid(cv)).astype(bf16)
        o_ref[0] = jnp.dot(a, w_ref[0].astype(bf16), preferred_element_type=f32) + b_ref[0]

    return pl.pallas_call(
        body, out_shape=jax.ShapeDtypeStruct((nl, nbp, na), f32), grid=(nl,),
        in_specs=[_const_spec(call), pl.BlockSpec((1, d, na), lambda l: (l, 0, 0)),
                  pl.BlockSpec((1, 1, na), lambda l: (l, 0, 0))],
        out_specs=pl.BlockSpec((1, nbp, na), lambda l: (l, 0, 0)),
        compiler_params=_params(1, VMEM_LIMIT), name=name)(call, w_ada, b_loc)


def _ada_bwd(name, call, w_ada, dmod, dmodc, cctx_row):
    nl, d, na = w_ada.shape
    nbp = call.shape[0]

    def body(c_ref, w_ref, dm_ref, dc_ref, gw_ref, gb_ref, gc_ref):
        l = pl.program_id(0)
        cv = c_ref[...]
        sg = jax.nn.sigmoid(cv)
        a = (cv * sg).astype(bf16)
        dctx = jnp.sum(dc_ref[0], axis=0, keepdims=True)
        rows = lax.broadcasted_iota(jnp.int32, (nbp, 1), 0)
        dm = jnp.where(rows == cctx_row, dctx, dm_ref[0])
        gw_ref[0] = lax.dot_general(a, dm.astype(bf16), TN, preferred_element_type=f32)
        gb_ref[0] = jnp.zeros((SUBLANE, na), f32)
        gb_ref[0, 0:1, :] = jnp.sum(dm, axis=0, keepdims=True)
        dc8 = jnp.broadcast_to(dctx, (SUBLANE, na)).astype(bf16)
        part = lax.dot_general(dc8, w_ref[0].astype(bf16), NT_, preferred_element_type=f32)
        cc = c_ref[cctx_row:cctx_row + 1, :]
        sc = jax.nn.sigmoid(cc)
        part = part * (sc * (1.0 + cc * (1.0 - sc)))

        @pl.when(l == 0)
        def _():
            gc_ref[...] = jnp.zeros_like(gc_ref)

        gc_ref[...] += part

    return pl.pallas_call(
        body,
        out_shape=(jax.ShapeDtypeStruct((nl, d, na), f32), jax.ShapeDtypeStruct((nl, SUBLANE, na), f32),
                   jax.ShapeDtypeStruct((SUBLANE, d), f32)),
        grid=(nl,),
        in_specs=[_const_spec(call), pl.BlockSpec((1, d, na), lambda l: (l, 0, 0)),
                  pl.BlockSpec((1, nbp, na), lambda l: (l, 0, 0)),
                  pl.BlockSpec((1, NDEV, na), lambda l: (l, 0, 0))],
        out_specs=(pl.BlockSpec((1, d, na), lambda l: (l, 0, 0)), pl.BlockSpec((1, SUBLANE, na), lambda l: (l, 0, 0)),
                   pl.BlockSpec((SUBLANE, d), lambda l: (0, 0))),
        compiler_params=_params(1, VMEM_LIMIT), name=name)(call, w_ada, dmod, dmodc)


def _adamw(name, w, g, m, v):
    rows, cols = w.shape
    tr = _divisor(rows, max(SUBLANE, (1 << 19) // cols), SUBLANE)
    c1 = 1.0 / (1.0 - ADAM_B1 ** ADAM_STEP)
    c2 = 1.0 / (1.0 - ADAM_B2 ** ADAM_STEP)

    def body(w_ref, g_ref, m_ref, v_ref, d_ref, nm_ref, nv_ref):
        gv = g_ref[...]
        nm = ADAM_B1 * m_ref[...] + (1.0 - ADAM_B1) * gv
        nv = ADAM_B2 * v_ref[...] + (1.0 - ADAM_B2) * (gv * gv)
        nm_ref[...] = nm
        nv_ref[...] = nv
        d_ref[...] = -ADAM_LR * ((nm * c1) / (jnp.sqrt(nv * c2) + ADAM_EPS) + ADAM_WD * w_ref[...])

    spec = pl.BlockSpec((tr, cols), lambda i: (i, 0))
    return pl.pallas_call(body, out_shape=(jax.ShapeDtypeStruct((rows, cols), f32),) * 3, grid=(rows // tr,),
                          in_specs=[spec] * 4, out_specs=(spec,) * 3,
                          compiler_params=_params(1, VMEM_LIMIT), name=name)(w, g, m, v)


def _sum_slots(name, x):
    nd, rows, cols = x.shape
    tr = _divisor(rows, max(SUBLANE, (1 << 18) // cols), SUBLANE)

    def body(x_ref, o_ref):
        acc = x_ref[0].astype(f32)
        for s in range(1, nd):
            acc = acc + x_ref[s].astype(f32)
        o_ref[...] = acc

    return pl.pallas_call(body, out_shape=jax.ShapeDtypeStruct((rows, cols), f32), grid=(rows // tr,),
                          in_specs=[pl.BlockSpec((nd, tr, cols), lambda i: (0, i, 0))],
                          out_specs=pl.BlockSpec((tr, cols), lambda i: (i, 0)),
                          compiler_params=_params(1, VMEM_LIMIT), name=name)(x)


def _peer(k):
    x, y, c = (lax.axis_index(a) for a in AXES)
    px = 1 - x if k & 4 else x
    py = 1 - y if k & 2 else y
    pc = 1 - c if k & 1 else c
    return (px, py, pc), 4 * px + 2 * py + pc


def _exchange_out_shapes(xs, scatter):
    return [jax.ShapeDtypeStruct(xa.shape if scatter else (NDEV,) + xa.shape, xa.dtype) for xa in xs]


def _exchange_scratch(n):
    nrel = NDEV - 1
    return [pltpu.SemaphoreType.DMA((n * nrel,)), pltpu.SemaphoreType.DMA((n * nrel,)), pltpu.SemaphoreType.DMA((n,))]


def _exchange_copies(ins, outs, sems, scatter, receiving):
    send_sems, recv_sems, _ = sems
    nrel = NDEV - 1
    x, y, c = (lax.axis_index(a) for a in AXES)
    me = 4 * x + 2 * y + c
    copies = []
    for a in range(len(ins)):
        for k in range(1, NDEV):
            peer, pidx = _peer(k)
            src = ins[a].at[pidx] if scatter else ins[a]
            copies.append(pltpu.make_async_remote_copy(
                src_ref=src, dst_ref=outs[a].at[pidx if receiving else me], send_sem=send_sems.at[a * nrel + k - 1],
                recv_sem=recv_sems.at[a * nrel + k - 1], device_id=peer, device_id_type=pl.DeviceIdType.MESH))
    return copies


def _exchange_local(ins, outs, sems, scatter):
    x, y, c = (lax.axis_index(a) for a in AXES)
    me = 4 * x + 2 * y + c
    return [pltpu.make_async_copy(ins[a].at[me] if scatter else ins[a], outs[a].at[me], sems[2].at[a])
            for a in range(len(ins))]


def _exchange_start(ins, outs, sems, scatter):
    for cp in _exchange_local(ins, outs, sems, scatter):
        cp.start()
    for cp in _exchange_copies(ins, outs, sems, scatter, False):
        cp.start()


def _exchange_wait(ins, outs, sems, scatter):
    for cp in _exchange_copies(ins, outs, sems, scatter, True):
        cp.wait_recv()
        cp.wait_send()
    for cp in _exchange_local(ins, outs, sems, scatter):
        cp.wait()


def _exchange(name, xs, scatter):
    n = len(xs)

    def body(*refs):
        ins, outs, sems = refs[:n], refs[n:2 * n], refs[2 * n:]
        _exchange_start(ins, outs, sems, scatter)
        _exchange_wait(ins, outs, sems, scatter)

    anyspec = pl.BlockSpec(memory_space=pl.ANY)
    outs = pl.pallas_call(
        body, out_shape=tuple(_exchange_out_shapes(xs, scatter)), in_specs=[anyspec] * n, out_specs=(anyspec,) * n,
        scratch_shapes=_exchange_scratch(n), name=name)(*xs)
    return list(outs)


def _all_gather(name, xs):
    return _exchange(name, xs, False)


def _all_to_all(name, xs):
    return _exchange(name, xs, True)


def _rope_table(cfg):
    s, lc = cfg.S, cfg.LC
    rows = s // GRID_W
    r_ids, c_ids = jnp.meshgrid(jnp.arange(rows), jnp.arange(GRID_W), indexing="ij")
    r_ids = r_ids.reshape(-1).astype(f32)
    c_ids = c_ids.reshape(-1).astype(f32)
    freqs = ROPE_THETA ** (-jnp.arange(0, AXIS_DIM, 2, dtype=f32) / AXIS_DIM)
    ang_r = r_ids[:, None] * freqs
    ang_c = c_ids[:, None] * freqs
    cos = jnp.concatenate([jnp.cos(ang_r), jnp.cos(ang_r), jnp.cos(ang_c), jnp.cos(ang_c)], axis=1)
    sin = jnp.concatenate([-jnp.sin(ang_r), jnp.sin(ang_r), -jnp.sin(ang_c), jnp.sin(ang_c)], axis=1)
    cos = jnp.concatenate([jnp.ones((lc, HEAD_DIM), f32), cos], axis=0)
    sin = jnp.concatenate([jnp.zeros((lc, HEAD_DIM), f32), sin], axis=0)
    reps = QKW // HEAD_DIM
    return jnp.stack([jnp.tile(cos, (1, reps)), jnp.tile(sin, (1, reps))])


def _block_diag_ones():
    idx = np.arange(QKW) // HEAD_DIM
    return jnp.asarray((idx[:, None] == idx[None, :]).astype(np.float32))


def _replicate_matrix():
    src = np.arange(KVW)
    dst = np.arange(AW)
    m = (src[:, None] // HEAD_DIM == dst[None, :] // GW) & (src[:, None] % HEAD_DIM == dst[None, :] % HEAD_DIM)
    return m.astype(np.float32)


def _pack_flat(parts):
    flat = jnp.concatenate([p.reshape(-1) for p in parts])
    n = flat.shape[0]
    unit = SUBLANE * LANE
    total = -(-n // unit) * unit
    flat = jnp.pad(flat, (0, total - n))
    return flat.reshape(total // LANE, LANE)


def _unpack_flat(flat, shapes):
    flat = flat.reshape(flat.shape[:-2] + (-1,))
    out, off = [], 0
    for shp in shapes:
        size = int(np.prod(shp))
        out.append(flat[..., off:off + size].reshape(flat.shape[:-1] + tuple(shp)))
        off += size
    return out


def kernel(x, c, ctx, c_ctx, w_ada, b_ada, w_in, q_norm, k_norm, w_attn_o, conf_dw_w, conf_dw_b, conf_ln_g, conf_ln_b, w_conf_out, sc_dw_w, w_sc_out, w_mix_out, w_ffn_in, w_ffn_out, loss_target, m_c_ctx, m_w_ada, m_b_ada, m_w_in, m_q_norm, m_k_norm, m_w_attn_o, m_conf_dw_w, m_conf_dw_b, m_conf_ln_g, m_conf_ln_b, m_w_conf_out, m_sc_dw_w, m_w_sc_out, m_w_mix_out, m_w_ffn_in, m_w_ffn_out, v_c_ctx, v_w_ada, v_b_ada, v_w_in, v_q_norm, v_k_norm, v_w_attn_o, v_conf_dw_w, v_conf_dw_b, v_conf_ln_g, v_conf_ln_b, v_w_conf_out, v_sc_dw_w, v_w_sc_out, v_w_mix_out, v_w_ffn_in, v_w_ffn_out):
    cfg = _Cfg()
    cfg.B, cfg.S, cfg.D = x.shape
    cfg.LC = ctx.shape[1]
    cfg.T = cfg.LC + cfg.S
    cfg.TR = min(256, cfg.LC)
    assert cfg.LC % cfg.TR == 0 and cfg.S % cfg.TR == 0 and cfg.S % GRID_W == 0
    cfg.NT, cfg.NC = cfg.T // cfg.TR, cfg.LC // cfg.TR
    cfg.R = cfg.B * cfg.T
    nl = w_in.shape[0]
    b, d = cfg.B, cfg.D
    cfg.CW = conf_dw_b.shape[1]
    cfg.K31, cfg.K3 = conf_dw_w.shape[1], sc_dw_w.shape[1]
    assert w_sc_out.shape[1] == cfg.CW and cfg.CW % LANE == 0
    assert cfg.K31 // 2 <= CONV_PAD_ROWS and cfg.K3 // 2 <= SC_PAD_ROWS
    cfg.CB = LANE
    cfg.OFF_CONF = QKVW
    cfg.OFF_SC = cfg.OFF_CONF + 2 * cfg.CW
    cfg.OFF_GATE = cfg.OFF_SC + 3 * cfg.CW
    n_in = w_in.shape[2] * NDEV
    assert n_in == cfg.OFF_GATE + 3 * d and w_attn_o.shape[1] == AW
    cfg.GB = math.gcd(cfg.OFF_GATE, d)
    assert cfg.GB % LANE == 0
    fh = w_ffn_out.shape[1] * NDEV
    na = w_ada.shape[2]
    cw8 = cfg.CW // NDEV

    xi, yi, ci = (lax.axis_index(a) for a in AXES)
    me = 4 * xi + 2 * yi + ci

    small_shapes = [c.shape, conf_dw_w.shape, sc_dw_w.shape]
    (g0,) = _all_gather("gather_small", [_pack_flat([c, conf_dw_w, sc_dw_w])])
    c_all, cw_all, sw_all = _unpack_flat(g0, small_shapes)
    nb = NDEV * b
    nbp = -(-(nb + 1) // SUBLANE) * SUBLANE
    call = jnp.concatenate([c_all.reshape(nb, d), c_ctx[None, :], jnp.zeros((nbp - nb - 1, d), f32)], axis=0)
    w31_full = jnp.moveaxis(cw_all, 0, 2).reshape(nl, cfg.K31, cfg.CW)
    w3_full = jnp.moveaxis(sw_all, 0, 2).reshape(nl, cfg.K3, cfg.CW)
    k31p = -(-(cfg.K31 + 1) // SUBLANE) * SUBLANE
    w31b = jnp.concatenate([w31_full, conf_dw_b[:, None, :], jnp.zeros((nl, k31p - cfg.K31 - 1, cfg.CW), f32)], axis=1)
    w3p = jnp.concatenate([w3_full, jnp.zeros((nl, SUBLANE - cfg.K3, cfg.CW), f32)], axis=1)

    b_loc = lax.dynamic_slice(b_ada, (0, me * na), (nl, na))[:, None, :]
    mod_loc = _ada_fwd("ada_fwd", call, w_ada, b_loc)
    (mod_g,) = _all_gather("gather_mod", [mod_loc.reshape(nl * nbp, na)])
    mod_full = jnp.transpose(mod_g.reshape(NDEV, nl, nbp, na), (1, 2, 0, 3)).reshape(nl, nbp, N_MOD, d)
    mod_lat = lax.dynamic_slice_in_dim(mod_full, me * b, b, axis=1)
    mod_ctx = jnp.broadcast_to(mod_full[:, nb][:, None], (nl, b, N_MOD, d))
    mod = jnp.stack([mod_ctx, mod_lat], axis=2)
    mod = jnp.pad(mod, ((0, 0), (0, 0), (0, 0), (0, MOD_ROWS - N_MOD), (0, 0)))

    cs = _rope_table(cfg)
    bd = _block_diag_ones()
    rep_np = _replicate_matrix()
    rep = jnp.asarray(rep_np, dtype=bf16)
    rept = jnp.asarray(rep_np.T)
    reps_q, reps_k = AW // HEAD_DIM, KVW // HEAD_DIM

    xin = jnp.concatenate([ctx, x], axis=1).reshape(cfg.R, d)
    target = loss_target.reshape(b * cfg.S, d)
    saved = []
    xcur, ocur = xin, None

    def weight_shards(l):
        return [jnp.transpose(w_in[l]).astype(bf16), jnp.transpose(w_attn_o[l]).astype(bf16),
                jnp.transpose(w_conf_out[l]).astype(bf16), jnp.transpose(w_sc_out[l]).astype(bf16),
                w_mix_out[l].astype(bf16), jnp.transpose(w_ffn_in[l]).astype(bf16), w_ffn_out[l].astype(bf16)]

    wl = _all_gather("gather_w0", weight_shards(0))
    for l in range(nl):
        win_t, wao_t, wco_t, wso_t, wmix, wfi_t, wfo = [g.reshape(NDEV * g.shape[1], g.shape[2]) for g in wl]
        gain = jnp.concatenate([jnp.tile(q_norm[l], reps_q), jnp.tile(k_norm[l], reps_k)])[None, :]
        lnp = jnp.concatenate([conf_ln_g[l][None], conf_ln_b[l][None], jnp.zeros((SUBLANE - 2, cfg.CW), f32)], axis=0)

        if l == 0:
            x0, h = _norm_fwd(cfg, f"norm_a{l}", xcur, None, None, 0, mod[l], 0)
        else:
            x0, h = _norm_fwd(cfg, f"norm_a{l}", xcur, ocur, mod[l - 1], 5, mod[l], 0)
        p = _mm(h, win_t, "nt", f32, f"mm_in{l}")
        q, krep, vrep = _qkv_fwd(cfg, f"qkv{l}", p, cs, gain, bd, rep)
        nxt = (weight_shards(l + 1), False) if l + 1 < nl else None
        (attn, lse), wl = _attn_fwd(cfg, f"attn{l}", q, krep, vrep, comm=nxt)
        hc, z = _conv_fwd(cfg, f"conv{l}", p, w31b[l], w3p[l])
        hs = _ln_silu_fwd(cfg, f"lnsilu{l}", hc, lnp)
        ya = _mm(attn, wao_t, "nt", f32, f"mm_ao{l}")
        yc = _mm(hs, wco_t, "nt", f32, f"mm_co{l}")
        ys = _mm(z, wso_t, "nt", f32, f"mm_so{l}")
        merged = _gate_fwd(cfg, f"gate{l}", p, ya, yc, ys)
        mixed = _mm(merged, wmix, "nn", f32, f"mm_mix{l}")
        x1, h2 = _norm_fwd(cfg, f"norm_b{l}", x0, mixed, mod[l], 2, mod[l], 3)
        ff = _mm(h2, wfi_t, "nt", f32, f"mm_fi{l}")
        sw = _swiglu_fwd(cfg, f"swiglu{l}", ff)
        o = _mm(sw, wfo, "nn", f32, f"mm_fo{l}")
        saved.append(dict(x0=x0, h=h, p=p, q=q, krep=krep, vrep=vrep, attn=attn, lse=lse, hc=hc, z=z, hs=hs,
                          ya=ya, yc=yc, ys=ys, merged=merged, mixed=mixed, x1=x1, h2=h2, ff=ff, sw=sw, o=o,
                          oprev=ocur, gain=gain, lnp=lnp,
                          w=(win_t, wao_t, wco_t, wso_t, wmix, wfi_t, wfo)))
        xcur, ocur = x1, o

    lsum, dres, do, dm_loss = _loss_head(cfg, "loss", xcur, ocur, mod[nl - 1], 5, target)
    loss = lax.psum((0.5 / d) * jnp.sum(lsum), AXES)

    dmod_rows = [[None] * N_MOD for _ in range(nl)]
    dmod_rows[nl - 1][5] = dm_loss[:, :, 2]
    g_qn, g_kn, g_w31, g_w3, g_ln = [None] * nl, [None] * nl, [None] * nl, [None] * nl, [None] * nl
    gbig = [None] * nl
    pending = None

    def sum_received(l, recv):
        gbig[l] = [_sum_slots(f"sum_g{l}_{i}", r) for i, r in enumerate(recv)]

    for l in reversed(range(nl)):
        sv = saved[l]
        win_t, wao_t, wco_t, wso_t, wmix, wfi_t, wfo = sv["w"]
        ds = _mm(do, wfo, "nt", f32, f"mm_dsw{l}")
        g_wfo = _mm(sv["sw"], do, "tn", bf16, f"mm_gfo{l}")
        dff = _swiglu_bwd(cfg, f"swiglu_b{l}", sv["ff"], ds)
        dh2 = _mm(dff, wfi_t, "nn", f32, f"mm_dh2{l}")
        g_wfi = _mm(dff, sv["h2"], "tn", bf16, f"mm_gfi{l}")
        dx0p, dmixed, dm_b = _norm_bwd(cfg, f"norm_b_b{l}", sv["x1"], sv["mixed"], dh2, dres, mod[l], 2, mod[l], 3)
        dmod_rows[l][3], dmod_rows[l][4], dmod_rows[l][2] = dm_b[:, :, 0], dm_b[:, :, 1], dm_b[:, :, 2]
        dmerged = _mm(dmixed, wmix, "nt", f32, f"mm_dmg{l}")
        g_wmix = _mm(sv["merged"], dmixed, "tn", bf16, f"mm_gmix{l}")
        dya, dyc, dys, dg0, dg1, dg2 = _gate_bwd(cfg, f"gate_b{l}", sv["p"], sv["ya"], sv["yc"], sv["ys"], dmerged)
        dattn = _mm(dya, wao_t, "nn", f32, f"mm_dat{l}")
        g_wao = _mm(dya, sv["attn"], "tn", bf16, f"mm_gao{l}")
        dhs = _mm(dyc, wco_t, "nn", f32, f"mm_dhs{l}")
        g_wco = _mm(dyc, sv["hs"], "tn", bf16, f"mm_gco{l}")
        dz = _mm(dys, wso_t, "nn", f32, f"mm_dz{l}")
        g_wso = _mm(dys, sv["z"], "tn", bf16, f"mm_gso{l}")
        dhc, g_ln[l] = _ln_silu_bwd(cfg, f"lnsilu_b{l}", sv["hc"], dhs, sv["lnp"])
        da, dg, dbg, dcg, dxs, g_w31[l], g_w3[l] = _conv_bwd(cfg, f"conv_b{l}", sv["p"], dhc, dz, w31b[l], w3p[l])
        (dq, dkr, dvr), recv = _attn_bwd(cfg, f"attn_b{l}", sv["q"], sv["krep"], sv["vrep"], sv["attn"], sv["lse"],
                                         dattn, comm=None if pending is None else (pending, True))
        if pending is not None:
            sum_received(l + 1, recv)
        dpq, dgain = _qkv_bwd(cfg, f"qkv_b{l}", sv["p"], dq, dkr, dvr, cs, sv["gain"], bd, rept)
        g_qn[l] = jnp.sum(dgain[0, :AW].reshape(reps_q, HEAD_DIM), axis=0)
        g_kn[l] = jnp.sum(dgain[0, AW:].reshape(reps_k, HEAD_DIM), axis=0)
        dp = jnp.concatenate([dpq, da, dg, dbg, dcg, dxs, dg0, dg1, dg2], axis=1)
        dh = _mm(dp, win_t, "nn", f32, f"mm_dh{l}")
        g_win = _mm(dp, sv["h"], "tn", bf16, f"mm_gin{l}")
        if l == 0:
            dres, _, dm_a = _norm_bwd(cfg, f"norm_a_b{l}", sv["x0"], None, dh, dx0p, None, 0, mod[l], 0)
        else:
            dres, do, dm_a = _norm_bwd(cfg, f"norm_a_b{l}", sv["x0"], sv["oprev"], dh, dx0p, mod[l - 1], 5, mod[l], 0)
            dmod_rows[l - 1][5] = dm_a[:, :, 2]
        dmod_rows[l][0], dmod_rows[l][1] = dm_a[:, :, 0], dm_a[:, :, 1]

        parts = [g_win, g_wao, g_wco, g_wso, g_wmix, g_wfi, g_wfo]
        pending = [g.reshape(NDEV, g.shape[0] // NDEV, g.shape[1]) for g in parts]

    sum_received(0, _all_to_all("scatter_g0", pending))
    grad_x = dres.reshape(b, cfg.T, d)[:, cfg.LC:, :]

    rb = -(-(b + 1) // SUBLANE) * SUBLANE
    dmod_l = jnp.stack([jnp.stack(rows, axis=2) for rows in dmod_rows])
    d_lat = dmod_l[:, :, 1].reshape(nl, b, NDEV, na)
    d_ctx = jnp.sum(dmod_l[:, :, 0], axis=1).reshape(nl, 1, NDEV, na)
    send = jnp.concatenate([d_lat, d_ctx, jnp.zeros((nl, rb - b - 1, NDEV, na), f32)], axis=1)
    send = jnp.transpose(send, (2, 0, 1, 3)).reshape(NDEV, nl * rb, na)
    (recv,) = _all_to_all("scatter_dmod", [send])
    recv = recv.reshape(NDEV, nl, rb, na)
    dmod_ex = jnp.transpose(recv[:, :, :b], (1, 0, 2, 3)).reshape(nl, nb, na)
    dmod_ex = jnp.pad(dmod_ex, ((0, 0), (0, nbp - nb), (0, 0)))
    dmodc = jnp.transpose(recv[:, :, b], (1, 0, 2))
    g_wada, g_bloc, g_cctx = _ada_bwd("ada_bwd", call, w_ada, dmod_ex, dmodc, nb)

    g_bada = lax.dynamic_update_slice(jnp.zeros((nl, N_MOD * d), f32), g_bloc[:, 0], (0, me * na))
    g_w31s = jnp.stack(g_w31)
    small_parts = [g_cctx[0], g_bada, jnp.stack(g_qn), jnp.stack(g_kn), g_w31s[:, :cfg.K31], g_w31s[:, cfg.K31],
                   jnp.stack([g[0] for g in g_ln]), jnp.stack([g[1] for g in g_ln]),
                   jnp.stack(g_w3)[:, :cfg.K3]]
    small_part_shapes = [p_.shape for p_ in small_parts]
    (gs,) = _all_gather("gather_gsmall", [_pack_flat(small_parts)])
    gsum = _sum_slots("sum_gsmall", gs)
    (gr_cctx, gr_bada, gr_qn, gr_kn, gr_w31, gr_b31, gr_lng, gr_lnb, gr_w3) = _unpack_flat(gsum, small_part_shapes)
    gr_w31 = lax.dynamic_slice_in_dim(gr_w31, me * cw8, cw8, axis=2)
    gr_w3 = lax.dynamic_slice_in_dim(gr_w3, me * cw8, cw8, axis=2)

    def big(i, transposed):
        g = jnp.stack([gbig[l][i] for l in range(nl)])
        return jnp.transpose(g, (0, 2, 1)) if transposed else g

    grads = {
        "c_ctx": gr_cctx, "w_ada": g_wada, "b_ada": gr_bada, "w_in": big(0, True), "q_norm": gr_qn, "k_norm": gr_kn,
        "w_attn_o": big(1, True), "conf_dw_w": gr_w31, "conf_dw_b": gr_b31, "conf_ln_g": gr_lng, "conf_ln_b": gr_lnb,
        "w_conf_out": big(2, True), "sc_dw_w": gr_w3, "w_sc_out": big(3, True), "w_mix_out": big(4, False),
        "w_ffn_in": big(5, True), "w_ffn_out": big(6, False)}
    weights = dict(c_ctx=c_ctx, w_ada=w_ada, b_ada=b_ada, w_in=w_in, q_norm=q_norm, k_norm=k_norm, w_attn_o=w_attn_o,
                   conf_dw_w=conf_dw_w, conf_dw_b=conf_dw_b, conf_ln_g=conf_ln_g, conf_ln_b=conf_ln_b,
                   w_conf_out=w_conf_out, sc_dw_w=sc_dw_w, w_sc_out=w_sc_out, w_mix_out=w_mix_out,
                   w_ffn_in=w_ffn_in, w_ffn_out=w_ffn_out)
    m_in = dict(c_ctx=m_c_ctx, w_ada=m_w_ada, b_ada=m_b_ada, w_in=m_w_in, q_norm=m_q_norm, k_norm=m_k_norm,
                w_attn_o=m_w_attn_o, conf_dw_w=m_conf_dw_w, conf_dw_b=m_conf_dw_b, conf_ln_g=m_conf_ln_g,
                conf_ln_b=m_conf_ln_b, w_conf_out=m_w_conf_out, sc_dw_w=m_sc_dw_w, w_sc_out=m_w_sc_out,
                w_mix_out=m_w_mix_out, w_ffn_in=m_w_ffn_in, w_ffn_out=m_w_ffn_out)
    v_in = dict(c_ctx=v_c_ctx, w_ada=v_w_ada, b_ada=v_b_ada, w_in=v_w_in, q_norm=v_q_norm, k_norm=v_k_norm,
                w_attn_o=v_w_attn_o, conf_dw_w=v_conf_dw_w, conf_dw_b=v_conf_dw_b, conf_ln_g=v_conf_ln_g,
                conf_ln_b=v_conf_ln_b, w_conf_out=v_w_conf_out, sc_dw_w=v_sc_dw_w, w_sc_out=v_w_sc_out,
                w_mix_out=v_w_mix_out, w_ffn_in=v_w_ffn_in, w_ffn_out=v_w_ffn_out)
    names = list(weights)
    big_names = ("w_ada", "w_in", "w_attn_o", "w_conf_out", "w_sc_out", "w_mix_out", "w_ffn_in", "w_ffn_out")
    small_names = [n for n in names if n not in big_names]
    delta, new_m, new_v = {}, {}, {}
    for n in big_names:
        shp = weights[n].shape
        two_d = (shp[0] * shp[1], shp[2])
        dl, nm, nv = _adamw(f"adamw_{n}", weights[n].reshape(two_d), grads[n].reshape(two_d),
                            m_in[n].reshape(two_d), v_in[n].reshape(two_d))
        delta[n], new_m[n], new_v[n] = dl.reshape(shp), nm.reshape(shp), nv.reshape(shp)
    sshapes = [weights[n].shape for n in small_names]
    dl, nm, nv = _adamw("adamw_small", _pack_flat([weights[n] for n in small_names]),
                        _pack_flat([grads[n] for n in small_names]), _pack_flat([m_in[n] for n in small_names]),
                        _pack_flat([v_in[n] for n in small_names]))
    for n, a_, b_, c_ in zip(small_names, _unpack_flat(dl, sshapes), _unpack_flat(nm, sshapes), _unpack_flat(nv, sshapes)):
        delta[n], new_m[n], new_v[n] = a_, b_, c_

    return (loss, grad_x, *[grads[n] for n in names], *[delta[n] for n in names],
            *[new_m[n] for n in names], *[new_v[n] for n in names])
```

```python
import functools
import math

import numpy as np
import jax
import jax.numpy as jnp
from jax import lax
from jax.experimental import pallas as pl
from jax.experimental.pallas import tpu as pltpu

f32 = jnp.float32
bf16 = jnp.bfloat16

NDEV = 8
AXES = ("x", "y", "c")
HEAD_DIM = 64
N_Q_HEADS = 8
N_KV_HEADS = 2
Q_GROUP = N_Q_HEADS // N_KV_HEADS
AW = N_Q_HEADS * HEAD_DIM
KVW = N_KV_HEADS * HEAD_DIM
GW = Q_GROUP * HEAD_DIM
QKW = AW + KVW
QKVW = AW + 2 * KVW
GRID_W = 64
AXIS_DIM = HEAD_DIM // 2
ROPE_THETA = 10000.0
ATTN_SCALE = HEAD_DIM ** -0.5
EPS = 1e-6
N_MOD = 6
MOD_ROWS = 8
CONV_PAD_ROWS = 16
SC_PAD_ROWS = 8
LANE = 128
SUBLANE = 8
VMEM_LIMIT = 56 * 1024 * 1024

ADAM_LR = 0.001
ADAM_B1 = 0.9
ADAM_B2 = 0.999
ADAM_EPS = 1e-08
ADAM_WD = 0.01
ADAM_STEP = 10

NN =(((1,), (0,)), ((), ()))
NT_ = (((1,), (1,)), ((), ()))
TN = (((0,), (0,)), ((), ()))


def _params(ndims, vmem=None):
    return pltpu.CompilerParams(dimension_semantics=("arbitrary",) * ndims, vmem_limit_bytes=vmem)


def _divisor(n, cap, mult):
    best = None
    for d in range(mult, min(n, cap) + 1, mult):
        if n % d == 0:
            best = d
    return n if best is None else best


def _const_spec(a):
    nd = a.ndim
    return pl.BlockSpec(a.shape, lambda *idx: (0,) * nd)


class _Cfg:
    pass


def _mm(a, b, mode, out_dtype, name, comm=None):
    if mode == "tn":
        m, ka = a.shape
        nb = b.shape[1]
        tka = _divisor(ka, 1408, LANE)
        tnb = _divisor(nb, 1024, LANE)
        tmr = _divisor(m, 1024, SUBLANE)
        nsteps = m // tmr

        def body(a_ref, b_ref, o_ref, acc_ref):
            k = pl.program_id(2)

            @pl.when(k == 0)
            def _():
                acc_ref[...] = jnp.zeros_like(acc_ref)

            acc_ref[...] += lax.dot_general(a_ref[...].astype(bf16), b_ref[...].astype(bf16), TN,
                                            preferred_element_type=f32)

            @pl.when(k == nsteps - 1)
            def _():
                o_ref[...] = acc_ref[...].astype(out_dtype)

        return pl.pallas_call(
            body, out_shape=jax.ShapeDtypeStruct((ka, nb), out_dtype),
            grid=(ka // tka, nb // tnb, nsteps),
            in_specs=[pl.BlockSpec((tmr, tka), lambda i, j, k: (k, i)),
                      pl.BlockSpec((tmr, tnb), lambda i, j, k: (k, j))],
            out_specs=pl.BlockSpec((tka, tnb), lambda i, j, k: (i, j)),
            scratch_shapes=[pltpu.VMEM((tka, tnb), f32)],
            compiler_params=_params(3, VMEM_LIMIT), name=name)(a, b)

    m, kdim = a.shape
    n = b.shape[1] if mode == "nn" else b.shape[0]
    tm = _divisor(m, 1024 if a.dtype == bf16 else 512, SUBLANE)
    tn = _divisor(n, 1408, LANE)
    tk = _divisor(kdim, 1408, LANE)
    nsteps = kdim // tk
    dims = NN if mode == "nn" else NT_

    def body(a_ref, b_ref, o_ref, acc_ref):
        k = pl.program_id(2)

        @pl.when(k == 0)
        def _():
            acc_ref[...] = jnp.zeros_like(acc_ref)

        acc_ref[...] += lax.dot_general(a_ref[...].astype(bf16), b_ref[...].astype(bf16), dims,
                                        preferred_element_type=f32)

        @pl.when(k == nsteps - 1)
        def _():
            o_ref[...] = acc_ref[...].astype(out_dtype)

    if mode == "nn":
        b_spec = pl.BlockSpec((tk, tn), lambda i, j, k: (k, j))
    else:
        b_spec = pl.BlockSpec((tn, tk), lambda i, j, k: (j, k))
    (out,), couts = _hosting_call(
        body, comm, out_shape=(jax.ShapeDtypeStruct((m, n), out_dtype),),
        grid=(m // tm, n // tn, nsteps),
        in_specs=[pl.BlockSpec((tm, tk), lambda i, j, k: (i, k)), b_spec],
        out_specs=(pl.BlockSpec((tm, tn), lambda i, j, k: (i, j)),),
        scratch_shapes=[pltpu.VMEM((tm, tn), f32)],
        compiler_params=_params(3, VMEM_LIMIT), name=name, args=(a, b))
    return out if comm is None else (out, couts)


def _row_spec(cfg, width, col=0):
    return pl.BlockSpec((cfg.TR, width), lambda i: (i, col))


def _mod_spec(cfg):
    nt, nc = cfg.NT, cfg.NC
    return pl.BlockSpec((1, 1, MOD_ROWS, cfg.D),
                        lambda i: (i // nt, ((i % nt) >= nc).astype(jnp.int32), 0, 0))


def _segment_start(cfg, i):
    j = i % cfg.NT
    return jnp.logical_or(j == 0, j == cfg.NC)


def _norm_fwd(cfg, name, xin, o, modg, gk, modn, sk):
    has_o = o is not None
    r_, d = xin.shape

    def body(*refs):
        if has_o:
            x_ref, o_ref, mg_ref, mn_ref, xn_ref, h_ref = refs
            x = x_ref[...] + mg_ref[0, 0, gk:gk + 1, :] * o_ref[...]
            xn_ref[...] = x
        else:
            x_ref, mn_ref, h_ref = refs
            x = x_ref[...]
        r = lax.rsqrt(jnp.mean(x * x, axis=-1, keepdims=True) + EPS)
        h = (x * r) * (1.0 + mn_ref[0, 0, sk + 1:sk + 2, :]) + mn_ref[0, 0, sk:sk + 1, :]
        h_ref[...] = h.astype(bf16)

    row = _row_spec(cfg, d)
    if has_o:
        ins, in_specs = (xin, o, modg, modn), [row, row, _mod_spec(cfg), _mod_spec(cfg)]
        out_shape = (jax.ShapeDtypeStruct((r_, d), f32), jax.ShapeDtypeStruct((r_, d), bf16))
        out_specs = (row, row)
    else:
        ins, in_specs = (xin, modn), [row, _mod_spec(cfg)]
        out_shape = jax.ShapeDtypeStruct((r_, d), bf16)
        out_specs = row
    res = pl.pallas_call(body, out_shape=out_shape, grid=(r_ // cfg.TR,), in_specs=in_specs,
                         out_specs=out_specs, compiler_params=_params(1), name=name)(*ins)
    return res if has_o else (xin, res)


def _norm_bwd(cfg, name, xnew, o, dh, dres, modg, gk, modn, sk):
    has_o = o is not None
    r_, d = xnew.shape

    def body(*refs):
        if has_o:
            xn_ref, o_ref, dh_ref, dr_ref, mg_ref, mn_ref, dx_ref, do_ref, dm_ref = refs
        else:
            xn_ref, dh_ref, dr_ref, mn_ref, dx_ref, dm_ref = refs
        i = pl.program_id(0)
        x = xn_ref[...]
        dhv = dh_ref[...].astype(f32)
        r = lax.rsqrt(jnp.mean(x * x, axis=-1, keepdims=True) + EPS)
        xh = x * r
        dxh = dhv * (1.0 + mn_ref[0, 0, sk + 1:sk + 2, :])
        dx = r * (dxh - xh * jnp.mean(dxh * xh, axis=-1, keepdims=True)) + dr_ref[...]
        dx_ref[...] = dx

        @pl.when(_segment_start(cfg, i))
        def _():
            dm_ref[...] = jnp.zeros_like(dm_ref)

        dm_ref[0, 0, 0:1, :] += jnp.sum(dhv, axis=0, keepdims=True)
        dm_ref[0, 0, 1:2, :] += jnp.sum(dhv * xh, axis=0, keepdims=True)
        if has_o:
            ov = o_ref[...]
            dm_ref[0, 0, 2:3, :] += jnp.sum(dx * ov, axis=0, keepdims=True)
            do_ref[...] = (dx * mg_ref[0, 0, gk:gk + 1, :]).astype(bf16)

    row = _row_spec(cfg, d)
    dm_shape = jax.ShapeDtypeStruct((cfg.B, 2, MOD_ROWS, d), f32)
    if has_o:
        ins = (xnew, o, dh, dres, modg, modn)
        in_specs = [row, row, row, row, _mod_spec(cfg), _mod_spec(cfg)]
        out_shape = (jax.ShapeDtypeStruct((r_, d), f32), jax.ShapeDtypeStruct((r_, d), bf16), dm_shape)
        out_specs = (row, row, _mod_spec(cfg))
    else:
        ins = (xnew, dh, dres, modn)
        in_specs = [row, row, row, _mod_spec(cfg)]
        out_shape = (jax.ShapeDtypeStruct((r_, d), f32), dm_shape)
        out_specs = (row, _mod_spec(cfg))
    res = pl.pallas_call(body, out_shape=out_shape, grid=(r_ // cfg.TR,), in_specs=in_specs,
                         out_specs=out_specs, compiler_params=_params(1), name=name)(*ins)
    if has_o:
        return res
    return res[0], None, res[1]


def _loss_head(cfg, name, x1, o, modg, gk, target):
    r_, d = x1.shape
    nt, nc = cfg.NT, cfg.NC
    nlat = nt - nc

    def body(x_ref, o_ref, mg_ref, t_ref, ls_ref, dy_ref, do_ref, dm_ref):
        i = pl.program_id(0)
        lat = (i % nt) >= nc
        gate = mg_ref[0, 0, gk:gk + 1, :]
        ov = o_ref[...]
        err = jnp.where(lat, x_ref[...] + gate * ov - t_ref[...], 0.0)
        ls_ref[...] = jnp.zeros_like(ls_ref)
        ls_ref[0, 0:1, :] = jnp.sum(err * err, axis=0, keepdims=True)
        dy = err * (1.0 / d)
        dy_ref[...] = dy
        do_ref[...] = (dy * gate).astype(bf16)

        @pl.when(_segment_start(cfg, i))
        def _():
            dm_ref[...] = jnp.zeros_like(dm_ref)

        dm_ref[0, 0, 2:3, :] += jnp.sum(dy * ov, axis=0, keepdims=True)

    row = _row_spec(cfg, d)
    t_spec = pl.BlockSpec((cfg.TR, d), lambda i: ((i // nt) * nlat + jnp.maximum((i % nt) - nc, 0), 0))
    return pl.pallas_call(
        body,
        out_shape=(jax.ShapeDtypeStruct((r_ // cfg.TR, SUBLANE, d), f32), jax.ShapeDtypeStruct((r_, d), f32),
                   jax.ShapeDtypeStruct((r_, d), bf16), jax.ShapeDtypeStruct((cfg.B, 2, MOD_ROWS, d), f32)),
        grid=(r_ // cfg.TR,),
        in_specs=[row, row, _mod_spec(cfg), t_spec],
        out_specs=(pl.BlockSpec((1, SUBLANE, d), lambda i: (i, 0, 0)), row, row, _mod_spec(cfg)),
        compiler_params=_params(1), name=name)(x1, o, modg, target)


def _swap16(y):
    w = y.shape[1]
    lane = lax.broadcasted_iota(jnp.int32, (1, w), 1)
    lo = jnp.bitwise_and(lane, 31) < 16
    return jnp.where(lo, pltpu.roll(y, w - 16, 1), pltpu.roll(y, 16, 1))


def _dot_split(v, m):
    hi = v.astype(bf16)
    lo = (v - hi.astype(f32)).astype(bf16)
    return jnp.dot(hi, m, preferred_element_type=f32) + jnp.dot(lo, m, preferred_element_type=f32)


def _head_sum(v, bd):
    return _dot_split(v, bd)


def _qkv_fwd(cfg, name, p, cs, gain, bd, rep):
    r_ = p.shape[0]
    nt = cfg.NT

    def body(p_ref, cs_ref, g_ref, bd_ref, rep_ref, q_ref, k_ref, v_ref):
        x = p_ref[:, 0:QKW]
        r = lax.rsqrt(_head_sum(x * x, bd_ref[...]) * (1.0 / HEAD_DIM) + EPS)
        y = (x * r) * g_ref[...]
        out = y * cs_ref[0] + _swap16(y) * cs_ref[1]
        q_ref[...] = out[:, 0:AW].astype(bf16)
        kr = out[:, AW:QKW].astype(bf16)
        k_ref[...] = jnp.dot(kr, rep_ref[...], preferred_element_type=f32).astype(bf16)
        vv = p_ref[:, QKW:QKVW].astype(bf16)
        v_ref[...] = jnp.dot(vv, rep_ref[...], preferred_element_type=f32).astype(bf16)

    row = _row_spec(cfg, AW)
    return pl.pallas_call(
        body, out_shape=(jax.ShapeDtypeStruct((r_, AW), bf16),) * 3, grid=(r_ // cfg.TR,),
        in_specs=[_row_spec(cfg, QKVW), pl.BlockSpec((2, cfg.TR, QKW), lambda i: (0, i % nt, 0)),
                  _const_spec(gain), _const_spec(bd), _const_spec(rep)],
        out_specs=(row, row, row), compiler_params=_params(1), name=name)(p, cs, gain, bd, rep)


def _qkv_bwd(cfg, name, p, dq, dkr, dvr, cs, gain, bd, rept):
    r_ = p.shape[0]
    nt = cfg.NT

    def body(p_ref, dq_ref, dk_ref, dv_ref, cs_ref, g_ref, bd_ref, rt_ref, dp_ref, dg_ref):
        i = pl.program_id(0)
        x = p_ref[:, 0:QKW]
        rt = rt_ref[...]
        dk = _dot_split(dk_ref[...], rt)
        g = jnp.concatenate([dq_ref[...], dk], axis=1)
        gy = g * cs_ref[0] + _swap16(g * cs_ref[1])
        bdv = bd_ref[...]
        r = lax.rsqrt(_head_sum(x * x, bdv) * (1.0 / HEAD_DIM) + EPS)
        xh = x * r
        dxh = gy * g_ref[...]
        dx = r * (dxh - xh * (_head_sum(dxh * xh, bdv) * (1.0 / HEAD_DIM)))
        dp_ref[:, 0:QKW] = dx.astype(bf16)
        dp_ref[:, QKW:QKVW] = _dot_split(dv_ref[...], rt).astype(bf16)

        @pl.when(i == 0)
        def _():
            dg_ref[...] = jnp.zeros_like(dg_ref)

        dg_ref[0:1, :] += jnp.sum(gy * xh, axis=0, keepdims=True)

    row = _row_spec(cfg, AW)
    return pl.pallas_call(
        body, out_shape=(jax.ShapeDtypeStruct((r_, QKVW), bf16), jax.ShapeDtypeStruct((SUBLANE, QKW), f32)),
        grid=(r_ // cfg.TR,),
        in_specs=[_row_spec(cfg, QKVW), row, row, row,
                  pl.BlockSpec((2, cfg.TR, QKW), lambda i: (0, i % nt, 0)),
                  _const_spec(gain), _const_spec(bd), _const_spec(rept)],
        out_specs=(_row_spec(cfg, QKVW), pl.BlockSpec((SUBLANE, QKW), lambda i: (0, 0))),
        compiler_params=_params(1), name=name)(p, dq, dkr, dvr, cs, gain, bd, rept)


def _head_masks():
    lane = lax.broadcasted_iota(jnp.int32, (1, GW), 1)
    return [jnp.logical_and(lane >= HEAD_DIM * h, lane < HEAD_DIM * (h + 1)) for h in range(Q_GROUP)]


def _attn_fwd(cfg, name, q, krep, vrep, comm=None):
    r_ = q.shape[0]
    tr, t, lc, nt, nc = cfg.TR, cfg.T, cfg.LC, cfg.NT, cfg.NC

    def body(q_ref, k_ref, v_ref, o_ref, l_ref):
        j = pl.program_id(2)
        masks = _head_masks()
        lane = lax.broadcasted_iota(jnp.int32, (1, LANE), 1)

        def run(nk):
            qv = q_ref[...]
            kv = k_ref[0:nk, :]
            vv = v_ref[0:nk, :]
            acc = jnp.zeros((tr, GW), f32)
            lse = jnp.zeros((tr, LANE), f32)
            for h in range(Q_GROUP):
                qh = jnp.where(masks[h], qv, jnp.zeros_like(qv))
                s = lax.dot_general(qh, kv, NT_, preferred_element_type=f32) * ATTN_SCALE
                m = jnp.max(s, axis=1, keepdims=True)
                pr = jnp.exp(s - m)
                l = jnp.sum(pr, axis=1, keepdims=True)
                oh = jnp.dot(pr.astype(bf16), vv, preferred_element_type=f32) / l
                acc = acc + jnp.where(masks[h], oh, 0.0)
                lse = lse + jnp.where(lane == h, m + jnp.log(l), 0.0)
            o_ref[...] = acc
            l_ref[...] = lse

        @pl.when(j < nc)
        def _():
            run(lc)

        @pl.when(j >= nc)
        def _():
            run(t)

    qspec = pl.BlockSpec((tr, GW), lambda e, g, j: (e * nt + j, g))
    kspec = pl.BlockSpec((t, GW), lambda e, g, j: (e, g))
    return _hosting_call(
        body, comm, out_shape=(jax.ShapeDtypeStruct((r_, AW), f32), jax.ShapeDtypeStruct((r_, N_KV_HEADS * LANE), f32)),
        grid=(cfg.B, N_KV_HEADS, nt), in_specs=[qspec, kspec, kspec],
        out_specs=(qspec, pl.BlockSpec((tr, LANE), lambda e, g, j: (e * nt + j, g))),
        compiler_params=_params(3, VMEM_LIMIT), name=name, args=(q, krep, vrep))


def _hosting_call(body, comm, *, out_shape, grid, in_specs, out_specs, compiler_params, name, args, scratch_shapes=()):
    if comm is None:
        return pl.pallas_call(body, out_shape=out_shape, grid=grid, in_specs=in_specs, out_specs=out_specs,
                              scratch_shapes=list(scratch_shapes), compiler_params=compiler_params,
                              name=name)(*args), None
    xs, scatter = comm
    n, n_in, n_out, n_scr = len(xs), len(args), len(out_shape), len(scratch_shapes)
    last = [g - 1 for g in grid]

    def hosted(*refs):
        ins, cins = refs[:n_in], refs[n_in:n_in + n]
        outs, couts = refs[n_in + n:n_in + n + n_out], refs[n_in + n + n_out:n_in + 2 * n + n_out]
        scratch = refs[n_in + 2 * n + n_out:n_in + 2 * n + n_out + n_scr]
        sems = refs[n_in + 2 * n + n_out + n_scr:]
        ids = [pl.program_id(a) for a in range(len(grid))]
        is_first = functools.reduce(jnp.logical_and, [i == 0 for i in ids])
        is_last = functools.reduce(jnp.logical_and, [i == l for i, l in zip(ids, last)])

        @pl.when(is_first)
        def _():
            _exchange_start(cins, couts, sems, scatter)

        body(*ins, *outs, *scratch)

        @pl.when(is_last)
        def _():
            _exchange_wait(cins, couts, sems, scatter)

    anyspec = pl.BlockSpec(memory_space=pl.ANY)
    res = pl.pallas_call(
        hosted, out_shape=tuple(out_shape) + tuple(_exchange_out_shapes(xs, scatter)), grid=grid,
        in_specs=list(in_specs) + [anyspec] * n, out_specs=tuple(out_specs) + (anyspec,) * n,
        scratch_shapes=list(scratch_shapes) + _exchange_scratch(n), compiler_params=compiler_params,
        name=name)(*args, *xs)
    return tuple(res[:n_out]), list(res[n_out:])


def _attn_bwd(cfg, name, q, krep, vrep, o, lse, do, comm=None):
    r_ = q.shape[0]
    tr, t, lc, nt, nc = cfg.TR, cfg.T, cfg.LC, cfg.NT, cfg.NC

    def body(q_ref, k_ref, v_ref, o_ref, l_ref, do_ref, dq_ref, dk_ref, dv_ref):
        j = pl.program_id(2)
        masks = _head_masks()
        lane = lax.broadcasted_iota(jnp.int32, (1, LANE), 1)

        @pl.when(j == 0)
        def _():
            dk_ref[...] = jnp.zeros_like(dk_ref)
            dv_ref[...] = jnp.zeros_like(dv_ref)

        def run(nk):
            qv = q_ref[...]
            kv = k_ref[0:nk, :]
            vv = v_ref[0:nk, :]
            ov = o_ref[...]
            dov = do_ref[...]
            lv = l_ref[...]
            dq = jnp.zeros((tr, GW), f32)
            dk = jnp.zeros((nk, GW), f32)
            dv = jnp.zeros((nk, GW), f32)
            for h in range(Q_GROUP):
                qh = jnp.where(masks[h], qv, jnp.zeros_like(qv))
                doh = jnp.where(masks[h], dov, 0.0)
                dohb = doh.astype(bf16)
                delta = jnp.sum(doh * ov, axis=1, keepdims=True)
                lse_h = jnp.sum(jnp.where(lane == h, lv, 0.0), axis=1, keepdims=True)
                s = lax.dot_general(qh, kv, NT_, preferred_element_type=f32) * ATTN_SCALE
                pr = jnp.exp(s - lse_h)
                dpr = lax.dot_general(dohb, vv, NT_, preferred_element_type=f32)
                ds = (pr * (dpr - delta) * ATTN_SCALE).astype(bf16)
                dq = dq + jnp.where(masks[h], jnp.dot(ds, kv, preferred_element_type=f32), 0.0)
                dk = dk + lax.dot_general(ds, qh, TN, preferred_element_type=f32)
                dv = dv + lax.dot_general(pr.astype(bf16), dohb, TN, preferred_element_type=f32)
            dq_ref[...] = dq
            dk_ref[0:nk, :] += dk
            dv_ref[0:nk, :] += dv

        @pl.when(j < nc)
        def _():
            run(lc)

        @pl.when(j >= nc)
        def _():
            run(t)

    qspec = pl.BlockSpec((tr, GW), lambda e, g, j: (e * nt + j, g))
    kspec = pl.BlockSpec((t, GW), lambda e, g, j: (e, g))
    lspec = pl.BlockSpec((tr, LANE), lambda e, g, j: (e * nt + j, g))
    return _hosting_call(
        body, comm, out_shape=(jax.ShapeDtypeStruct((r_, AW), f32),) * 3,
        grid=(cfg.B, N_KV_HEADS, nt), in_specs=[qspec, kspec, kspec, qspec, lspec, qspec],
        out_specs=(qspec, kspec, kspec),
        compiler_params=_params(3, VMEM_LIMIT), name=name, args=(q, krep, vrep, o, lse, do))


def _padded(x, pad_rows):
    z = jnp.zeros((pad_rows, x.shape[1]), f32)
    return jnp.concatenate([z, x, z], axis=0)


def _shifted(buf, shift):
    n = buf.shape[0]
    shift = shift % n
    return buf if shift == 0 else pltpu.roll(buf, shift, 0)


def _dwconv(x, w_ref, taps, pad_rows):
    n = x.shape[0]
    buf = _padded(x, pad_rows)
    acc = jnp.zeros_like(buf)
    for k in range(taps):
        acc = acc + w_ref[k:k + 1, :] * _shifted(buf, taps // 2 - k)
    return acc[pad_rows:pad_rows + n]


def _dwconv_t(dy, w_ref, taps, pad_rows):
    n = dy.shape[0]
    buf = _padded(dy, pad_rows)
    acc = jnp.zeros_like(buf)
    for k in range(taps):
        acc = acc + w_ref[k:k + 1, :] * _shifted(buf, k - taps // 2)
    return acc[pad_rows:pad_rows + n]


def _dwconv_dw(x, dy, dw_ref, taps, pad_rows):
    n = x.shape[0]
    buf = _padded(x, pad_rows)
    for k in range(taps):
        xs = _shifted(buf, taps // 2 - k)[pad_rows:pad_rows + n]
        dw_ref[k:k + 1, :] += jnp.sum(xs * dy, axis=0, keepdims=True)


def _conv_specs(cfg):
    t, cb, cw = cfg.T, cfg.CB, cfg.CW
    offs = (cfg.OFF_CONF, cfg.OFF_CONF + cw, cfg.OFF_SC, cfg.OFF_SC + cw, cfg.OFF_SC + 2 * cw)
    return [pl.BlockSpec((t, cb), lambda c, e, off=off: (e, off // cb + c)) for off in offs]


def _conv_fwd(cfg, name, p, w31, w3):
    r_ = p.shape[0]
    t, lc, cb, cw = cfg.T, cfg.LC, cfg.CB, cfg.CW
    k31, k3 = cfg.K31, cfg.K3

    def body(a_ref, g_ref, bg_ref, cg_ref, xs_ref, w31_ref, w3_ref, hc_ref, z_ref):
        for lo, hi in ((0, lc), (lc, t)):
            hh = a_ref[lo:hi, :] * jax.nn.sigmoid(g_ref[lo:hi, :])
            hc_ref[lo:hi, :] = _dwconv(hh, w31_ref, k31, CONV_PAD_ROWS) + w31_ref[k31:k31 + 1, :]
            u = cg_ref[lo:hi, :] * xs_ref[lo:hi, :]
            z_ref[lo:hi, :] = (bg_ref[lo:hi, :] * _dwconv(u, w3_ref, k3, SC_PAD_ROWS)).astype(bf16)

    ospec = pl.BlockSpec((t, cb), lambda c, e: (e, c))
    return pl.pallas_call(
        body, out_shape=(jax.ShapeDtypeStruct((r_, cw), f32), jax.ShapeDtypeStruct((r_, cw), bf16)),
        grid=(cw // cb, cfg.B),
        in_specs=_conv_specs(cfg) + [pl.BlockSpec((w31.shape[0], cb), lambda c, e: (0, c)),
                                     pl.BlockSpec((w3.shape[0], cb), lambda c, e: (0, c))],
        out_specs=(ospec, ospec), compiler_params=_params(2, VMEM_LIMIT), name=name)(p, p, p, p, p, w31, w3)


def _conv_bwd(cfg, name, p, dhc, dz, w31, w3, comm=None):
    r_ = p.shape[0]
    t, lc, cb, cw = cfg.T, cfg.LC, cfg.CB, cfg.CW
    k31, k3 = cfg.K31, cfg.K3

    def body(a_ref, g_ref, bg_ref, cg_ref, xs_ref, dhc_ref, dz_ref, w31_ref, w3_ref,
             da_ref, dg_ref, dbg_ref, dcg_ref, dxs_ref, dw31_ref, dw3_ref):
        e = pl.program_id(1)

        @pl.when(e == 0)
        def _():
            dw31_ref[...] = jnp.zeros_like(dw31_ref)
            dw3_ref[...] = jnp.zeros_like(dw3_ref)

        for lo, hi in ((0, lc), (lc, t)):
            a = a_ref[lo:hi, :]
            sg = jax.nn.sigmoid(g_ref[lo:hi, :])
            hh = a * sg
            dy = dhc_ref[lo:hi, :]
            dhh = _dwconv_t(dy, w31_ref, k31, CONV_PAD_ROWS)
            _dwconv_dw(hh, dy, dw31_ref, k31, CONV_PAD_ROWS)
            dw31_ref[k31:k31 + 1, :] += jnp.sum(dy, axis=0, keepdims=True)
            da_ref[lo:hi, :] = (dhh * sg).astype(bf16)
            dg_ref[lo:hi, :] = (dhh * hh * (1.0 - sg)).astype(bf16)

            cg = cg_ref[lo:hi, :]
            xs = xs_ref[lo:hi, :]
            u = cg * xs
            dzv = dz_ref[lo:hi, :]
            dbg_ref[lo:hi, :] = (dzv * _dwconv(u, w3_ref, k3, SC_PAD_ROWS)).astype(bf16)
            duc = dzv * bg_ref[lo:hi, :]
            du = _dwconv_t(duc, w3_ref, k3, SC_PAD_ROWS)
            _dwconv_dw(u, duc, dw3_ref, k3, SC_PAD_ROWS)
            dcg_ref[lo:hi, :] = (du * xs).astype(bf16)
            dxs_ref[lo:hi, :] = (du * cg).astype(bf16)

    ospec = pl.BlockSpec((t, cb), lambda c, e: (e, c))
    w31_spec = pl.BlockSpec((w31.shape[0], cb), lambda c, e: (0, c))
    w3_spec = pl.BlockSpec((w3.shape[0], cb), lambda c, e: (0, c))
    return _hosting_call(
        body, comm,
        out_shape=(jax.ShapeDtypeStruct((r_, cw), bf16),) * 5
        + (jax.ShapeDtypeStruct(w31.shape, f32), jax.ShapeDtypeStruct(w3.shape, f32)),
        grid=(cw // cb, cfg.B),
        in_specs=_conv_specs(cfg) + [ospec, ospec, w31_spec, w3_spec],
        out_specs=(ospec,) * 5 + (w31_spec, w3_spec),
        compiler_params=_params(2, VMEM_LIMIT), name=name, args=(p, p, p, p, p, dhc, dz, w31, w3))


def _ln_silu_fwd(cfg, name, hc, lnp):
    r_, cw = hc.shape

    def body(x_ref, p_ref, o_ref):
        x = x_ref[...]
        mu = jnp.mean(x, axis=-1, keepdims=True)
        xc = x - mu
        rs = lax.rsqrt(jnp.mean(xc * xc, axis=-1, keepdims=True) + EPS)
        hn = (xc * rs) * p_ref[0:1, :] + p_ref[1:2, :]
        o_ref[...] = (hn * jax.nn.sigmoid(hn)).astype(bf16)

    row = _row_spec(cfg, cw)
    return pl.pallas_call(body, out_shape=jax.ShapeDtypeStruct((r_, cw), bf16), grid=(r_ // cfg.TR,),
                          in_specs=[row, _const_spec(lnp)], out_specs=row,
                          compiler_params=_params(1), name=name)(hc, lnp)


def _ln_silu_bwd(cfg, name, hc, dhs, lnp):
    r_, cw = hc.shape

    def body(x_ref, d_ref, p_ref, dx_ref, dp_ref):
        i = pl.program_id(0)
        x = x_ref[...]
        mu = jnp.mean(x, axis=-1, keepdims=True)
        xc = x - mu
        rs = lax.rsqrt(jnp.mean(xc * xc, axis=-1, keepdims=True) + EPS)
        xh = xc * rs
        gain = p_ref[0:1, :]
        hn = xh * gain + p_ref[1:2, :]
        sg = jax.nn.sigmoid(hn)
        dhn = d_ref[...] * (sg * (1.0 + hn * (1.0 - sg)))
        dxh = dhn * gain
        dx_ref[...] = rs * (dxh - jnp.mean(dxh, axis=-1, keepdims=True)
                            - xh * jnp.mean(dxh * xh, axis=-1, keepdims=True))

        @pl.when(i == 0)
        def _():
            dp_ref[...] = jnp.zeros_like(dp_ref)

        dp_ref[0:1, :] += jnp.sum(dhn * xh, axis=0, keepdims=True)
        dp_ref[1:2, :] += jnp.sum(dhn, axis=0, keepdims=True)

    row = _row_spec(cfg, cw)
    return pl.pallas_call(
        body, out_shape=(jax.ShapeDtypeStruct((r_, cw), f32), jax.ShapeDtypeStruct((SUBLANE, cw), f32)),
        grid=(r_ // cfg.TR,), in_specs=[row, row, _const_spec(lnp)],
        out_specs=(row, pl.BlockSpec((SUBLANE, cw), lambda i: (0, 0))),
        compiler_params=_params(1), name=name)(hc, dhs, lnp)


def _gate_fwd(cfg, name, p, ya, yc, ys):
    r_, d = ya.shape
    gb = cfg.GB
    base = cfg.OFF_GATE // gb
    per = d // gb

    def body(g0_ref, g1_ref, g2_ref, ya_ref, yc_ref, ys_ref, o_ref):
        m = (jax.nn.sigmoid(g0_ref[...]) * ya_ref[...] + jax.nn.sigmoid(g1_ref[...]) * yc_ref[...]
             + jax.nn.sigmoid(g2_ref[...]) * ys_ref[...])
        o_ref[...] = m.astype(bf16)

    yspec = pl.BlockSpec((cfg.TR, gb), lambda i, c: (i, c))
    gspecs = [pl.BlockSpec((cfg.TR, gb), lambda i, c, k=k: (i, base + k * per + c)) for k in range(3)]
    return pl.pallas_call(body, out_shape=jax.ShapeDtypeStruct((r_, d), bf16), grid=(r_ // cfg.TR, per),
                          in_specs=gspecs + [yspec] * 3, out_specs=yspec,
                          compiler_params=_params(2), name=name)(p, p, p, ya, yc, ys)


def _gate_bwd(cfg, name, p, ya, yc, ys, dm):
    r_, d = ya.shape
    gb = cfg.GB
    base = cfg.OFF_GATE // gb
    per = d // gb

    def body(g0_ref, g1_ref, g2_ref, ya_ref, yc_ref, ys_ref, dm_ref, da_ref, dc_ref, ds_ref, d0_ref, d1_ref, d2_ref):
        dmv = dm_ref[...]
        for g_ref, y_ref, dy_ref, dg_ref in ((g0_ref, ya_ref, da_ref, d0_ref), (g1_ref, yc_ref, dc_ref, d1_ref),
                                             (g2_ref, ys_ref, ds_ref, d2_ref)):
            sg = jax.nn.sigmoid(g_ref[...])
            dy_ref[...] = (dmv * sg).astype(bf16)
            dg_ref[...] = (dmv * y_ref[...] * sg * (1.0 - sg)).astype(bf16)

    yspec = pl.BlockSpec((cfg.TR, gb), lambda i, c: (i, c))
    gspecs = [pl.BlockSpec((cfg.TR, gb), lambda i, c, k=k: (i, base + k * per + c)) for k in range(3)]
    return pl.pallas_call(body, out_shape=(jax.ShapeDtypeStruct((r_, d), bf16),) * 6, grid=(r_ // cfg.TR, per),
                          in_specs=gspecs + [yspec] * 4, out_specs=(yspec,) * 6,
                          compiler_params=_params(2), name=name)(p, p, p, ya, yc, ys, dm)


def _swiglu_fwd(cfg, name, ff):
    r_, f2 = ff.shape
    fh = f2 // 2

    def body(a_ref, b_ref, o_ref):
        a = a_ref[...]
        o_ref[...] = (a * jax.nn.sigmoid(a) * b_ref[...]).astype(bf16)

    return pl.pallas_call(body, out_shape=jax.ShapeDtypeStruct((r_, fh), bf16), grid=(r_ // cfg.TR,),
                          in_specs=[_row_spec(cfg, fh, 0), _row_spec(cfg, fh, 1)], out_specs=_row_spec(cfg, fh),
                          compiler_params=_params(1, VMEM_LIMIT), name=name)(ff, ff)


def _swiglu_bwd(cfg, name, ff, ds):
    r_, f2 = ff.shape
    fh = f2 // 2

    def body(a_ref, b_ref, d_ref, o_ref):
        a = a_ref[...]
        sg = jax.nn.sigmoid(a)
        dsv = d_ref[...]
        o_ref[:, 0:fh] = (dsv * b_ref[...] * (sg * (1.0 + a * (1.0 - sg)))).astype(bf16)
        o_ref[:, fh:f2] = (dsv * a * sg).astype(bf16)

    return pl.pallas_call(body, out_shape=jax.ShapeDtypeStruct((r_, f2), bf16), grid=(r_ // cfg.TR,),
                          in_specs=[_row_spec(cfg, fh, 0), _row_spec(cfg, fh, 1), _row_spec(cfg, fh)],
                          out_specs=_row_spec(cfg, f2), compiler_params=_params(1, VMEM_LIMIT), name=name)(ff, ff, ds)


def _ada_fwd(name, call, w_ada, b_loc):
    nl, d, na = w_ada.shape
    nbp = call.shape[0]

    def body(c_ref, w_ref, b_ref, o_ref):
        cv = c_ref[...]
        a = (cv * jax.nn.sigmoid(cv)).astype(bf16)
        o_ref[0] = jnp.dot(a, w_ref[0].astype(bf16), preferred_element_type=f32) + b_ref[0]

    return pl.pallas_call(
        body, out_shape=jax.ShapeDtypeStruct((nl, nbp, na), f32), grid=(nl,),
        in_specs=[_const_spec(call), pl.BlockSpec((1, d, na), lambda l: (l, 0, 0)),
                  pl.BlockSpec((1, 1, na), lambda l: (l, 0, 0))],
        out_specs=pl.BlockSpec((1, nbp, na), lambda l: (l, 0, 0)),
        compiler_params=_params(1, VMEM_LIMIT), name=name)(call, w_ada, b_loc)


def _ada_bwd(name, call, w_ada, dmod, dmodc, cctx_row):
    nl, d, na = w_ada.shape
    nbp = call.shape[0]

    def body(c_ref, w_ref, dm_ref, dc_ref, gw_ref, gb_ref, gc_ref):
        l = pl.program_id(0)
        cv = c_ref[...]
        sg = jax.nn.sigmoid(cv)
        a = (cv * sg).astype(bf16)
        dctx = jnp.sum(dc_ref[0], axis=0, keepdims=True)
        rows = lax.broadcasted_iota(jnp.int32, (nbp, 1), 0)
        dm = jnp.where(rows == cctx_row, dctx, dm_ref[0])
        gw_ref[0] = lax.dot_general(a, dm.astype(bf16), TN, preferred_element_type=f32)
        gb_ref[0] = jnp.zeros((SUBLANE, na), f32)
        gb_ref[0, 0:1, :] = jnp.sum(dm, axis=0, keepdims=True)
        dc8 = jnp.broadcast_to(dctx, (SUBLANE, na)).astype(bf16)
        part = lax.dot_general(dc8, w_ref[0].astype(bf16), NT_, preferred_element_type=f32)
        cc = c_ref[cctx_row:cctx_row + 1, :]
        sc = jax.nn.sigmoid(cc)
        part = part * (sc * (1.0 + cc * (1.0 - sc)))

        @pl.when(l == 0)
        def _():
            gc_ref[...] = jnp.zeros_like(gc_ref)

        gc_ref[...] += part

    return pl.pallas_call(
        body,
        out_shape=(jax.ShapeDtypeStruct((nl, d, na), f32), jax.ShapeDtypeStruct((nl, SUBLANE, na), f32),
                   jax.ShapeDtypeStruct((SUBLANE, d), f32)),
        grid=(nl,),
        in_specs=[_const_spec(call), pl.BlockSpec((1, d, na), lambda l: (l, 0, 0)),
                  pl.BlockSpec((1, nbp, na), lambda l: (l, 0, 0)),
                  pl.BlockSpec((1, NDEV, na), lambda l: (l, 0, 0))],
        out_specs=(pl.BlockSpec((1, d, na), lambda l: (l, 0, 0)), pl.BlockSpec((1, SUBLANE, na), lambda l: (l, 0, 0)),
                   pl.BlockSpec((SUBLANE, d), lambda l: (0, 0))),
        compiler_params=_params(1, VMEM_LIMIT), name=name)(call, w_ada, dmod, dmodc)


def _adamw(name, w, g, m, v):
    rows, cols = w.shape
    tr = _divisor(rows, max(SUBLANE, (1 << 19) // cols), SUBLANE)
    c1 = 1.0 / (1.0 - ADAM_B1 ** ADAM_STEP)
    c2 = 1.0 / (1.0 - ADAM_B2 ** ADAM_STEP)

    def body(w_ref, g_ref, m_ref, v_ref, d_ref, nm_ref, nv_ref):
        gv = g_ref[...]
        nm = ADAM_B1 * m_ref[...] + (1.0 - ADAM_B1) * gv
        nv = ADAM_B2 * v_ref[...] + (1.0 - ADAM_B2) * (gv * gv)
        nm_ref[...] = nm
        nv_ref[...] = nv
        d_ref[...] = -ADAM_LR * ((nm * c1) / (jnp.sqrt(nv * c2) + ADAM_EPS) + ADAM_WD * w_ref[...])

    spec = pl.BlockSpec((tr, cols), lambda i: (i, 0))
    return pl.pallas_call(body, out_shape=(jax.ShapeDtypeStruct((rows, cols), f32),) * 3, grid=(rows // tr,),
                          in_specs=[spec] * 4, out_specs=(spec,) * 3,
                          compiler_params=_params(1, VMEM_LIMIT), name=name)(w, g, m, v)


def _sum_slots(name, x):
    nd, rows, cols = x.shape
    tr = _divisor(rows, max(SUBLANE, (1 << 18) // cols), SUBLANE)

    def body(x_ref, o_ref):
        acc = x_ref[0].astype(f32)
        for s in range(1, nd):
            acc = acc + x_ref[s].astype(f32)
        o_ref[...] = acc

    return pl.pallas_call(body, out_shape=jax.ShapeDtypeStruct((rows, cols), f32), grid=(rows // tr,),
                          in_specs=[pl.BlockSpec((nd, tr, cols), lambda i: (0, i, 0))],
                          out_specs=pl.BlockSpec((tr, cols), lambda i: (i, 0)),
                          compiler_params=_params(1, VMEM_LIMIT), name=name)(x)


def _peer(k):
    x, y, c = (lax.axis_index(a) for a in AXES)
    px = 1 - x if k & 4 else x
    py = 1 - y if k & 2 else y
    pc = 1 - c if k & 1 else c
    return (px, py, pc), 4 * px + 2 * py + pc


def _exchange_out_shapes(xs, scatter):
    return [jax.ShapeDtypeStruct(xa.shape if scatter else (NDEV,) + xa.shape, xa.dtype) for xa in xs]


def _exchange_scratch(n):
    nrel = NDEV - 1
    return [pltpu.SemaphoreType.DMA((n * nrel,)), pltpu.SemaphoreType.DMA((n * nrel,)), pltpu.SemaphoreType.DMA((n,))]


def _exchange_copies(ins, outs, sems, scatter, receiving):
    send_sems, recv_sems, _ = sems
    nrel = NDEV - 1
    x, y, c = (lax.axis_index(a) for a in AXES)
    me = 4 * x + 2 * y + c
    copies = []
    for a in range(len(ins)):
        for k in range(1, NDEV):
            peer, pidx = _peer(k)
            src = ins[a].at[pidx] if scatter else ins[a]
            copies.append(pltpu.make_async_remote_copy(
                src_ref=src, dst_ref=outs[a].at[pidx if receiving else me], send_sem=send_sems.at[a * nrel + k - 1],
                recv_sem=recv_sems.at[a * nrel + k - 1], device_id=peer, device_id_type=pl.DeviceIdType.MESH))
    return copies


def _exchange_local(ins, outs, sems, scatter):
    x, y, c = (lax.axis_index(a) for a in AXES)
    me = 4 * x + 2 * y + c
    return [pltpu.make_async_copy(ins[a].at[me] if scatter else ins[a], outs[a].at[me], sems[2].at[a])
            for a in range(len(ins))]


def _exchange_start(ins, outs, sems, scatter):
    for cp in _exchange_local(ins, outs, sems, scatter):
        cp.start()
    for cp in _exchange_copies(ins, outs, sems, scatter, False):
        cp.start()


def _exchange_wait(ins, outs, sems, scatter):
    for cp in _exchange_copies(ins, outs, sems, scatter, True):
        cp.wait_recv()
        cp.wait_send()
    for cp in _exchange_local(ins, outs, sems, scatter):
        cp.wait()


def _exchange(name, xs, scatter):
    n = len(xs)

    def body(*refs):
        ins, outs, sems = refs[:n], refs[n:2 * n], refs[2 * n:]
        _exchange_start(ins, outs, sems, scatter)
        _exchange_wait(ins, outs, sems, scatter)

    anyspec = pl.BlockSpec(memory_space=pl.ANY)
    outs = pl.pallas_call(
        body, out_shape=tuple(_exchange_out_shapes(xs, scatter)), in_specs=[anyspec] * n, out_specs=(anyspec,) * n,
        scratch_shapes=_exchange_scratch(n), name=name)(*xs)
    return list(outs)


def _all_gather(name, xs):
    return _exchange(name, xs, False)


def _all_to_all(name, xs):
    return _exchange(name, xs, True)


def _rope_table(cfg):
    s, lc = cfg.S, cfg.LC
    rows = s // GRID_W
    r_ids, c_ids = jnp.meshgrid(jnp.arange(rows), jnp.arange(GRID_W), indexing="ij")
    r_ids = r_ids.reshape(-1).astype(f32)
    c_ids = c_ids.reshape(-1).astype(f32)
    freqs = ROPE_THETA ** (-jnp.arange(0, AXIS_DIM, 2, dtype=f32) / AXIS_DIM)
    ang_r = r_ids[:, None] * freqs
    ang_c = c_ids[:, None] * freqs
    cos = jnp.concatenate([jnp.cos(ang_r), jnp.cos(ang_r), jnp.cos(ang_c), jnp.cos(ang_c)], axis=1)
    sin = jnp.concatenate([-jnp.sin(ang_r), jnp.sin(ang_r), -jnp.sin(ang_c), jnp.sin(ang_c)], axis=1)
    cos = jnp.concatenate([jnp.ones((lc, HEAD_DIM), f32), cos], axis=0)
    sin = jnp.concatenate([jnp.zeros((lc, HEAD_DIM), f32), sin], axis=0)
    reps = QKW // HEAD_DIM
    return jnp.stack([jnp.tile(cos, (1, reps)), jnp.tile(sin, (1, reps))])


def _block_diag_ones():
    idx = np.arange(QKW) // HEAD_DIM
    return jnp.asarray((idx[:, None] == idx[None, :]).astype(np.float32))


def _replicate_matrix():
    src = np.arange(KVW)
    dst = np.arange(AW)
    m = (src[:, None] // HEAD_DIM == dst[None, :] // GW) & (src[:, None] % HEAD_DIM == dst[None, :] % HEAD_DIM)
    return m.astype(np.float32)


def _pack_flat(parts):
    flat = jnp.concatenate([p.reshape(-1) for p in parts])
    n = flat.shape[0]
    unit = SUBLANE * LANE
    total = -(-n // unit) * unit
    flat = jnp.pad(flat, (0, total - n))
    return flat.reshape(total // LANE, LANE)


def _unpack_flat(flat, shapes):
    flat = flat.reshape(flat.shape[:-2] + (-1,))
    out, off = [], 0
    for shp in shapes:
        size = int(np.prod(shp))
        out.append(flat[..., off:off + size].reshape(flat.shape[:-1] + tuple(shp)))
        off += size
    return out


def kernel(x, c, ctx, c_ctx, w_ada, b_ada, w_in, q_norm, k_norm, w_attn_o, conf_dw_w, conf_dw_b, conf_ln_g, conf_ln_b, w_conf_out, sc_dw_w, w_sc_out, w_mix_out, w_ffn_in, w_ffn_out, loss_target, m_c_ctx, m_w_ada, m_b_ada, m_w_in, m_q_norm, m_k_norm, m_w_attn_o, m_conf_dw_w, m_conf_dw_b, m_conf_ln_g, m_conf_ln_b, m_w_conf_out, m_sc_dw_w, m_w_sc_out, m_w_mix_out, m_w_ffn_in, m_w_ffn_out, v_c_ctx, v_w_ada, v_b_ada, v_w_in, v_q_norm, v_k_norm, v_w_attn_o, v_conf_dw_w, v_conf_dw_b, v_conf_ln_g, v_conf_ln_b, v_w_conf_out, v_sc_dw_w, v_w_sc_out, v_w_mix_out, v_w_ffn_in, v_w_ffn_out):
    cfg = _Cfg()
    cfg.B, cfg.S, cfg.D = x.shape
    cfg.LC = ctx.shape[1]
    cfg.T = cfg.LC + cfg.S
    cfg.TR = min(256, cfg.LC)
    assert cfg.LC % cfg.TR == 0 and cfg.S % cfg.TR == 0 and cfg.S % GRID_W == 0
    cfg.NT, cfg.NC = cfg.T // cfg.TR, cfg.LC // cfg.TR
    cfg.R = cfg.B * cfg.T
    nl = w_in.shape[0]
    b, d = cfg.B, cfg.D
    cfg.CW = conf_dw_b.shape[1]
    cfg.K31, cfg.K3 = conf_dw_w.shape[1], sc_dw_w.shape[1]
    assert w_sc_out.shape[1] == cfg.CW and cfg.CW % LANE == 0
    assert cfg.K31 // 2 <= CONV_PAD_ROWS and cfg.K3 // 2 <= SC_PAD_ROWS
    cfg.CB = LANE
    cfg.OFF_CONF = QKVW
    cfg.OFF_SC = cfg.OFF_CONF + 2 * cfg.CW
    cfg.OFF_GATE = cfg.OFF_SC + 3 * cfg.CW
    n_in = w_in.shape[2] * NDEV
    assert n_in == cfg.OFF_GATE + 3 * d and w_attn_o.shape[1] == AW
    cfg.GB = math.gcd(cfg.OFF_GATE, d)
    assert cfg.GB % LANE == 0
    fh = w_ffn_out.shape[1] * NDEV
    na = w_ada.shape[2]
    cw8 = cfg.CW // NDEV

    xi, yi, ci = (lax.axis_index(a) for a in AXES)
    me = 4 * xi + 2 * yi + ci

    small_shapes = [c.shape, conf_dw_w.shape, sc_dw_w.shape]
    (g0,) = _all_gather("gather_small", [_pack_flat([c, conf_dw_w, sc_dw_w])])
    c_all, cw_all, sw_all = _unpack_flat(g0, small_shapes)
    nb = NDEV * b
    nbp = -(-(nb + 1) // SUBLANE) * SUBLANE
    call = jnp.concatenate([c_all.reshape(nb, d), c_ctx[None, :], jnp.zeros((nbp - nb - 1, d), f32)], axis=0)
    w31_full = jnp.moveaxis(cw_all, 0, 2).reshape(nl, cfg.K31, cfg.CW)
    w3_full = jnp.moveaxis(sw_all, 0, 2).reshape(nl, cfg.K3, cfg.CW)
    k31p = -(-(cfg.K31 + 1) // SUBLANE) * SUBLANE
    w31b = jnp.concatenate([w31_full, conf_dw_b[:, None, :], jnp.zeros((nl, k31p - cfg.K31 - 1, cfg.CW), f32)], axis=1)
    w3p = jnp.concatenate([w3_full, jnp.zeros((nl, SUBLANE - cfg.K3, cfg.CW), f32)], axis=1)

    b_loc = lax.dynamic_slice(b_ada, (0, me * na), (nl, na))[:, None, :]
    mod_loc = _ada_fwd("ada_fwd", call, w_ada, b_loc)
    (mod_g,) = _all_gather("gather_mod", [mod_loc.reshape(nl * nbp, na)])
    mod_full = jnp.transpose(mod_g.reshape(NDEV, nl, nbp, na), (1, 2, 0, 3)).reshape(nl, nbp, N_MOD, d)
    mod_lat = lax.dynamic_slice_in_dim(mod_full, me * b, b, axis=1)
    mod_ctx = jnp.broadcast_to(mod_full[:, nb][:, None], (nl, b, N_MOD, d))
    mod = jnp.stack([mod_ctx, mod_lat], axis=2)
    mod = jnp.pad(mod, ((0, 0), (0, 0), (0, 0), (0, MOD_ROWS - N_MOD), (0, 0)))

    cs = _rope_table(cfg)
    bd = _block_diag_ones().astype(bf16)
    rep_np = _replicate_matrix()
    rep = jnp.asarray(rep_np, dtype=bf16)
    rept = jnp.asarray(rep_np.T, dtype=bf16)
    reps_q, reps_k = AW // HEAD_DIM, KVW // HEAD_DIM

    xin = jnp.concatenate([ctx, x], axis=1).reshape(cfg.R, d)
    target = loss_target.reshape(b * cfg.S, d)
    saved = []
    xcur, ocur = xin, None

    def weight_shards(l):
        return [jnp.transpose(w_in[l]).astype(bf16), jnp.transpose(w_attn_o[l]).astype(bf16),
                jnp.transpose(w_conf_out[l]).astype(bf16), jnp.transpose(w_sc_out[l]).astype(bf16),
                w_mix_out[l].astype(bf16), jnp.transpose(w_ffn_in[l]).astype(bf16), w_ffn_out[l].astype(bf16)]

    def whole(g):
        return g.reshape(NDEV * g.shape[1], g.shape[2])

    shards0 = weight_shards(0)
    (w_first,) = _all_gather("gather_w0", shards0[:1])
    w_rest = None
    for l in range(nl):
        win_t = whole(w_first)
        gain = jnp.concatenate([jnp.tile(q_norm[l], reps_q), jnp.tile(k_norm[l], reps_k)])[None, :]
        lnp = jnp.concatenate([conf_ln_g[l][None], conf_ln_b[l][None], jnp.zeros((SUBLANE - 2, cfg.CW), f32)], axis=0)
        nxt = weight_shards(l + 1) if l + 1 < nl else None

        if l == 0:
            x0, h = _norm_fwd(cfg, f"norm_a{l}", xcur, None, None, 0, mod[l], 0)
            p, w_rest = _mm(h, win_t, "nt", f32, f"mm_in{l}", comm=(shards0[1:], False))
        else:
            x0, h = _norm_fwd(cfg, f"norm_a{l}", xcur, ocur, mod[l - 1], 5, mod[l], 0)
            p = _mm(h, win_t, "nt", f32, f"mm_in{l}")
        wao_t, wco_t, wso_t, wmix, wfi_t, wfo = [whole(g) for g in w_rest]
        q, krep, vrep = _qkv_fwd(cfg, f"qkv{l}", p, cs, gain, bd, rep)
        (attn, lse), w_first = _attn_fwd(cfg, f"attn{l}", q, krep, vrep,
                                         comm=None if nxt is None else (nxt[:1], False))
        if nxt is not None:
            (w_first,) = w_first
        hc, z = _conv_fwd(cfg, f"conv{l}", p, w31b[l], w3p[l])
        hs = _ln_silu_fwd(cfg, f"lnsilu{l}", hc, lnp)
        ya = _mm(attn, wao_t, "nt", f32, f"mm_ao{l}")
        yc = _mm(hs, wco_t, "nt", f32, f"mm_co{l}")
        ys = _mm(z, wso_t, "nt", f32, f"mm_so{l}")
        merged = _gate_fwd(cfg, f"gate{l}", p, ya, yc, ys)
        mixed = _mm(merged, wmix, "nn", f32, f"mm_mix{l}")
        x1, h2 = _norm_fwd(cfg, f"norm_b{l}", x0, mixed, mod[l], 2, mod[l], 3)
        if nxt is None:
            ff = _mm(h2, wfi_t, "nt", f32, f"mm_fi{l}")
        else:
            ff, w_rest = _mm(h2, wfi_t, "nt", f32, f"mm_fi{l}", comm=(nxt[1:], False))
        sw = _swiglu_fwd(cfg, f"swiglu{l}", ff)
        o = _mm(sw, wfo, "nn", f32, f"mm_fo{l}")
        saved.append(dict(x0=x0, h=h, p=p, q=q, krep=krep, vrep=vrep, attn=attn, lse=lse, hc=hc, z=z, hs=hs,
                          ya=ya, yc=yc, ys=ys, merged=merged, mixed=mixed, x1=x1, h2=h2, ff=ff, sw=sw, o=o,
                          oprev=ocur, gain=gain, lnp=lnp,
                          w=(win_t, wao_t, wco_t, wso_t, wmix, wfi_t, wfo)))
        xcur, ocur = x1, o

    lsum, dres, do, dm_loss = _loss_head(cfg, "loss", xcur, ocur, mod[nl - 1], 5, target)
    loss = lax.psum((0.5 / d) * jnp.sum(lsum), AXES)

    dmod_rows = [[None] * N_MOD for _ in range(nl)]
    dmod_rows[nl - 1][5] = dm_loss[:, :, 2]
    g_qn, g_kn, g_w31, g_w3, g_ln = [None] * nl, [None] * nl, [None] * nl, [None] * nl, [None] * nl
    gbig = [None] * nl
    pending = None
    pending_early = None

    def blocks(g):
        return g.reshape(NDEV, g.shape[0] // NDEV, g.shape[1])

    def sum_received(l, recv):
        gbig[l] = [_sum_slots(f"sum_g{l}_{i}", r) for i, r in enumerate(recv)]

    for l in reversed(range(nl)):
        sv = saved[l]
        win_t, wao_t, wco_t, wso_t, wmix, wfi_t, wfo = sv["w"]
        ds = _mm(do, wfo, "nt", f32, f"mm_dsw{l}")
        g_wfo = _mm(sv["sw"], do, "tn", bf16, f"mm_gfo{l}")
        dff = _swiglu_bwd(cfg, f"swiglu_b{l}", sv["ff"], ds)
        dh2 = _mm(dff, wfi_t, "nn", f32, f"mm_dh2{l}")
        g_wfi = _mm(dff, sv["h2"], "tn", bf16, f"mm_gfi{l}")
        dx0p, dmixed, dm_b = _norm_bwd(cfg, f"norm_b_b{l}", sv["x1"], sv["mixed"], dh2, dres, mod[l], 2, mod[l], 3)
        dmod_rows[l][3], dmod_rows[l][4], dmod_rows[l][2] = dm_b[:, :, 0], dm_b[:, :, 1], dm_b[:, :, 2]
        dmerged = _mm(dmixed, wmix, "nt", f32, f"mm_dmg{l}")
        g_wmix = _mm(sv["merged"], dmixed, "tn", bf16, f"mm_gmix{l}")
        dya, dyc, dys, dg0, dg1, dg2 = _gate_bwd(cfg, f"gate_b{l}", sv["p"], sv["ya"], sv["yc"], sv["ys"], dmerged)
        dattn = _mm(dya, wao_t, "nn", f32, f"mm_dat{l}")
        g_wao = _mm(dya, sv["attn"], "tn", bf16, f"mm_gao{l}")
        dhs = _mm(dyc, wco_t, "nn", f32, f"mm_dhs{l}")
        g_wco = _mm(dyc, sv["hs"], "tn", bf16, f"mm_gco{l}")
        dz = _mm(dys, wso_t, "nn", f32, f"mm_dz{l}")
        g_wso = _mm(dys, sv["z"], "tn", bf16, f"mm_gso{l}")
        dhc, g_ln[l] = _ln_silu_bwd(cfg, f"lnsilu_b{l}", sv["hc"], dhs, sv["lnp"])
        early = [blocks(g) for g in (g_wao, g_wco, g_wso, g_wmix, g_wfi, g_wfo)]
        (da, dg, dbg, dcg, dxs, g_w31[l], g_w3[l]), recv_early = _conv_bwd(
            cfg, f"conv_b{l}", sv["p"], dhc, dz, w31b[l], w3p[l], comm=(early, True))
        (dq, dkr, dvr), recv = _attn_bwd(cfg, f"attn_b{l}", sv["q"], sv["krep"], sv["vrep"], sv["attn"], sv["lse"],
                                         dattn, comm=None if pending is None else ([pending], True))
        if pending is not None:
            sum_received(l + 1, recv + pending_early)
        pending_early = recv_early
        dpq, dgain = _qkv_bwd(cfg, f"qkv_b{l}", sv["p"], dq, dkr, dvr, cs, sv["gain"], bd, rept)
        g_qn[l] = jnp.sum(dgain[0, :AW].reshape(reps_q, HEAD_DIM), axis=0)
        g_kn[l] = jnp.sum(dgain[0, AW:].reshape(reps_k, HEAD_DIM), axis=0)
        dp = jnp.concatenate([dpq, da, dg, dbg, dcg, dxs, dg0, dg1, dg2], axis=1)
        dh = _mm(dp, win_t, "nn", f32, f"mm_dh{l}")
        g_win = _mm(dp, sv["h"], "tn", bf16, f"mm_gin{l}")
        if l == 0:
            dres, _, dm_a = _norm_bwd(cfg, f"norm_a_b{l}", sv["x0"], None, dh, dx0p, None, 0, mod[l], 0)
        else:
            dres, do, dm_a = _norm_bwd(cfg, f"norm_a_b{l}", sv["x0"], sv["oprev"], dh, dx0p, mod[l - 1], 5, mod[l], 0)
            dmod_rows[l - 1][5] = dm_a[:, :, 2]
        dmod_rows[l][0], dmod_rows[l][1] = dm_a[:, :, 0], dm_a[:, :, 1]

        pending = blocks(g_win)

    sum_received(0, _all_to_all("scatter_g0", [pending]) + pending_early)
    grad_x = dres.reshape(b, cfg.T, d)[:, cfg.LC:, :]

    rb = -(-(b + 1) // SUBLANE) * SUBLANE
    dmod_l = jnp.stack([jnp.stack(rows, axis=2) for rows in dmod_rows])
    d_lat = dmod_l[:, :, 1].reshape(nl, b, NDEV, na)
    d_ctx = jnp.sum(dmod_l[:, :, 0], axis=1).reshape(nl, 1, NDEV, na)
    send = jnp.concatenate([d_lat, d_ctx, jnp.zeros((nl, rb - b - 1, NDEV, na), f32)], axis=1)
    send = jnp.transpose(send, (2, 0, 1, 3)).reshape(NDEV, nl * rb, na)
    (recv,) = _all_to_all("scatter_dmod", [send])
    recv = recv.reshape(NDEV, nl, rb, na)
    dmod_ex = jnp.transpose(recv[:, :, :b], (1, 0, 2, 3)).reshape(nl, nb, na)
    dmod_ex = jnp.pad(dmod_ex, ((0, 0), (0, nbp - nb), (0, 0)))
    dmodc = jnp.transpose(recv[:, :, b], (1, 0, 2))
    g_wada, g_bloc, g_cctx = _ada_bwd("ada_bwd", call, w_ada, dmod_ex, dmodc, nb)

    g_bada = lax.dynamic_update_slice(jnp.zeros((nl, N_MOD * d), f32), g_bloc[:, 0], (0, me * na))
    g_w31s = jnp.stack(g_w31)
    small_parts = [g_cctx[0], g_bada, jnp.stack(g_qn), jnp.stack(g_kn), g_w31s[:, :cfg.K31], g_w31s[:, cfg.K31],
                   jnp.stack([g[0] for g in g_ln]), jnp.stack([g[1] for g in g_ln]),
                   jnp.stack(g_w3)[:, :cfg.K3]]
    small_part_shapes = [p_.shape for p_ in small_parts]
    (gs,) = _all_gather("gather_gsmall", [_pack_flat(small_parts)])
    gsum = _sum_slots("sum_gsmall", gs)
    (gr_cctx, gr_bada, gr_qn, gr_kn, gr_w31, gr_b31, gr_lng, gr_lnb, gr_w3) = _unpack_flat(gsum, small_part_shapes)
    gr_w31 = lax.dynamic_slice_in_dim(gr_w31, me * cw8, cw8, axis=2)
    gr_w3 = lax.dynamic_slice_in_dim(gr_w3, me * cw8, cw8, axis=2)

    def big(i, transposed):
        g = jnp.stack([gbig[l][i] for l in range(nl)])
        return jnp.transpose(g, (0, 2, 1)) if transposed else g

    grads = {
        "c_ctx": gr_cctx, "w_ada": g_wada, "b_ada": gr_bada, "w_in": big(0, True), "q_norm": gr_qn, "k_norm": gr_kn,
        "w_attn_o": big(1, True), "conf_dw_w": gr_w31, "conf_dw_b": gr_b31, "conf_ln_g": gr_lng, "conf_ln_b": gr_lnb,
        "w_conf_out": big(2, True), "sc_dw_w": gr_w3, "w_sc_out": big(3, True), "w_mix_out": big(4, False),
        "w_ffn_in": big(5, True), "w_ffn_out": big(6, False)}
    weights = dict(c_ctx=c_ctx, w_ada=w_ada, b_ada=b_ada, w_in=w_in, q_norm=q_norm, k_norm=k_norm, w_attn_o=w_attn_o,
                   conf_dw_w=conf_dw_w, conf_dw_b=conf_dw_b, conf_ln_g=conf_ln_g, conf_ln_b=conf_ln_b,
                   w_conf_out=w_conf_out, sc_dw_w=sc_dw_w, w_sc_out=w_sc_out, w_mix_out=w_mix_out,
                   w_ffn_in=w_ffn_in, w_ffn_out=w_ffn_out)
    m_in = dict(c_ctx=m_c_ctx, w_ada=m_w_ada, b_ada=m_b_ada, w_in=m_w_in, q_norm=m_q_norm, k_norm=m_k_norm,
                w_attn_o=m_w_attn_o, conf_dw_w=m_conf_dw_w, conf_dw_b=m_conf_dw_b, conf_ln_g=m_conf_ln_g,
                conf_ln_b=m_conf_ln_b, w_conf_out=m_w_conf_out, sc_dw_w=m_sc_dw_w, w_sc_out=m_w_sc_out,
                w_mix_out=m_w_mix_out, w_ffn_in=m_w_ffn_in, w_ffn_out=m_w_ffn_out)
    v_in = dict(c_ctx=v_c_ctx, w_ada=v_w_ada, b_ada=v_b_ada, w_in=v_w_in, q_norm=v_q_norm, k_norm=v_k_norm,
                w_attn_o=v_w_attn_o, conf_dw_w=v_conf_dw_w, conf_dw_b=v_conf_dw_b, conf_ln_g=v_conf_ln_g,
                conf_ln_b=v_conf_ln_b, w_conf_out=v_w_conf_out, sc_dw_w=v_sc_dw_w, w_sc_out=v_w_sc_out,
                w_mix_out=v_w_mix_out, w_ffn_in=v_w_ffn_in, w_ffn_out=v_w_ffn_out)
    names = list(weights)
    big_names = ("w_ada", "w_in", "w_attn_o", "w_conf_out", "w_sc_out", "w_mix_out", "w_ffn_in", "w_ffn_out")
    small_names = [n for n in names if n not in big_names]
    delta, new_m, new_v = {}, {}, {}
    for n in big_names:
        shp = weights[n].shape
        two_d = (shp[0] * shp[1], shp[2])
        dl, nm, nv = _adamw(f"adamw_{n}", weights[n].reshape(two_d), grads[n].reshape(two_d),
                            m_in[n].reshape(two_d), v_in[n].reshape(two_d))
        delta[n], new_m[n], new_v[n] = dl.reshape(shp), nm.reshape(shp), nv.reshape(shp)
    sshapes = [weights[n].shape for n in small_names]
    dl, nm, nv = _adamw("adamw_small", _pack_flat([weights[n] for n in small_names]),
                        _pack_flat([grads[n] for n in small_names]), _pack_flat([m_in[n] for n in small_names]),
                        _pack_flat([v_in[n] for n in small_names]))
    for n, a_, b_, c_ in zip(small_names, _unpack_flat(dl, sshapes), _unpack_flat(nm, sshapes), _unpack_flat(nv, sshapes)):
        delta[n], new_m[n], new_v[n] = a_, b_, c_

    return (loss, grad_x, *[grads[n] for n in names], *[delta[n] for n in names],
            *[new_m[n] for n in names], *[new_v[n] for n in names])
```

```python
import functools
import math

import numpy as np
import jax
import jax.numpy as jnp
from jax import lax
from jax.experimental import pallas as pl
from jax.experimental.pallas import tpu as pltpu

f32 = jnp.float32
bf16 = jnp.bfloat16
ACT = bf16

NDEV = 8
AXES = ("x", "y", "c")
HEAD_DIM = 64
N_Q_HEADS = 8
N_KV_HEADS = 2
Q_GROUP = N_Q_HEADS // N_KV_HEADS
AW = N_Q_HEADS * HEAD_DIM
KVW = N_KV_HEADS * HEAD_DIM
GW = Q_GROUP * HEAD_DIM
QKW = AW + KVW
QKVW = AW + 2 * KVW
GRID_W = 64
AXIS_DIM = HEAD_DIM // 2
ROPE_THETA = 10000.0
ATTN_SCALE = HEAD_DIM ** -0.5
EPS = 1e-6
N_MOD = 6
MOD_ROWS = 8
CONV_PAD_ROWS = 16
SC_PAD_ROWS = 8
GATE_ROWS = 1024
LANE = 128
SUBLANE = 8
VMEM_LIMIT = 56 * 1024 * 1024

ADAM_LR = 0.001
ADAM_B1 = 0.9
ADAM_B2 = 0.999
ADAM_EPS = 1e-08
ADAM_WD = 0.01
ADAM_STEP = 10

NN =(((1,), (0,)), ((), ()))
NT_ = (((1,), (1,)), ((), ()))
TN = (((0,), (0,)), ((), ()))


def _params(ndims, vmem=None):
    return pltpu.CompilerParams(dimension_semantics=("arbitrary",) * ndims, vmem_limit_bytes=vmem)


def _divisor(n, cap, mult):
    best = None
    for d in range(mult, min(n, cap) + 1, mult):
        if n % d == 0:
            best = d
    return n if best is None else best


def _const_spec(a):
    nd = a.ndim
    return pl.BlockSpec(a.shape, lambda *idx: (0,) * nd)


class _Cfg:
    pass


def _mm(a, b, mode, out_dtype, name, comm=None):
    if mode == "tn":
        m, ka = a.shape
        nb = b.shape[1]
        tka = _divisor(ka, 1408, LANE)
        tnb = _divisor(nb, 1024, LANE)
        tmr = _divisor(m, 1024, SUBLANE)
        nsteps = m // tmr

        def body(a_ref, b_ref, o_ref, acc_ref):
            k = pl.program_id(2)

            @pl.when(k == 0)
            def _():
                acc_ref[...] = jnp.zeros_like(acc_ref)

            acc_ref[...] += lax.dot_general(a_ref[...].astype(bf16), b_ref[...].astype(bf16), TN,
                                            preferred_element_type=f32)

            @pl.when(k == nsteps - 1)
            def _():
                o_ref[...] = acc_ref[...].astype(out_dtype)

        return pl.pallas_call(
            body, out_shape=jax.ShapeDtypeStruct((ka, nb), out_dtype),
            grid=(ka // tka, nb // tnb, nsteps),
            in_specs=[pl.BlockSpec((tmr, tka), lambda i, j, k: (k, i)),
                      pl.BlockSpec((tmr, tnb), lambda i, j, k: (k, j))],
            out_specs=pl.BlockSpec((tka, tnb), lambda i, j, k: (i, j)),
            scratch_shapes=[pltpu.VMEM((tka, tnb), f32)],
            compiler_params=_params(3, VMEM_LIMIT), name=name)(a, b)

    m, kdim = a.shape
    n = b.shape[1] if mode == "nn" else b.shape[0]
    tm = _divisor(m, 1024 if a.dtype == bf16 else 512, SUBLANE)
    tn = _divisor(n, 1408, LANE)
    tk = _divisor(kdim, 1408, LANE)
    nsteps = kdim // tk
    dims = NN if mode == "nn" else NT_

    def body(a_ref, b_ref, o_ref, acc_ref):
        k = pl.program_id(2)

        @pl.when(k == 0)
        def _():
            acc_ref[...] = jnp.zeros_like(acc_ref)

        acc_ref[...] += lax.dot_general(a_ref[...].astype(bf16), b_ref[...].astype(bf16), dims,
                                        preferred_element_type=f32)

        @pl.when(k == nsteps - 1)
        def _():
            o_ref[...] = acc_ref[...].astype(out_dtype)

    if mode == "nn":
        b_spec = pl.BlockSpec((tk, tn), lambda i, j, k: (k, j))
    else:
        b_spec = pl.BlockSpec((tn, tk), lambda i, j, k: (j, k))
    (out,), couts = _hosting_call(
        body, comm, out_shape=(jax.ShapeDtypeStruct((m, n), out_dtype),),
        grid=(m // tm, n // tn, nsteps),
        in_specs=[pl.BlockSpec((tm, tk), lambda i, j, k: (i, k)), b_spec],
        out_specs=(pl.BlockSpec((tm, tn), lambda i, j, k: (i, j)),),
        scratch_shapes=[pltpu.VMEM((tm, tn), f32)],
        compiler_params=_params(3, VMEM_LIMIT), name=name, args=(a, b))
    return out if comm is None else (out, couts)


def _row_spec(cfg, width, col=0):
    return pl.BlockSpec((cfg.TR, width), lambda i: (i, col))


def _mod_spec(cfg):
    nt, nc = cfg.NT, cfg.NC
    return pl.BlockSpec((1, 1, MOD_ROWS, cfg.D),
                        lambda i: (i // nt, ((i % nt) >= nc).astype(jnp.int32), 0, 0))


def _segment_start(cfg, i):
    j = i % cfg.NT
    return jnp.logical_or(j == 0, j == cfg.NC)


def _norm_fwd(cfg, name, xin, o, modg, gk, modn, sk):
    has_o = o is not None
    r_, d = xin.shape

    def body(*refs):
        if has_o:
            x_ref, o_ref, mg_ref, mn_ref, xn_ref, h_ref = refs
            x = x_ref[...] + mg_ref[0, 0, gk:gk + 1, :] * o_ref[...]
            xn_ref[...] = x
        else:
            x_ref, mn_ref, h_ref = refs
            x = x_ref[...]
        r = lax.rsqrt(jnp.mean(x * x, axis=-1, keepdims=True) + EPS)
        h = (x * r) * (1.0 + mn_ref[0, 0, sk + 1:sk + 2, :]) + mn_ref[0, 0, sk:sk + 1, :]
        h_ref[...] = h.astype(bf16)

    row = _row_spec(cfg, d)
    if has_o:
        ins, in_specs = (xin, o, modg, modn), [row, row, _mod_spec(cfg), _mod_spec(cfg)]
        out_shape = (jax.ShapeDtypeStruct((r_, d), f32), jax.ShapeDtypeStruct((r_, d), bf16))
        out_specs = (row, row)
    else:
        ins, in_specs = (xin, modn), [row, _mod_spec(cfg)]
        out_shape = jax.ShapeDtypeStruct((r_, d), bf16)
        out_specs = row
    res = pl.pallas_call(body, out_shape=out_shape, grid=(r_ // cfg.TR,), in_specs=in_specs,
                         out_specs=out_specs, compiler_params=_params(1), name=name)(*ins)
    return res if has_o else (xin, res)


def _norm_bwd(cfg, name, xnew, o, dh, dres, modg, gk, modn, sk):
    has_o = o is not None
    r_, d = xnew.shape

    def body(*refs):
        if has_o:
            xn_ref, o_ref, dh_ref, dr_ref, mg_ref, mn_ref, dx_ref, do_ref, dm_ref = refs
        else:
            xn_ref, dh_ref, dr_ref, mn_ref, dx_ref, dm_ref = refs
        i = pl.program_id(0)
        x = xn_ref[...]
        dhv = dh_ref[...].astype(f32)
        r = lax.rsqrt(jnp.mean(x * x, axis=-1, keepdims=True) + EPS)
        xh = x * r
        dxh = dhv * (1.0 + mn_ref[0, 0, sk + 1:sk + 2, :])
        dx = r * (dxh - xh * jnp.mean(dxh * xh, axis=-1, keepdims=True)) + dr_ref[...]
        dx_ref[...] = dx

        @pl.when(_segment_start(cfg, i))
        def _():
            dm_ref[...] = jnp.zeros_like(dm_ref)

        dm_ref[0, 0, 0:1, :] += jnp.sum(dhv, axis=0, keepdims=True)
        dm_ref[0, 0, 1:2, :] += jnp.sum(dhv * xh, axis=0, keepdims=True)
        if has_o:
            ov = o_ref[...]
            dm_ref[0, 0, 2:3, :] += jnp.sum(dx * ov, axis=0, keepdims=True)
            do_ref[...] = (dx * mg_ref[0, 0, gk:gk + 1, :]).astype(bf16)

    row = _row_spec(cfg, d)
    dm_shape = jax.ShapeDtypeStruct((cfg.B, 2, MOD_ROWS, d), f32)
    if has_o:
        ins = (xnew, o, dh, dres, modg, modn)
        in_specs = [row, row, row, row, _mod_spec(cfg), _mod_spec(cfg)]
        out_shape = (jax.ShapeDtypeStruct((r_, d), f32), jax.ShapeDtypeStruct((r_, d), bf16), dm_shape)
        out_specs = (row, row, _mod_spec(cfg))
    else:
        ins = (xnew, dh, dres, modn)
        in_specs = [row, row, row, _mod_spec(cfg)]
        out_shape = (jax.ShapeDtypeStruct((r_, d), f32), dm_shape)
        out_specs = (row, _mod_spec(cfg))
    res = pl.pallas_call(body, out_shape=out_shape, grid=(r_ // cfg.TR,), in_specs=in_specs,
                         out_specs=out_specs, compiler_params=_params(1), name=name)(*ins)
    if has_o:
        return res
    return res[0], None, res[1]


def _loss_head(cfg, name, x1, o, modg, gk, target):
    r_, d = x1.shape
    nt, nc = cfg.NT, cfg.NC
    nlat = nt - nc

    def body(x_ref, o_ref, mg_ref, t_ref, ls_ref, dy_ref, do_ref, dm_ref):
        i = pl.program_id(0)
        lat = (i % nt) >= nc
        gate = mg_ref[0, 0, gk:gk + 1, :]
        ov = o_ref[...]
        err = jnp.where(lat, x_ref[...] + gate * ov - t_ref[...], 0.0)
        ls_ref[...] = jnp.zeros_like(ls_ref)
        ls_ref[0, 0:1, :] = jnp.sum(err * err, axis=0, keepdims=True)
        dy = err * (1.0 / d)
        dy_ref[...] = dy
        do_ref[...] = (dy * gate).astype(bf16)

        @pl.when(_segment_start(cfg, i))
        def _():
            dm_ref[...] = jnp.zeros_like(dm_ref)

        dm_ref[0, 0, 2:3, :] += jnp.sum(dy * ov, axis=0, keepdims=True)

    row = _row_spec(cfg, d)
    t_spec = pl.BlockSpec((cfg.TR, d), lambda i: ((i // nt) * nlat + jnp.maximum((i % nt) - nc, 0), 0))
    return pl.pallas_call(
        body,
        out_shape=(jax.ShapeDtypeStruct((r_ // cfg.TR, SUBLANE, d), f32), jax.ShapeDtypeStruct((r_, d), f32),
                   jax.ShapeDtypeStruct((r_, d), bf16), jax.ShapeDtypeStruct((cfg.B, 2, MOD_ROWS, d), f32)),
        grid=(r_ // cfg.TR,),
        in_specs=[row, row, _mod_spec(cfg), t_spec],
        out_specs=(pl.BlockSpec((1, SUBLANE, d), lambda i: (i, 0, 0)), row, row, _mod_spec(cfg)),
        compiler_params=_params(1), name=name)(x1, o, modg, target)


def _swap16(y):
    w = y.shape[1]
    lane = lax.broadcasted_iota(jnp.int32, (1, w), 1)
    lo = jnp.bitwise_and(lane, 31) < 16
    return jnp.where(lo, pltpu.roll(y, w - 16, 1), pltpu.roll(y, 16, 1))


def _dot_split(v, m):
    hi = v.astype(bf16)
    lo = (v - hi.astype(f32)).astype(bf16)
    return jnp.dot(hi, m, preferred_element_type=f32) + jnp.dot(lo, m, preferred_element_type=f32)


def _head_sum(v, bd):
    return _dot_split(v, bd)


def _qkv_fwd(cfg, name, p, cs, gain, bd, rep):
    r_ = p.shape[0]
    nt = cfg.NT

    def body(p_ref, cs_ref, g_ref, bd_ref, rep_ref, q_ref, k_ref, v_ref):
        x = p_ref[:, 0:QKW].astype(f32)
        r = lax.rsqrt(_head_sum(x * x, bd_ref[...]) * (1.0 / HEAD_DIM) + EPS)
        y = (x * r) * g_ref[...]
        out = y * cs_ref[0] + _swap16(y) * cs_ref[1]
        q_ref[...] = out[:, 0:AW].astype(bf16)
        kr = out[:, AW:QKW].astype(bf16)
        k_ref[...] = jnp.dot(kr, rep_ref[...], preferred_element_type=f32).astype(bf16)
        vv = p_ref[:, QKW:QKVW].astype(bf16)
        v_ref[...] = jnp.dot(vv, rep_ref[...], preferred_element_type=f32).astype(bf16)

    row = _row_spec(cfg, AW)
    return pl.pallas_call(
        body, out_shape=(jax.ShapeDtypeStruct((r_, AW), bf16),) * 3, grid=(r_ // cfg.TR,),
        in_specs=[_row_spec(cfg, QKVW), pl.BlockSpec((2, cfg.TR, QKW), lambda i: (0, i % nt, 0)),
                  _const_spec(gain), _const_spec(bd), _const_spec(rep)],
        out_specs=(row, row, row), compiler_params=_params(1), name=name)(p, cs, gain, bd, rep)


def _qkv_bwd(cfg, name, p, dq, dkr, dvr, cs, gain, bd, rept):
    r_ = p.shape[0]
    nt = cfg.NT

    def body(p_ref, dq_ref, dk_ref, dv_ref, cs_ref, g_ref, bd_ref, rt_ref, dp_ref, dg_ref):
        i = pl.program_id(0)
        x = p_ref[:, 0:QKW].astype(f32)
        rt = rt_ref[...]
        dk = _dot_split(dk_ref[...], rt)
        g = jnp.concatenate([dq_ref[...], dk], axis=1)
        gy = g * cs_ref[0] + _swap16(g * cs_ref[1])
        bdv = bd_ref[...]
        r = lax.rsqrt(_head_sum(x * x, bdv) * (1.0 / HEAD_DIM) + EPS)
        xh = x * r
        dxh = gy * g_ref[...]
        dx = r * (dxh - xh * (_head_sum(dxh * xh, bdv) * (1.0 / HEAD_DIM)))
        dp_ref[:, 0:QKW] = dx.astype(bf16)
        dp_ref[:, QKW:QKVW] = _dot_split(dv_ref[...], rt).astype(bf16)

        @pl.when(i == 0)
        def _():
            dg_ref[...] = jnp.zeros_like(dg_ref)

        dg_ref[0:1, :] += jnp.sum(gy * xh, axis=0, keepdims=True)

    row = _row_spec(cfg, AW)
    return pl.pallas_call(
        body, out_shape=(jax.ShapeDtypeStruct((r_, QKVW), bf16), jax.ShapeDtypeStruct((SUBLANE, QKW), f32)),
        grid=(r_ // cfg.TR,),
        in_specs=[_row_spec(cfg, QKVW), row, row, row,
                  pl.BlockSpec((2, cfg.TR, QKW), lambda i: (0, i % nt, 0)),
                  _const_spec(gain), _const_spec(bd), _const_spec(rept)],
        out_specs=(_row_spec(cfg, QKVW), pl.BlockSpec((SUBLANE, QKW), lambda i: (0, 0))),
        compiler_params=_params(1), name=name)(p, dq, dkr, dvr, cs, gain, bd, rept)


def _head_masks():
    lane = lax.broadcasted_iota(jnp.int32, (1, GW), 1)
    return [jnp.logical_and(lane >= HEAD_DIM * h, lane < HEAD_DIM * (h + 1)) for h in range(Q_GROUP)]


def _attn_fwd(cfg, name, q, krep, vrep, comm=None):
    r_ = q.shape[0]
    tr, t, lc, nt, nc = cfg.TR, cfg.T, cfg.LC, cfg.NT, cfg.NC

    def body(q_ref, k_ref, v_ref, o_ref, l_ref):
        j = pl.program_id(2)
        masks = _head_masks()
        lane = lax.broadcasted_iota(jnp.int32, (1, LANE), 1)

        def run(nk):
            qv = q_ref[...]
            kv = k_ref[0:nk, :]
            vv = v_ref[0:nk, :]
            acc = jnp.zeros((tr, GW), f32)
            lse = jnp.zeros((tr, LANE), f32)
            for h in range(Q_GROUP):
                qh = jnp.where(masks[h], qv, jnp.zeros_like(qv))
                s = lax.dot_general(qh, kv, NT_, preferred_element_type=f32) * ATTN_SCALE
                m = jnp.max(s, axis=1, keepdims=True)
                pr = jnp.exp(s - m)
                l = jnp.sum(pr, axis=1, keepdims=True)
                oh = jnp.dot(pr.astype(bf16), vv, preferred_element_type=f32) / l
                acc = acc + jnp.where(masks[h], oh, 0.0)
                lse = lse + jnp.where(lane == h, m + jnp.log(l), 0.0)
            o_ref[...] = acc
            l_ref[...] = lse

        @pl.when(j < nc)
        def _():
            run(lc)

        @pl.when(j >= nc)
        def _():
            run(t)

    qspec = pl.BlockSpec((tr, GW), lambda e, g, j: (e * nt + j, g))
    kspec = pl.BlockSpec((t, GW), lambda e, g, j: (e, g))
    return _hosting_call(
        body, comm, out_shape=(jax.ShapeDtypeStruct((r_, AW), f32), jax.ShapeDtypeStruct((r_, N_KV_HEADS * LANE), f32)),
        grid=(cfg.B, N_KV_HEADS, nt), in_specs=[qspec, kspec, kspec],
        out_specs=(qspec, pl.BlockSpec((tr, LANE), lambda e, g, j: (e * nt + j, g))),
        compiler_params=_params(3, VMEM_LIMIT), name=name, args=(q, krep, vrep))


def _hosting_call(body, comm, *, out_shape, grid, in_specs, out_specs, compiler_params, name, args, scratch_shapes=()):
    if comm is None:
        return pl.pallas_call(body, out_shape=out_shape, grid=grid, in_specs=in_specs, out_specs=out_specs,
                              scratch_shapes=list(scratch_shapes), compiler_params=compiler_params,
                              name=name)(*args), None
    xs, scatter = comm
    n, n_in, n_out, n_scr = len(xs), len(args), len(out_shape), len(scratch_shapes)
    last = [g - 1 for g in grid]

    def hosted(*refs):
        ins, cins = refs[:n_in], refs[n_in:n_in + n]
        outs, couts = refs[n_in + n:n_in + n + n_out], refs[n_in + n + n_out:n_in + 2 * n + n_out]
        scratch = refs[n_in + 2 * n + n_out:n_in + 2 * n + n_out + n_scr]
        sems = refs[n_in + 2 * n + n_out + n_scr:]
        ids = [pl.program_id(a) for a in range(len(grid))]
        is_first = functools.reduce(jnp.logical_and, [i == 0 for i in ids])
        is_last = functools.reduce(jnp.logical_and, [i == l for i, l in zip(ids, last)])

        @pl.when(is_first)
        def _():
            _exchange_start(cins, couts, sems, scatter)

        body(*ins, *outs, *scratch)

        @pl.when(is_last)
        def _():
            _exchange_wait(cins, couts, sems, scatter)

    anyspec = pl.BlockSpec(memory_space=pl.ANY)
    res = pl.pallas_call(
        hosted, out_shape=tuple(out_shape) + tuple(_exchange_out_shapes(xs, scatter)), grid=grid,
        in_specs=list(in_specs) + [anyspec] * n, out_specs=tuple(out_specs) + (anyspec,) * n,
        scratch_shapes=list(scratch_shapes) + _exchange_scratch(n), compiler_params=compiler_params,
        name=name)(*args, *xs)
    return tuple(res[:n_out]), list(res[n_out:])


def _attn_bwd(cfg, name, q, krep, vrep, o, lse, do, comm=None):
    r_ = q.shape[0]
    tr, t, lc, nt, nc = cfg.TR, cfg.T, cfg.LC, cfg.NT, cfg.NC

    def body(q_ref, k_ref, v_ref, o_ref, l_ref, do_ref, dq_ref, dk_ref, dv_ref):
        j = pl.program_id(2)
        masks = _head_masks()
        lane = lax.broadcasted_iota(jnp.int32, (1, LANE), 1)

        @pl.when(j == 0)
        def _():
            dk_ref[...] = jnp.zeros_like(dk_ref)
            dv_ref[...] = jnp.zeros_like(dv_ref)

        def run(nk):
            qv = q_ref[...]
            kv = k_ref[0:nk, :]
            vv = v_ref[0:nk, :]
            ov = o_ref[...]
            dov = do_ref[...]
            lv = l_ref[...]
            dq = jnp.zeros((tr, GW), f32)
            dk = jnp.zeros((nk, GW), f32)
            dv = jnp.zeros((nk, GW), f32)
            for h in range(Q_GROUP):
                qh = jnp.where(masks[h], qv, jnp.zeros_like(qv))
                doh = jnp.where(masks[h], dov, 0.0)
                dohb = doh.astype(bf16)
                delta = jnp.sum(doh * ov, axis=1, keepdims=True)
                lse_h = jnp.sum(jnp.where(lane == h, lv, 0.0), axis=1, keepdims=True)
                s = lax.dot_general(qh, kv, NT_, preferred_element_type=f32) * ATTN_SCALE
                pr = jnp.exp(s - lse_h)
                dpr = lax.dot_general(dohb, vv, NT_, preferred_element_type=f32)
                ds = (pr * (dpr - delta) * ATTN_SCALE).astype(bf16)
                dq = dq + jnp.where(masks[h], jnp.dot(ds, kv, preferred_element_type=f32), 0.0)
                dk = dk + lax.dot_general(ds, qh, TN, preferred_element_type=f32)
                dv = dv + lax.dot_general(pr.astype(bf16), dohb, TN, preferred_element_type=f32)
            dq_ref[...] = dq
            dk_ref[0:nk, :] += dk
            dv_ref[0:nk, :] += dv

        @pl.when(j < nc)
        def _():
            run(lc)

        @pl.when(j >= nc)
        def _():
            run(t)

    qspec = pl.BlockSpec((tr, GW), lambda e, g, j: (e * nt + j, g))
    kspec = pl.BlockSpec((t, GW), lambda e, g, j: (e, g))
    lspec = pl.BlockSpec((tr, LANE), lambda e, g, j: (e * nt + j, g))
    return _hosting_call(
        body, comm, out_shape=(jax.ShapeDtypeStruct((r_, AW), f32),) * 3,
        grid=(cfg.B, N_KV_HEADS, nt), in_specs=[qspec, kspec, kspec, qspec, lspec, qspec],
        out_specs=(qspec, kspec, kspec),
        compiler_params=_params(3, VMEM_LIMIT), name=name, args=(q, krep, vrep, o, lse, do))


def _padded(x, pad_rows):
    z = jnp.zeros((pad_rows, x.shape[1]), f32)
    return jnp.concatenate([z, x, z], axis=0)


def _shifted(buf, shift):
    n = buf.shape[0]
    shift = shift % n
    return buf if shift == 0 else pltpu.roll(buf, shift, 0)


def _dwconv(x, w_ref, taps, pad_rows):
    n = x.shape[0]
    buf = _padded(x, pad_rows)
    acc = jnp.zeros_like(buf)
    for k in range(taps):
        acc = acc + w_ref[k:k + 1, :] * _shifted(buf, taps // 2 - k)
    return acc[pad_rows:pad_rows + n]


def _dwconv_t(dy, w_ref, taps, pad_rows):
    n = dy.shape[0]
    buf = _padded(dy, pad_rows)
    acc = jnp.zeros_like(buf)
    for k in range(taps):
        acc = acc + w_ref[k:k + 1, :] * _shifted(buf, k - taps // 2)
    return acc[pad_rows:pad_rows + n]


def _dwconv_dw(x, dy, dw_ref, taps, pad_rows):
    n = x.shape[0]
    buf = _padded(x, pad_rows)
    for k in range(taps):
        xs = _shifted(buf, taps // 2 - k)[pad_rows:pad_rows + n]
        dw_ref[k:k + 1, :] += jnp.sum(xs * dy, axis=0, keepdims=True)


def _conv_specs(cfg):
    t, cb, cw = cfg.T, cfg.CB, cfg.CW
    offs = (cfg.OFF_CONF, cfg.OFF_CONF + cw, cfg.OFF_SC, cfg.OFF_SC + cw, cfg.OFF_SC + 2 * cw)
    return [pl.BlockSpec((t, cb), lambda c, e, off=off: (e, off // cb + c)) for off in offs]


def _conv_fwd(cfg, name, p, w31, w3):
    r_ = p.shape[0]
    t, lc, cb, cw = cfg.T, cfg.LC, cfg.CB, cfg.CW
    k31, k3 = cfg.K31, cfg.K3

    def body(a_ref, g_ref, bg_ref, cg_ref, xs_ref, w31_ref, w3_ref, hc_ref, z_ref):
        for lo, hi in ((0, lc), (lc, t)):
            hh = a_ref[lo:hi, :].astype(f32) * jax.nn.sigmoid(g_ref[lo:hi, :].astype(f32))
            hc_ref[lo:hi, :] = _dwconv(hh, w31_ref, k31, CONV_PAD_ROWS) + w31_ref[k31:k31 + 1, :]
            u = cg_ref[lo:hi, :].astype(f32) * xs_ref[lo:hi, :].astype(f32)
            z_ref[lo:hi, :] = (bg_ref[lo:hi, :].astype(f32) * _dwconv(u, w3_ref, k3, SC_PAD_ROWS)).astype(bf16)

    ospec = pl.BlockSpec((t, cb), lambda c, e: (e, c))
    return pl.pallas_call(
        body, out_shape=(jax.ShapeDtypeStruct((r_, cw), f32), jax.ShapeDtypeStruct((r_, cw), bf16)),
        grid=(cw // cb, cfg.B),
        in_specs=_conv_specs(cfg) + [pl.BlockSpec((w31.shape[0], cb), lambda c, e: (0, c)),
                                     pl.BlockSpec((w3.shape[0], cb), lambda c, e: (0, c))],
        out_specs=(ospec, ospec), compiler_params=_params(2, VMEM_LIMIT), name=name)(p, p, p, p, p, w31, w3)


def _conv_bwd(cfg, name, p, dhc, dz, w31, w3, comm=None):
    r_ = p.shape[0]
    t, lc, cb, cw = cfg.T, cfg.LC, cfg.CB, cfg.CW
    k31, k3 = cfg.K31, cfg.K3

    def body(a_ref, g_ref, bg_ref, cg_ref, xs_ref, dhc_ref, dz_ref, w31_ref, w3_ref,
             da_ref, dg_ref, dbg_ref, dcg_ref, dxs_ref, dw31_ref, dw3_ref):
        e = pl.program_id(1)

        @pl.when(e == 0)
        def _():
            dw31_ref[...] = jnp.zeros_like(dw31_ref)
            dw3_ref[...] = jnp.zeros_like(dw3_ref)

        for lo, hi in ((0, lc), (lc, t)):
            a = a_ref[lo:hi, :].astype(f32)
            sg = jax.nn.sigmoid(g_ref[lo:hi, :].astype(f32))
            hh = a * sg
            dy = dhc_ref[lo:hi, :]
            dhh = _dwconv_t(dy, w31_ref, k31, CONV_PAD_ROWS)
            _dwconv_dw(hh, dy, dw31_ref, k31, CONV_PAD_ROWS)
            dw31_ref[k31:k31 + 1, :] += jnp.sum(dy, axis=0, keepdims=True)
            da_ref[lo:hi, :] = (dhh * sg).astype(bf16)
            dg_ref[lo:hi, :] = (dhh * hh * (1.0 - sg)).astype(bf16)

            cg = cg_ref[lo:hi, :].astype(f32)
            xs = xs_ref[lo:hi, :].astype(f32)
            u = cg * xs
            dzv = dz_ref[lo:hi, :]
            dbg_ref[lo:hi, :] = (dzv * _dwconv(u, w3_ref, k3, SC_PAD_ROWS)).astype(bf16)
            duc = dzv * bg_ref[lo:hi, :].astype(f32)
            du = _dwconv_t(duc, w3_ref, k3, SC_PAD_ROWS)
            _dwconv_dw(u, duc, dw3_ref, k3, SC_PAD_ROWS)
            dcg_ref[lo:hi, :] = (du * xs).astype(bf16)
            dxs_ref[lo:hi, :] = (du * cg).astype(bf16)

    ospec = pl.BlockSpec((t, cb), lambda c, e: (e, c))
    w31_spec = pl.BlockSpec((w31.shape[0], cb), lambda c, e: (0, c))
    w3_spec = pl.BlockSpec((w3.shape[0], cb), lambda c, e: (0, c))
    return _hosting_call(
        body, comm,
        out_shape=(jax.ShapeDtypeStruct((r_, cw), bf16),) * 5
        + (jax.ShapeDtypeStruct(w31.shape, f32), jax.ShapeDtypeStruct(w3.shape, f32)),
        grid=(cw // cb, cfg.B),
        in_specs=_conv_specs(cfg) + [ospec, ospec, w31_spec, w3_spec],
        out_specs=(ospec,) * 5 + (w31_spec, w3_spec),
        compiler_params=_params(2, VMEM_LIMIT), name=name, args=(p, p, p, p, p, dhc, dz, w31, w3))


def _ln_silu_fwd(cfg, name, hc, lnp):
    r_, cw = hc.shape

    def body(x_ref, p_ref, o_ref):
        x = x_ref[...]
        mu = jnp.mean(x, axis=-1, keepdims=True)
        xc = x - mu
        rs = lax.rsqrt(jnp.mean(xc * xc, axis=-1, keepdims=True) + EPS)
        hn = (xc * rs) * p_ref[0:1, :] + p_ref[1:2, :]
        o_ref[...] = (hn * jax.nn.sigmoid(hn)).astype(bf16)

    row = _row_spec(cfg, cw)
    return pl.pallas_call(body, out_shape=jax.ShapeDtypeStruct((r_, cw), bf16), grid=(r_ // cfg.TR,),
                          in_specs=[row, _const_spec(lnp)], out_specs=row,
                          compiler_params=_params(1), name=name)(hc, lnp)


def _ln_silu_bwd(cfg, name, hc, dhs, lnp):
    r_, cw = hc.shape

    def body(x_ref, d_ref, p_ref, dx_ref, dp_ref):
        i = pl.program_id(0)
        x = x_ref[...]
        mu = jnp.mean(x, axis=-1, keepdims=True)
        xc = x - mu
        rs = lax.rsqrt(jnp.mean(xc * xc, axis=-1, keepdims=True) + EPS)
        xh = xc * rs
        gain = p_ref[0:1, :]
        hn = xh * gain + p_ref[1:2, :]
        sg = jax.nn.sigmoid(hn)
        dhn = d_ref[...] * (sg * (1.0 + hn * (1.0 - sg)))
        dxh = dhn * gain
        dx_ref[...] = rs * (dxh - jnp.mean(dxh, axis=-1, keepdims=True)
                            - xh * jnp.mean(dxh * xh, axis=-1, keepdims=True))

        @pl.when(i == 0)
        def _():
            dp_ref[...] = jnp.zeros_like(dp_ref)

        dp_ref[0:1, :] += jnp.sum(dhn * xh, axis=0, keepdims=True)
        dp_ref[1:2, :] += jnp.sum(dhn, axis=0, keepdims=True)

    row = _row_spec(cfg, cw)
    return pl.pallas_call(
        body, out_shape=(jax.ShapeDtypeStruct((r_, cw), f32), jax.ShapeDtypeStruct((SUBLANE, cw), f32)),
        grid=(r_ // cfg.TR,), in_specs=[row, row, _const_spec(lnp)],
        out_specs=(row, pl.BlockSpec((SUBLANE, cw), lambda i: (0, 0))),
        compiler_params=_params(1), name=name)(hc, dhs, lnp)


def _gate_fwd(cfg, name, p, ya, yc, ys):
    r_, d = ya.shape
    gb = cfg.GB
    base = cfg.OFF_GATE // gb
    per = d // gb

    tr = _divisor(r_, GATE_ROWS, cfg.TR)

    def body(g0_ref, g1_ref, g2_ref, ya_ref, yc_ref, ys_ref, o_ref):
        m = (jax.nn.sigmoid(g0_ref[...].astype(f32)) * ya_ref[...].astype(f32)
             + jax.nn.sigmoid(g1_ref[...].astype(f32)) * yc_ref[...].astype(f32)
             + jax.nn.sigmoid(g2_ref[...].astype(f32)) * ys_ref[...].astype(f32))
        o_ref[...] = m.astype(bf16)

    yspec = pl.BlockSpec((tr, gb), lambda i, c: (i, c))
    gspecs = [pl.BlockSpec((tr, gb), lambda i, c, k=k: (i, base + k * per + c)) for k in range(3)]
    return pl.pallas_call(body, out_shape=jax.ShapeDtypeStruct((r_, d), bf16), grid=(r_ // tr, per),
                          in_specs=gspecs + [yspec] * 3, out_specs=yspec,
                          compiler_params=_params(2), name=name)(p, p, p, ya, yc, ys)


def _gate_bwd(cfg, name, p, ya, yc, ys, dm):
    r_, d = ya.shape
    gb = cfg.GB
    base = cfg.OFF_GATE // gb
    per = d // gb

    tr = _divisor(r_, GATE_ROWS, cfg.TR)

    def body(g0_ref, g1_ref, g2_ref, ya_ref, yc_ref, ys_ref, dm_ref, da_ref, dc_ref, ds_ref, d0_ref, d1_ref, d2_ref):
        dmv = dm_ref[...]
        for g_ref, y_ref, dy_ref, dg_ref in ((g0_ref, ya_ref, da_ref, d0_ref), (g1_ref, yc_ref, dc_ref, d1_ref),
                                             (g2_ref, ys_ref, ds_ref, d2_ref)):
            sg = jax.nn.sigmoid(g_ref[...].astype(f32))
            dy_ref[...] = (dmv * sg).astype(bf16)
            dg_ref[...] = (dmv * y_ref[...].astype(f32) * sg * (1.0 - sg)).astype(bf16)

    yspec = pl.BlockSpec((tr, gb), lambda i, c: (i, c))
    gspecs = [pl.BlockSpec((tr, gb), lambda i, c, k=k: (i, base + k * per + c)) for k in range(3)]
    return pl.pallas_call(body, out_shape=(jax.ShapeDtypeStruct((r_, d), bf16),) * 6, grid=(r_ // tr, per),
                          in_specs=gspecs + [yspec] * 4, out_specs=(yspec,) * 6,
                          compiler_params=_params(2), name=name)(p, p, p, ya, yc, ys, dm)


def _swiglu_fwd(cfg, name, ff):
    r_, f2 = ff.shape
    fh = f2 // 2

    def body(a_ref, b_ref, o_ref):
        a = a_ref[...].astype(f32)
        o_ref[...] = (a * jax.nn.sigmoid(a) * b_ref[...].astype(f32)).astype(bf16)

    return pl.pallas_call(body, out_shape=jax.ShapeDtypeStruct((r_, fh), bf16), grid=(r_ // cfg.TR,),
                          in_specs=[_row_spec(cfg, fh, 0), _row_spec(cfg, fh, 1)], out_specs=_row_spec(cfg, fh),
                          compiler_params=_params(1, VMEM_LIMIT), name=name)(ff, ff)


def _swiglu_bwd(cfg, name, ff, ds):
    r_, f2 = ff.shape
    fh = f2 // 2

    def body(a_ref, b_ref, d_ref, o_ref):
        a = a_ref[...].astype(f32)
        sg = jax.nn.sigmoid(a)
        dsv = d_ref[...].astype(f32)
        o_ref[:, 0:fh] = (dsv * b_ref[...].astype(f32) * (sg * (1.0 + a * (1.0 - sg)))).astype(bf16)
        o_ref[:, fh:f2] = (dsv * a * sg).astype(bf16)

    return pl.pallas_call(body, out_shape=jax.ShapeDtypeStruct((r_, f2), bf16), grid=(r_ // cfg.TR,),
                          in_specs=[_row_spec(cfg, fh, 0), _row_spec(cfg, fh, 1), _row_spec(cfg, fh)],
                          out_specs=_row_spec(cfg, f2), compiler_params=_params(1, VMEM_LIMIT), name=name)(ff, ff, ds)


def _ada_fwd(name, call, w_ada, b_loc):
    nl, d, na = w_ada.shape
    nbp = call.shape[0]

    def body(c_ref, w_ref, b_ref, o_ref):
        cv = c_ref[...]
        a = (cv * jax.nn.sigmoid(cv)).astype(bf16)
        o_ref[0] = jnp.dot(a, w_ref[0].astype(bf16), preferred_element_type=f32) + b_ref[0]

    return pl.pallas_call(
        body, out_shape=jax.ShapeDtypeStruct((nl, nbp, na), f32), grid=(nl,),
        in_specs=[_const_spec(call), pl.BlockSpec((1, d, na), lambda l: (l, 0, 0)),
                  pl.BlockSpec((1, 1, na), lambda l: (l, 0, 0))],
        out_specs=pl.BlockSpec((1, nbp, na), lambda l: (l, 0, 0)),
        compiler_params=_params(1, VMEM_LIMIT), name=name)(call, w_ada, b_loc)


def _ada_bwd(name, call, w_ada, dmod, dmodc, cctx_row):
    nl, d, na = w_ada.shape
    nbp = call.shape[0]

    def body(c_ref, w_ref, dm_ref, dc_ref, gw_ref, gb_ref, gc_ref):
        l = pl.program_id(0)
        cv = c_ref[...]
        sg = jax.nn.sigmoid(cv)
        a = (cv * sg).astype(bf16)
        dctx = jnp.sum(dc_ref[0], axis=0, keepdims=True)
        rows = lax.broadcasted_iota(jnp.int32, (nbp, 1), 0)
        dm = jnp.where(rows == cctx_row, dctx, dm_ref[0])
        gw_ref[0] = lax.dot_general(a, dm.astype(bf16), TN, preferred_element_type=f32)
        gb_ref[0] = jnp.zeros((SUBLANE, na), f32)
        gb_ref[0, 0:1, :] = jnp.sum(dm, axis=0, keepdims=True)
        dc8 = jnp.broadcast_to(dctx, (SUBLANE, na)).astype(bf16)
        part = lax.dot_general(dc8, w_ref[0].astype(bf16), NT_, preferred_element_type=f32)
        cc = c_ref[cctx_row:cctx_row + 1, :]
        sc = jax.nn.sigmoid(cc)
        part = part * (sc * (1.0 + cc * (1.0 - sc)))

        @pl.when(l == 0)
        def _():
            gc_ref[...] = jnp.zeros_like(gc_ref)

        gc_ref[...] += part

    return pl.pallas_call(
        body,
        out_shape=(jax.ShapeDtypeStruct((nl, d, na), f32), jax.ShapeDtypeStruct((nl, SUBLANE, na), f32),
                   jax.ShapeDtypeStruct((SUBLANE, d), f32)),
        grid=(nl,),
        in_specs=[_const_spec(call), pl.BlockSpec((1, d, na), lambda l: (l, 0, 0)),
                  pl.BlockSpec((1, nbp, na), lambda l: (l, 0, 0)),
                  pl.BlockSpec((1, NDEV, na), lambda l: (l, 0, 0))],
        out_specs=(pl.BlockSpec((1, d, na), lambda l: (l, 0, 0)), pl.BlockSpec((1, SUBLANE, na), lambda l: (l, 0, 0)),
                   pl.BlockSpec((SUBLANE, d), lambda l: (0, 0))),
        compiler_params=_params(1, VMEM_LIMIT), name=name)(call, w_ada, dmod, dmodc)


def _adamw(name, w, g, m, v):
    rows, cols = w.shape
    tr = _divisor(rows, max(SUBLANE, (1 << 19) // cols), SUBLANE)
    c1 = 1.0 / (1.0 - ADAM_B1 ** ADAM_STEP)
    c2 = 1.0 / (1.0 - ADAM_B2 ** ADAM_STEP)

    def body(w_ref, g_ref, m_ref, v_ref, d_ref, nm_ref, nv_ref):
        gv = g_ref[...]
        nm = ADAM_B1 * m_ref[...] + (1.0 - ADAM_B1) * gv
        nv = ADAM_B2 * v_ref[...] + (1.0 - ADAM_B2) * (gv * gv)
        nm_ref[...] = nm
        nv_ref[...] = nv
        d_ref[...] = -ADAM_LR * ((nm * c1) / (jnp.sqrt(nv * c2) + ADAM_EPS) + ADAM_WD * w_ref[...])

    spec = pl.BlockSpec((tr, cols), lambda i: (i, 0))
    return pl.pallas_call(body, out_shape=(jax.ShapeDtypeStruct((rows, cols), f32),) * 3, grid=(rows // tr,),
                          in_specs=[spec] * 4, out_specs=(spec,) * 3,
                          compiler_params=_params(1, VMEM_LIMIT), name=name)(w, g, m, v)


def _sum_slots(name, x):
    nd, rows, cols = x.shape
    tr = _divisor(rows, max(SUBLANE, (1 << 18) // cols), SUBLANE)

    def body(x_ref, o_ref):
        acc = x_ref[0].astype(f32)
        for s in range(1, nd):
            acc = acc + x_ref[s].astype(f32)
        o_ref[...] = acc

    return pl.pallas_call(body, out_shape=jax.ShapeDtypeStruct((rows, cols), f32), grid=(rows // tr,),
                          in_specs=[pl.BlockSpec((nd, tr, cols), lambda i: (0, i, 0))],
                          out_specs=pl.BlockSpec((tr, cols), lambda i: (i, 0)),
                          compiler_params=_params(1, VMEM_LIMIT), name=name)(x)


def _peer(k):
    x, y, c = (lax.axis_index(a) for a in AXES)
    px = 1 - x if k & 4 else x
    py = 1 - y if k & 2 else y
    pc = 1 - c if k & 1 else c
    return (px, py, pc), 4 * px + 2 * py + pc


def _exchange_out_shapes(xs, scatter):
    return [jax.ShapeDtypeStruct(xa.shape if scatter else (NDEV,) + xa.shape, xa.dtype) for xa in xs]


def _exchange_scratch(n):
    nrel = NDEV - 1
    return [pltpu.SemaphoreType.DMA((n * nrel,)), pltpu.SemaphoreType.DMA((n * nrel,)), pltpu.SemaphoreType.DMA((n,))]


def _exchange_copies(ins, outs, sems, scatter, receiving):
    send_sems, recv_sems, _ = sems
    nrel = NDEV - 1
    x, y, c = (lax.axis_index(a) for a in AXES)
    me = 4 * x + 2 * y + c
    copies = []
    for a in range(len(ins)):
        for k in range(1, NDEV):
            peer, pidx = _peer(k)
            src = ins[a].at[pidx] if scatter else ins[a]
            copies.append(pltpu.make_async_remote_copy(
                src_ref=src, dst_ref=outs[a].at[pidx if receiving else me], send_sem=send_sems.at[a * nrel + k - 1],
                recv_sem=recv_sems.at[a * nrel + k - 1], device_id=peer, device_id_type=pl.DeviceIdType.MESH))
    return copies


def _exchange_local(ins, outs, sems, scatter):
    x, y, c = (lax.axis_index(a) for a in AXES)
    me = 4 * x + 2 * y + c
    return [pltpu.make_async_copy(ins[a].at[me] if scatter else ins[a], outs[a].at[me], sems[2].at[a])
            for a in range(len(ins))]


CHIP_RELATIONS = (2, 4, 6)
SIBLING = 1


def _gather_copy(ins, outs, sems, a, slot, src, block, to):
    nrel = NDEV - 1
    return pltpu.make_async_remote_copy(
        src_ref=src, dst_ref=outs[a].at[block], send_sem=sems[0].at[a * nrel + slot],
        recv_sem=sems[1].at[a * nrel + slot], device_id=to, device_id_type=pl.DeviceIdType.MESH)


def _exchange_start(ins, outs, sems, scatter):
    for cp in _exchange_local(ins, outs, sems, scatter):
        cp.start()
    if scatter:
        for cp in _exchange_copies(ins, outs, sems, scatter, False):
            cp.start()
        return
    x, y, c = (lax.axis_index(a) for a in AXES)
    me = 4 * x + 2 * y + c
    for a in range(len(ins)):
        for slot, k in enumerate((SIBLING,) + CHIP_RELATIONS):
            _gather_copy(ins, outs, sems, a, slot, ins[a], me, _peer(k)[0]).start()


def _exchange_wait(ins, outs, sems, scatter):
    if scatter:
        for cp in _exchange_copies(ins, outs, sems, scatter, True):
            cp.wait_recv()
            cp.wait_send()
    else:
        x, y, c = (lax.axis_index(a) for a in AXES)
        me = 4 * x + 2 * y + c
        sibling, sibling_idx = _peer(SIBLING)
        n = len(ins)
        passed = []
        for a in range(n):
            for j, k in enumerate(CHIP_RELATIONS):
                peer, pidx = _peer(k)
                _gather_copy(ins, outs, sems, a, 1 + j, ins[a], pidx, peer).wait_recv()
                fwd = _gather_copy(ins, outs, sems, a, 4 + j, outs[a].at[pidx], pidx, sibling)
                fwd.start()
                passed.append(fwd)
        for a in range(n):
            _gather_copy(ins, outs, sems, a, 0, ins[a], sibling_idx, sibling).wait_recv()
            for j, k in enumerate(CHIP_RELATIONS):
                _gather_copy(ins, outs, sems, a, 4 + j, ins[a], _peer(k | SIBLING)[1], sibling).wait_recv()
        for a in range(n):
            for slot, k in enumerate((SIBLING,) + CHIP_RELATIONS):
                _gather_copy(ins, outs, sems, a, slot, ins[a], me, _peer(k)[0]).wait_send()
        for fwd in passed:
            fwd.wait_send()
    for cp in _exchange_local(ins, outs, sems, scatter):
        cp.wait()


def _exchange(name, xs, scatter):
    n = len(xs)

    def body(*refs):
        ins, outs, sems = refs[:n], refs[n:2 * n], refs[2 * n:]
        _exchange_start(ins, outs, sems, scatter)
        _exchange_wait(ins, outs, sems, scatter)

    anyspec = pl.BlockSpec(memory_space=pl.ANY)
    outs = pl.pallas_call(
        body, out_shape=tuple(_exchange_out_shapes(xs, scatter)), in_specs=[anyspec] * n, out_specs=(anyspec,) * n,
        scratch_shapes=_exchange_scratch(n), name=name)(*xs)
    return list(outs)


def _all_gather(name, xs):
    return _exchange(name, xs, False)


def _all_to_all(name, xs):
    return _exchange(name, xs, True)


def _rope_table(cfg):
    s, lc = cfg.S, cfg.LC
    rows = s // GRID_W
    r_ids, c_ids = jnp.meshgrid(jnp.arange(rows), jnp.arange(GRID_W), indexing="ij")
    r_ids = r_ids.reshape(-1).astype(f32)
    c_ids = c_ids.reshape(-1).astype(f32)
    freqs = ROPE_THETA ** (-jnp.arange(0, AXIS_DIM, 2, dtype=f32) / AXIS_DIM)
    ang_r = r_ids[:, None] * freqs
    ang_c = c_ids[:, None] * freqs
    cos = jnp.concatenate([jnp.cos(ang_r), jnp.cos(ang_r), jnp.cos(ang_c), jnp.cos(ang_c)], axis=1)
    sin = jnp.concatenate([-jnp.sin(ang_r), jnp.sin(ang_r), -jnp.sin(ang_c), jnp.sin(ang_c)], axis=1)
    cos = jnp.concatenate([jnp.ones((lc, HEAD_DIM), f32), cos], axis=0)
    sin = jnp.concatenate([jnp.zeros((lc, HEAD_DIM), f32), sin], axis=0)
    reps = QKW // HEAD_DIM
    return jnp.stack([jnp.tile(cos, (1, reps)), jnp.tile(sin, (1, reps))])


def _block_diag_ones():
    idx = np.arange(QKW) // HEAD_DIM
    return jnp.asarray((idx[:, None] == idx[None, :]).astype(np.float32))


def _replicate_matrix():
    src = np.arange(KVW)
    dst = np.arange(AW)
    m = (src[:, None] // HEAD_DIM == dst[None, :] // GW) & (src[:, None] % HEAD_DIM == dst[None, :] % HEAD_DIM)
    return m.astype(np.float32)


def _pack_flat(parts):
    flat = jnp.concatenate([p.reshape(-1) for p in parts])
    n = flat.shape[0]
    unit = SUBLANE * LANE
    total = -(-n // unit) * unit
    flat = jnp.pad(flat, (0, total - n))
    return flat.reshape(total // LANE, LANE)


def _unpack_flat(flat, shapes):
    flat = flat.reshape(flat.shape[:-2] + (-1,))
    out, off = [], 0
    for shp in shapes:
        size = int(np.prod(shp))
        out.append(flat[..., off:off + size].reshape(flat.shape[:-1] + tuple(shp)))
        off += size
    return out


def kernel(x, c, ctx, c_ctx, w_ada, b_ada, w_in, q_norm, k_norm, w_attn_o, conf_dw_w, conf_dw_b, conf_ln_g, conf_ln_b, w_conf_out, sc_dw_w, w_sc_out, w_mix_out, w_ffn_in, w_ffn_out, loss_target, m_c_ctx, m_w_ada, m_b_ada, m_w_in, m_q_norm, m_k_norm, m_w_attn_o, m_conf_dw_w, m_conf_dw_b, m_conf_ln_g, m_conf_ln_b, m_w_conf_out, m_sc_dw_w, m_w_sc_out, m_w_mix_out, m_w_ffn_in, m_w_ffn_out, v_c_ctx, v_w_ada, v_b_ada, v_w_in, v_q_norm, v_k_norm, v_w_attn_o, v_conf_dw_w, v_conf_dw_b, v_conf_ln_g, v_conf_ln_b, v_w_conf_out, v_sc_dw_w, v_w_sc_out, v_w_mix_out, v_w_ffn_in, v_w_ffn_out):
    cfg = _Cfg()
    cfg.B, cfg.S, cfg.D = x.shape
    cfg.LC = ctx.shape[1]
    cfg.T = cfg.LC + cfg.S
    cfg.TR = min(256, cfg.LC)
    assert cfg.LC % cfg.TR == 0 and cfg.S % cfg.TR == 0 and cfg.S % GRID_W == 0
    cfg.NT, cfg.NC = cfg.T // cfg.TR, cfg.LC // cfg.TR
    cfg.R = cfg.B * cfg.T
    nl = w_in.shape[0]
    b, d = cfg.B, cfg.D
    cfg.CW = conf_dw_b.shape[1]
    cfg.K31, cfg.K3 = conf_dw_w.shape[1], sc_dw_w.shape[1]
    assert w_sc_out.shape[1] == cfg.CW and cfg.CW % LANE == 0
    assert cfg.K31 // 2 <= CONV_PAD_ROWS and cfg.K3 // 2 <= SC_PAD_ROWS
    cfg.CB = LANE
    cfg.OFF_CONF = QKVW
    cfg.OFF_SC = cfg.OFF_CONF + 2 * cfg.CW
    cfg.OFF_GATE = cfg.OFF_SC + 3 * cfg.CW
    n_in = w_in.shape[2] * NDEV
    assert n_in == cfg.OFF_GATE + 3 * d and w_attn_o.shape[1] == AW
    cfg.GB = math.gcd(cfg.OFF_GATE, d)
    assert cfg.GB % LANE == 0
    fh = w_ffn_out.shape[1] * NDEV
    na = w_ada.shape[2]
    cw8 = cfg.CW // NDEV

    xi, yi, ci = (lax.axis_index(a) for a in AXES)
    me = 4 * xi + 2 * yi + ci

    small_shapes = [c.shape, conf_dw_w.shape, sc_dw_w.shape]
    (g0,) = _all_gather("gather_small", [_pack_flat([c, conf_dw_w, sc_dw_w])])
    c_all, cw_all, sw_all = _unpack_flat(g0, small_shapes)
    nb = NDEV * b
    nbp = -(-(nb + 1) // SUBLANE) * SUBLANE
    call = jnp.concatenate([c_all.reshape(nb, d), c_ctx[None, :], jnp.zeros((nbp - nb - 1, d), f32)], axis=0)
    w31_full = jnp.moveaxis(cw_all, 0, 2).reshape(nl, cfg.K31, cfg.CW)
    w3_full = jnp.moveaxis(sw_all, 0, 2).reshape(nl, cfg.K3, cfg.CW)
    k31p = -(-(cfg.K31 + 1) // SUBLANE) * SUBLANE
    w31b = jnp.concatenate([w31_full, conf_dw_b[:, None, :], jnp.zeros((nl, k31p - cfg.K31 - 1, cfg.CW), f32)], axis=1)
    w3p = jnp.concatenate([w3_full, jnp.zeros((nl, SUBLANE - cfg.K3, cfg.CW), f32)], axis=1)

    b_loc = lax.dynamic_slice(b_ada, (0, me * na), (nl, na))[:, None, :]
    mod_loc = _ada_fwd("ada_fwd", call, w_ada, b_loc)
    (mod_g,) = _all_gather("gather_mod", [mod_loc.reshape(nl * nbp, na)])
    mod_full = jnp.transpose(mod_g.reshape(NDEV, nl, nbp, na), (1, 2, 0, 3)).reshape(nl, nbp, N_MOD, d)
    mod_lat = lax.dynamic_slice_in_dim(mod_full, me * b, b, axis=1)
    mod_ctx = jnp.broadcast_to(mod_full[:, nb][:, None], (nl, b, N_MOD, d))
    mod = jnp.stack([mod_ctx, mod_lat], axis=2)
    mod = jnp.pad(mod, ((0, 0), (0, 0), (0, 0), (0, MOD_ROWS - N_MOD), (0, 0)))

    cs = _rope_table(cfg)
    bd = _block_diag_ones().astype(bf16)
    rep_np = _replicate_matrix()
    rep = jnp.asarray(rep_np, dtype=bf16)
    rept = jnp.asarray(rep_np.T, dtype=bf16)
    reps_q, reps_k = AW // HEAD_DIM, KVW // HEAD_DIM

    xin = jnp.concatenate([ctx, x], axis=1).reshape(cfg.R, d)
    target = loss_target.reshape(b * cfg.S, d)
    saved = []
    xcur, ocur = xin, None

    def weight_shards(l):
        return [jnp.transpose(w_in[l]).astype(bf16), jnp.transpose(w_attn_o[l]).astype(bf16),
                jnp.transpose(w_conf_out[l]).astype(bf16), jnp.transpose(w_sc_out[l]).astype(bf16),
                w_mix_out[l].astype(bf16), jnp.transpose(w_ffn_in[l]).astype(bf16), w_ffn_out[l].astype(bf16)]

    def whole(g):
        return g.reshape(NDEV * g.shape[1], g.shape[2])

    shards0 = weight_shards(0)
    (w_first,) = _all_gather("gather_w0", shards0[:1])
    w_rest = None
    for l in range(nl):
        win_t = whole(w_first)
        gain = jnp.concatenate([jnp.tile(q_norm[l], reps_q), jnp.tile(k_norm[l], reps_k)])[None, :]
        lnp = jnp.concatenate([conf_ln_g[l][None], conf_ln_b[l][None], jnp.zeros((SUBLANE - 2, cfg.CW), f32)], axis=0)
        nxt = weight_shards(l + 1) if l + 1 < nl else None

        if l == 0:
            x0, h = _norm_fwd(cfg, f"norm_a{l}", xcur, None, None, 0, mod[l], 0)
            p, w_rest = _mm(h, win_t, "nt", ACT, f"mm_in{l}", comm=(shards0[1:], False))
        else:
            x0, h = _norm_fwd(cfg, f"norm_a{l}", xcur, ocur, mod[l - 1], 5, mod[l], 0)
            p = _mm(h, win_t, "nt", ACT, f"mm_in{l}")
        wao_t, wco_t, wso_t, wmix, wfi_t, wfo = [whole(g) for g in w_rest]
        q, krep, vrep = _qkv_fwd(cfg, f"qkv{l}", p, cs, gain, bd, rep)
        (attn, lse), w_rest = _attn_fwd(cfg, f"attn{l}", q, krep, vrep,
                                        comm=None if nxt is None else (nxt[1:], False))
        hc, z = _conv_fwd(cfg, f"conv{l}", p, w31b[l], w3p[l])
        hs = _ln_silu_fwd(cfg, f"lnsilu{l}", hc, lnp)
        ya = _mm(attn, wao_t, "nt", ACT, f"mm_ao{l}")
        yc = _mm(hs, wco_t, "nt", ACT, f"mm_co{l}")
        ys = _mm(z, wso_t, "nt", ACT, f"mm_so{l}")
        merged = _gate_fwd(cfg, f"gate{l}", p, ya, yc, ys)
        mixed = _mm(merged, wmix, "nn", f32, f"mm_mix{l}")
        x1, h2 = _norm_fwd(cfg, f"norm_b{l}", x0, mixed, mod[l], 2, mod[l], 3)
        if nxt is None:
            ff = _mm(h2, wfi_t, "nt", ACT, f"mm_fi{l}")
        else:
            ff, (w_first,) = _mm(h2, wfi_t, "nt", ACT, f"mm_fi{l}", comm=(nxt[:1], False))
        sw = _swiglu_fwd(cfg, f"swiglu{l}", ff)
        o = _mm(sw, wfo, "nn", f32, f"mm_fo{l}")
        saved.append(dict(x0=x0, h=h, p=p, q=q, krep=krep, vrep=vrep, attn=attn, lse=lse, hc=hc, z=z, hs=hs,
                          ya=ya, yc=yc, ys=ys, merged=merged, mixed=mixed, x1=x1, h2=h2, ff=ff, sw=sw, o=o,
                          oprev=ocur, gain=gain, lnp=lnp,
                          w=(win_t, wao_t, wco_t, wso_t, wmix, wfi_t, wfo)))
        xcur, ocur = x1, o

    lsum, dres, do, dm_loss = _loss_head(cfg, "loss", xcur, ocur, mod[nl - 1], 5, target)
    loss = lax.psum((0.5 / d) * jnp.sum(lsum), AXES)

    dmod_rows = [[None] * N_MOD for _ in range(nl)]
    dmod_rows[nl - 1][5] = dm_loss[:, :, 2]
    g_qn, g_kn, g_w31, g_w3, g_ln = [None] * nl, [None] * nl, [None] * nl, [None] * nl, [None] * nl
    gbig = [None] * nl
    pending = None
    pending_early = None

    def blocks(g):
        return g.reshape(NDEV, g.shape[0] // NDEV, g.shape[1])

    def sum_received(l, recv):
        gbig[l] = [_sum_slots(f"sum_g{l}_{i}", r) for i, r in enumerate(recv)]

    for l in reversed(range(nl)):
        sv = saved[l]
        win_t, wao_t, wco_t, wso_t, wmix, wfi_t, wfo = sv["w"]
        ds = _mm(do, wfo, "nt", ACT, f"mm_dsw{l}")
        g_wfo = _mm(sv["sw"], do, "tn", bf16, f"mm_gfo{l}")
        dff = _swiglu_bwd(cfg, f"swiglu_b{l}", sv["ff"], ds)
        dh2 = _mm(dff, wfi_t, "nn", f32, f"mm_dh2{l}")
        g_wfi = _mm(dff, sv["h2"], "tn", bf16, f"mm_gfi{l}")
        dx0p, dmixed, dm_b = _norm_bwd(cfg, f"norm_b_b{l}", sv["x1"], sv["mixed"], dh2, dres, mod[l], 2, mod[l], 3)
        dmod_rows[l][3], dmod_rows[l][4], dmod_rows[l][2] = dm_b[:, :, 0], dm_b[:, :, 1], dm_b[:, :, 2]
        dmerged = _mm(dmixed, wmix, "nt", f32, f"mm_dmg{l}")
        g_wmix = _mm(sv["merged"], dmixed, "tn", bf16, f"mm_gmix{l}")
        dya, dyc, dys, dg0, dg1, dg2 = _gate_bwd(cfg, f"gate_b{l}", sv["p"], sv["ya"], sv["yc"], sv["ys"], dmerged)
        dattn = _mm(dya, wao_t, "nn", f32, f"mm_dat{l}")
        g_wao = _mm(dya, sv["attn"], "tn", bf16, f"mm_gao{l}")
        dhs = _mm(dyc, wco_t, "nn", f32, f"mm_dhs{l}")
        g_wco = _mm(dyc, sv["hs"], "tn", bf16, f"mm_gco{l}")
        dz = _mm(dys, wso_t, "nn", f32, f"mm_dz{l}")
        g_wso = _mm(dys, sv["z"], "tn", bf16, f"mm_gso{l}")
        dhc, g_ln[l] = _ln_silu_bwd(cfg, f"lnsilu_b{l}", sv["hc"], dhs, sv["lnp"])
        (da, dg, dbg, dcg, dxs, g_w31[l], g_w3[l]), recv = _conv_bwd(
            cfg, f"conv_b{l}", sv["p"], dhc, dz, w31b[l], w3p[l], comm=None if pending is None else ([pending], True))
        if pending is not None:
            sum_received(l + 1, recv + pending_early)
        early = [blocks(g) for g in (g_wao, g_wco, g_wso, g_wmix, g_wfi, g_wfo)]
        (dq, dkr, dvr), pending_early = _attn_bwd(cfg, f"attn_b{l}", sv["q"], sv["krep"], sv["vrep"], sv["attn"],
                                                  sv["lse"], dattn, comm=(early, True))
        dpq, dgain = _qkv_bwd(cfg, f"qkv_b{l}", sv["p"], dq, dkr, dvr, cs, sv["gain"], bd, rept)
        g_qn[l] = jnp.sum(dgain[0, :AW].reshape(reps_q, HEAD_DIM), axis=0)
        g_kn[l] = jnp.sum(dgain[0, AW:].reshape(reps_k, HEAD_DIM), axis=0)
        dp = jnp.concatenate([dpq, da, dg, dbg, dcg, dxs, dg0, dg1, dg2], axis=1)
        dh = _mm(dp, win_t, "nn", f32, f"mm_dh{l}")
        g_win = _mm(dp, sv["h"], "tn", bf16, f"mm_gin{l}")
        if l == 0:
            dres, _, dm_a = _norm_bwd(cfg, f"norm_a_b{l}", sv["x0"], None, dh, dx0p, None, 0, mod[l], 0)
        else:
            dres, do, dm_a = _norm_bwd(cfg, f"norm_a_b{l}", sv["x0"], sv["oprev"], dh, dx0p, mod[l - 1], 5, mod[l], 0)
            dmod_rows[l - 1][5] = dm_a[:, :, 2]
        dmod_rows[l][0], dmod_rows[l][1] = dm_a[:, :, 0], dm_a[:, :, 1]

        pending = blocks(g_win)

    sum_received(0, _all_to_all("scatter_g0", [pending]) + pending_early)
    grad_x = dres.reshape(b, cfg.T, d)[:, cfg.LC:, :]

    rb = -(-(b + 1) // SUBLANE) * SUBLANE
    dmod_l = jnp.stack([jnp.stack(rows, axis=2) for rows in dmod_rows])
    d_lat = dmod_l[:, :, 1].reshape(nl, b, NDEV, na)
    d_ctx = jnp.sum(dmod_l[:, :, 0], axis=1).reshape(nl, 1, NDEV, na)
    send = jnp.concatenate([d_lat, d_ctx, jnp.zeros((nl, rb - b - 1, NDEV, na), f32)], axis=1)
    send = jnp.transpose(send, (2, 0, 1, 3)).reshape(NDEV, nl * rb, na)
    (recv,) = _all_to_all("scatter_dmod", [send])
    recv = recv.reshape(NDEV, nl, rb, na)
    dmod_ex = jnp.transpose(recv[:, :, :b], (1, 0, 2, 3)).reshape(nl, nb, na)
    dmod_ex = jnp.pad(dmod_ex, ((0, 0), (0, nbp - nb), (0, 0)))
    dmodc = jnp.transpose(recv[:, :, b], (1, 0, 2))
    g_wada, g_bloc, g_cctx = _ada_bwd("ada_bwd", call, w_ada, dmod_ex, dmodc, nb)

    g_bada = lax.dynamic_update_slice(jnp.zeros((nl, N_MOD * d), f32), g_bloc[:, 0], (0, me * na))
    g_w31s = jnp.stack(g_w31)
    small_parts = [g_cctx[0], g_bada, jnp.stack(g_qn), jnp.stack(g_kn), g_w31s[:, :cfg.K31], g_w31s[:, cfg.K31],
                   jnp.stack([g[0] for g in g_ln]), jnp.stack([g[1] for g in g_ln]),
                   jnp.stack(g_w3)[:, :cfg.K3]]
    small_part_shapes = [p_.shape for p_ in small_parts]
    (gs,) = _all_gather("gather_gsmall", [_pack_flat(small_parts)])
    gsum = _sum_slots("sum_gsmall", gs)
    (gr_cctx, gr_bada, gr_qn, gr_kn, gr_w31, gr_b31, gr_lng, gr_lnb, gr_w3) = _unpack_flat(gsum, small_part_shapes)
    gr_w31 = lax.dynamic_slice_in_dim(gr_w31, me * cw8, cw8, axis=2)
    gr_w3 = lax.dynamic_slice_in_dim(gr_w3, me * cw8, cw8, axis=2)

    def big(i, transposed):
        g = jnp.stack([gbig[l][i] for l in range(nl)])
        return jnp.transpose(g, (0, 2, 1)) if transposed else g

    grads = {
        "c_ctx": gr_cctx, "w_ada": g_wada, "b_ada": gr_bada, "w_in": big(0, True), "q_norm": gr_qn, "k_norm": gr_kn,
        "w_attn_o": big(1, True), "conf_dw_w": gr_w31, "conf_dw_b": gr_b31, "conf_ln_g": gr_lng, "conf_ln_b": gr_lnb,
        "w_conf_out": big(2, True), "sc_dw_w": gr_w3, "w_sc_out": big(3, True), "w_mix_out": big(4, False),
        "w_ffn_in": big(5, True), "w_ffn_out": big(6, False)}
    weights = dict(c_ctx=c_ctx, w_ada=w_ada, b_ada=b_ada, w_in=w_in, q_norm=q_norm, k_norm=k_norm, w_attn_o=w_attn_o,
                   conf_dw_w=conf_dw_w, conf_dw_b=conf_dw_b, conf_ln_g=conf_ln_g, conf_ln_b=conf_ln_b,
                   w_conf_out=w_conf_out, sc_dw_w=sc_dw_w, w_sc_out=w_sc_out, w_mix_out=w_mix_out,
                   w_ffn_in=w_ffn_in, w_ffn_out=w_ffn_out)
    m_in = dict(c_ctx=m_c_ctx, w_ada=m_w_ada, b_ada=m_b_ada, w_in=m_w_in, q_norm=m_q_norm, k_norm=m_k_norm,
                w_attn_o=m_w_attn_o, conf_dw_w=m_conf_dw_w, conf_dw_b=m_conf_dw_b, conf_ln_g=m_conf_ln_g,
                conf_ln_b=m_conf_ln_b, w_conf_out=m_w_conf_out, sc_dw_w=m_sc_dw_w, w_sc_out=m_w_sc_out,
                w_mix_out=m_w_mix_out, w_ffn_in=m_w_ffn_in, w_ffn_out=m_w_ffn_out)
    v_in = dict(c_ctx=v_c_ctx, w_ada=v_w_ada, b_ada=v_b_ada, w_in=v_w_in, q_norm=v_q_norm, k_norm=v_k_norm,
                w_attn_o=v_w_attn_o, conf_dw_w=v_conf_dw_w, conf_dw_b=v_conf_dw_b, conf_ln_g=v_conf_ln_g,
                conf_ln_b=v_conf_ln_b, w_conf_out=v_w_conf_out, sc_dw_w=v_sc_dw_w, w_sc_out=v_w_sc_out,
                w_mix_out=v_w_mix_out, w_ffn_in=v_w_ffn_in, w_ffn_out=v_w_ffn_out)
    names = list(weights)
    big_names = ("w_ada", "w_in", "w_attn_o", "w_conf_out", "w_sc_out", "w_mix_out", "w_ffn_in", "w_ffn_out")
    small_names = [n for n in names if n not in big_names]
    delta, new_m, new_v = {}, {}, {}
    for n in big_names:
        shp = weights[n].shape
        two_d = (shp[0] * shp[1], shp[2])
        dl, nm, nv = _adamw(f"adamw_{n}", weights[n].reshape(two_d), grads[n].reshape(two_d),
                            m_in[n].reshape(two_d), v_in[n].reshape(two_d))
        delta[n], new_m[n], new_v[n] = dl.reshape(shp), nm.reshape(shp), nv.reshape(shp)
    sshapes = [weights[n].shape for n in small_names]
    dl, nm, nv = _adamw("adamw_small", _pack_flat([weights[n] for n in small_names]),
                        _pack_flat([grads[n] for n in small_names]), _pack_flat([m_in[n] for n in small_names]),
                        _pack_flat([v_in[n] for n in small_names]))
    for n, a_, b_, c_ in zip(small_names, _unpack_flat(dl, sshapes), _unpack_flat(nm, sshapes), _unpack_flat(nv, sshapes)):
        delta[n], new_m[n], new_v[n] = a_, b_, c_

    return (loss, grad_x, *[grads[n] for n in names], *[delta[n] for n in names],
            *[new_m[n] for n in names], *[new_v[n] for n in names])
```

```python
import functools
import math

import numpy as np
import jax
import jax.numpy as jnp
from jax import lax
from jax.experimental import pallas as pl
from jax.experimental.pallas import tpu as pltpu

f32 = jnp.float32
bf16 = jnp.bfloat16
ACT = bf16

NDEV = 8
AXES = ("x", "y", "c")
HEAD_DIM = 64
N_Q_HEADS = 8
N_KV_HEADS = 2
Q_GROUP = N_Q_HEADS // N_KV_HEADS
AW = N_Q_HEADS * HEAD_DIM
KVW = N_KV_HEADS * HEAD_DIM
GW = Q_GROUP * HEAD_DIM
QKW = AW + KVW
QKVW = AW + 2 * KVW
GRID_W = 64
AXIS_DIM = HEAD_DIM // 2
ROPE_THETA = 10000.0
ATTN_SCALE = HEAD_DIM ** -0.5
EPS = 1e-6
N_MOD = 6
MOD_ROWS = 8
CONV_PAD_ROWS = 16
GATE_ROWS = 1024
HEADS_PER_PASS_FWD = 4
HEADS_PER_PASS_BWD = 2
CONV_CHUNK = 64
LANE = 128
SUBLANE = 8
VMEM_LIMIT = 56 * 1024 * 1024

ADAM_LR = 0.001
ADAM_B1 = 0.9
ADAM_B2 = 0.999
ADAM_EPS = 1e-08
ADAM_WD = 0.01
ADAM_STEP = 10

NN =(((1,), (0,)), ((), ()))
NT_ = (((1,), (1,)), ((), ()))
TN = (((0,), (0,)), ((), ()))


def _params(ndims, vmem=None):
    return pltpu.CompilerParams(dimension_semantics=("arbitrary",) * ndims, vmem_limit_bytes=vmem)


def _divisor(n, cap, mult):
    best = None
    for d in range(mult, min(n, cap) + 1, mult):
        if n % d == 0:
            best = d
    return n if best is None else best


def _const_spec(a):
    nd = a.ndim
    return pl.BlockSpec(a.shape, lambda *idx: (0,) * nd)


class _Cfg:
    pass


def _mm(a, b, mode, out_dtype, name, comm=None):
    if mode == "tn":
        m, ka = a.shape
        nb = b.shape[1]
        tka = _divisor(ka, 1408, LANE)
        tnb = _divisor(nb, 1024, LANE)
        tmr = _divisor(m, 1024, SUBLANE)
        nsteps = m // tmr

        def body(a_ref, b_ref, o_ref, acc_ref):
            k = pl.program_id(2)

            @pl.when(k == 0)
            def _():
                acc_ref[...] = jnp.zeros_like(acc_ref)

            acc_ref[...] += lax.dot_general(a_ref[...].astype(bf16), b_ref[...].astype(bf16), TN,
                                            preferred_element_type=f32)

            @pl.when(k == nsteps - 1)
            def _():
                o_ref[...] = acc_ref[...].astype(out_dtype)

        return pl.pallas_call(
            body, out_shape=jax.ShapeDtypeStruct((ka, nb), out_dtype),
            grid=(ka // tka, nb // tnb, nsteps),
            in_specs=[pl.BlockSpec((tmr, tka), lambda i, j, k: (k, i)),
                      pl.BlockSpec((tmr, tnb), lambda i, j, k: (k, j))],
            out_specs=pl.BlockSpec((tka, tnb), lambda i, j, k: (i, j)),
            scratch_shapes=[pltpu.VMEM((tka, tnb), f32)],
            compiler_params=_params(3, VMEM_LIMIT), name=name)(a, b)

    m, kdim = a.shape
    n = b.shape[1] if mode == "nn" else b.shape[0]
    tm = _divisor(m, 1024 if a.dtype == bf16 else 512, SUBLANE)
    tn = _divisor(n, 1408, LANE)
    tk = _divisor(kdim, 1408, LANE)
    nsteps = kdim // tk
    dims = NN if mode == "nn" else NT_

    def body(a_ref, b_ref, o_ref, acc_ref):
        k = pl.program_id(2)

        @pl.when(k == 0)
        def _():
            acc_ref[...] = jnp.zeros_like(acc_ref)

        acc_ref[...] += lax.dot_general(a_ref[...].astype(bf16), b_ref[...].astype(bf16), dims,
                                        preferred_element_type=f32)

        @pl.when(k == nsteps - 1)
        def _():
            o_ref[...] = acc_ref[...].astype(out_dtype)

    if mode == "nn":
        b_spec = pl.BlockSpec((tk, tn), lambda i, j, k: (k, j))
    else:
        b_spec = pl.BlockSpec((tn, tk), lambda i, j, k: (j, k))
    (out,), couts = _hosting_call(
        body, comm, out_shape=(jax.ShapeDtypeStruct((m, n), out_dtype),),
        grid=(m // tm, n // tn, nsteps),
        in_specs=[pl.BlockSpec((tm, tk), lambda i, j, k: (i, k)), b_spec],
        out_specs=(pl.BlockSpec((tm, tn), lambda i, j, k: (i, j)),),
        scratch_shapes=[pltpu.VMEM((tm, tn), f32)],
        compiler_params=_params(3, VMEM_LIMIT), name=name, args=(a, b))
    return out if comm is None else (out, couts)


def _row_spec(cfg, width, col=0):
    return pl.BlockSpec((cfg.TR, width), lambda i: (i, col))


def _mod_spec(cfg):
    nt, nc = cfg.NT, cfg.NC
    return pl.BlockSpec((1, 1, MOD_ROWS, cfg.D),
                        lambda i: (i // nt, ((i % nt) >= nc).astype(jnp.int32), 0, 0))


def _segment_start(cfg, i):
    j = i % cfg.NT
    return jnp.logical_or(j == 0, j == cfg.NC)


def _norm_fwd(cfg, name, xin, o, modg, gk, modn, sk):
    has_o = o is not None
    r_, d = xin.shape

    def body(*refs):
        if has_o:
            x_ref, o_ref, mg_ref, mn_ref, xn_ref, h_ref = refs
            x = x_ref[...] + mg_ref[0, 0, gk:gk + 1, :] * o_ref[...]
            xn_ref[...] = x
        else:
            x_ref, mn_ref, h_ref = refs
            x = x_ref[...]
        r = lax.rsqrt(jnp.mean(x * x, axis=-1, keepdims=True) + EPS)
        h = (x * r) * (1.0 + mn_ref[0, 0, sk + 1:sk + 2, :]) + mn_ref[0, 0, sk:sk + 1, :]
        h_ref[...] = h.astype(bf16)

    row = _row_spec(cfg, d)
    if has_o:
        ins, in_specs = (xin, o, modg, modn), [row, row, _mod_spec(cfg), _mod_spec(cfg)]
        out_shape = (jax.ShapeDtypeStruct((r_, d), f32), jax.ShapeDtypeStruct((r_, d), bf16))
        out_specs = (row, row)
    else:
        ins, in_specs = (xin, modn), [row, _mod_spec(cfg)]
        out_shape = jax.ShapeDtypeStruct((r_, d), bf16)
        out_specs = row
    res = pl.pallas_call(body, out_shape=out_shape, grid=(r_ // cfg.TR,), in_specs=in_specs,
                         out_specs=out_specs, compiler_params=_params(1), name=name)(*ins)
    return res if has_o else (xin, res)


def _norm_bwd(cfg, name, xnew, o, dh, dres, modg, gk, modn, sk):
    has_o = o is not None
    r_, d = xnew.shape

    def body(*refs):
        if has_o:
            xn_ref, o_ref, dh_ref, dr_ref, mg_ref, mn_ref, dx_ref, do_ref, dm_ref = refs
        else:
            xn_ref, dh_ref, dr_ref, mn_ref, dx_ref, dm_ref = refs
        i = pl.program_id(0)
        x = xn_ref[...]
        dhv = dh_ref[...].astype(f32)
        r = lax.rsqrt(jnp.mean(x * x, axis=-1, keepdims=True) + EPS)
        xh = x * r
        dxh = dhv * (1.0 + mn_ref[0, 0, sk + 1:sk + 2, :])
        dx = r * (dxh - xh * jnp.mean(dxh * xh, axis=-1, keepdims=True)) + dr_ref[...]
        dx_ref[...] = dx

        @pl.when(_segment_start(cfg, i))
        def _():
            dm_ref[...] = jnp.zeros_like(dm_ref)

        dm_ref[0, 0, 0:1, :] += jnp.sum(dhv, axis=0, keepdims=True)
        dm_ref[0, 0, 1:2, :] += jnp.sum(dhv * xh, axis=0, keepdims=True)
        if has_o:
            ov = o_ref[...]
            dm_ref[0, 0, 2:3, :] += jnp.sum(dx * ov, axis=0, keepdims=True)
            do_ref[...] = (dx * mg_ref[0, 0, gk:gk + 1, :]).astype(bf16)

    row = _row_spec(cfg, d)
    dm_shape = jax.ShapeDtypeStruct((cfg.B, 2, MOD_ROWS, d), f32)
    if has_o:
        ins = (xnew, o, dh, dres, modg, modn)
        in_specs = [row, row, row, row, _mod_spec(cfg), _mod_spec(cfg)]
        out_shape = (jax.ShapeDtypeStruct((r_, d), f32), jax.ShapeDtypeStruct((r_, d), bf16), dm_shape)
        out_specs = (row, row, _mod_spec(cfg))
    else:
        ins = (xnew, dh, dres, modn)
        in_specs = [row, row, row, _mod_spec(cfg)]
        out_shape = (jax.ShapeDtypeStruct((r_, d), f32), dm_shape)
        out_specs = (row, _mod_spec(cfg))
    res = pl.pallas_call(body, out_shape=out_shape, grid=(r_ // cfg.TR,), in_specs=in_specs,
                         out_specs=out_specs, compiler_params=_params(1), name=name)(*ins)
    if has_o:
        return res
    return res[0], None, res[1]


def _loss_head(cfg, name, x1, o, modg, gk, target):
    r_, d = x1.shape
    nt, nc = cfg.NT, cfg.NC
    nlat = nt - nc

    def body(x_ref, o_ref, mg_ref, t_ref, ls_ref, dy_ref, do_ref, dm_ref):
        i = pl.program_id(0)
        lat = (i % nt) >= nc
        gate = mg_ref[0, 0, gk:gk + 1, :]
        ov = o_ref[...]
        err = jnp.where(lat, x_ref[...] + gate * ov - t_ref[...], 0.0)
        ls_ref[...] = jnp.zeros_like(ls_ref)
        ls_ref[0, 0:1, :] = jnp.sum(err * err, axis=0, keepdims=True)
        dy = err * (1.0 / d)
        dy_ref[...] = dy
        do_ref[...] = (dy * gate).astype(bf16)

        @pl.when(_segment_start(cfg, i))
        def _():
            dm_ref[...] = jnp.zeros_like(dm_ref)

        dm_ref[0, 0, 2:3, :] += jnp.sum(dy * ov, axis=0, keepdims=True)

    row = _row_spec(cfg, d)
    t_spec = pl.BlockSpec((cfg.TR, d), lambda i: ((i // nt) * nlat + jnp.maximum((i % nt) - nc, 0), 0))
    return pl.pallas_call(
        body,
        out_shape=(jax.ShapeDtypeStruct((r_ // cfg.TR, SUBLANE, d), f32), jax.ShapeDtypeStruct((r_, d), f32),
                   jax.ShapeDtypeStruct((r_, d), bf16), jax.ShapeDtypeStruct((cfg.B, 2, MOD_ROWS, d), f32)),
        grid=(r_ // cfg.TR,),
        in_specs=[row, row, _mod_spec(cfg), t_spec],
        out_specs=(pl.BlockSpec((1, SUBLANE, d), lambda i: (i, 0, 0)), row, row, _mod_spec(cfg)),
        compiler_params=_params(1), name=name)(x1, o, modg, target)


def _swap16(y):
    w = y.shape[1]
    lane = lax.broadcasted_iota(jnp.int32, (1, w), 1)
    lo = jnp.bitwise_and(lane, 31) < 16
    return jnp.where(lo, pltpu.roll(y, w - 16, 1), pltpu.roll(y, 16, 1))


def _dot_split(v, m):
    hi = v.astype(bf16)
    lo = (v - hi.astype(f32)).astype(bf16)
    return jnp.dot(hi, m, preferred_element_type=f32) + jnp.dot(lo, m, preferred_element_type=f32)


def _head_sum(v, bd):
    return _dot_split(v, bd)


def _qkv_fwd(cfg, name, p, cs, gain, bd, rep):
    r_ = p.shape[0]
    nt = cfg.NT

    def body(p_ref, cs_ref, g_ref, bd_ref, rep_ref, q_ref, k_ref, v_ref):
        x = p_ref[:, 0:QKW].astype(f32)
        r = lax.rsqrt(_head_sum(x * x, bd_ref[...]) * (1.0 / HEAD_DIM) + EPS)
        y = (x * r) * g_ref[...]
        out = y * cs_ref[0] + _swap16(y) * cs_ref[1]
        q_ref[...] = out[:, 0:AW].astype(bf16)
        kr = out[:, AW:QKW].astype(bf16)
        k_ref[...] = jnp.dot(kr, rep_ref[...], preferred_element_type=f32).astype(bf16)
        vv = p_ref[:, QKW:QKVW].astype(bf16)
        v_ref[...] = jnp.dot(vv, rep_ref[...], preferred_element_type=f32).astype(bf16)

    row = _row_spec(cfg, AW)
    return pl.pallas_call(
        body, out_shape=(jax.ShapeDtypeStruct((r_, AW), bf16),) * 3, grid=(r_ // cfg.TR,),
        in_specs=[_row_spec(cfg, QKVW), pl.BlockSpec((2, cfg.TR, QKW), lambda i: (0, i % nt, 0)),
                  _const_spec(gain), _const_spec(bd), _const_spec(rep)],
        out_specs=(row, row, row), compiler_params=_params(1), name=name)(p, cs, gain, bd, rep)


def _qkv_bwd(cfg, name, p, dq, dkr, dvr, cs, gain, bd, rept):
    r_ = p.shape[0]
    nt = cfg.NT

    def body(p_ref, dq_ref, dk_ref, dv_ref, cs_ref, g_ref, bd_ref, rt_ref, dp_ref, dg_ref):
        i = pl.program_id(0)
        x = p_ref[:, 0:QKW].astype(f32)
        rt = rt_ref[...]
        dk = _dot_split(dk_ref[...], rt)
        g = jnp.concatenate([dq_ref[...], dk], axis=1)
        gy = g * cs_ref[0] + _swap16(g * cs_ref[1])
        bdv = bd_ref[...]
        r = lax.rsqrt(_head_sum(x * x, bdv) * (1.0 / HEAD_DIM) + EPS)
        xh = x * r
        dxh = gy * g_ref[...]
        dx = r * (dxh - xh * (_head_sum(dxh * xh, bdv) * (1.0 / HEAD_DIM)))
        dp_ref[:, 0:QKW] = dx.astype(bf16)
        dp_ref[:, QKW:QKVW] = _dot_split(dv_ref[...], rt).astype(bf16)

        @pl.when(i == 0)
        def _():
            dg_ref[...] = jnp.zeros_like(dg_ref)

        dg_ref[0:1, :] += jnp.sum(gy * xh, axis=0, keepdims=True)

    row = _row_spec(cfg, AW)
    return pl.pallas_call(
        body, out_shape=(jax.ShapeDtypeStruct((r_, QKVW), bf16), jax.ShapeDtypeStruct((SUBLANE, QKW), f32)),
        grid=(r_ // cfg.TR,),
        in_specs=[_row_spec(cfg, QKVW), row, row, row,
                  pl.BlockSpec((2, cfg.TR, QKW), lambda i: (0, i % nt, 0)),
                  _const_spec(gain), _const_spec(bd), _const_spec(rept)],
        out_specs=(_row_spec(cfg, QKVW), pl.BlockSpec((SUBLANE, QKW), lambda i: (0, 0))),
        compiler_params=_params(1), name=name)(p, dq, dkr, dvr, cs, gain, bd, rept)


def _head_masks():
    lane = lax.broadcasted_iota(jnp.int32, (1, GW), 1)
    return [jnp.logical_and(lane >= HEAD_DIM * h, lane < HEAD_DIM * (h + 1)) for h in range(Q_GROUP)]


def _attn_fwd(cfg, name, q, krep, vrep, comm=None):
    r_ = q.shape[0]
    tr, t, lc, nt, nc = cfg.TR, cfg.T, cfg.LC, cfg.NT, cfg.NC

    def body(q_ref, k_ref, v_ref, o_ref, l_ref):
        j = pl.program_id(2)
        masks = _head_masks()
        lane = lax.broadcasted_iota(jnp.int32, (1, LANE), 1)

        def run(nk):
            qv = q_ref[...]
            kv = k_ref[0:nk, :]
            vv = v_ref[0:nk, :]
            acc = jnp.zeros((tr, GW), f32)
            lse = jnp.zeros((tr, LANE), f32)
            for h0 in range(0, Q_GROUP, HEADS_PER_PASS_FWD):
                hs = range(h0, h0 + HEADS_PER_PASS_FWD)
                qs = jnp.concatenate([jnp.where(masks[h], qv, jnp.zeros_like(qv)) for h in hs], axis=0)
                s = lax.dot_general(qs, kv, NT_, preferred_element_type=f32) * ATTN_SCALE
                m = jnp.max(s, axis=1, keepdims=True)
                pr = jnp.exp(s - m)
                l = jnp.sum(pr, axis=1, keepdims=True)
                os_ = jnp.dot(pr.astype(bf16), vv, preferred_element_type=f32) / l
                ls = m + jnp.log(l)
                for i, h in enumerate(hs):
                    acc = acc + jnp.where(masks[h], os_[i * tr:(i + 1) * tr], 0.0)
                    lse = lse + jnp.where(lane == h, ls[i * tr:(i + 1) * tr], 0.0)
            o_ref[...] = acc
            l_ref[...] = lse

        @pl.when(j < nc)
        def _():
            run(lc)

        @pl.when(j >= nc)
        def _():
            run(t)

    qspec = pl.BlockSpec((tr, GW), lambda e, g, j: (e * nt + j, g))
    kspec = pl.BlockSpec((t, GW), lambda e, g, j: (e, g))
    return _hosting_call(
        body, comm, out_shape=(jax.ShapeDtypeStruct((r_, AW), f32), jax.ShapeDtypeStruct((r_, N_KV_HEADS * LANE), f32)),
        grid=(cfg.B, N_KV_HEADS, nt), in_specs=[qspec, kspec, kspec],
        out_specs=(qspec, pl.BlockSpec((tr, LANE), lambda e, g, j: (e * nt + j, g))),
        compiler_params=_params(3, VMEM_LIMIT), name=name, args=(q, krep, vrep))


def _hosting_call(body, comm, *, out_shape, grid, in_specs, out_specs, compiler_params, name, args, scratch_shapes=()):
    if comm is None:
        return pl.pallas_call(body, out_shape=out_shape, grid=grid, in_specs=in_specs, out_specs=out_specs,
                              scratch_shapes=list(scratch_shapes), compiler_params=compiler_params,
                              name=name)(*args), None
    xs, scatter = comm
    n, n_in, n_out, n_scr = len(xs), len(args), len(out_shape), len(scratch_shapes)
    last = [g - 1 for g in grid]

    def hosted(*refs):
        ins, cins = refs[:n_in], refs[n_in:n_in + n]
        outs, couts = refs[n_in + n:n_in + n + n_out], refs[n_in + n + n_out:n_in + 2 * n + n_out]
        scratch = refs[n_in + 2 * n + n_out:n_in + 2 * n + n_out + n_scr]
        sems = refs[n_in + 2 * n + n_out + n_scr:]
        ids = [pl.program_id(a) for a in range(len(grid))]
        is_first = functools.reduce(jnp.logical_and, [i == 0 for i in ids])
        is_last = functools.reduce(jnp.logical_and, [i == l for i, l in zip(ids, last)])

        @pl.when(is_first)
        def _():
            _exchange_start(cins, couts, sems, scatter)

        body(*ins, *outs, *scratch)

        @pl.when(is_last)
        def _():
            _exchange_wait(cins, couts, sems, scatter)

    anyspec = pl.BlockSpec(memory_space=pl.ANY)
    res = pl.pallas_call(
        hosted, out_shape=tuple(out_shape) + tuple(_exchange_out_shapes(xs, scatter)), grid=grid,
        in_specs=list(in_specs) + [anyspec] * n, out_specs=tuple(out_specs) + (anyspec,) * n,
        scratch_shapes=list(scratch_shapes) + _exchange_scratch(n), compiler_params=compiler_params,
        name=name)(*args, *xs)
    return tuple(res[:n_out]), list(res[n_out:])


def _attn_bwd(cfg, name, q, krep, vrep, o, lse, do, comm=None):
    r_ = q.shape[0]
    tr, t, lc, nt, nc = cfg.TR, cfg.T, cfg.LC, cfg.NT, cfg.NC

    def body(q_ref, k_ref, v_ref, o_ref, l_ref, do_ref, dq_ref, dk_ref, dv_ref):
        j = pl.program_id(2)
        masks = _head_masks()
        lane = lax.broadcasted_iota(jnp.int32, (1, LANE), 1)

        @pl.when(j == 0)
        def _():
            dk_ref[...] = jnp.zeros_like(dk_ref)
            dv_ref[...] = jnp.zeros_like(dv_ref)

        def run(nk):
            qv = q_ref[...]
            kv = k_ref[0:nk, :]
            vv = v_ref[0:nk, :]
            ov = o_ref[...]
            dov = do_ref[...]
            lv = l_ref[...]
            dq = jnp.zeros((tr, GW), f32)
            dk = jnp.zeros((nk, GW), f32)
            dv = jnp.zeros((nk, GW), f32)
            for h0 in range(0, Q_GROUP, HEADS_PER_PASS_BWD):
                hs = range(h0, h0 + HEADS_PER_PASS_BWD)
                qs = jnp.concatenate([jnp.where(masks[h], qv, jnp.zeros_like(qv)) for h in hs], axis=0)
                dos = jnp.concatenate([jnp.where(masks[h], dov, 0.0) for h in hs], axis=0)
                dosb = dos.astype(bf16)
                delta = jnp.sum(dos * jnp.concatenate([ov] * len(hs), axis=0), axis=1, keepdims=True)
                lses = jnp.concatenate([jnp.sum(jnp.where(lane == h, lv, 0.0), axis=1, keepdims=True) for h in hs],
                                       axis=0)
                s = lax.dot_general(qs, kv, NT_, preferred_element_type=f32) * ATTN_SCALE
                pr = jnp.exp(s - lses)
                dpr = lax.dot_general(dosb, vv, NT_, preferred_element_type=f32)
                ds = (pr * (dpr - delta) * ATTN_SCALE).astype(bf16)
                dqs = jnp.dot(ds, kv, preferred_element_type=f32)
                for i, h in enumerate(hs):
                    dq = dq + jnp.where(masks[h], dqs[i * tr:(i + 1) * tr], 0.0)
                dk = dk + lax.dot_general(ds, qs, TN, preferred_element_type=f32)
                dv = dv + lax.dot_general(pr.astype(bf16), dosb, TN, preferred_element_type=f32)
            dq_ref[...] = dq
            dk_ref[0:nk, :] += dk
            dv_ref[0:nk, :] += dv

        @pl.when(j < nc)
        def _():
            run(lc)

        @pl.when(j >= nc)
        def _():
            run(t)

    qspec = pl.BlockSpec((tr, GW), lambda e, g, j: (e * nt + j, g))
    kspec = pl.BlockSpec((t, GW), lambda e, g, j: (e, g))
    lspec = pl.BlockSpec((tr, LANE), lambda e, g, j: (e * nt + j, g))
    return _hosting_call(
        body, comm, out_shape=(jax.ShapeDtypeStruct((r_, AW), f32),) * 3,
        grid=(cfg.B, N_KV_HEADS, nt), in_specs=[qspec, kspec, kspec, qspec, lspec, qspec],
        out_specs=(qspec, kspec, kspec),
        compiler_params=_params(3, VMEM_LIMIT), name=name, args=(q, krep, vrep, o, lse, do))


def _segments(cfg):
    pad = CONV_PAD_ROWS
    return ((pad, 0, cfg.LC), (cfg.LC + 3 * pad, cfg.LC, cfg.S)), cfg.T + 4 * pad


def _fill_padded(buf, cfg, fn):
    buf[...] = jnp.zeros_like(buf)
    for off, lo, n in _segments(cfg)[0]:
        buf[off:off + n, :] = fn(lo, lo + n)


def _taps(buf, w_ref, base, taps, sign):
    acc = None
    for k in range(taps):
        term = w_ref[k:k + 1, :] * buf[pl.ds(base + sign * (k - taps // 2), CONV_CHUNK), :]
        acc = term if acc is None else acc + term
    return acc


def _fold_tiles(v):
    acc = v[0:SUBLANE]
    for i in range(1, v.shape[0] // SUBLANE):
        acc = acc + v[SUBLANE * i:SUBLANE * (i + 1)]
    return acc


def _tap_grads(buf, dy, acc_ref, base, taps):
    for k in range(taps):
        acc_ref[SUBLANE * k:SUBLANE * (k + 1), :] += _fold_tiles(
            buf[pl.ds(base + k - taps // 2, CONV_CHUNK), :] * dy)


def _conv_specs(cfg):
    t, cb, cw = cfg.T, cfg.CB, cfg.CW
    offs = (cfg.OFF_CONF, cfg.OFF_CONF + cw, cfg.OFF_SC, cfg.OFF_SC + cw, cfg.OFF_SC + 2 * cw)
    return [pl.BlockSpec((t, cb), lambda c, e, off=off: (e, off // cb + c)) for off in offs]


def _conv_fwd(cfg, name, p, w31, w3):
    r_ = p.shape[0]
    t, lc, cb, cw = cfg.T, cfg.LC, cfg.CB, cfg.CW
    k31, k3 = cfg.K31, cfg.K3

    segs, buf_rows = _segments(cfg)
    ch = CONV_CHUNK

    def body(a_ref, g_ref, bg_ref, cg_ref, xs_ref, w31_ref, w3_ref, hc_ref, z_ref, hbuf, ubuf):
        _fill_padded(hbuf, cfg, lambda lo, hi: a_ref[lo:hi, :].astype(f32) * jax.nn.sigmoid(g_ref[lo:hi, :].astype(f32)))
        _fill_padded(ubuf, cfg, lambda lo, hi: cg_ref[lo:hi, :].astype(f32) * xs_ref[lo:hi, :].astype(f32))
        bias = w31_ref[k31:k31 + 1, :]
        for off, lo, n in segs:
            def chunk(c, carry, off=off, lo=lo):
                base = off + pl.multiple_of(c * ch, ch)
                rows = pl.ds(pl.multiple_of(lo + c * ch, ch), ch)
                hc_ref[rows, :] = _taps(hbuf, w31_ref, base, k31, 1) + bias
                z_ref[rows, :] = (bg_ref[rows, :].astype(f32) * _taps(ubuf, w3_ref, base, k3, 1)).astype(bf16)
                return carry

            lax.fori_loop(0, n // ch, chunk, 0)

    ospec = pl.BlockSpec((t, cb), lambda c, e: (e, c))
    return pl.pallas_call(
        body, out_shape=(jax.ShapeDtypeStruct((r_, cw), f32), jax.ShapeDtypeStruct((r_, cw), bf16)),
        grid=(cw // cb, cfg.B),
        in_specs=_conv_specs(cfg) + [pl.BlockSpec((w31.shape[0], cb), lambda c, e: (0, c)),
                                     pl.BlockSpec((w3.shape[0], cb), lambda c, e: (0, c))],
        out_specs=(ospec, ospec), scratch_shapes=[pltpu.VMEM((buf_rows, cb), f32)] * 2,
        compiler_params=_params(2, VMEM_LIMIT), name=name)(p, p, p, p, p, w31, w3)


def _conv_bwd(cfg, name, p, dhc, dz, w31, w3, comm=None):
    r_ = p.shape[0]
    t, lc, cb, cw = cfg.T, cfg.LC, cfg.CB, cfg.CW
    k31, k3 = cfg.K31, cfg.K3

    segs, buf_rows = _segments(cfg)
    ch = CONV_CHUNK

    def body(a_ref, g_ref, bg_ref, cg_ref, xs_ref, dhc_ref, dz_ref, w31_ref, w3_ref,
             da_ref, dg_ref, dbg_ref, dcg_ref, dxs_ref, dw31_ref, dw3_ref, hbuf, dbuf, ubuf, ebuf, acc31, acc3):
        e = pl.program_id(1)

        @pl.when(e == 0)
        def _():
            dw31_ref[...] = jnp.zeros_like(dw31_ref)
            dw3_ref[...] = jnp.zeros_like(dw3_ref)

        _fill_padded(hbuf, cfg, lambda lo, hi: a_ref[lo:hi, :].astype(f32) * jax.nn.sigmoid(g_ref[lo:hi, :].astype(f32)))
        _fill_padded(dbuf, cfg, lambda lo, hi: dhc_ref[lo:hi, :])
        _fill_padded(ubuf, cfg, lambda lo, hi: cg_ref[lo:hi, :].astype(f32) * xs_ref[lo:hi, :].astype(f32))
        _fill_padded(ebuf, cfg, lambda lo, hi: dz_ref[lo:hi, :] * bg_ref[lo:hi, :].astype(f32))
        acc31[...] = jnp.zeros_like(acc31)
        acc3[...] = jnp.zeros_like(acc3)
        for off, lo, n in segs:
            def chunk(c, carry, off=off, lo=lo):
                base = off + pl.multiple_of(c * ch, ch)
                here = pl.ds(base, ch)
                rows = pl.ds(pl.multiple_of(lo + c * ch, ch), ch)
                dy = dbuf[here, :]
                dhh = _taps(dbuf, w31_ref, base, k31, -1)
                _tap_grads(hbuf, dy, acc31, base, k31)
                acc31[SUBLANE * k31:SUBLANE * (k31 + 1), :] += _fold_tiles(dy)
                sg = jax.nn.sigmoid(g_ref[rows, :].astype(f32))
                da_ref[rows, :] = (dhh * sg).astype(bf16)
                dg_ref[rows, :] = (dhh * hbuf[here, :] * (1.0 - sg)).astype(bf16)

                duc = ebuf[here, :]
                dbg_ref[rows, :] = (dz_ref[rows, :] * _taps(ubuf, w3_ref, base, k3, 1)).astype(bf16)
                du = _taps(ebuf, w3_ref, base, k3, -1)
                _tap_grads(ubuf, duc, acc3, base, k3)
                dcg_ref[rows, :] = (du * xs_ref[rows, :].astype(f32)).astype(bf16)
                dxs_ref[rows, :] = (du * cg_ref[rows, :].astype(f32)).astype(bf16)
                return carry

            lax.fori_loop(0, n // ch, chunk, 0)
        for k in range(k31 + 1):
            dw31_ref[k:k + 1, :] += jnp.sum(acc31[SUBLANE * k:SUBLANE * (k + 1), :], axis=0, keepdims=True)
        for k in range(k3):
            dw3_ref[k:k + 1, :] += jnp.sum(acc3[SUBLANE * k:SUBLANE * (k + 1), :], axis=0, keepdims=True)

    ospec = pl.BlockSpec((t, cb), lambda c, e: (e, c))
    w31_spec = pl.BlockSpec((w31.shape[0], cb), lambda c, e: (0, c))
    w3_spec = pl.BlockSpec((w3.shape[0], cb), lambda c, e: (0, c))
    return _hosting_call(
        body, comm,
        out_shape=(jax.ShapeDtypeStruct((r_, cw), bf16),) * 5
        + (jax.ShapeDtypeStruct(w31.shape, f32), jax.ShapeDtypeStruct(w3.shape, f32)),
        grid=(cw // cb, cfg.B),
        in_specs=_conv_specs(cfg) + [ospec, ospec, w31_spec, w3_spec],
        out_specs=(ospec,) * 5 + (w31_spec, w3_spec),
        scratch_shapes=[pltpu.VMEM((buf_rows, cb), f32)] * 4
        + [pltpu.VMEM((SUBLANE * w31.shape[0], cb), f32), pltpu.VMEM((SUBLANE * w3.shape[0], cb), f32)],
        compiler_params=_params(2, VMEM_LIMIT), name=name, args=(p, p, p, p, p, dhc, dz, w31, w3))


def _ln_silu_fwd(cfg, name, hc, lnp):
    r_, cw = hc.shape

    def body(x_ref, p_ref, o_ref):
        x = x_ref[...]
        mu = jnp.mean(x, axis=-1, keepdims=True)
        xc = x - mu
        rs = lax.rsqrt(jnp.mean(xc * xc, axis=-1, keepdims=True) + EPS)
        hn = (xc * rs) * p_ref[0:1, :] + p_ref[1:2, :]
        o_ref[...] = (hn * jax.nn.sigmoid(hn)).astype(bf16)

    row = _row_spec(cfg, cw)
    return pl.pallas_call(body, out_shape=jax.ShapeDtypeStruct((r_, cw), bf16), grid=(r_ // cfg.TR,),
                          in_specs=[row, _const_spec(lnp)], out_specs=row,
                          compiler_params=_params(1), name=name)(hc, lnp)


def _ln_silu_bwd(cfg, name, hc, dhs, lnp):
    r_, cw = hc.shape

    def body(x_ref, d_ref, p_ref, dx_ref, dp_ref):
        i = pl.program_id(0)
        x = x_ref[...]
        mu = jnp.mean(x, axis=-1, keepdims=True)
        xc = x - mu
        rs = lax.rsqrt(jnp.mean(xc * xc, axis=-1, keepdims=True) + EPS)
        xh = xc * rs
        gain = p_ref[0:1, :]
        hn = xh * gain + p_ref[1:2, :]
        sg = jax.nn.sigmoid(hn)
        dhn = d_ref[...] * (sg * (1.0 + hn * (1.0 - sg)))
        dxh = dhn * gain
        dx_ref[...] = rs * (dxh - jnp.mean(dxh, axis=-1, keepdims=True)
                            - xh * jnp.mean(dxh * xh, axis=-1, keepdims=True))

        @pl.when(i == 0)
        def _():
            dp_ref[...] = jnp.zeros_like(dp_ref)

        dp_ref[0:1, :] += jnp.sum(dhn * xh, axis=0, keepdims=True)
        dp_ref[1:2, :] += jnp.sum(dhn, axis=0, keepdims=True)

    row = _row_spec(cfg, cw)
    return pl.pallas_call(
        body, out_shape=(jax.ShapeDtypeStruct((r_, cw), f32), jax.ShapeDtypeStruct((SUBLANE, cw), f32)),
        grid=(r_ // cfg.TR,), in_specs=[row, row, _const_spec(lnp)],
        out_specs=(row, pl.BlockSpec((SUBLANE, cw), lambda i: (0, 0))),
        compiler_params=_params(1), name=name)(hc, dhs, lnp)


def _gate_fwd(cfg, name, p, ya, yc, ys):
    r_, d = ya.shape
    gb = cfg.GB
    base = cfg.OFF_GATE // gb
    per = d // gb

    tr = _divisor(r_, GATE_ROWS, cfg.TR)

    def body(g0_ref, g1_ref, g2_ref, ya_ref, yc_ref, ys_ref, o_ref):
        m = (jax.nn.sigmoid(g0_ref[...].astype(f32)) * ya_ref[...].astype(f32)
             + jax.nn.sigmoid(g1_ref[...].astype(f32)) * yc_ref[...].astype(f32)
             + jax.nn.sigmoid(g2_ref[...].astype(f32)) * ys_ref[...].astype(f32))
        o_ref[...] = m.astype(bf16)

    yspec = pl.BlockSpec((tr, gb), lambda i, c: (i, c))
    gspecs = [pl.BlockSpec((tr, gb), lambda i, c, k=k: (i, base + k * per + c)) for k in range(3)]
    return pl.pallas_call(body, out_shape=jax.ShapeDtypeStruct((r_, d), bf16), grid=(r_ // tr, per),
                          in_specs=gspecs + [yspec] * 3, out_specs=yspec,
                          compiler_params=_params(2), name=name)(p, p, p, ya, yc, ys)


def _gate_bwd(cfg, name, p, ya, yc, ys, dm):
    r_, d = ya.shape
    gb = cfg.GB
    base = cfg.OFF_GATE // gb
    per = d // gb

    tr = _divisor(r_, GATE_ROWS, cfg.TR)

    def body(g0_ref, g1_ref, g2_ref, ya_ref, yc_ref, ys_ref, dm_ref, da_ref, dc_ref, ds_ref, d0_ref, d1_ref, d2_ref):
        dmv = dm_ref[...]
        for g_ref, y_ref, dy_ref, dg_ref in ((g0_ref, ya_ref, da_ref, d0_ref), (g1_ref, yc_ref, dc_ref, d1_ref),
                                             (g2_ref, ys_ref, ds_ref, d2_ref)):
            sg = jax.nn.sigmoid(g_ref[...].astype(f32))
            dy_ref[...] = (dmv * sg).astype(bf16)
            dg_ref[...] = (dmv * y_ref[...].astype(f32) * sg * (1.0 - sg)).astype(bf16)

    yspec = pl.BlockSpec((tr, gb), lambda i, c: (i, c))
    gspecs = [pl.BlockSpec((tr, gb), lambda i, c, k=k: (i, base + k * per + c)) for k in range(3)]
    return pl.pallas_call(body, out_shape=(jax.ShapeDtypeStruct((r_, d), bf16),) * 6, grid=(r_ // tr, per),
                          in_specs=gspecs + [yspec] * 4, out_specs=(yspec,) * 6,
                          compiler_params=_params(2), name=name)(p, p, p, ya, yc, ys, dm)


def _swiglu_fwd(cfg, name, ff):
    r_, f2 = ff.shape
    fh = f2 // 2

    def body(a_ref, b_ref, o_ref):
        a = a_ref[...].astype(f32)
        o_ref[...] = (a * jax.nn.sigmoid(a) * b_ref[...].astype(f32)).astype(bf16)

    return pl.pallas_call(body, out_shape=jax.ShapeDtypeStruct((r_, fh), bf16), grid=(r_ // cfg.TR,),
                          in_specs=[_row_spec(cfg, fh, 0), _row_spec(cfg, fh, 1)], out_specs=_row_spec(cfg, fh),
                          compiler_params=_params(1, VMEM_LIMIT), name=name)(ff, ff)


def _swiglu_bwd(cfg, name, ff, ds):
    r_, f2 = ff.shape
    fh = f2 // 2

    def body(a_ref, b_ref, d_ref, o_ref):
        a = a_ref[...].astype(f32)
        sg = jax.nn.sigmoid(a)
        dsv = d_ref[...].astype(f32)
        o_ref[:, 0:fh] = (dsv * b_ref[...].astype(f32) * (sg * (1.0 + a * (1.0 - sg)))).astype(bf16)
        o_ref[:, fh:f2] = (dsv * a * sg).astype(bf16)

    return pl.pallas_call(body, out_shape=jax.ShapeDtypeStruct((r_, f2), bf16), grid=(r_ // cfg.TR,),
                          in_specs=[_row_spec(cfg, fh, 0), _row_spec(cfg, fh, 1), _row_spec(cfg, fh)],
                          out_specs=_row_spec(cfg, f2), compiler_params=_params(1, VMEM_LIMIT), name=name)(ff, ff, ds)


def _ada_fwd(name, call, w_ada, b_loc):
    nl, d, na = w_ada.shape
    nbp = call.shape[0]

    def body(c_ref, w_ref, b_ref, o_ref):
        cv = c_ref[...]
        a = (cv * jax.nn.sigmoid(cv)).astype(bf16)
        o_ref[0] = jnp.dot(a, w_ref[0].astype(bf16), preferred_element_type=f32) + b_ref[0]

    return pl.pallas_call(
        body, out_shape=jax.ShapeDtypeStruct((nl, nbp, na), f32), grid=(nl,),
        in_specs=[_const_spec(call), pl.BlockSpec((1, d, na), lambda l: (l, 0, 0)),
                  pl.BlockSpec((1, 1, na), lambda l: (l, 0, 0))],
        out_specs=pl.BlockSpec((1, nbp, na), lambda l: (l, 0, 0)),
        compiler_params=_params(1, VMEM_LIMIT), name=name)(call, w_ada, b_loc)


def _ada_bwd(name, call, w_ada, dmod, dmodc, cctx_row):
    nl, d, na = w_ada.shape
    nbp = call.shape[0]

    def body(c_ref, w_ref, dm_ref, dc_ref, gw_ref, gb_ref, gc_ref):
        l = pl.program_id(0)
        cv = c_ref[...]
        sg = jax.nn.sigmoid(cv)
        a = (cv * sg).astype(bf16)
        dctx = jnp.sum(dc_ref[0], axis=0, keepdims=True)
        rows = lax.broadcasted_iota(jnp.int32, (nbp, 1), 0)
        dm = jnp.where(rows == cctx_row, dctx, dm_ref[0])
        gw_ref[0] = lax.dot_general(a, dm.astype(bf16), TN, preferred_element_type=f32)
        gb_ref[0] = jnp.zeros((SUBLANE, na), f32)
        gb_ref[0, 0:1, :] = jnp.sum(dm, axis=0, keepdims=True)
        dc8 = jnp.broadcast_to(dctx, (SUBLANE, na)).astype(bf16)
        part = lax.dot_general(dc8, w_ref[0].astype(bf16), NT_, preferred_element_type=f32)
        cc = c_ref[cctx_row:cctx_row + 1, :]
        sc = jax.nn.sigmoid(cc)
        part = part * (sc * (1.0 + cc * (1.0 - sc)))

        @pl.when(l == 0)
        def _():
            gc_ref[...] = jnp.zeros_like(gc_ref)

        gc_ref[...] += part

    return pl.pallas_call(
        body,
        out_shape=(jax.ShapeDtypeStruct((nl, d, na), f32), jax.ShapeDtypeStruct((nl, SUBLANE, na), f32),
                   jax.ShapeDtypeStruct((SUBLANE, d), f32)),
        grid=(nl,),
        in_specs=[_const_spec(call), pl.BlockSpec((1, d, na), lambda l: (l, 0, 0)),
                  pl.BlockSpec((1, nbp, na), lambda l: (l, 0, 0)),
                  pl.BlockSpec((1, NDEV, na), lambda l: (l, 0, 0))],
        out_specs=(pl.BlockSpec((1, d, na), lambda l: (l, 0, 0)), pl.BlockSpec((1, SUBLANE, na), lambda l: (l, 0, 0)),
                   pl.BlockSpec((SUBLANE, d), lambda l: (0, 0))),
        compiler_params=_params(1, VMEM_LIMIT), name=name)(call, w_ada, dmod, dmodc)


def _adamw(name, w, g, m, v):
    rows, cols = w.shape
    tr = _divisor(rows, max(SUBLANE, (1 << 19) // cols), SUBLANE)
    c1 = 1.0 / (1.0 - ADAM_B1 ** ADAM_STEP)
    c2 = 1.0 / (1.0 - ADAM_B2 ** ADAM_STEP)

    def body(w_ref, g_ref, m_ref, v_ref, d_ref, nm_ref, nv_ref):
        gv = g_ref[...]
        nm = ADAM_B1 * m_ref[...] + (1.0 - ADAM_B1) * gv
        nv = ADAM_B2 * v_ref[...] + (1.0 - ADAM_B2) * (gv * gv)
        nm_ref[...] = nm
        nv_ref[...] = nv
        d_ref[...] = -ADAM_LR * ((nm * c1) / (jnp.sqrt(nv * c2) + ADAM_EPS) + ADAM_WD * w_ref[...])

    spec = pl.BlockSpec((tr, cols), lambda i: (i, 0))
    return pl.pallas_call(body, out_shape=(jax.ShapeDtypeStruct((rows, cols), f32),) * 3, grid=(rows // tr,),
                          in_specs=[spec] * 4, out_specs=(spec,) * 3,
                          compiler_params=_params(1, VMEM_LIMIT), name=name)(w, g, m, v)


def _sum_slots(name, x):
    nd, rows, cols = x.shape
    tr = _divisor(rows, max(SUBLANE, (1 << 18) // cols), SUBLANE)

    def body(x_ref, o_ref):
        acc = x_ref[0].astype(f32)
        for s in range(1, nd):
            acc = acc + x_ref[s].astype(f32)
        o_ref[...] = acc

    return pl.pallas_call(body, out_shape=jax.ShapeDtypeStruct((rows, cols), f32), grid=(rows // tr,),
                          in_specs=[pl.BlockSpec((nd, tr, cols), lambda i: (0, i, 0))],
                          out_specs=pl.BlockSpec((tr, cols), lambda i: (i, 0)),
                          compiler_params=_params(1, VMEM_LIMIT), name=name)(x)


def _peer(k):
    x, y, c = (lax.axis_index(a) for a in AXES)
    px = 1 - x if k & 4 else x
    py = 1 - y if k & 2 else y
    pc = 1 - c if k & 1 else c
    return (px, py, pc), 4 * px + 2 * py + pc


def _exchange_out_shapes(xs, scatter):
    return [jax.ShapeDtypeStruct(xa.shape if scatter else (NDEV,) + xa.shape, xa.dtype) for xa in xs]


def _exchange_scratch(n):
    nrel = NDEV - 1
    return [pltpu.SemaphoreType.DMA((n * nrel,)), pltpu.SemaphoreType.DMA((n * nrel,)), pltpu.SemaphoreType.DMA((n,))]


def _exchange_copies(ins, outs, sems, scatter, receiving):
    send_sems, recv_sems, _ = sems
    nrel = NDEV - 1
    x, y, c = (lax.axis_index(a) for a in AXES)
    me = 4 * x + 2 * y + c
    copies = []
    for a in range(len(ins)):
        for k in range(1, NDEV):
            peer, pidx = _peer(k)
            src = ins[a].at[pidx] if scatter else ins[a]
            copies.append(pltpu.make_async_remote_copy(
                src_ref=src, dst_ref=outs[a].at[pidx if receiving else me], send_sem=send_sems.at[a * nrel + k - 1],
                recv_sem=recv_sems.at[a * nrel + k - 1], device_id=peer, device_id_type=pl.DeviceIdType.MESH))
    return copies


def _exchange_local(ins, outs, sems, scatter):
    x, y, c = (lax.axis_index(a) for a in AXES)
    me = 4 * x + 2 * y + c
    return [pltpu.make_async_copy(ins[a].at[me] if scatter else ins[a], outs[a].at[me], sems[2].at[a])
            for a in range(len(ins))]


CHIP_RELATIONS = (2, 4, 6)
SIBLING = 1


def _gather_copy(ins, outs, sems, a, slot, src, block, to):
    nrel = NDEV - 1
    return pltpu.make_async_remote_copy(
        src_ref=src, dst_ref=outs[a].at[block], send_sem=sems[0].at[a * nrel + slot],
        recv_sem=sems[1].at[a * nrel + slot], device_id=to, device_id_type=pl.DeviceIdType.MESH)


def _exchange_start(ins, outs, sems, scatter):
    for cp in _exchange_local(ins, outs, sems, scatter):
        cp.start()
    if scatter:
        for cp in _exchange_copies(ins, outs, sems, scatter, False):
            cp.start()
        return
    x, y, c = (lax.axis_index(a) for a in AXES)
    me = 4 * x + 2 * y + c
    for a in range(len(ins)):
        for slot, k in enumerate((SIBLING,) + CHIP_RELATIONS):
            _gather_copy(ins, outs, sems, a, slot, ins[a], me, _peer(k)[0]).start()


def _exchange_wait(ins, outs, sems, scatter):
    if scatter:
        for cp in _exchange_copies(ins, outs, sems, scatter, True):
            cp.wait_recv()
            cp.wait_send()
    else:
        x, y, c = (lax.axis_index(a) for a in AXES)
        me = 4 * x + 2 * y + c
        sibling, sibling_idx = _peer(SIBLING)
        n = len(ins)
        passed = []
        for a in range(n):
            for j, k in enumerate(CHIP_RELATIONS):
                peer, pidx = _peer(k)
                _gather_copy(ins, outs, sems, a, 1 + j, ins[a], pidx, peer).wait_recv()
                fwd = _gather_copy(ins, outs, sems, a, 4 + j, outs[a].at[pidx], pidx, sibling)
                fwd.start()
                passed.append(fwd)
        for a in range(n):
            _gather_copy(ins, outs, sems, a, 0, ins[a], sibling_idx, sibling).wait_recv()
            for j, k in enumerate(CHIP_RELATIONS):
                _gather_copy(ins, outs, sems, a, 4 + j, ins[a], _peer(k | SIBLING)[1], sibling).wait_recv()
        for a in range(n):
            for slot, k in enumerate((SIBLING,) + CHIP_RELATIONS):
                _gather_copy(ins, outs, sems, a, slot, ins[a], me, _peer(k)[0]).wait_send()
        for fwd in passed:
            fwd.wait_send()
    for cp in _exchange_local(ins, outs, sems, scatter):
        cp.wait()


def _exchange(name, xs, scatter):
    n = len(xs)

    def body(*refs):
        ins, outs, sems = refs[:n], refs[n:2 * n], refs[2 * n:]
        _exchange_start(ins, outs, sems, scatter)
        _exchange_wait(ins, outs, sems, scatter)

    anyspec = pl.BlockSpec(memory_space=pl.ANY)
    outs = pl.pallas_call(
        body, out_shape=tuple(_exchange_out_shapes(xs, scatter)), in_specs=[anyspec] * n, out_specs=(anyspec,) * n,
        scratch_shapes=_exchange_scratch(n), name=name)(*xs)
    return list(outs)


def _all_gather(name, xs):
    return _exchange(name, xs, False)


def _all_to_all(name, xs):
    return _exchange(name, xs, True)


def _rope_table(cfg):
    s, lc = cfg.S, cfg.LC
    rows = s // GRID_W
    r_ids, c_ids = jnp.meshgrid(jnp.arange(rows), jnp.arange(GRID_W), indexing="ij")
    r_ids = r_ids.reshape(-1).astype(f32)
    c_ids = c_ids.reshape(-1).astype(f32)
    freqs = ROPE_THETA ** (-jnp.arange(0, AXIS_DIM, 2, dtype=f32) / AXIS_DIM)
    ang_r = r_ids[:, None] * freqs
    ang_c = c_ids[:, None] * freqs
    cos = jnp.concatenate([jnp.cos(ang_r), jnp.cos(ang_r), jnp.cos(ang_c), jnp.cos(ang_c)], axis=1)
    sin = jnp.concatenate([-jnp.sin(ang_r), jnp.sin(ang_r), -jnp.sin(ang_c), jnp.sin(ang_c)], axis=1)
    cos = jnp.concatenate([jnp.ones((lc, HEAD_DIM), f32), cos], axis=0)
    sin = jnp.concatenate([jnp.zeros((lc, HEAD_DIM), f32), sin], axis=0)
    reps = QKW // HEAD_DIM
    return jnp.stack([jnp.tile(cos, (1, reps)), jnp.tile(sin, (1, reps))])


def _block_diag_ones():
    idx = np.arange(QKW) // HEAD_DIM
    return jnp.asarray((idx[:, None] == idx[None, :]).astype(np.float32))


def _replicate_matrix():
    src = np.arange(KVW)
    dst = np.arange(AW)
    m = (src[:, None] // HEAD_DIM == dst[None, :] // GW) & (src[:, None] % HEAD_DIM == dst[None, :] % HEAD_DIM)
    return m.astype(np.float32)


def _pack_flat(parts):
    flat = jnp.concatenate([p.reshape(-1) for p in parts])
    n = flat.shape[0]
    unit = SUBLANE * LANE
    total = -(-n // unit) * unit
    flat = jnp.pad(flat, (0, total - n))
    return flat.reshape(total // LANE, LANE)


def _unpack_flat(flat, shapes):
    flat = flat.reshape(flat.shape[:-2] + (-1,))
    out, off = [], 0
    for shp in shapes:
        size = int(np.prod(shp))
        out.append(flat[..., off:off + size].reshape(flat.shape[:-1] + tuple(shp)))
        off += size
    return out


def kernel(x, c, ctx, c_ctx, w_ada, b_ada, w_in, q_norm, k_norm, w_attn_o, conf_dw_w, conf_dw_b, conf_ln_g, conf_ln_b, w_conf_out, sc_dw_w, w_sc_out, w_mix_out, w_ffn_in, w_ffn_out, loss_target, m_c_ctx, m_w_ada, m_b_ada, m_w_in, m_q_norm, m_k_norm, m_w_attn_o, m_conf_dw_w, m_conf_dw_b, m_conf_ln_g, m_conf_ln_b, m_w_conf_out, m_sc_dw_w, m_w_sc_out, m_w_mix_out, m_w_ffn_in, m_w_ffn_out, v_c_ctx, v_w_ada, v_b_ada, v_w_in, v_q_norm, v_k_norm, v_w_attn_o, v_conf_dw_w, v_conf_dw_b, v_conf_ln_g, v_conf_ln_b, v_w_conf_out, v_sc_dw_w, v_w_sc_out, v_w_mix_out, v_w_ffn_in, v_w_ffn_out):
    cfg = _Cfg()
    cfg.B, cfg.S, cfg.D = x.shape
    cfg.LC = ctx.shape[1]
    cfg.T = cfg.LC + cfg.S
    cfg.TR = min(256, cfg.LC)
    assert cfg.LC % cfg.TR == 0 and cfg.S % cfg.TR == 0 and cfg.S % GRID_W == 0
    cfg.NT, cfg.NC = cfg.T // cfg.TR, cfg.LC // cfg.TR
    cfg.R = cfg.B * cfg.T
    nl = w_in.shape[0]
    b, d = cfg.B, cfg.D
    cfg.CW = conf_dw_b.shape[1]
    cfg.K31, cfg.K3 = conf_dw_w.shape[1], sc_dw_w.shape[1]
    assert w_sc_out.shape[1] == cfg.CW and cfg.CW % LANE == 0
    assert cfg.K3 // 2 <= cfg.K31 // 2 <= CONV_PAD_ROWS and cfg.LC % CONV_CHUNK == 0 and cfg.S % CONV_CHUNK == 0
    cfg.CB = LANE
    cfg.OFF_CONF = QKVW
    cfg.OFF_SC = cfg.OFF_CONF + 2 * cfg.CW
    cfg.OFF_GATE = cfg.OFF_SC + 3 * cfg.CW
    n_in = w_in.shape[2] * NDEV
    assert n_in == cfg.OFF_GATE + 3 * d and w_attn_o.shape[1] == AW
    cfg.GB = math.gcd(cfg.OFF_GATE, d)
    assert cfg.GB % LANE == 0
    fh = w_ffn_out.shape[1] * NDEV
    na = w_ada.shape[2]
    cw8 = cfg.CW // NDEV

    xi, yi, ci = (lax.axis_index(a) for a in AXES)
    me = 4 * xi + 2 * yi + ci

    small_shapes = [c.shape, conf_dw_w.shape, sc_dw_w.shape]
    (g0,) = _all_gather("gather_small", [_pack_flat([c, conf_dw_w, sc_dw_w])])
    c_all, cw_all, sw_all = _unpack_flat(g0, small_shapes)
    nb = NDEV * b
    nbp = -(-(nb + 1) // SUBLANE) * SUBLANE
    call = jnp.concatenate([c_all.reshape(nb, d), c_ctx[None, :], jnp.zeros((nbp - nb - 1, d), f32)], axis=0)
    w31_full = jnp.moveaxis(cw_all, 0, 2).reshape(nl, cfg.K31, cfg.CW)
    w3_full = jnp.moveaxis(sw_all, 0, 2).reshape(nl, cfg.K3, cfg.CW)
    k31p = -(-(cfg.K31 + 1) // SUBLANE) * SUBLANE
    w31b = jnp.concatenate([w31_full, conf_dw_b[:, None, :], jnp.zeros((nl, k31p - cfg.K31 - 1, cfg.CW), f32)], axis=1)
    w3p = jnp.concatenate([w3_full, jnp.zeros((nl, SUBLANE - cfg.K3, cfg.CW), f32)], axis=1)

    b_loc = lax.dynamic_slice(b_ada, (0, me * na), (nl, na))[:, None, :]
    mod_loc = _ada_fwd("ada_fwd", call, w_ada, b_loc)
    (mod_g,) = _all_gather("gather_mod", [mod_loc.reshape(nl * nbp, na)])
    mod_full = jnp.transpose(mod_g.reshape(NDEV, nl, nbp, na), (1, 2, 0, 3)).reshape(nl, nbp, N_MOD, d)
    mod_lat = lax.dynamic_slice_in_dim(mod_full, me * b, b, axis=1)
    mod_ctx = jnp.broadcast_to(mod_full[:, nb][:, None], (nl, b, N_MOD, d))
    mod = jnp.stack([mod_ctx, mod_lat], axis=2)
    mod = jnp.pad(mod, ((0, 0), (0, 0), (0, 0), (0, MOD_ROWS - N_MOD), (0, 0)))

    cs = _rope_table(cfg)
    bd = _block_diag_ones().astype(bf16)
    rep_np = _replicate_matrix()
    rep = jnp.asarray(rep_np, dtype=bf16)
    rept = jnp.asarray(rep_np.T, dtype=bf16)
    reps_q, reps_k = AW // HEAD_DIM, KVW // HEAD_DIM

    xin = jnp.concatenate([ctx, x], axis=1).reshape(cfg.R, d)
    target = loss_target.reshape(b * cfg.S, d)
    saved = []
    xcur, ocur = xin, None

    def weight_shards(l):
        return [jnp.transpose(w_in[l]).astype(bf16), jnp.transpose(w_attn_o[l]).astype(bf16),
                jnp.transpose(w_conf_out[l]).astype(bf16), jnp.transpose(w_sc_out[l]).astype(bf16),
                w_mix_out[l].astype(bf16), jnp.transpose(w_ffn_in[l]).astype(bf16), w_ffn_out[l].astype(bf16)]

    def whole(g):
        return g.reshape(NDEV * g.shape[1], g.shape[2])

    shards0 = weight_shards(0)
    (w_first,) = _all_gather("gather_w0", shards0[:1])
    w_rest = None
    for l in range(nl):
        win_t = whole(w_first)
        gain = jnp.concatenate([jnp.tile(q_norm[l], reps_q), jnp.tile(k_norm[l], reps_k)])[None, :]
        lnp = jnp.concatenate([conf_ln_g[l][None], conf_ln_b[l][None], jnp.zeros((SUBLANE - 2, cfg.CW), f32)], axis=0)
        nxt = weight_shards(l + 1) if l + 1 < nl else None

        if l == 0:
            x0, h = _norm_fwd(cfg, f"norm_a{l}", xcur, None, None, 0, mod[l], 0)
            p, w_rest = _mm(h, win_t, "nt", ACT, f"mm_in{l}", comm=(shards0[1:], False))
        else:
            x0, h = _norm_fwd(cfg, f"norm_a{l}", xcur, ocur, mod[l - 1], 5, mod[l], 0)
            p = _mm(h, win_t, "nt", ACT, f"mm_in{l}")
        wao_t, wco_t, wso_t, wmix, wfi_t, wfo = [whole(g) for g in w_rest]
        q, krep, vrep = _qkv_fwd(cfg, f"qkv{l}", p, cs, gain, bd, rep)
        (attn, lse), w_rest = _attn_fwd(cfg, f"attn{l}", q, krep, vrep,
                                        comm=None if nxt is None else (nxt[1:], False))
        hc, z = _conv_fwd(cfg, f"conv{l}", p, w31b[l], w3p[l])
        hs = _ln_silu_fwd(cfg, f"lnsilu{l}", hc, lnp)
        ya = _mm(attn, wao_t, "nt", ACT, f"mm_ao{l}")
        yc = _mm(hs, wco_t, "nt", ACT, f"mm_co{l}")
        ys = _mm(z, wso_t, "nt", ACT, f"mm_so{l}")
        merged = _gate_fwd(cfg, f"gate{l}", p, ya, yc, ys)
        mixed = _mm(merged, wmix, "nn", f32, f"mm_mix{l}")
        x1, h2 = _norm_fwd(cfg, f"norm_b{l}", x0, mixed, mod[l], 2, mod[l], 3)
        if nxt is None:
            ff = _mm(h2, wfi_t, "nt", ACT, f"mm_fi{l}")
        else:
            ff, (w_first,) = _mm(h2, wfi_t, "nt", ACT, f"mm_fi{l}", comm=(nxt[:1], False))
        sw = _swiglu_fwd(cfg, f"swiglu{l}", ff)
        o = _mm(sw, wfo, "nn", f32, f"mm_fo{l}")
        saved.append(dict(x0=x0, h=h, p=p, q=q, krep=krep, vrep=vrep, attn=attn, lse=lse, hc=hc, z=z, hs=hs,
                          ya=ya, yc=yc, ys=ys, merged=merged, mixed=mixed, x1=x1, h2=h2, ff=ff, sw=sw, o=o,
                          oprev=ocur, gain=gain, lnp=lnp,
                          w=(win_t, wao_t, wco_t, wso_t, wmix, wfi_t, wfo)))
        xcur, ocur = x1, o

    lsum, dres, do, dm_loss = _loss_head(cfg, "loss", xcur, ocur, mod[nl - 1], 5, target)
    loss = lax.psum((0.5 / d) * jnp.sum(lsum), AXES)

    dmod_rows = [[None] * N_MOD for _ in range(nl)]
    dmod_rows[nl - 1][5] = dm_loss[:, :, 2]
    g_qn, g_kn, g_w31, g_w3, g_ln = [None] * nl, [None] * nl, [None] * nl, [None] * nl, [None] * nl
    gbig = [None] * nl
    pending = None
    pending_early = None

    def blocks(g):
        return g.reshape(NDEV, g.shape[0] // NDEV, g.shape[1])

    def sum_received(l, recv):
        gbig[l] = [_sum_slots(f"sum_g{l}_{i}", r) for i, r in enumerate(recv)]

    for l in reversed(range(nl)):
        sv = saved[l]
        win_t, wao_t, wco_t, wso_t, wmix, wfi_t, wfo = sv["w"]
        ds = _mm(do, wfo, "nt", ACT, f"mm_dsw{l}")
        g_wfo = _mm(sv["sw"], do, "tn", bf16, f"mm_gfo{l}")
        dff = _swiglu_bwd(cfg, f"swiglu_b{l}", sv["ff"], ds)
        dh2 = _mm(dff, wfi_t, "nn", f32, f"mm_dh2{l}")
        g_wfi = _mm(dff, sv["h2"], "tn", bf16, f"mm_gfi{l}")
        dx0p, dmixed, dm_b = _norm_bwd(cfg, f"norm_b_b{l}", sv["x1"], sv["mixed"], dh2, dres, mod[l], 2, mod[l], 3)
        dmod_rows[l][3], dmod_rows[l][4], dmod_rows[l][2] = dm_b[:, :, 0], dm_b[:, :, 1], dm_b[:, :, 2]
        dmerged = _mm(dmixed, wmix, "nt", f32, f"mm_dmg{l}")
        g_wmix = _mm(sv["merged"], dmixed, "tn", bf16, f"mm_gmix{l}")
        dya, dyc, dys, dg0, dg1, dg2 = _gate_bwd(cfg, f"gate_b{l}", sv["p"], sv["ya"], sv["yc"], sv["ys"], dmerged)
        dattn = _mm(dya, wao_t, "nn", f32, f"mm_dat{l}")
        g_wao = _mm(dya, sv["attn"], "tn", bf16, f"mm_gao{l}")
        dhs = _mm(dyc, wco_t, "nn", f32, f"mm_dhs{l}")
        g_wco = _mm(dyc, sv["hs"], "tn", bf16, f"mm_gco{l}")
        dz = _mm(dys, wso_t, "nn", f32, f"mm_dz{l}")
        g_wso = _mm(dys, sv["z"], "tn", bf16, f"mm_gso{l}")
        dhc, g_ln[l] = _ln_silu_bwd(cfg, f"lnsilu_b{l}", sv["hc"], dhs, sv["lnp"])
        (da, dg, dbg, dcg, dxs, g_w31[l], g_w3[l]), recv = _conv_bwd(
            cfg, f"conv_b{l}", sv["p"], dhc, dz, w31b[l], w3p[l], comm=None if pending is None else ([pending], True))
        if pending is not None:
            sum_received(l + 1, recv + pending_early)
        early = [blocks(g) for g in (g_wao, g_wco, g_wso, g_wmix, g_wfi, g_wfo)]
        (dq, dkr, dvr), pending_early = _attn_bwd(cfg, f"attn_b{l}", sv["q"], sv["krep"], sv["vrep"], sv["attn"],
                                                  sv["lse"], dattn, comm=(early, True))
        dpq, dgain = _qkv_bwd(cfg, f"qkv_b{l}", sv["p"], dq, dkr, dvr, cs, sv["gain"], bd, rept)
        g_qn[l] = jnp.sum(dgain[0, :AW].reshape(reps_q, HEAD_DIM), axis=0)
        g_kn[l] = jnp.sum(dgain[0, AW:].reshape(reps_k, HEAD_DIM), axis=0)
        dp = jnp.concatenate([dpq, da, dg, dbg, dcg, dxs, dg0, dg1, dg2], axis=1)
        dh = _mm(dp, win_t, "nn", f32, f"mm_dh{l}")
        g_win = _mm(dp, sv["h"], "tn", bf16, f"mm_gin{l}")
        if l == 0:
            dres, _, dm_a = _norm_bwd(cfg, f"norm_a_b{l}", sv["x0"], None, dh, dx0p, None, 0, mod[l], 0)
        else:
            dres, do, dm_a = _norm_bwd(cfg, f"norm_a_b{l}", sv["x0"], sv["oprev"], dh, dx0p, mod[l - 1], 5, mod[l], 0)
            dmod_rows[l - 1][5] = dm_a[:, :, 2]
        dmod_rows[l][0], dmod_rows[l][1] = dm_a[:, :, 0], dm_a[:, :, 1]

        pending = blocks(g_win)

    sum_received(0, _all_to_all("scatter_g0", [pending]) + pending_early)
    grad_x = dres.reshape(b, cfg.T, d)[:, cfg.LC:, :]

    rb = -(-(b + 1) // SUBLANE) * SUBLANE
    dmod_l = jnp.stack([jnp.stack(rows, axis=2) for rows in dmod_rows])
    d_lat = dmod_l[:, :, 1].reshape(nl, b, NDEV, na)
    d_ctx = jnp.sum(dmod_l[:, :, 0], axis=1).reshape(nl, 1, NDEV, na)
    send = jnp.concatenate([d_lat, d_ctx, jnp.zeros((nl, rb - b - 1, NDEV, na), f32)], axis=1)
    send = jnp.transpose(send, (2, 0, 1, 3)).reshape(NDEV, nl * rb, na)
    (recv,) = _all_to_all("scatter_dmod", [send])
    recv = recv.reshape(NDEV, nl, rb, na)
    dmod_ex = jnp.transpose(recv[:, :, :b], (1, 0, 2, 3)).reshape(nl, nb, na)
    dmod_ex = jnp.pad(dmod_ex, ((0, 0), (0, nbp - nb), (0, 0)))
    dmodc = jnp.transpose(recv[:, :, b], (1, 0, 2))
    g_wada, g_bloc, g_cctx = _ada_bwd("ada_bwd", call, w_ada, dmod_ex, dmodc, nb)

    g_bada = lax.dynamic_update_slice(jnp.zeros((nl, N_MOD * d), f32), g_bloc[:, 0], (0, me * na))
    g_w31s = jnp.stack(g_w31)
    small_parts = [g_cctx[0], g_bada, jnp.stack(g_qn), jnp.stack(g_kn), g_w31s[:, :cfg.K31], g_w31s[:, cfg.K31],
                   jnp.stack([g[0] for g in g_ln]), jnp.stack([g[1] for g in g_ln]),
                   jnp.stack(g_w3)[:, :cfg.K3]]
    small_part_shapes = [p_.shape for p_ in small_parts]
    (gs,) = _all_gather("gather_gsmall", [_pack_flat(small_parts)])
    gsum = _sum_slots("sum_gsmall", gs)
    (gr_cctx, gr_bada, gr_qn, gr_kn, gr_w31, gr_b31, gr_lng, gr_lnb, gr_w3) = _unpack_flat(gsum, small_part_shapes)
    gr_w31 = lax.dynamic_slice_in_dim(gr_w31, me * cw8, cw8, axis=2)
    gr_w3 = lax.dynamic_slice_in_dim(gr_w3, me * cw8, cw8, axis=2)

    def big(i, transposed):
        g = jnp.stack([gbig[l][i] for l in range(nl)])
        return jnp.transpose(g, (0, 2, 1)) if transposed else g

    grads = {
        "c_ctx": gr_cctx, "w_ada": g_wada, "b_ada": gr_bada, "w_in": big(0, True), "q_norm": gr_qn, "k_norm": gr_kn,
        "w_attn_o": big(1, True), "conf_dw_w": gr_w31, "conf_dw_b": gr_b31, "conf_ln_g": gr_lng, "conf_ln_b": gr_lnb,
        "w_conf_out": big(2, True), "sc_dw_w": gr_w3, "w_sc_out": big(3, True), "w_mix_out": big(4, False),
        "w_ffn_in": big(5, True), "w_ffn_out": big(6, False)}
    weights = dict(c_ctx=c_ctx, w_ada=w_ada, b_ada=b_ada, w_in=w_in, q_norm=q_norm, k_norm=k_norm, w_attn_o=w_attn_o,
                   conf_dw_w=conf_dw_w, conf_dw_b=conf_dw_b, conf_ln_g=conf_ln_g, conf_ln_b=conf_ln_b,
                   w_conf_out=w_conf_out, sc_dw_w=sc_dw_w, w_sc_out=w_sc_out, w_mix_out=w_mix_out,
                   w_ffn_in=w_ffn_in, w_ffn_out=w_ffn_out)
    m_in = dict(c_ctx=m_c_ctx, w_ada=m_w_ada, b_ada=m_b_ada, w_in=m_w_in, q_norm=m_q_norm, k_norm=m_k_norm,
                w_attn_o=m_w_attn_o, conf_dw_w=m_conf_dw_w, conf_dw_b=m_conf_dw_b, conf_ln_g=m_conf_ln_g,
                conf_ln_b=m_conf_ln_b, w_conf_out=m_w_conf_out, sc_dw_w=m_sc_dw_w, w_sc_out=m_w_sc_out,
                w_mix_out=m_w_mix_out, w_ffn_in=m_w_ffn_in, w_ffn_out=m_w_ffn_out)
    v_in = dict(c_ctx=v_c_ctx, w_ada=v_w_ada, b_ada=v_b_ada, w_in=v_w_in, q_norm=v_q_norm, k_norm=v_k_norm,
                w_attn_o=v_w_attn_o, conf_dw_w=v_conf_dw_w, conf_dw_b=v_conf_dw_b, conf_ln_g=v_conf_ln_g,
                conf_ln_b=v_conf_ln_b, w_conf_out=v_w_conf_out, sc_dw_w=v_sc_dw_w, w_sc_out=v_w_sc_out,
                w_mix_out=v_w_mix_out, w_ffn_in=v_w_ffn_in, w_ffn_out=v_w_ffn_out)
    names = list(weights)
    big_names = ("w_ada", "w_in", "w_attn_o", "w_conf_out", "w_sc_out", "w_mix_out", "w_ffn_in", "w_ffn_out")
    small_names = [n for n in names if n not in big_names]
    delta, new_m, new_v = {}, {}, {}
    for n in big_names:
        shp = weights[n].shape
        two_d = (shp[0] * shp[1], shp[2])
        dl, nm, nv = _adamw(f"adamw_{n}", weights[n].reshape(two_d), grads[n].reshape(two_d),
                            m_in[n].reshape(two_d), v_in[n].reshape(two_d))
        delta[n], new_m[n], new_v[n] = dl.reshape(shp), nm.reshape(shp), nv.reshape(shp)
    sshapes = [weights[n].shape for n in small_names]
    dl, nm, nv = _adamw("adamw_small", _pack_flat([weights[n] for n in small_names]),
                        _pack_flat([grads[n] for n in small_names]), _pack_flat([m_in[n] for n in small_names]),
                        _pack_flat([v_in[n] for n in small_names]))
    for n, a_, b_, c_ in zip(small_names, _unpack_flat(dl, sshapes), _unpack_flat(nm, sshapes), _unpack_flat(nv, sshapes)):
        delta[n], new_m[n], new_v[n] = a_, b_, c_

    return (loss, grad_x, *[grads[n] for n in names], *[delta[n] for n in names],
            *[new_m[n] for n in names], *[new_v[n] for n in names])
```

```python
import functools
import math

import numpy as np
import jax
import jax.numpy as jnp
from jax import lax
from jax.experimental import pallas as pl
from jax.experimental.pallas import tpu as pltpu

f32 = jnp.float32
bf16 = jnp.bfloat16
ACT = bf16

NDEV = 8
AXES = ("x", "y", "c")
HEAD_DIM = 64
N_Q_HEADS = 8
N_KV_HEADS = 2
Q_GROUP = N_Q_HEADS // N_KV_HEADS
AW = N_Q_HEADS * HEAD_DIM
KVW = N_KV_HEADS * HEAD_DIM
GW = Q_GROUP * HEAD_DIM
QKW = AW + KVW
QKVW = AW + 2 * KVW
GRID_W = 64
AXIS_DIM = HEAD_DIM // 2
ROPE_THETA = 10000.0
ATTN_SCALE = HEAD_DIM ** -0.5
EPS = 1e-6
N_MOD = 6
MOD_ROWS = 8
CONV_PAD_ROWS = 16
GATE_ROWS = 1024
HEADS_PER_PASS_FWD = 1
HEADS_PER_PASS_BWD = 2
CONV_CHUNK = 64
LANE = 128
SUBLANE = 8
VMEM_LIMIT = 56 * 1024 * 1024
MM_VMEM_BUDGET = 40 * 1024 * 1024

ADAM_LR = 0.001
ADAM_B1 = 0.9
ADAM_B2 = 0.999
ADAM_EPS = 1e-08
ADAM_WD = 0.01
ADAM_STEP = 10

NN =(((1,), (0,)), ((), ()))
NT_ = (((1,), (1,)), ((), ()))
TN = (((0,), (0,)), ((), ()))


def _params(ndims, vmem=None):
    return pltpu.CompilerParams(dimension_semantics=("arbitrary",) * ndims, vmem_limit_bytes=vmem)


def _divisor(n, cap, mult):
    best = None
    for d in range(mult, min(n, cap) + 1, mult):
        if n % d == 0:
            best = d
    return n if best is None else best


def _const_spec(a):
    nd = a.ndim
    return pl.BlockSpec(a.shape, lambda *idx: (0,) * nd)


class _Cfg:
    pass


def _mm(a, b, mode, out_dtype, name, comm=None):
    def accumulate(prod, o_ref, scratch, nsteps):
        if nsteps == 1:
            o_ref[...] = prod.astype(out_dtype)
            return
        acc_ref = scratch[0]
        k = pl.program_id(2)

        @pl.when(k == 0)
        def _():
            acc_ref[...] = prod

        if nsteps > 2:
            @pl.when(jnp.logical_and(k > 0, k < nsteps - 1))
            def _():
                acc_ref[...] += prod

        @pl.when(k == nsteps - 1)
        def _():
            o_ref[...] = (acc_ref[...] + prod).astype(out_dtype)

    def fits(*tiles):
        return sum(2 * r * c * s for r, c, s in tiles[:-1]) + tiles[-1][0] * tiles[-1][1] * 4 <= MM_VMEM_BUDGET

    osz = jnp.dtype(out_dtype).itemsize
    if mode == "tn":
        m, ka = a.shape
        nb = b.shape[1]
        tka = _divisor(ka, 1408, LANE)
        tnb = _divisor(nb, 1024, LANE)
        tmr = SUBLANE
        for cand in range(SUBLANE, min(m, 2304) + 1, SUBLANE):
            if m % cand == 0 and fits((cand, tka, a.dtype.itemsize), (cand, tnb, b.dtype.itemsize),
                                      (tka, tnb, osz), (tka, tnb, 4)):
                tmr = cand
        nsteps = m // tmr

        def body(a_ref, b_ref, o_ref, *scratch):
            prod = lax.dot_general(a_ref[...].astype(bf16), b_ref[...].astype(bf16), TN, preferred_element_type=f32)
            accumulate(prod, o_ref, scratch, nsteps)

        return pl.pallas_call(
            body, out_shape=jax.ShapeDtypeStruct((ka, nb), out_dtype),
            grid=(ka // tka, nb // tnb, nsteps),
            in_specs=[pl.BlockSpec((tmr, tka), lambda i, j, k: (k, i)),
                      pl.BlockSpec((tmr, tnb), lambda i, j, k: (k, j))],
            out_specs=pl.BlockSpec((tka, tnb), lambda i, j, k: (i, j)),
            scratch_shapes=[pltpu.VMEM((tka, tnb), f32)] if nsteps > 1 else [],
            compiler_params=_params(3, VMEM_LIMIT), name=name)(a, b)

    m, kdim = a.shape
    n = b.shape[1] if mode == "nn" else b.shape[0]
    tm = _divisor(m, 1024 if a.dtype == bf16 else 512, SUBLANE)
    tn = _divisor(n, 1408, LANE)
    tk = LANE if kdim % LANE == 0 else kdim
    for cand in range(LANE, min(kdim, 3328) + 1, LANE):
        if kdim % cand == 0 and fits((tm, cand, a.dtype.itemsize), (cand, tn, b.dtype.itemsize), (tm, tn, osz),
                                     (tm, tn, 4)):
            tk = cand
    nsteps = kdim // tk
    dims = NN if mode == "nn" else NT_

    def body(a_ref, b_ref, o_ref, *scratch):
        prod = lax.dot_general(a_ref[...].astype(bf16), b_ref[...].astype(bf16), dims, preferred_element_type=f32)
        accumulate(prod, o_ref, scratch, nsteps)

    if mode == "nn":
        b_spec = pl.BlockSpec((tk, tn), lambda i, j, k: (k, j))
    else:
        b_spec = pl.BlockSpec((tn, tk), lambda i, j, k: (j, k))
    (out,), couts = _hosting_call(
        body, comm, out_shape=(jax.ShapeDtypeStruct((m, n), out_dtype),),
        grid=(m // tm, n // tn, nsteps),
        in_specs=[pl.BlockSpec((tm, tk), lambda i, j, k: (i, k)), b_spec],
        out_specs=(pl.BlockSpec((tm, tn), lambda i, j, k: (i, j)),),
        scratch_shapes=[pltpu.VMEM((tm, tn), f32)] if nsteps > 1 else [],
        compiler_params=_params(3, VMEM_LIMIT), name=name, args=(a, b))
    return out if comm is None else (out, couts)


def _row_spec(cfg, width, col=0):
    return pl.BlockSpec((cfg.TR, width), lambda i: (i, col))


def _mod_spec(cfg):
    nt, nc = cfg.NT, cfg.NC
    return pl.BlockSpec((1, 1, MOD_ROWS, cfg.D),
                        lambda i: (i // nt, ((i % nt) >= nc).astype(jnp.int32), 0, 0))


def _segment_start(cfg, i):
    j = i % cfg.NT
    return jnp.logical_or(j == 0, j == cfg.NC)


def _norm_fwd(cfg, name, xin, o, modg, gk, modn, sk):
    has_o = o is not None
    r_, d = xin.shape

    def body(*refs):
        if has_o:
            x_ref, o_ref, mg_ref, mn_ref, xn_ref, h_ref = refs
            x = x_ref[...] + mg_ref[0, 0, gk:gk + 1, :] * o_ref[...]
            xn_ref[...] = x
        else:
            x_ref, mn_ref, h_ref = refs
            x = x_ref[...]
        r = lax.rsqrt(jnp.mean(x * x, axis=-1, keepdims=True) + EPS)
        h = (x * r) * (1.0 + mn_ref[0, 0, sk + 1:sk + 2, :]) + mn_ref[0, 0, sk:sk + 1, :]
        h_ref[...] = h.astype(bf16)

    row = _row_spec(cfg, d)
    if has_o:
        ins, in_specs = (xin, o, modg, modn), [row, row, _mod_spec(cfg), _mod_spec(cfg)]
        out_shape = (jax.ShapeDtypeStruct((r_, d), f32), jax.ShapeDtypeStruct((r_, d), bf16))
        out_specs = (row, row)
    else:
        ins, in_specs = (xin, modn), [row, _mod_spec(cfg)]
        out_shape = jax.ShapeDtypeStruct((r_, d), bf16)
        out_specs = row
    res = pl.pallas_call(body, out_shape=out_shape, grid=(r_ // cfg.TR,), in_specs=in_specs,
                         out_specs=out_specs, compiler_params=_params(1), name=name)(*ins)
    return res if has_o else (xin, res)


def _norm_bwd(cfg, name, xnew, o, dh, dres, modg, gk, modn, sk):
    has_o = o is not None
    r_, d = xnew.shape

    def body(*refs):
        if has_o:
            xn_ref, o_ref, dh_ref, dr_ref, mg_ref, mn_ref, dx_ref, do_ref, dm_ref = refs
        else:
            xn_ref, dh_ref, dr_ref, mn_ref, dx_ref, dm_ref = refs
        i = pl.program_id(0)
        x = xn_ref[...]
        dhv = dh_ref[...].astype(f32)
        r = lax.rsqrt(jnp.mean(x * x, axis=-1, keepdims=True) + EPS)
        xh = x * r
        dxh = dhv * (1.0 + mn_ref[0, 0, sk + 1:sk + 2, :])
        dx = r * (dxh - xh * jnp.mean(dxh * xh, axis=-1, keepdims=True)) + dr_ref[...]
        dx_ref[...] = dx

        @pl.when(_segment_start(cfg, i))
        def _():
            dm_ref[...] = jnp.zeros_like(dm_ref)

        dm_ref[0, 0, 0:1, :] += jnp.sum(dhv, axis=0, keepdims=True)
        dm_ref[0, 0, 1:2, :] += jnp.sum(dhv * xh, axis=0, keepdims=True)
        if has_o:
            ov = o_ref[...]
            dm_ref[0, 0, 2:3, :] += jnp.sum(dx * ov, axis=0, keepdims=True)
            do_ref[...] = (dx * mg_ref[0, 0, gk:gk + 1, :]).astype(bf16)

    row = _row_spec(cfg, d)
    dm_shape = jax.ShapeDtypeStruct((cfg.B, 2, MOD_ROWS, d), f32)
    if has_o:
        ins = (xnew, o, dh, dres, modg, modn)
        in_specs = [row, row, row, row, _mod_spec(cfg), _mod_spec(cfg)]
        out_shape = (jax.ShapeDtypeStruct((r_, d), f32), jax.ShapeDtypeStruct((r_, d), bf16), dm_shape)
        out_specs = (row, row, _mod_spec(cfg))
    else:
        ins = (xnew, dh, dres, modn)
        in_specs = [row, row, row, _mod_spec(cfg)]
        out_shape = (jax.ShapeDtypeStruct((r_, d), f32), dm_shape)
        out_specs = (row, _mod_spec(cfg))
    res = pl.pallas_call(body, out_shape=out_shape, grid=(r_ // cfg.TR,), in_specs=in_specs,
                         out_specs=out_specs, compiler_params=_params(1), name=name)(*ins)
    if has_o:
        return res
    return res[0], None, res[1]


def _loss_head(cfg, name, x1, o, modg, gk, target):
    r_, d = x1.shape
    nt, nc = cfg.NT, cfg.NC
    nlat = nt - nc

    def body(x_ref, o_ref, mg_ref, t_ref, ls_ref, dy_ref, do_ref, dm_ref):
        i = pl.program_id(0)
        lat = (i % nt) >= nc
        gate = mg_ref[0, 0, gk:gk + 1, :]
        ov = o_ref[...]
        err = jnp.where(lat, x_ref[...] + gate * ov - t_ref[...], 0.0)
        ls_ref[...] = jnp.zeros_like(ls_ref)
        ls_ref[0, 0:1, :] = jnp.sum(err * err, axis=0, keepdims=True)
        dy = err * (1.0 / d)
        dy_ref[...] = dy
        do_ref[...] = (dy * gate).astype(bf16)

        @pl.when(_segment_start(cfg, i))
        def _():
            dm_ref[...] = jnp.zeros_like(dm_ref)

        dm_ref[0, 0, 2:3, :] += jnp.sum(dy * ov, axis=0, keepdims=True)

    row = _row_spec(cfg, d)
    t_spec = pl.BlockSpec((cfg.TR, d), lambda i: ((i // nt) * nlat + jnp.maximum((i % nt) - nc, 0), 0))
    return pl.pallas_call(
        body,
        out_shape=(jax.ShapeDtypeStruct((r_ // cfg.TR, SUBLANE, d), f32), jax.ShapeDtypeStruct((r_, d), f32),
                   jax.ShapeDtypeStruct((r_, d), bf16), jax.ShapeDtypeStruct((cfg.B, 2, MOD_ROWS, d), f32)),
        grid=(r_ // cfg.TR,),
        in_specs=[row, row, _mod_spec(cfg), t_spec],
        out_specs=(pl.BlockSpec((1, SUBLANE, d), lambda i: (i, 0, 0)), row, row, _mod_spec(cfg)),
        compiler_params=_params(1), name=name)(x1, o, modg, target)


def _swap16(y):
    w = y.shape[1]
    lane = lax.broadcasted_iota(jnp.int32, (1, w), 1)
    lo = jnp.bitwise_and(lane, 31) < 16
    return jnp.where(lo, pltpu.roll(y, w - 16, 1), pltpu.roll(y, 16, 1))


def _dot_split(v, m):
    hi = v.astype(bf16)
    lo = (v - hi.astype(f32)).astype(bf16)
    return jnp.dot(hi, m, preferred_element_type=f32) + jnp.dot(lo, m, preferred_element_type=f32)


def _head_sum(v, bd):
    return _dot_split(v, bd)


def _qkv_fwd(cfg, name, p, cs, gain, bd, rep):
    r_ = p.shape[0]
    nt = cfg.NT

    def body(p_ref, cs_ref, g_ref, bd_ref, rep_ref, q_ref, k_ref, v_ref):
        x = p_ref[:, 0:QKW].astype(f32)
        r = lax.rsqrt(_head_sum(x * x, bd_ref[...]) * (1.0 / HEAD_DIM) + EPS)
        y = (x * r) * g_ref[...]
        out = y * cs_ref[0] + _swap16(y) * cs_ref[1]
        q_ref[...] = out[:, 0:AW].astype(bf16)
        kr = out[:, AW:QKW].astype(bf16)
        k_ref[...] = jnp.dot(kr, rep_ref[...], preferred_element_type=f32).astype(bf16)
        vv = p_ref[:, QKW:QKVW].astype(bf16)
        v_ref[...] = jnp.dot(vv, rep_ref[...], preferred_element_type=f32).astype(bf16)

    row = pl.BlockSpec((cfg.TR, AW), lambda j, e: (e * nt + j, 0))
    return pl.pallas_call(
        body, out_shape=(jax.ShapeDtypeStruct((r_, AW), bf16),) * 3, grid=(nt, cfg.B),
        in_specs=[pl.BlockSpec((cfg.TR, QKVW), lambda j, e: (e * nt + j, 0)),
                  pl.BlockSpec((2, cfg.TR, QKW), lambda j, e: (0, j, 0)),
                  _const_spec(gain), _const_spec(bd), _const_spec(rep)],
        out_specs=(row, row, row), compiler_params=_params(2), name=name)(p, cs, gain, bd, rep)


def _qkv_bwd(cfg, name, p, dq, dkr, dvr, cs, gain, bd, rept):
    r_ = p.shape[0]
    nt = cfg.NT

    def body(p_ref, dq_ref, dk_ref, dv_ref, cs_ref, g_ref, bd_ref, rt_ref, dp_ref, dg_ref):
        first = jnp.logical_and(pl.program_id(0) == 0, pl.program_id(1) == 0)
        x = p_ref[:, 0:QKW].astype(f32)
        rt = rt_ref[...]
        dk = _dot_split(dk_ref[...], rt)
        g = jnp.concatenate([dq_ref[...], dk], axis=1)
        gy = g * cs_ref[0] + _swap16(g * cs_ref[1])
        bdv = bd_ref[...]
        r = lax.rsqrt(_head_sum(x * x, bdv) * (1.0 / HEAD_DIM) + EPS)
        xh = x * r
        dxh = gy * g_ref[...]
        dx = r * (dxh - xh * (_head_sum(dxh * xh, bdv) * (1.0 / HEAD_DIM)))
        dp_ref[:, 0:QKW] = dx.astype(bf16)
        dp_ref[:, QKW:QKVW] = _dot_split(dv_ref[...], rt).astype(bf16)

        @pl.when(first)
        def _():
            dg_ref[...] = jnp.zeros_like(dg_ref)

        dg_ref[0:1, :] += jnp.sum(gy * xh, axis=0, keepdims=True)

    row = pl.BlockSpec((cfg.TR, AW), lambda j, e: (e * nt + j, 0))
    wide = pl.BlockSpec((cfg.TR, QKVW), lambda j, e: (e * nt + j, 0))
    return pl.pallas_call(
        body, out_shape=(jax.ShapeDtypeStruct((r_, QKVW), bf16), jax.ShapeDtypeStruct((SUBLANE, QKW), f32)),
        grid=(nt, cfg.B),
        in_specs=[wide, row, row, row, pl.BlockSpec((2, cfg.TR, QKW), lambda j, e: (0, j, 0)),
                  _const_spec(gain), _const_spec(bd), _const_spec(rept)],
        out_specs=(wide, pl.BlockSpec((SUBLANE, QKW), lambda j, e: (0, 0))),
        compiler_params=_params(2), name=name)(p, dq, dkr, dvr, cs, gain, bd, rept)


def _head_masks():
    lane = lax.broadcasted_iota(jnp.int32, (1, GW), 1)
    return [jnp.logical_and(lane >= HEAD_DIM * h, lane < HEAD_DIM * (h + 1)) for h in range(Q_GROUP)]


def _attn_fwd(cfg, name, q, krep, vrep, comm=None):
    r_ = q.shape[0]
    tr, t, lc, nt, nc = cfg.TR, cfg.T, cfg.LC, cfg.NT, cfg.NC

    def body(q_ref, k_ref, v_ref, o_ref, l_ref):
        j = pl.program_id(2)
        masks = _head_masks()
        lane = lax.broadcasted_iota(jnp.int32, (1, LANE), 1)

        def run(nk):
            qv = q_ref[...]
            kv = k_ref[0:nk, :]
            vv = v_ref[0:nk, :]
            acc = jnp.zeros((tr, GW), f32)
            lse = jnp.zeros((tr, LANE), f32)
            for h0 in range(0, Q_GROUP, HEADS_PER_PASS_FWD):
                hs = range(h0, h0 + HEADS_PER_PASS_FWD)
                qs = jnp.concatenate([jnp.where(masks[h], qv, jnp.zeros_like(qv)) for h in hs], axis=0)
                s = lax.dot_general(qs, kv, NT_, preferred_element_type=f32) * ATTN_SCALE
                m = jnp.max(s, axis=1, keepdims=True)
                pr = jnp.exp(s - m)
                l = jnp.sum(pr, axis=1, keepdims=True)
                os_ = jnp.dot(pr.astype(bf16), vv, preferred_element_type=f32) / l
                ls = m + jnp.log(l)
                for i, h in enumerate(hs):
                    acc = acc + jnp.where(masks[h], os_[i * tr:(i + 1) * tr], 0.0)
                    lse = lse + jnp.where(lane == h, ls[i * tr:(i + 1) * tr], 0.0)
            o_ref[...] = acc
            l_ref[...] = lse

        @pl.when(j < nc)
        def _():
            run(lc)

        @pl.when(j >= nc)
        def _():
            run(t)

    qspec = pl.BlockSpec((tr, GW), lambda e, g, j: (e * nt + j, g))
    kspec = pl.BlockSpec((t, GW), lambda e, g, j: (e, g))
    return _hosting_call(
        body, comm, out_shape=(jax.ShapeDtypeStruct((r_, AW), f32), jax.ShapeDtypeStruct((r_, N_KV_HEADS * LANE), f32)),
        grid=(cfg.B, N_KV_HEADS, nt), in_specs=[qspec, kspec, kspec],
        out_specs=(qspec, pl.BlockSpec((tr, LANE), lambda e, g, j: (e * nt + j, g))),
        compiler_params=_params(3, VMEM_LIMIT), name=name, args=(q, krep, vrep))


def _hosting_call(body, comm, *, out_shape, grid, in_specs, out_specs, compiler_params, name, args, scratch_shapes=()):
    if comm is None:
        return pl.pallas_call(body, out_shape=out_shape, grid=grid, in_specs=in_specs, out_specs=out_specs,
                              scratch_shapes=list(scratch_shapes), compiler_params=compiler_params,
                              name=name)(*args), None
    xs, scatter = comm
    n, n_in, n_out, n_scr = len(xs), len(args), len(out_shape), len(scratch_shapes)
    last = [g - 1 for g in grid]

    def hosted(*refs):
        ins, cins = refs[:n_in], refs[n_in:n_in + n]
        outs, couts = refs[n_in + n:n_in + n + n_out], refs[n_in + n + n_out:n_in + 2 * n + n_out]
        scratch = refs[n_in + 2 * n + n_out:n_in + 2 * n + n_out + n_scr]
        sems = refs[n_in + 2 * n + n_out + n_scr:]
        ids = [pl.program_id(a) for a in range(len(grid))]
        is_first = functools.reduce(jnp.logical_and, [i == 0 for i in ids])
        is_last = functools.reduce(jnp.logical_and, [i == l for i, l in zip(ids, last)])

        @pl.when(is_first)
        def _():
            _exchange_start(cins, couts, sems, scatter)

        body(*ins, *outs, *scratch)

        @pl.when(is_last)
        def _():
            _exchange_wait(cins, couts, sems, scatter)

    anyspec = pl.BlockSpec(memory_space=pl.ANY)
    res = pl.pallas_call(
        hosted, out_shape=tuple(out_shape) + tuple(_exchange_out_shapes(xs, scatter)), grid=grid,
        in_specs=list(in_specs) + [anyspec] * n, out_specs=tuple(out_specs) + (anyspec,) * n,
        scratch_shapes=list(scratch_shapes) + _exchange_scratch(n), compiler_params=compiler_params,
        name=name)(*args, *xs)
    return tuple(res[:n_out]), list(res[n_out:])


def _attn_bwd(cfg, name, q, krep, vrep, o, lse, do, comm=None):
    r_ = q.shape[0]
    tr, t, lc, nt, nc = cfg.TR, cfg.T, cfg.LC, cfg.NT, cfg.NC

    def body(q_ref, k_ref, v_ref, o_ref, l_ref, do_ref, dq_ref, dk_ref, dv_ref):
        j = pl.program_id(2)
        masks = _head_masks()
        lane = lax.broadcasted_iota(jnp.int32, (1, LANE), 1)

        @pl.when(j == 0)
        def _():
            dk_ref[...] = jnp.zeros_like(dk_ref)
            dv_ref[...] = jnp.zeros_like(dv_ref)

        def run(nk):
            qv = q_ref[...]
            kv = k_ref[0:nk, :]
            vv = v_ref[0:nk, :]
            ov = o_ref[...]
            dov = do_ref[...]
            lv = l_ref[...]
            dq = jnp.zeros((tr, GW), f32)
            dk = jnp.zeros((nk, GW), f32)
            dv = jnp.zeros((nk, GW), f32)
            for h0 in range(0, Q_GROUP, HEADS_PER_PASS_BWD):
                hs = range(h0, h0 + HEADS_PER_PASS_BWD)
                qs = jnp.concatenate([jnp.where(masks[h], qv, jnp.zeros_like(qv)) for h in hs], axis=0)
                dos = jnp.concatenate([jnp.where(masks[h], dov, 0.0) for h in hs], axis=0)
                dosb = dos.astype(bf16)
                delta = jnp.sum(dos * jnp.concatenate([ov] * len(hs), axis=0), axis=1, keepdims=True)
                lses = jnp.concatenate([jnp.sum(jnp.where(lane == h, lv, 0.0), axis=1, keepdims=True) for h in hs],
                                       axis=0)
                s = lax.dot_general(qs, kv, NT_, preferred_element_type=f32) * ATTN_SCALE
                pr = jnp.exp(s - lses)
                dpr = lax.dot_general(dosb, vv, NT_, preferred_element_type=f32)
                ds = (pr * (dpr - delta) * ATTN_SCALE).astype(bf16)
                dqs = jnp.dot(ds, kv, preferred_element_type=f32)
                for i, h in enumerate(hs):
                    dq = dq + jnp.where(masks[h], dqs[i * tr:(i + 1) * tr], 0.0)
                dk = dk + lax.dot_general(ds, qs, TN, preferred_element_type=f32)
                dv = dv + lax.dot_general(pr.astype(bf16), dosb, TN, preferred_element_type=f32)
            dq_ref[...] = dq
            dk_ref[0:nk, :] += dk
            dv_ref[0:nk, :] += dv

        @pl.when(j < nc)
        def _():
            run(lc)

        @pl.when(j >= nc)
        def _():
            run(t)

    qspec = pl.BlockSpec((tr, GW), lambda e, g, j: (e * nt + j, g))
    kspec = pl.BlockSpec((t, GW), lambda e, g, j: (e, g))
    lspec = pl.BlockSpec((tr, LANE), lambda e, g, j: (e * nt + j, g))
    return _hosting_call(
        body, comm, out_shape=(jax.ShapeDtypeStruct((r_, AW), f32),) * 3,
        grid=(cfg.B, N_KV_HEADS, nt), in_specs=[qspec, kspec, kspec, qspec, lspec, qspec],
        out_specs=(qspec, kspec, kspec),
        compiler_params=_params(3, VMEM_LIMIT), name=name, args=(q, krep, vrep, o, lse, do))


def _segments(cfg):
    pad = CONV_PAD_ROWS
    return ((pad, 0, cfg.LC), (cfg.LC + 3 * pad, cfg.LC, cfg.S)), cfg.T + 4 * pad


def _fill_padded(buf, cfg, fn):
    buf[...] = jnp.zeros_like(buf)
    for off, lo, n in _segments(cfg)[0]:
        buf[off:off + n, :] = fn(lo, lo + n)


def _taps(buf, w_ref, base, taps, sign):
    acc = None
    for k in range(taps):
        term = w_ref[k:k + 1, :] * buf[pl.ds(base + sign * (k - taps // 2), CONV_CHUNK), :]
        acc = term if acc is None else acc + term
    return acc


def _fold_tiles(v):
    acc = v[0:SUBLANE]
    for i in range(1, v.shape[0] // SUBLANE):
        acc = acc + v[SUBLANE * i:SUBLANE * (i + 1)]
    return acc


def _tap_grads(buf, dy, acc_ref, base, taps):
    for k in range(taps):
        acc_ref[SUBLANE * k:SUBLANE * (k + 1), :] += _fold_tiles(
            buf[pl.ds(base + k - taps // 2, CONV_CHUNK), :] * dy)


def _conv_specs(cfg):
    t, cb, cw = cfg.T, cfg.CB, cfg.CW
    offs = (cfg.OFF_CONF, cfg.OFF_CONF + cw, cfg.OFF_SC, cfg.OFF_SC + cw, cfg.OFF_SC + 2 * cw)
    return [pl.BlockSpec((t, cb), lambda c, e, off=off: (e, off // cb + c)) for off in offs]


def _conv_fwd(cfg, name, p, w31, w3):
    r_ = p.shape[0]
    t, lc, cb, cw = cfg.T, cfg.LC, cfg.CB, cfg.CW
    k31, k3 = cfg.K31, cfg.K3

    segs, buf_rows = _segments(cfg)
    ch = CONV_CHUNK

    def body(a_ref, g_ref, bg_ref, cg_ref, xs_ref, w31_ref, w3_ref, hc_ref, z_ref, hbuf, ubuf):
        _fill_padded(hbuf, cfg, lambda lo, hi: a_ref[lo:hi, :].astype(f32) * jax.nn.sigmoid(g_ref[lo:hi, :].astype(f32)))
        _fill_padded(ubuf, cfg, lambda lo, hi: cg_ref[lo:hi, :].astype(f32) * xs_ref[lo:hi, :].astype(f32))
        bias = w31_ref[k31:k31 + 1, :]
        for off, lo, n in segs:
            def chunk(c, carry, off=off, lo=lo):
                base = off + pl.multiple_of(c * ch, ch)
                rows = pl.ds(pl.multiple_of(lo + c * ch, ch), ch)
                hc_ref[rows, :] = _taps(hbuf, w31_ref, base, k31, 1) + bias
                z_ref[rows, :] = (bg_ref[rows, :].astype(f32) * _taps(ubuf, w3_ref, base, k3, 1)).astype(bf16)
                return carry

            lax.fori_loop(0, n // ch, chunk, 0)

    ospec = pl.BlockSpec((t, cb), lambda c, e: (e, c))
    return pl.pallas_call(
        body, out_shape=(jax.ShapeDtypeStruct((r_, cw), f32), jax.ShapeDtypeStruct((r_, cw), bf16)),
        grid=(cw // cb, cfg.B),
        in_specs=_conv_specs(cfg) + [pl.BlockSpec((w31.shape[0], cb), lambda c, e: (0, c)),
                                     pl.BlockSpec((w3.shape[0], cb), lambda c, e: (0, c))],
        out_specs=(ospec, ospec), scratch_shapes=[pltpu.VMEM((buf_rows, cb), f32)] * 2,
        compiler_params=_params(2, VMEM_LIMIT), name=name)(p, p, p, p, p, w31, w3)


def _conv_bwd(cfg, name, p, dhc, dz, w31, w3, comm=None):
    r_ = p.shape[0]
    t, lc, cb, cw = cfg.T, cfg.LC, cfg.CB, cfg.CW
    k31, k3 = cfg.K31, cfg.K3

    segs, buf_rows = _segments(cfg)
    ch = CONV_CHUNK

    def body(a_ref, g_ref, bg_ref, cg_ref, xs_ref, dhc_ref, dz_ref, w31_ref, w3_ref,
             da_ref, dg_ref, dbg_ref, dcg_ref, dxs_ref, dw31_ref, dw3_ref, hbuf, dbuf, ubuf, ebuf, acc31, acc3):
        e = pl.program_id(1)

        @pl.when(e == 0)
        def _():
            dw31_ref[...] = jnp.zeros_like(dw31_ref)
            dw3_ref[...] = jnp.zeros_like(dw3_ref)

        _fill_padded(hbuf, cfg, lambda lo, hi: a_ref[lo:hi, :].astype(f32) * jax.nn.sigmoid(g_ref[lo:hi, :].astype(f32)))
        _fill_padded(dbuf, cfg, lambda lo, hi: dhc_ref[lo:hi, :])
        _fill_padded(ubuf, cfg, lambda lo, hi: cg_ref[lo:hi, :].astype(f32) * xs_ref[lo:hi, :].astype(f32))
        _fill_padded(ebuf, cfg, lambda lo, hi: dz_ref[lo:hi, :] * bg_ref[lo:hi, :].astype(f32))
        acc31[...] = jnp.zeros_like(acc31)
        acc3[...] = jnp.zeros_like(acc3)
        for off, lo, n in segs:
            def chunk(c, carry, off=off, lo=lo):
                base = off + pl.multiple_of(c * ch, ch)
                here = pl.ds(base, ch)
                rows = pl.ds(pl.multiple_of(lo + c * ch, ch), ch)
                dy = dbuf[here, :]
                dhh = _taps(dbuf, w31_ref, base, k31, -1)
                _tap_grads(hbuf, dy, acc31, base, k31)
                acc31[SUBLANE * k31:SUBLANE * (k31 + 1), :] += _fold_tiles(dy)
                sg = jax.nn.sigmoid(g_ref[rows, :].astype(f32))
                da_ref[rows, :] = (dhh * sg).astype(bf16)
                dg_ref[rows, :] = (dhh * hbuf[here, :] * (1.0 - sg)).astype(bf16)

                duc = ebuf[here, :]
                dbg_ref[rows, :] = (dz_ref[rows, :] * _taps(ubuf, w3_ref, base, k3, 1)).astype(bf16)
                du = _taps(ebuf, w3_ref, base, k3, -1)
                _tap_grads(ubuf, duc, acc3, base, k3)
                dcg_ref[rows, :] = (du * xs_ref[rows, :].astype(f32)).astype(bf16)
                dxs_ref[rows, :] = (du * cg_ref[rows, :].astype(f32)).astype(bf16)
                return carry

            lax.fori_loop(0, n // ch, chunk, 0)
        for k in range(k31 + 1):
            dw31_ref[k:k + 1, :] += jnp.sum(acc31[SUBLANE * k:SUBLANE * (k + 1), :], axis=0, keepdims=True)
        for k in range(k3):
            dw3_ref[k:k + 1, :] += jnp.sum(acc3[SUBLANE * k:SUBLANE * (k + 1), :], axis=0, keepdims=True)

    ospec = pl.BlockSpec((t, cb), lambda c, e: (e, c))
    w31_spec = pl.BlockSpec((w31.shape[0], cb), lambda c, e: (0, c))
    w3_spec = pl.BlockSpec((w3.shape[0], cb), lambda c, e: (0, c))
    return _hosting_call(
        body, comm,
        out_shape=(jax.ShapeDtypeStruct((r_, cw), bf16),) * 5
        + (jax.ShapeDtypeStruct(w31.shape, f32), jax.ShapeDtypeStruct(w3.shape, f32)),
        grid=(cw // cb, cfg.B),
        in_specs=_conv_specs(cfg) + [ospec, ospec, w31_spec, w3_spec],
        out_specs=(ospec,) * 5 + (w31_spec, w3_spec),
        scratch_shapes=[pltpu.VMEM((buf_rows, cb), f32)] * 4
        + [pltpu.VMEM((SUBLANE * w31.shape[0], cb), f32), pltpu.VMEM((SUBLANE * w3.shape[0], cb), f32)],
        compiler_params=_params(2, VMEM_LIMIT), name=name, args=(p, p, p, p, p, dhc, dz, w31, w3))


def _ln_silu_fwd(cfg, name, hc, lnp):
    r_, cw = hc.shape

    def body(x_ref, p_ref, o_ref):
        x = x_ref[...]
        mu = jnp.mean(x, axis=-1, keepdims=True)
        xc = x - mu
        rs = lax.rsqrt(jnp.mean(xc * xc, axis=-1, keepdims=True) + EPS)
        hn = (xc * rs) * p_ref[0:1, :] + p_ref[1:2, :]
        o_ref[...] = (hn * jax.nn.sigmoid(hn)).astype(bf16)

    row = _row_spec(cfg, cw)
    return pl.pallas_call(body, out_shape=jax.ShapeDtypeStruct((r_, cw), bf16), grid=(r_ // cfg.TR,),
                          in_specs=[row, _const_spec(lnp)], out_specs=row,
                          compiler_params=_params(1), name=name)(hc, lnp)


def _ln_silu_bwd(cfg, name, hc, dhs, lnp):
    r_, cw = hc.shape

    def body(x_ref, d_ref, p_ref, dx_ref, dp_ref):
        i = pl.program_id(0)
        x = x_ref[...]
        mu = jnp.mean(x, axis=-1, keepdims=True)
        xc = x - mu
        rs = lax.rsqrt(jnp.mean(xc * xc, axis=-1, keepdims=True) + EPS)
        xh = xc * rs
        gain = p_ref[0:1, :]
        hn = xh * gain + p_ref[1:2, :]
        sg = jax.nn.sigmoid(hn)
        dhn = d_ref[...] * (sg * (1.0 + hn * (1.0 - sg)))
        dxh = dhn * gain
        dx_ref[...] = rs * (dxh - jnp.mean(dxh, axis=-1, keepdims=True)
                            - xh * jnp.mean(dxh * xh, axis=-1, keepdims=True))

        @pl.when(i == 0)
        def _():
            dp_ref[...] = jnp.zeros_like(dp_ref)

        dp_ref[0:1, :] += jnp.sum(dhn * xh, axis=0, keepdims=True)
        dp_ref[1:2, :] += jnp.sum(dhn, axis=0, keepdims=True)

    row = _row_spec(cfg, cw)
    return pl.pallas_call(
        body, out_shape=(jax.ShapeDtypeStruct((r_, cw), f32), jax.ShapeDtypeStruct((SUBLANE, cw), f32)),
        grid=(r_ // cfg.TR,), in_specs=[row, row, _const_spec(lnp)],
        out_specs=(row, pl.BlockSpec((SUBLANE, cw), lambda i: (0, 0))),
        compiler_params=_params(1), name=name)(hc, dhs, lnp)


def _gate_fwd(cfg, name, p, ya, yc, ys):
    r_, d = ya.shape
    gb = cfg.GB
    base = cfg.OFF_GATE // gb
    per = d // gb

    tr = _divisor(r_, GATE_ROWS, cfg.TR)

    def body(g0_ref, g1_ref, g2_ref, ya_ref, yc_ref, ys_ref, o_ref):
        m = (jax.nn.sigmoid(g0_ref[...].astype(f32)) * ya_ref[...].astype(f32)
             + jax.nn.sigmoid(g1_ref[...].astype(f32)) * yc_ref[...].astype(f32)
             + jax.nn.sigmoid(g2_ref[...].astype(f32)) * ys_ref[...].astype(f32))
        o_ref[...] = m.astype(bf16)

    yspec = pl.BlockSpec((tr, gb), lambda i, c: (i, c))
    gspecs = [pl.BlockSpec((tr, gb), lambda i, c, k=k: (i, base + k * per + c)) for k in range(3)]
    return pl.pallas_call(body, out_shape=jax.ShapeDtypeStruct((r_, d), bf16), grid=(r_ // tr, per),
                          in_specs=gspecs + [yspec] * 3, out_specs=yspec,
                          compiler_params=_params(2), name=name)(p, p, p, ya, yc, ys)


def _gate_bwd(cfg, name, p, ya, yc, ys, dm):
    r_, d = ya.shape
    gb = cfg.GB
    base = cfg.OFF_GATE // gb
    per = d // gb

    tr = _divisor(r_, GATE_ROWS, cfg.TR)

    def body(g0_ref, g1_ref, g2_ref, ya_ref, yc_ref, ys_ref, dm_ref, da_ref, dc_ref, ds_ref, d0_ref, d1_ref, d2_ref):
        dmv = dm_ref[...]
        for g_ref, y_ref, dy_ref, dg_ref in ((g0_ref, ya_ref, da_ref, d0_ref), (g1_ref, yc_ref, dc_ref, d1_ref),
                                             (g2_ref, ys_ref, ds_ref, d2_ref)):
            sg = jax.nn.sigmoid(g_ref[...].astype(f32))
            dy_ref[...] = (dmv * sg).astype(bf16)
            dg_ref[...] = (dmv * y_ref[...].astype(f32) * sg * (1.0 - sg)).astype(bf16)

    yspec = pl.BlockSpec((tr, gb), lambda i, c: (i, c))
    gspecs = [pl.BlockSpec((tr, gb), lambda i, c, k=k: (i, base + k * per + c)) for k in range(3)]
    return pl.pallas_call(body, out_shape=(jax.ShapeDtypeStruct((r_, d), bf16),) * 6, grid=(r_ // tr, per),
                          in_specs=gspecs + [yspec] * 4, out_specs=(yspec,) * 6,
                          compiler_params=_params(2), name=name)(p, p, p, ya, yc, ys, dm)


def _swiglu_fwd(cfg, name, ff):
    r_, f2 = ff.shape
    fh = f2 // 2

    def body(a_ref, b_ref, o_ref):
        a = a_ref[...].astype(f32)
        o_ref[...] = (a * jax.nn.sigmoid(a) * b_ref[...].astype(f32)).astype(bf16)

    return pl.pallas_call(body, out_shape=jax.ShapeDtypeStruct((r_, fh), bf16), grid=(r_ // cfg.TR,),
                          in_specs=[_row_spec(cfg, fh, 0), _row_spec(cfg, fh, 1)], out_specs=_row_spec(cfg, fh),
                          compiler_params=_params(1, VMEM_LIMIT), name=name)(ff, ff)


def _swiglu_bwd(cfg, name, ff, ds):
    r_, f2 = ff.shape
    fh = f2 // 2

    def body(a_ref, b_ref, d_ref, o_ref):
        a = a_ref[...].astype(f32)
        sg = jax.nn.sigmoid(a)
        dsv = d_ref[...].astype(f32)
        o_ref[:, 0:fh] = (dsv * b_ref[...].astype(f32) * (sg * (1.0 + a * (1.0 - sg)))).astype(bf16)
        o_ref[:, fh:f2] = (dsv * a * sg).astype(bf16)

    return pl.pallas_call(body, out_shape=jax.ShapeDtypeStruct((r_, f2), bf16), grid=(r_ // cfg.TR,),
                          in_specs=[_row_spec(cfg, fh, 0), _row_spec(cfg, fh, 1), _row_spec(cfg, fh)],
                          out_specs=_row_spec(cfg, f2), compiler_params=_params(1, VMEM_LIMIT), name=name)(ff, ff, ds)


def _ada_fwd(name, call, w_ada, b_loc):
    nl, d, na = w_ada.shape
    nbp = call.shape[0]

    def body(c_ref, w_ref, b_ref, o_ref):
        cv = c_ref[...]
        a = (cv * jax.nn.sigmoid(cv)).astype(bf16)
        o_ref[0] = jnp.dot(a, w_ref[0].astype(bf16), preferred_element_type=f32) + b_ref[0]

    return pl.pallas_call(
        body, out_shape=jax.ShapeDtypeStruct((nl, nbp, na), f32), grid=(nl,),
        in_specs=[_const_spec(call), pl.BlockSpec((1, d, na), lambda l: (l, 0, 0)),
                  pl.BlockSpec((1, 1, na), lambda l: (l, 0, 0))],
        out_specs=pl.BlockSpec((1, nbp, na), lambda l: (l, 0, 0)),
        compiler_params=_params(1, VMEM_LIMIT), name=name)(call, w_ada, b_loc)


def _ada_bwd(name, call, w_ada, dmod, dmodc, cctx_row):
    nl, d, na = w_ada.shape
    nbp = call.shape[0]

    def body(c_ref, w_ref, dm_ref, dc_ref, gw_ref, gb_ref, gc_ref):
        l = pl.program_id(0)
        cv = c_ref[...]
        sg = jax.nn.sigmoid(cv)
        a = (cv * sg).astype(bf16)
        dctx = jnp.sum(dc_ref[0], axis=0, keepdims=True)
        rows = lax.broadcasted_iota(jnp.int32, (nbp, 1), 0)
        dm = jnp.where(rows == cctx_row, dctx, dm_ref[0])
        gw_ref[0] = lax.dot_general(a, dm.astype(bf16), TN, preferred_element_type=f32)
        gb_ref[0] = jnp.zeros((SUBLANE, na), f32)
        gb_ref[0, 0:1, :] = jnp.sum(dm, axis=0, keepdims=True)
        dc8 = jnp.broadcast_to(dctx, (SUBLANE, na)).astype(bf16)
        part = lax.dot_general(dc8, w_ref[0].astype(bf16), NT_, preferred_element_type=f32)
        cc = c_ref[cctx_row:cctx_row + 1, :]
        sc = jax.nn.sigmoid(cc)
        part = part * (sc * (1.0 + cc * (1.0 - sc)))

        @pl.when(l == 0)
        def _():
            gc_ref[...] = jnp.zeros_like(gc_ref)

        gc_ref[...] += part

    return pl.pallas_call(
        body,
        out_shape=(jax.ShapeDtypeStruct((nl, d, na), f32), jax.ShapeDtypeStruct((nl, SUBLANE, na), f32),
                   jax.ShapeDtypeStruct((SUBLANE, d), f32)),
        grid=(nl,),
        in_specs=[_const_spec(call), pl.BlockSpec((1, d, na), lambda l: (l, 0, 0)),
                  pl.BlockSpec((1, nbp, na), lambda l: (l, 0, 0)),
                  pl.BlockSpec((1, NDEV, na), lambda l: (l, 0, 0))],
        out_specs=(pl.BlockSpec((1, d, na), lambda l: (l, 0, 0)), pl.BlockSpec((1, SUBLANE, na), lambda l: (l, 0, 0)),
                   pl.BlockSpec((SUBLANE, d), lambda l: (0, 0))),
        compiler_params=_params(1, VMEM_LIMIT), name=name)(call, w_ada, dmod, dmodc)


def _adamw(name, w, g, m, v):
    rows, cols = w.shape
    tr = _divisor(rows, max(SUBLANE, (1 << 19) // cols), SUBLANE)
    c1 = 1.0 / (1.0 - ADAM_B1 ** ADAM_STEP)
    c2 = 1.0 / (1.0 - ADAM_B2 ** ADAM_STEP)

    def body(w_ref, g_ref, m_ref, v_ref, d_ref, nm_ref, nv_ref):
        gv = g_ref[...]
        nm = ADAM_B1 * m_ref[...] + (1.0 - ADAM_B1) * gv
        nv = ADAM_B2 * v_ref[...] + (1.0 - ADAM_B2) * (gv * gv)
        nm_ref[...] = nm
        nv_ref[...] = nv
        d_ref[...] = -ADAM_LR * ((nm * c1) / (jnp.sqrt(nv * c2) + ADAM_EPS) + ADAM_WD * w_ref[...])

    spec = pl.BlockSpec((tr, cols), lambda i: (i, 0))
    return pl.pallas_call(body, out_shape=(jax.ShapeDtypeStruct((rows, cols), f32),) * 3, grid=(rows // tr,),
                          in_specs=[spec] * 4, out_specs=(spec,) * 3,
                          compiler_params=_params(1, VMEM_LIMIT), name=name)(w, g, m, v)


def _sum_slots(name, x):
    nd, rows, cols = x.shape
    tr = _divisor(rows, max(SUBLANE, (1 << 18) // cols), SUBLANE)

    def body(x_ref, o_ref):
        acc = x_ref[0].astype(f32)
        for s in range(1, nd):
            acc = acc + x_ref[s].astype(f32)
        o_ref[...] = acc

    return pl.pallas_call(body, out_shape=jax.ShapeDtypeStruct((rows, cols), f32), grid=(rows // tr,),
                          in_specs=[pl.BlockSpec((nd, tr, cols), lambda i: (0, i, 0))],
                          out_specs=pl.BlockSpec((tr, cols), lambda i: (i, 0)),
                          compiler_params=_params(1, VMEM_LIMIT), name=name)(x)


def _peer(k):
    x, y, c = (lax.axis_index(a) for a in AXES)
    px = 1 - x if k & 4 else x
    py = 1 - y if k & 2 else y
    pc = 1 - c if k & 1 else c
    return (px, py, pc), 4 * px + 2 * py + pc


def _exchange_out_shapes(xs, scatter):
    return [jax.ShapeDtypeStruct(xa.shape if scatter else (NDEV,) + xa.shape, xa.dtype) for xa in xs]


def _exchange_scratch(n):
    nrel = NDEV - 1
    return [pltpu.SemaphoreType.DMA((n * nrel,)), pltpu.SemaphoreType.DMA((n * nrel,)), pltpu.SemaphoreType.DMA((n,))]


def _exchange_copies(ins, outs, sems, scatter, receiving):
    send_sems, recv_sems, _ = sems
    nrel = NDEV - 1
    x, y, c = (lax.axis_index(a) for a in AXES)
    me = 4 * x + 2 * y + c
    copies = []
    for a in range(len(ins)):
        for k in range(1, NDEV):
            peer, pidx = _peer(k)
            src = ins[a].at[pidx] if scatter else ins[a]
            copies.append(pltpu.make_async_remote_copy(
                src_ref=src, dst_ref=outs[a].at[pidx if receiving else me], send_sem=send_sems.at[a * nrel + k - 1],
                recv_sem=recv_sems.at[a * nrel + k - 1], device_id=peer, device_id_type=pl.DeviceIdType.MESH))
    return copies


def _exchange_local(ins, outs, sems, scatter):
    x, y, c = (lax.axis_index(a) for a in AXES)
    me = 4 * x + 2 * y + c
    return [pltpu.make_async_copy(ins[a].at[me] if scatter else ins[a], outs[a].at[me], sems[2].at[a])
            for a in range(len(ins))]


CHIP_RELATIONS = (2, 4, 6)
SIBLING = 1


def _gather_copy(ins, outs, sems, a, slot, src, block, to):
    nrel = NDEV - 1
    return pltpu.make_async_remote_copy(
        src_ref=src, dst_ref=outs[a].at[block], send_sem=sems[0].at[a * nrel + slot],
        recv_sem=sems[1].at[a * nrel + slot], device_id=to, device_id_type=pl.DeviceIdType.MESH)


def _exchange_start(ins, outs, sems, scatter):
    for cp in _exchange_local(ins, outs, sems, scatter):
        cp.start()
    if scatter:
        for cp in _exchange_copies(ins, outs, sems, scatter, False):
            cp.start()
        return
    x, y, c = (lax.axis_index(a) for a in AXES)
    me = 4 * x + 2 * y + c
    for a in range(len(ins)):
        for slot, k in enumerate((SIBLING,) + CHIP_RELATIONS):
            _gather_copy(ins, outs, sems, a, slot, ins[a], me, _peer(k)[0]).start()


def _exchange_wait(ins, outs, sems, scatter):
    if scatter:
        for cp in _exchange_copies(ins, outs, sems, scatter, True):
            cp.wait_recv()
            cp.wait_send()
    else:
        x, y, c = (lax.axis_index(a) for a in AXES)
        me = 4 * x + 2 * y + c
        sibling, sibling_idx = _peer(SIBLING)
        n = len(ins)
        passed = []
        for a in range(n):
            for j, k in enumerate(CHIP_RELATIONS):
                peer, pidx = _peer(k)
                _gather_copy(ins, outs, sems, a, 1 + j, ins[a], pidx, peer).wait_recv()
                fwd = _gather_copy(ins, outs, sems, a, 4 + j, outs[a].at[pidx], pidx, sibling)
                fwd.start()
                passed.append(fwd)
        for a in range(n):
            _gather_copy(ins, outs, sems, a, 0, ins[a], sibling_idx, sibling).wait_recv()
            for j, k in enumerate(CHIP_RELATIONS):
                _gather_copy(ins, outs, sems, a, 4 + j, ins[a], _peer(k | SIBLING)[1], sibling).wait_recv()
        for a in range(n):
            for slot, k in enumerate((SIBLING,) + CHIP_RELATIONS):
                _gather_copy(ins, outs, sems, a, slot, ins[a], me, _peer(k)[0]).wait_send()
        for fwd in passed:
            fwd.wait_send()
    for cp in _exchange_local(ins, outs, sems, scatter):
        cp.wait()


def _exchange(name, xs, scatter):
    n = len(xs)

    def body(*refs):
        ins, outs, sems = refs[:n], refs[n:2 * n], refs[2 * n:]
        _exchange_start(ins, outs, sems, scatter)
        _exchange_wait(ins, outs, sems, scatter)

    anyspec = pl.BlockSpec(memory_space=pl.ANY)
    outs = pl.pallas_call(
        body, out_shape=tuple(_exchange_out_shapes(xs, scatter)), in_specs=[anyspec] * n, out_specs=(anyspec,) * n,
        scratch_shapes=_exchange_scratch(n), name=name)(*xs)
    return list(outs)


def _all_gather(name, xs):
    return _exchange(name, xs, False)


def _all_to_all(name, xs):
    return _exchange(name, xs, True)


def _rope_table(cfg):
    s, lc = cfg.S, cfg.LC
    rows = s // GRID_W
    r_ids, c_ids = jnp.meshgrid(jnp.arange(rows), jnp.arange(GRID_W), indexing="ij")
    r_ids = r_ids.reshape(-1).astype(f32)
    c_ids = c_ids.reshape(-1).astype(f32)
    freqs = ROPE_THETA ** (-jnp.arange(0, AXIS_DIM, 2, dtype=f32) / AXIS_DIM)
    ang_r = r_ids[:, None] * freqs
    ang_c = c_ids[:, None] * freqs
    cos = jnp.concatenate([jnp.cos(ang_r), jnp.cos(ang_r), jnp.cos(ang_c), jnp.cos(ang_c)], axis=1)
    sin = jnp.concatenate([-jnp.sin(ang_r), jnp.sin(ang_r), -jnp.sin(ang_c), jnp.sin(ang_c)], axis=1)
    cos = jnp.concatenate([jnp.ones((lc, HEAD_DIM), f32), cos], axis=0)
    sin = jnp.concatenate([jnp.zeros((lc, HEAD_DIM), f32), sin], axis=0)
    reps = QKW // HEAD_DIM
    return jnp.stack([jnp.tile(cos, (1, reps)), jnp.tile(sin, (1, reps))])


def _block_diag_ones():
    idx = np.arange(QKW) // HEAD_DIM
    return jnp.asarray((idx[:, None] == idx[None, :]).astype(np.float32))


def _replicate_matrix():
    src = np.arange(KVW)
    dst = np.arange(AW)
    m = (src[:, None] // HEAD_DIM == dst[None, :] // GW) & (src[:, None] % HEAD_DIM == dst[None, :] % HEAD_DIM)
    return m.astype(np.float32)


def _pack_flat(parts):
    flat = jnp.concatenate([p.reshape(-1) for p in parts])
    n = flat.shape[0]
    unit = SUBLANE * LANE
    total = -(-n // unit) * unit
    flat = jnp.pad(flat, (0, total - n))
    return flat.reshape(total // LANE, LANE)


def _unpack_flat(flat, shapes):
    flat = flat.reshape(flat.shape[:-2] + (-1,))
    out, off = [], 0
    for shp in shapes:
        size = int(np.prod(shp))
        out.append(flat[..., off:off + size].reshape(flat.shape[:-1] + tuple(shp)))
        off += size
    return out


def kernel(x, c, ctx, c_ctx, w_ada, b_ada, w_in, q_norm, k_norm, w_attn_o, conf_dw_w, conf_dw_b, conf_ln_g, conf_ln_b, w_conf_out, sc_dw_w, w_sc_out, w_mix_out, w_ffn_in, w_ffn_out, loss_target, m_c_ctx, m_w_ada, m_b_ada, m_w_in, m_q_norm, m_k_norm, m_w_attn_o, m_conf_dw_w, m_conf_dw_b, m_conf_ln_g, m_conf_ln_b, m_w_conf_out, m_sc_dw_w, m_w_sc_out, m_w_mix_out, m_w_ffn_in, m_w_ffn_out, v_c_ctx, v_w_ada, v_b_ada, v_w_in, v_q_norm, v_k_norm, v_w_attn_o, v_conf_dw_w, v_conf_dw_b, v_conf_ln_g, v_conf_ln_b, v_w_conf_out, v_sc_dw_w, v_w_sc_out, v_w_mix_out, v_w_ffn_in, v_w_ffn_out):
    cfg = _Cfg()
    cfg.B, cfg.S, cfg.D = x.shape
    cfg.LC = ctx.shape[1]
    cfg.T = cfg.LC + cfg.S
    cfg.TR = min(256, cfg.LC)
    assert cfg.LC % cfg.TR == 0 and cfg.S % cfg.TR == 0 and cfg.S % GRID_W == 0
    cfg.NT, cfg.NC = cfg.T // cfg.TR, cfg.LC // cfg.TR
    cfg.R = cfg.B * cfg.T
    nl = w_in.shape[0]
    b, d = cfg.B, cfg.D
    cfg.CW = conf_dw_b.shape[1]
    cfg.K31, cfg.K3 = conf_dw_w.shape[1], sc_dw_w.shape[1]
    assert w_sc_out.shape[1] == cfg.CW and cfg.CW % LANE == 0
    assert cfg.K3 // 2 <= cfg.K31 // 2 <= CONV_PAD_ROWS and cfg.LC % CONV_CHUNK == 0 and cfg.S % CONV_CHUNK == 0
    cfg.CB = LANE
    cfg.OFF_CONF = QKVW
    cfg.OFF_SC = cfg.OFF_CONF + 2 * cfg.CW
    cfg.OFF_GATE = cfg.OFF_SC + 3 * cfg.CW
    n_in = w_in.shape[2] * NDEV
    assert n_in == cfg.OFF_GATE + 3 * d and w_attn_o.shape[1] == AW
    cfg.GB = math.gcd(cfg.OFF_GATE, d)
    assert cfg.GB % LANE == 0
    fh = w_ffn_out.shape[1] * NDEV
    na = w_ada.shape[2]
    cw8 = cfg.CW // NDEV

    xi, yi, ci = (lax.axis_index(a) for a in AXES)
    me = 4 * xi + 2 * yi + ci

    small_shapes = [c.shape, conf_dw_w.shape, sc_dw_w.shape]
    (g0,) = _all_gather("gather_small", [_pack_flat([c, conf_dw_w, sc_dw_w])])
    c_all, cw_all, sw_all = _unpack_flat(g0, small_shapes)
    nb = NDEV * b
    nbp = -(-(nb + 1) // SUBLANE) * SUBLANE
    call = jnp.concatenate([c_all.reshape(nb, d), c_ctx[None, :], jnp.zeros((nbp - nb - 1, d), f32)], axis=0)
    w31_full = jnp.moveaxis(cw_all, 0, 2).reshape(nl, cfg.K31, cfg.CW)
    w3_full = jnp.moveaxis(sw_all, 0, 2).reshape(nl, cfg.K3, cfg.CW)
    k31p = -(-(cfg.K31 + 1) // SUBLANE) * SUBLANE
    w31b = jnp.concatenate([w31_full, conf_dw_b[:, None, :], jnp.zeros((nl, k31p - cfg.K31 - 1, cfg.CW), f32)], axis=1)
    w3p = jnp.concatenate([w3_full, jnp.zeros((nl, SUBLANE - cfg.K3, cfg.CW), f32)], axis=1)

    b_loc = lax.dynamic_slice(b_ada, (0, me * na), (nl, na))[:, None, :]
    mod_loc = _ada_fwd("ada_fwd", call, w_ada, b_loc)
    (mod_g,) = _all_gather("gather_mod", [mod_loc.reshape(nl * nbp, na)])
    mod_full = jnp.transpose(mod_g.reshape(NDEV, nl, nbp, na), (1, 2, 0, 3)).reshape(nl, nbp, N_MOD, d)
    mod_lat = lax.dynamic_slice_in_dim(mod_full, me * b, b, axis=1)
    mod_ctx = jnp.broadcast_to(mod_full[:, nb][:, None], (nl, b, N_MOD, d))
    mod = jnp.stack([mod_ctx, mod_lat], axis=2)
    mod = jnp.pad(mod, ((0, 0), (0, 0), (0, 0), (0, MOD_ROWS - N_MOD), (0, 0)))

    cs = _rope_table(cfg)
    bd = _block_diag_ones().astype(bf16)
    rep_np = _replicate_matrix()
    rep = jnp.asarray(rep_np, dtype=bf16)
    rept = jnp.asarray(rep_np.T, dtype=bf16)
    reps_q, reps_k = AW // HEAD_DIM, KVW // HEAD_DIM

    xin = jnp.concatenate([ctx, x], axis=1).reshape(cfg.R, d)
    target = loss_target.reshape(b * cfg.S, d)
    saved = []
    xcur, ocur = xin, None

    def weight_shards(l):
        return [jnp.transpose(w_in[l]).astype(bf16), jnp.transpose(w_attn_o[l]).astype(bf16),
                jnp.transpose(w_conf_out[l]).astype(bf16), jnp.transpose(w_sc_out[l]).astype(bf16),
                w_mix_out[l].astype(bf16), jnp.transpose(w_ffn_in[l]).astype(bf16), w_ffn_out[l].astype(bf16)]

    def whole(g):
        return g.reshape(NDEV * g.shape[1], g.shape[2])

    shards0 = weight_shards(0)
    (w_first,) = _all_gather("gather_w0", shards0[:1])
    w_rest = None
    for l in range(nl):
        win_t = whole(w_first)
        gain = jnp.concatenate([jnp.tile(q_norm[l], reps_q), jnp.tile(k_norm[l], reps_k)])[None, :]
        lnp = jnp.concatenate([conf_ln_g[l][None], conf_ln_b[l][None], jnp.zeros((SUBLANE - 2, cfg.CW), f32)], axis=0)
        nxt = weight_shards(l + 1) if l + 1 < nl else None

        if l == 0:
            x0, h = _norm_fwd(cfg, f"norm_a{l}", xcur, None, None, 0, mod[l], 0)
            p, w_rest = _mm(h, win_t, "nt", ACT, f"mm_in{l}", comm=(shards0[1:], False))
        else:
            x0, h = _norm_fwd(cfg, f"norm_a{l}", xcur, ocur, mod[l - 1], 5, mod[l], 0)
            p = _mm(h, win_t, "nt", ACT, f"mm_in{l}")
        wao_t, wco_t, wso_t, wmix, wfi_t, wfo = [whole(g) for g in w_rest]
        q, krep, vrep = _qkv_fwd(cfg, f"qkv{l}", p, cs, gain, bd, rep)
        (attn, lse), w_rest = _attn_fwd(cfg, f"attn{l}", q, krep, vrep,
                                        comm=None if nxt is None else (nxt[1:], False))
        hc, z = _conv_fwd(cfg, f"conv{l}", p, w31b[l], w3p[l])
        hs = _ln_silu_fwd(cfg, f"lnsilu{l}", hc, lnp)
        ya = _mm(attn, wao_t, "nt", ACT, f"mm_ao{l}")
        yc = _mm(hs, wco_t, "nt", ACT, f"mm_co{l}")
        ys = _mm(z, wso_t, "nt", ACT, f"mm_so{l}")
        merged = _gate_fwd(cfg, f"gate{l}", p, ya, yc, ys)
        mixed = _mm(merged, wmix, "nn", f32, f"mm_mix{l}")
        x1, h2 = _norm_fwd(cfg, f"norm_b{l}", x0, mixed, mod[l], 2, mod[l], 3)
        if nxt is None:
            ff = _mm(h2, wfi_t, "nt", ACT, f"mm_fi{l}")
        else:
            ff, (w_first,) = _mm(h2, wfi_t, "nt", ACT, f"mm_fi{l}", comm=(nxt[:1], False))
        sw = _swiglu_fwd(cfg, f"swiglu{l}", ff)
        o = _mm(sw, wfo, "nn", f32, f"mm_fo{l}")
        saved.append(dict(x0=x0, h=h, p=p, q=q, krep=krep, vrep=vrep, attn=attn, lse=lse, hc=hc, z=z, hs=hs,
                          ya=ya, yc=yc, ys=ys, merged=merged, mixed=mixed, x1=x1, h2=h2, ff=ff, sw=sw, o=o,
                          oprev=ocur, gain=gain, lnp=lnp,
                          w=(win_t, wao_t, wco_t, wso_t, wmix, wfi_t, wfo)))
        xcur, ocur = x1, o

    lsum, dres, do, dm_loss = _loss_head(cfg, "loss", xcur, ocur, mod[nl - 1], 5, target)
    loss = lax.psum((0.5 / d) * jnp.sum(lsum), AXES)

    dmod_rows = [[None] * N_MOD for _ in range(nl)]
    dmod_rows[nl - 1][5] = dm_loss[:, :, 2]
    g_qn, g_kn, g_w31, g_w3, g_ln = [None] * nl, [None] * nl, [None] * nl, [None] * nl, [None] * nl
    gbig = [None] * nl
    pending = None
    pending_early = None

    def blocks(g):
        return g.reshape(NDEV, g.shape[0] // NDEV, g.shape[1])

    def sum_received(l, recv):
        gbig[l] = [_sum_slots(f"sum_g{l}_{i}", r) for i, r in enumerate(recv)]

    for l in reversed(range(nl)):
        sv = saved[l]
        win_t, wao_t, wco_t, wso_t, wmix, wfi_t, wfo = sv["w"]
        ds = _mm(do, wfo, "nt", ACT, f"mm_dsw{l}")
        g_wfo = _mm(sv["sw"], do, "tn", bf16, f"mm_gfo{l}")
        dff = _swiglu_bwd(cfg, f"swiglu_b{l}", sv["ff"], ds)
        dh2 = _mm(dff, wfi_t, "nn", f32, f"mm_dh2{l}")
        g_wfi = _mm(dff, sv["h2"], "tn", bf16, f"mm_gfi{l}")
        dx0p, dmixed, dm_b = _norm_bwd(cfg, f"norm_b_b{l}", sv["x1"], sv["mixed"], dh2, dres, mod[l], 2, mod[l], 3)
        dmod_rows[l][3], dmod_rows[l][4], dmod_rows[l][2] = dm_b[:, :, 0], dm_b[:, :, 1], dm_b[:, :, 2]
        dmerged = _mm(dmixed, wmix, "nt", f32, f"mm_dmg{l}")
        g_wmix = _mm(sv["merged"], dmixed, "tn", bf16, f"mm_gmix{l}")
        dya, dyc, dys, dg0, dg1, dg2 = _gate_bwd(cfg, f"gate_b{l}", sv["p"], sv["ya"], sv["yc"], sv["ys"], dmerged)
        dattn = _mm(dya, wao_t, "nn", f32, f"mm_dat{l}")
        g_wao = _mm(dya, sv["attn"], "tn", bf16, f"mm_gao{l}")
        dhs = _mm(dyc, wco_t, "nn", f32, f"mm_dhs{l}")
        g_wco = _mm(dyc, sv["hs"], "tn", bf16, f"mm_gco{l}")
        dz = _mm(dys, wso_t, "nn", f32, f"mm_dz{l}")
        g_wso = _mm(dys, sv["z"], "tn", bf16, f"mm_gso{l}")
        dhc, g_ln[l] = _ln_silu_bwd(cfg, f"lnsilu_b{l}", sv["hc"], dhs, sv["lnp"])
        (da, dg, dbg, dcg, dxs, g_w31[l], g_w3[l]), recv = _conv_bwd(
            cfg, f"conv_b{l}", sv["p"], dhc, dz, w31b[l], w3p[l], comm=None if pending is None else ([pending], True))
        if pending is not None:
            sum_received(l + 1, recv + pending_early)
        early = [blocks(g) for g in (g_wao, g_wco, g_wso, g_wmix, g_wfi, g_wfo)]
        (dq, dkr, dvr), pending_early = _attn_bwd(cfg, f"attn_b{l}", sv["q"], sv["krep"], sv["vrep"], sv["attn"],
                                                  sv["lse"], dattn, comm=(early, True))
        dpq, dgain = _qkv_bwd(cfg, f"qkv_b{l}", sv["p"], dq, dkr, dvr, cs, sv["gain"], bd, rept)
        g_qn[l] = jnp.sum(dgain[0, :AW].reshape(reps_q, HEAD_DIM), axis=0)
        g_kn[l] = jnp.sum(dgain[0, AW:].reshape(reps_k, HEAD_DIM), axis=0)
        dp = jnp.concatenate([dpq, da, dg, dbg, dcg, dxs, dg0, dg1, dg2], axis=1)
        g_win = _mm(dp, sv["h"], "tn", bf16, f"mm_gin{l}")
        pending = blocks(g_win)
        if l == 0:
            dh, recv = _mm(dp, win_t, "nn", f32, f"mm_dh{l}", comm=([pending], True))
            sum_received(0, recv + pending_early)
            dres, _, dm_a = _norm_bwd(cfg, f"norm_a_b{l}", sv["x0"], None, dh, dx0p, None, 0, mod[l], 0)
        else:
            dh = _mm(dp, win_t, "nn", f32, f"mm_dh{l}")
            dres, do, dm_a = _norm_bwd(cfg, f"norm_a_b{l}", sv["x0"], sv["oprev"], dh, dx0p, mod[l - 1], 5, mod[l], 0)
            dmod_rows[l - 1][5] = dm_a[:, :, 2]
        dmod_rows[l][0], dmod_rows[l][1] = dm_a[:, :, 0], dm_a[:, :, 1]

    grad_x = dres.reshape(b, cfg.T, d)[:, cfg.LC:, :]

    rb = -(-(b + 1) // SUBLANE) * SUBLANE
    dmod_l = jnp.stack([jnp.stack(rows, axis=2) for rows in dmod_rows])
    d_lat = dmod_l[:, :, 1].reshape(nl, b, NDEV, na)
    d_ctx = jnp.sum(dmod_l[:, :, 0], axis=1).reshape(nl, 1, NDEV, na)
    send = jnp.concatenate([d_lat, d_ctx, jnp.zeros((nl, rb - b - 1, NDEV, na), f32)], axis=1)
    send = jnp.transpose(send, (2, 0, 1, 3)).reshape(NDEV, nl * rb, na)
    (recv,) = _all_to_all("scatter_dmod", [send])
    recv = recv.reshape(NDEV, nl, rb, na)
    dmod_ex = jnp.transpose(recv[:, :, :b], (1, 0, 2, 3)).reshape(nl, nb, na)
    dmod_ex = jnp.pad(dmod_ex, ((0, 0), (0, nbp - nb), (0, 0)))
    dmodc = jnp.transpose(recv[:, :, b], (1, 0, 2))
    g_wada, g_bloc, g_cctx = _ada_bwd("ada_bwd", call, w_ada, dmod_ex, dmodc, nb)

    g_bada = lax.dynamic_update_slice(jnp.zeros((nl, N_MOD * d), f32), g_bloc[:, 0], (0, me * na))
    g_w31s = jnp.stack(g_w31)
    small_parts = [g_cctx[0], g_bada, jnp.stack(g_qn), jnp.stack(g_kn), g_w31s[:, :cfg.K31], g_w31s[:, cfg.K31],
                   jnp.stack([g[0] for g in g_ln]), jnp.stack([g[1] for g in g_ln]),
                   jnp.stack(g_w3)[:, :cfg.K3]]
    small_part_shapes = [p_.shape for p_ in small_parts]
    (gs,) = _all_gather("gather_gsmall", [_pack_flat(small_parts)])
    gsum = _sum_slots("sum_gsmall", gs)
    (gr_cctx, gr_bada, gr_qn, gr_kn, gr_w31, gr_b31, gr_lng, gr_lnb, gr_w3) = _unpack_flat(gsum, small_part_shapes)
    gr_w31 = lax.dynamic_slice_in_dim(gr_w31, me * cw8, cw8, axis=2)
    gr_w3 = lax.dynamic_slice_in_dim(gr_w3, me * cw8, cw8, axis=2)

    def big(i, transposed):
        g = jnp.stack([gbig[l][i] for l in range(nl)])
        return jnp.transpose(g, (0, 2, 1)) if transposed else g

    grads = {
        "c_ctx": gr_cctx, "w_ada": g_wada, "b_ada": gr_bada, "w_in": big(0, True), "q_norm": gr_qn, "k_norm": gr_kn,
        "w_attn_o": big(1, True), "conf_dw_w": gr_w31, "conf_dw_b": gr_b31, "conf_ln_g": gr_lng, "conf_ln_b": gr_lnb,
        "w_conf_out": big(2, True), "sc_dw_w": gr_w3, "w_sc_out": big(3, True), "w_mix_out": big(4, False),
        "w_ffn_in": big(5, True), "w_ffn_out": big(6, False)}
    weights = dict(c_ctx=c_ctx, w_ada=w_ada, b_ada=b_ada, w_in=w_in, q_norm=q_norm, k_norm=k_norm, w_attn_o=w_attn_o,
                   conf_dw_w=conf_dw_w, conf_dw_b=conf_dw_b, conf_ln_g=conf_ln_g, conf_ln_b=conf_ln_b,
                   w_conf_out=w_conf_out, sc_dw_w=sc_dw_w, w_sc_out=w_sc_out, w_mix_out=w_mix_out,
                   w_ffn_in=w_ffn_in, w_ffn_out=w_ffn_out)
    m_in = dict(c_ctx=m_c_ctx, w_ada=m_w_ada, b_ada=m_b_ada, w_in=m_w_in, q_norm=m_q_norm, k_norm=m_k_norm,
                w_attn_o=m_w_attn_o, conf_dw_w=m_conf_dw_w, conf_dw_b=m_conf_dw_b, conf_ln_g=m_conf_ln_g,
                conf_ln_b=m_conf_ln_b, w_conf_out=m_w_conf_out, sc_dw_w=m_sc_dw_w, w_sc_out=m_w_sc_out,
                w_mix_out=m_w_mix_out, w_ffn_in=m_w_ffn_in, w_ffn_out=m_w_ffn_out)
    v_in = dict(c_ctx=v_c_ctx, w_ada=v_w_ada, b_ada=v_b_ada, w_in=v_w_in, q_norm=v_q_norm, k_norm=v_k_norm,
                w_attn_o=v_w_attn_o, conf_dw_w=v_conf_dw_w, conf_dw_b=v_conf_dw_b, conf_ln_g=v_conf_ln_g,
                conf_ln_b=v_conf_ln_b, w_conf_out=v_w_conf_out, sc_dw_w=v_sc_dw_w, w_sc_out=v_w_sc_out,
                w_mix_out=v_w_mix_out, w_ffn_in=v_w_ffn_in, w_ffn_out=v_w_ffn_out)
    names = list(weights)
    big_names = ("w_ada", "w_in", "w_attn_o", "w_conf_out", "w_sc_out", "w_mix_out", "w_ffn_in", "w_ffn_out")
    small_names = [n for n in names if n not in big_names]
    delta, new_m, new_v = {}, {}, {}
    for n in big_names:
        shp = weights[n].shape
        two_d = (shp[0] * shp[1], shp[2])
        dl, nm, nv = _adamw(f"adamw_{n}", weights[n].reshape(two_d), grads[n].reshape(two_d),
                            m_in[n].reshape(two_d), v_in[n].reshape(two_d))
        delta[n], new_m[n], new_v[n] = dl.reshape(shp), nm.reshape(shp), nv.reshape(shp)
    sshapes = [weights[n].shape for n in small_names]
    dl, nm, nv = _adamw("adamw_small", _pack_flat([weights[n] for n in small_names]),
                        _pack_flat([grads[n] for n in small_names]), _pack_flat([m_in[n] for n in small_names]),
                        _pack_flat([v_in[n] for n in small_names]))
    for n, a_, b_, c_ in zip(small_names, _unpack_flat(dl, sshapes), _unpack_flat(nm, sshapes), _unpack_flat(nv, sshapes)):
        delta[n], new_m[n], new_v[n] = a_, b_, c_

    return (loss, grad_x, *[grads[n] for n in names], *[delta[n] for n in names],
            *[new_m[n] for n in names], *[new_v[n] for n in names])
```

```python
import functools
import math

import numpy as np
import jax
import jax.numpy as jnp
from jax import lax
from jax.experimental import pallas as pl
from jax.experimental.pallas import tpu as pltpu

f32 = jnp.float32
bf16 = jnp.bfloat16
ACT = bf16
GRAD = bf16

NDEV = 8
AXES = ("x", "y", "c")
HEAD_DIM = 64
N_Q_HEADS = 8
N_KV_HEADS = 2
Q_GROUP = N_Q_HEADS // N_KV_HEADS
AW = N_Q_HEADS * HEAD_DIM
KVW = N_KV_HEADS * HEAD_DIM
GW = Q_GROUP * HEAD_DIM
QKW = AW + KVW
QKVW = AW + 2 * KVW
GRID_W = 64
AXIS_DIM = HEAD_DIM // 2
ROPE_THETA = 10000.0
ATTN_SCALE = HEAD_DIM ** -0.5
EPS = 1e-6
N_MOD = 6
MOD_ROWS = 8
CONV_PAD_ROWS = 16
GATE_ROWS = 1024
HEADS_PER_PASS_FWD = 1
HEADS_PER_PASS_BWD = 2
CONV_CHUNK = 64
LANE = 128
SUBLANE = 8
VMEM_LIMIT = 56 * 1024 * 1024
MM_VMEM_BUDGET = 40 * 1024 * 1024

ADAM_LR = 0.001
ADAM_B1 = 0.9
ADAM_B2 = 0.999
ADAM_EPS = 1e-08
ADAM_WD = 0.01
ADAM_STEP = 10

NN =(((1,), (0,)), ((), ()))
NT_ = (((1,), (1,)), ((), ()))
TN = (((0,), (0,)), ((), ()))


def _params(ndims, vmem=None):
    return pltpu.CompilerParams(dimension_semantics=("arbitrary",) * ndims, vmem_limit_bytes=vmem)


def _divisor(n, cap, mult):
    best = None
    for d in range(mult, min(n, cap) + 1, mult):
        if n % d == 0:
            best = d
    return n if best is None else best


def _const_spec(a):
    nd = a.ndim
    return pl.BlockSpec(a.shape, lambda *idx: (0,) * nd)


class _Cfg:
    pass


def _mm(a, b, mode, out_dtype, name, comm=None):
    def accumulate(prod, o_ref, scratch, nsteps):
        if nsteps == 1:
            o_ref[...] = prod.astype(out_dtype)
            return
        acc_ref = scratch[0]
        k = pl.program_id(2)

        @pl.when(k == 0)
        def _():
            acc_ref[...] = prod

        if nsteps > 2:
            @pl.when(jnp.logical_and(k > 0, k < nsteps - 1))
            def _():
                acc_ref[...] += prod

        @pl.when(k == nsteps - 1)
        def _():
            o_ref[...] = (acc_ref[...] + prod).astype(out_dtype)

    def fits(*tiles):
        return sum(2 * r * c * s for r, c, s in tiles[:-1]) + tiles[-1][0] * tiles[-1][1] * 4 <= MM_VMEM_BUDGET

    osz = jnp.dtype(out_dtype).itemsize
    if mode == "tn":
        m, ka = a.shape
        nb = b.shape[1]
        tka = _divisor(ka, 1408, LANE)
        tnb = _divisor(nb, 1024, LANE)
        tmr = SUBLANE
        for cand in range(SUBLANE, min(m, 2304) + 1, SUBLANE):
            if m % cand == 0 and fits((cand, tka, a.dtype.itemsize), (cand, tnb, b.dtype.itemsize),
                                      (tka, tnb, osz), (tka, tnb, 4)):
                tmr = cand
        nsteps = m // tmr

        def body(a_ref, b_ref, o_ref, *scratch):
            prod = lax.dot_general(a_ref[...].astype(bf16), b_ref[...].astype(bf16), TN, preferred_element_type=f32)
            accumulate(prod, o_ref, scratch, nsteps)

        return pl.pallas_call(
            body, out_shape=jax.ShapeDtypeStruct((ka, nb), out_dtype),
            grid=(ka // tka, nb // tnb, nsteps),
            in_specs=[pl.BlockSpec((tmr, tka), lambda i, j, k: (k, i)),
                      pl.BlockSpec((tmr, tnb), lambda i, j, k: (k, j))],
            out_specs=pl.BlockSpec((tka, tnb), lambda i, j, k: (i, j)),
            scratch_shapes=[pltpu.VMEM((tka, tnb), f32)] if nsteps > 1 else [],
            compiler_params=_params(3, VMEM_LIMIT), name=name)(a, b)

    m, kdim = a.shape
    n = b.shape[1] if mode == "nn" else b.shape[0]
    tm = _divisor(m, 1024 if a.dtype == bf16 else 512, SUBLANE)
    tn = _divisor(n, 1408, LANE)
    tk = LANE if kdim % LANE == 0 else kdim
    for cand in range(LANE, min(kdim, 3328) + 1, LANE):
        if kdim % cand == 0 and fits((tm, cand, a.dtype.itemsize), (cand, tn, b.dtype.itemsize), (tm, tn, osz),
                                     (tm, tn, 4)):
            tk = cand
    nsteps = kdim // tk
    dims = NN if mode == "nn" else NT_

    def body(a_ref, b_ref, o_ref, *scratch):
        prod = lax.dot_general(a_ref[...].astype(bf16), b_ref[...].astype(bf16), dims, preferred_element_type=f32)
        accumulate(prod, o_ref, scratch, nsteps)

    if mode == "nn":
        b_spec = pl.BlockSpec((tk, tn), lambda i, j, k: (k, j))
    else:
        b_spec = pl.BlockSpec((tn, tk), lambda i, j, k: (j, k))
    (out,), couts = _hosting_call(
        body, comm, out_shape=(jax.ShapeDtypeStruct((m, n), out_dtype),),
        grid=(m // tm, n // tn, nsteps),
        in_specs=[pl.BlockSpec((tm, tk), lambda i, j, k: (i, k)), b_spec],
        out_specs=(pl.BlockSpec((tm, tn), lambda i, j, k: (i, j)),),
        scratch_shapes=[pltpu.VMEM((tm, tn), f32)] if nsteps > 1 else [],
        compiler_params=_params(3, VMEM_LIMIT), name=name, args=(a, b))
    return out if comm is None else (out, couts)


def _row_spec(cfg, width, col=0):
    return pl.BlockSpec((cfg.TR, width), lambda i: (i, col))


def _mod_spec(cfg):
    nt, nc = cfg.NT, cfg.NC
    return pl.BlockSpec((1, 1, MOD_ROWS, cfg.D),
                        lambda i: (i // nt, ((i % nt) >= nc).astype(jnp.int32), 0, 0))


def _segment_start(cfg, i):
    j = i % cfg.NT
    return jnp.logical_or(j == 0, j == cfg.NC)


def _norm_fwd(cfg, name, xin, o, modg, gk, modn, sk):
    has_o = o is not None
    r_, d = xin.shape

    def body(*refs):
        if has_o:
            x_ref, o_ref, mg_ref, mn_ref, xn_ref, h_ref = refs
            x = x_ref[...] + mg_ref[0, 0, gk:gk + 1, :] * o_ref[...]
            xn_ref[...] = x
        else:
            x_ref, mn_ref, h_ref = refs
            x = x_ref[...]
        r = lax.rsqrt(jnp.mean(x * x, axis=-1, keepdims=True) + EPS)
        h = (x * r) * (1.0 + mn_ref[0, 0, sk + 1:sk + 2, :]) + mn_ref[0, 0, sk:sk + 1, :]
        h_ref[...] = h.astype(bf16)

    row = _row_spec(cfg, d)
    if has_o:
        ins, in_specs = (xin, o, modg, modn), [row, row, _mod_spec(cfg), _mod_spec(cfg)]
        out_shape = (jax.ShapeDtypeStruct((r_, d), f32), jax.ShapeDtypeStruct((r_, d), bf16))
        out_specs = (row, row)
    else:
        ins, in_specs = (xin, modn), [row, _mod_spec(cfg)]
        out_shape = jax.ShapeDtypeStruct((r_, d), bf16)
        out_specs = row
    res = pl.pallas_call(body, out_shape=out_shape, grid=(r_ // cfg.TR,), in_specs=in_specs,
                         out_specs=out_specs, compiler_params=_params(1), name=name)(*ins)
    return res if has_o else (xin, res)


def _norm_bwd(cfg, name, xnew, o, dh, dres, modg, gk, modn, sk):
    has_o = o is not None
    r_, d = xnew.shape

    def body(*refs):
        if has_o:
            xn_ref, o_ref, dh_ref, dr_ref, mg_ref, mn_ref, dx_ref, do_ref, dm_ref = refs
        else:
            xn_ref, dh_ref, dr_ref, mn_ref, dx_ref, dm_ref = refs
        i = pl.program_id(0)
        x = xn_ref[...]
        dhv = dh_ref[...].astype(f32)
        r = lax.rsqrt(jnp.mean(x * x, axis=-1, keepdims=True) + EPS)
        xh = x * r
        dxh = dhv * (1.0 + mn_ref[0, 0, sk + 1:sk + 2, :])
        dx = r * (dxh - xh * jnp.mean(dxh * xh, axis=-1, keepdims=True)) + dr_ref[...]
        dx_ref[...] = dx

        @pl.when(_segment_start(cfg, i))
        def _():
            dm_ref[...] = jnp.zeros_like(dm_ref)

        dm_ref[0, 0, 0:1, :] += jnp.sum(dhv, axis=0, keepdims=True)
        dm_ref[0, 0, 1:2, :] += jnp.sum(dhv * xh, axis=0, keepdims=True)
        if has_o:
            ov = o_ref[...]
            dm_ref[0, 0, 2:3, :] += jnp.sum(dx * ov, axis=0, keepdims=True)
            do_ref[...] = (dx * mg_ref[0, 0, gk:gk + 1, :]).astype(bf16)

    row = _row_spec(cfg, d)
    dm_shape = jax.ShapeDtypeStruct((cfg.B, 2, MOD_ROWS, d), f32)
    if has_o:
        ins = (xnew, o, dh, dres, modg, modn)
        in_specs = [row, row, row, row, _mod_spec(cfg), _mod_spec(cfg)]
        out_shape = (jax.ShapeDtypeStruct((r_, d), f32), jax.ShapeDtypeStruct((r_, d), bf16), dm_shape)
        out_specs = (row, row, _mod_spec(cfg))
    else:
        ins = (xnew, dh, dres, modn)
        in_specs = [row, row, row, _mod_spec(cfg)]
        out_shape = (jax.ShapeDtypeStruct((r_, d), f32), dm_shape)
        out_specs = (row, _mod_spec(cfg))
    res = pl.pallas_call(body, out_shape=out_shape, grid=(r_ // cfg.TR,), in_specs=in_specs,
                         out_specs=out_specs, compiler_params=_params(1), name=name)(*ins)
    if has_o:
        return res
    return res[0], None, res[1]


def _loss_head(cfg, name, x1, o, modg, gk, target):
    r_, d = x1.shape
    nt, nc = cfg.NT, cfg.NC
    nlat = nt - nc

    def body(x_ref, o_ref, mg_ref, t_ref, ls_ref, dy_ref, do_ref, dm_ref):
        i = pl.program_id(0)
        lat = (i % nt) >= nc
        gate = mg_ref[0, 0, gk:gk + 1, :]
        ov = o_ref[...]
        err = jnp.where(lat, x_ref[...] + gate * ov - t_ref[...], 0.0)
        ls_ref[...] = jnp.zeros_like(ls_ref)
        ls_ref[0, 0:1, :] = jnp.sum(err * err, axis=0, keepdims=True)
        dy = err * (1.0 / d)
        dy_ref[...] = dy
        do_ref[...] = (dy * gate).astype(bf16)

        @pl.when(_segment_start(cfg, i))
        def _():
            dm_ref[...] = jnp.zeros_like(dm_ref)

        dm_ref[0, 0, 2:3, :] += jnp.sum(dy * ov, axis=0, keepdims=True)

    row = _row_spec(cfg, d)
    t_spec = pl.BlockSpec((cfg.TR, d), lambda i: ((i // nt) * nlat + jnp.maximum((i % nt) - nc, 0), 0))
    return pl.pallas_call(
        body,
        out_shape=(jax.ShapeDtypeStruct((r_ // cfg.TR, SUBLANE, d), f32), jax.ShapeDtypeStruct((r_, d), f32),
                   jax.ShapeDtypeStruct((r_, d), bf16), jax.ShapeDtypeStruct((cfg.B, 2, MOD_ROWS, d), f32)),
        grid=(r_ // cfg.TR,),
        in_specs=[row, row, _mod_spec(cfg), t_spec],
        out_specs=(pl.BlockSpec((1, SUBLANE, d), lambda i: (i, 0, 0)), row, row, _mod_spec(cfg)),
        compiler_params=_params(1), name=name)(x1, o, modg, target)


def _swap16(y):
    w = y.shape[1]
    lane = lax.broadcasted_iota(jnp.int32, (1, w), 1)
    lo = jnp.bitwise_and(lane, 31) < 16
    return jnp.where(lo, pltpu.roll(y, w - 16, 1), pltpu.roll(y, 16, 1))


def _dot_split(v, m):
    hi = v.astype(bf16)
    lo = (v - hi.astype(f32)).astype(bf16)
    return jnp.dot(hi, m, preferred_element_type=f32) + jnp.dot(lo, m, preferred_element_type=f32)


def _head_sum(v, bd):
    return _dot_split(v, bd)


def _qkv_fwd(cfg, name, p, cs, gain, bd, rep):
    r_ = p.shape[0]
    nt = cfg.NT

    def body(p_ref, cs_ref, g_ref, bd_ref, rep_ref, q_ref, k_ref, v_ref):
        x = p_ref[:, 0:QKW].astype(f32)
        r = lax.rsqrt(_head_sum(x * x, bd_ref[...]) * (1.0 / HEAD_DIM) + EPS)
        y = (x * r) * g_ref[...]
        out = y * cs_ref[0] + _swap16(y) * cs_ref[1]
        q_ref[...] = out[:, 0:AW].astype(bf16)
        kr = out[:, AW:QKW].astype(bf16)
        k_ref[...] = jnp.dot(kr, rep_ref[...], preferred_element_type=f32).astype(bf16)
        vv = p_ref[:, QKW:QKVW].astype(bf16)
        v_ref[...] = jnp.dot(vv, rep_ref[...], preferred_element_type=f32).astype(bf16)

    row = pl.BlockSpec((cfg.TR, AW), lambda j, e: (e * nt + j, 0))
    return pl.pallas_call(
        body, out_shape=(jax.ShapeDtypeStruct((r_, AW), bf16),) * 3, grid=(nt, cfg.B),
        in_specs=[pl.BlockSpec((cfg.TR, QKVW), lambda j, e: (e * nt + j, 0)),
                  pl.BlockSpec((2, cfg.TR, QKW), lambda j, e: (0, j, 0)),
                  _const_spec(gain), _const_spec(bd), _const_spec(rep)],
        out_specs=(row, row, row), compiler_params=_params(2), name=name)(p, cs, gain, bd, rep)


def _qkv_bwd(cfg, name, p, dq, dkr, dvr, cs, gain, bd, rept):
    r_ = p.shape[0]
    nt = cfg.NT

    def body(p_ref, dq_ref, dk_ref, dv_ref, cs_ref, g_ref, bd_ref, rt_ref, dp_ref, dg_ref):
        first = jnp.logical_and(pl.program_id(0) == 0, pl.program_id(1) == 0)
        x = p_ref[:, 0:QKW].astype(f32)
        rt = rt_ref[...]
        dk = _dot_split(dk_ref[...], rt)
        g = jnp.concatenate([dq_ref[...].astype(f32), dk], axis=1)
        gy = g * cs_ref[0] + _swap16(g * cs_ref[1])
        bdv = bd_ref[...]
        r = lax.rsqrt(_head_sum(x * x, bdv) * (1.0 / HEAD_DIM) + EPS)
        xh = x * r
        dxh = gy * g_ref[...]
        dx = r * (dxh - xh * (_head_sum(dxh * xh, bdv) * (1.0 / HEAD_DIM)))
        dp_ref[:, 0:QKW] = dx.astype(bf16)
        dp_ref[:, QKW:QKVW] = _dot_split(dv_ref[...], rt).astype(bf16)

        @pl.when(first)
        def _():
            dg_ref[...] = jnp.zeros_like(dg_ref)

        dg_ref[0:1, :] += jnp.sum(gy * xh, axis=0, keepdims=True)

    row = pl.BlockSpec((cfg.TR, AW), lambda j, e: (e * nt + j, 0))
    wide = pl.BlockSpec((cfg.TR, QKVW), lambda j, e: (e * nt + j, 0))
    return pl.pallas_call(
        body, out_shape=(jax.ShapeDtypeStruct((r_, QKVW), bf16), jax.ShapeDtypeStruct((SUBLANE, QKW), f32)),
        grid=(nt, cfg.B),
        in_specs=[wide, row, row, row, pl.BlockSpec((2, cfg.TR, QKW), lambda j, e: (0, j, 0)),
                  _const_spec(gain), _const_spec(bd), _const_spec(rept)],
        out_specs=(wide, pl.BlockSpec((SUBLANE, QKW), lambda j, e: (0, 0))),
        compiler_params=_params(2), name=name)(p, dq, dkr, dvr, cs, gain, bd, rept)


def _head_masks():
    lane = lax.broadcasted_iota(jnp.int32, (1, GW), 1)
    return [jnp.logical_and(lane >= HEAD_DIM * h, lane < HEAD_DIM * (h + 1)) for h in range(Q_GROUP)]


def _attn_fwd(cfg, name, q, krep, vrep, comm=None):
    r_ = q.shape[0]
    tr, t, lc, nt, nc = cfg.TR, cfg.T, cfg.LC, cfg.NT, cfg.NC

    def body(q_ref, k_ref, v_ref, o_ref, l_ref):
        j = pl.program_id(2)
        masks = _head_masks()
        lane = lax.broadcasted_iota(jnp.int32, (1, LANE), 1)

        def run(nk):
            qv = q_ref[...]
            kv = k_ref[0:nk, :]
            vv = v_ref[0:nk, :]
            acc = jnp.zeros((tr, GW), f32)
            lse = jnp.zeros((tr, LANE), f32)
            for h0 in range(0, Q_GROUP, HEADS_PER_PASS_FWD):
                hs = range(h0, h0 + HEADS_PER_PASS_FWD)
                qs = jnp.concatenate([jnp.where(masks[h], qv, jnp.zeros_like(qv)) for h in hs], axis=0)
                s = lax.dot_general(qs, kv, NT_, preferred_element_type=f32) * ATTN_SCALE
                m = jnp.max(s, axis=1, keepdims=True)
                pr = jnp.exp(s - m)
                l = jnp.sum(pr, axis=1, keepdims=True)
                os_ = jnp.dot(pr.astype(bf16), vv, preferred_element_type=f32) / l
                ls = m + jnp.log(l)
                for i, h in enumerate(hs):
                    acc = acc + jnp.where(masks[h], os_[i * tr:(i + 1) * tr], 0.0)
                    lse = lse + jnp.where(lane == h, ls[i * tr:(i + 1) * tr], 0.0)
            o_ref[...] = acc
            l_ref[...] = lse

        @pl.when(j < nc)
        def _():
            run(lc)

        @pl.when(j >= nc)
        def _():
            run(t)

    qspec = pl.BlockSpec((tr, GW), lambda e, g, j: (e * nt + j, g))
    kspec = pl.BlockSpec((t, GW), lambda e, g, j: (e, g))
    return _hosting_call(
        body, comm, out_shape=(jax.ShapeDtypeStruct((r_, AW), f32), jax.ShapeDtypeStruct((r_, N_KV_HEADS * LANE), f32)),
        grid=(cfg.B, N_KV_HEADS, nt), in_specs=[qspec, kspec, kspec],
        out_specs=(qspec, pl.BlockSpec((tr, LANE), lambda e, g, j: (e * nt + j, g))),
        compiler_params=_params(3, VMEM_LIMIT), name=name, args=(q, krep, vrep))


def _hosting_call(body, comm, *, out_shape, grid, in_specs, out_specs, compiler_params, name, args, scratch_shapes=()):
    if comm is None:
        return pl.pallas_call(body, out_shape=out_shape, grid=grid, in_specs=in_specs, out_specs=out_specs,
                              scratch_shapes=list(scratch_shapes), compiler_params=compiler_params,
                              name=name)(*args), None
    xs, scatter = comm
    n, n_in, n_out, n_scr = len(xs), len(args), len(out_shape), len(scratch_shapes)
    last = [g - 1 for g in grid]

    def hosted(*refs):
        ins, cins = refs[:n_in], refs[n_in:n_in + n]
        outs, couts = refs[n_in + n:n_in + n + n_out], refs[n_in + n + n_out:n_in + 2 * n + n_out]
        scratch = refs[n_in + 2 * n + n_out:n_in + 2 * n + n_out + n_scr]
        sems = refs[n_in + 2 * n + n_out + n_scr:]
        ids = [pl.program_id(a) for a in range(len(grid))]
        is_first = functools.reduce(jnp.logical_and, [i == 0 for i in ids])
        is_last = functools.reduce(jnp.logical_and, [i == l for i, l in zip(ids, last)])

        @pl.when(is_first)
        def _():
            _exchange_start(cins, couts, sems, scatter)

        body(*ins, *outs, *scratch)

        @pl.when(is_last)
        def _():
            _exchange_wait(cins, couts, sems, scatter)

    anyspec = pl.BlockSpec(memory_space=pl.ANY)
    res = pl.pallas_call(
        hosted, out_shape=tuple(out_shape) + tuple(_exchange_out_shapes(xs, scatter)), grid=grid,
        in_specs=list(in_specs) + [anyspec] * n, out_specs=tuple(out_specs) + (anyspec,) * n,
        scratch_shapes=list(scratch_shapes) + _exchange_scratch(n), compiler_params=compiler_params,
        name=name)(*args, *xs)
    return tuple(res[:n_out]), list(res[n_out:])


def _attn_bwd(cfg, name, q, krep, vrep, o, lse, do, comm=None):
    r_ = q.shape[0]
    tr, t, lc, nt, nc = cfg.TR, cfg.T, cfg.LC, cfg.NT, cfg.NC

    def body(q_ref, k_ref, v_ref, o_ref, l_ref, do_ref, dq_ref, dk_ref, dv_ref):
        j = pl.program_id(2)
        masks = _head_masks()
        lane = lax.broadcasted_iota(jnp.int32, (1, LANE), 1)

        @pl.when(j == 0)
        def _():
            dk_ref[...] = jnp.zeros_like(dk_ref)
            dv_ref[...] = jnp.zeros_like(dv_ref)

        def run(nk):
            qv = q_ref[...]
            kv = k_ref[0:nk, :]
            vv = v_ref[0:nk, :]
            ov = o_ref[...]
            dov = do_ref[...].astype(f32)
            lv = l_ref[...]
            dq = jnp.zeros((tr, GW), f32)
            dk = jnp.zeros((nk, GW), f32)
            dv = jnp.zeros((nk, GW), f32)
            for h0 in range(0, Q_GROUP, HEADS_PER_PASS_BWD):
                hs = range(h0, h0 + HEADS_PER_PASS_BWD)
                qs = jnp.concatenate([jnp.where(masks[h], qv, jnp.zeros_like(qv)) for h in hs], axis=0)
                dos = jnp.concatenate([jnp.where(masks[h], dov, 0.0) for h in hs], axis=0)
                dosb = dos.astype(bf16)
                delta = jnp.sum(dos * jnp.concatenate([ov] * len(hs), axis=0), axis=1, keepdims=True)
                lses = jnp.concatenate([jnp.sum(jnp.where(lane == h, lv, 0.0), axis=1, keepdims=True) for h in hs],
                                       axis=0)
                s = lax.dot_general(qs, kv, NT_, preferred_element_type=f32) * ATTN_SCALE
                pr = jnp.exp(s - lses)
                dpr = lax.dot_general(dosb, vv, NT_, preferred_element_type=f32)
                ds = (pr * (dpr - delta) * ATTN_SCALE).astype(bf16)
                dqs = jnp.dot(ds, kv, preferred_element_type=f32)
                for i, h in enumerate(hs):
                    dq = dq + jnp.where(masks[h], dqs[i * tr:(i + 1) * tr], 0.0)
                dk = dk + lax.dot_general(ds, qs, TN, preferred_element_type=f32)
                dv = dv + lax.dot_general(pr.astype(bf16), dosb, TN, preferred_element_type=f32)
            dq_ref[...] = dq.astype(dq_ref.dtype)
            dk_ref[0:nk, :] += dk
            dv_ref[0:nk, :] += dv

        @pl.when(j < nc)
        def _():
            run(lc)

        @pl.when(j >= nc)
        def _():
            run(t)

    qspec = pl.BlockSpec((tr, GW), lambda e, g, j: (e * nt + j, g))
    kspec = pl.BlockSpec((t, GW), lambda e, g, j: (e, g))
    lspec = pl.BlockSpec((tr, LANE), lambda e, g, j: (e * nt + j, g))
    return _hosting_call(
        body, comm, out_shape=(jax.ShapeDtypeStruct((r_, AW), GRAD),) + (jax.ShapeDtypeStruct((r_, AW), f32),) * 2,
        grid=(cfg.B, N_KV_HEADS, nt), in_specs=[qspec, kspec, kspec, qspec, lspec, qspec],
        out_specs=(qspec, kspec, kspec),
        compiler_params=_params(3, VMEM_LIMIT), name=name, args=(q, krep, vrep, o, lse, do))


def _segments(cfg):
    pad = CONV_PAD_ROWS
    return ((pad, 0, cfg.LC), (cfg.LC + 3 * pad, cfg.LC, cfg.S)), cfg.T + 4 * pad


def _fill_padded(buf, cfg, fn):
    buf[...] = jnp.zeros_like(buf)
    for off, lo, n in _segments(cfg)[0]:
        buf[off:off + n, :] = fn(lo, lo + n)


def _taps(buf, w_ref, base, taps, sign):
    acc = None
    for k in range(taps):
        term = w_ref[k:k + 1, :] * buf[pl.ds(base + sign * (k - taps // 2), CONV_CHUNK), :]
        acc = term if acc is None else acc + term
    return acc


def _fold_tiles(v):
    acc = v[0:SUBLANE]
    for i in range(1, v.shape[0] // SUBLANE):
        acc = acc + v[SUBLANE * i:SUBLANE * (i + 1)]
    return acc


def _tap_grads(buf, dy, acc_ref, base, taps):
    for k in range(taps):
        acc_ref[SUBLANE * k:SUBLANE * (k + 1), :] += _fold_tiles(
            buf[pl.ds(base + k - taps // 2, CONV_CHUNK), :] * dy)


def _conv_specs(cfg):
    t, cb, cw = cfg.T, cfg.CB, cfg.CW
    offs = (cfg.OFF_CONF, cfg.OFF_CONF + cw, cfg.OFF_SC, cfg.OFF_SC + cw, cfg.OFF_SC + 2 * cw)
    return [pl.BlockSpec((t, cb), lambda c, e, off=off: (e, off // cb + c)) for off in offs]


def _conv_fwd(cfg, name, p, w31, w3):
    r_ = p.shape[0]
    t, lc, cb, cw = cfg.T, cfg.LC, cfg.CB, cfg.CW
    k31, k3 = cfg.K31, cfg.K3

    segs, buf_rows = _segments(cfg)
    ch = CONV_CHUNK

    def body(a_ref, g_ref, bg_ref, cg_ref, xs_ref, w31_ref, w3_ref, hc_ref, z_ref, hbuf, ubuf):
        _fill_padded(hbuf, cfg, lambda lo, hi: a_ref[lo:hi, :].astype(f32) * jax.nn.sigmoid(g_ref[lo:hi, :].astype(f32)))
        _fill_padded(ubuf, cfg, lambda lo, hi: cg_ref[lo:hi, :].astype(f32) * xs_ref[lo:hi, :].astype(f32))
        bias = w31_ref[k31:k31 + 1, :]
        for off, lo, n in segs:
            def chunk(c, carry, off=off, lo=lo):
                base = off + pl.multiple_of(c * ch, ch)
                rows = pl.ds(pl.multiple_of(lo + c * ch, ch), ch)
                hc_ref[rows, :] = _taps(hbuf, w31_ref, base, k31, 1) + bias
                z_ref[rows, :] = (bg_ref[rows, :].astype(f32) * _taps(ubuf, w3_ref, base, k3, 1)).astype(bf16)
                return carry

            lax.fori_loop(0, n // ch, chunk, 0)

    ospec = pl.BlockSpec((t, cb), lambda c, e: (e, c))
    return pl.pallas_call(
        body, out_shape=(jax.ShapeDtypeStruct((r_, cw), f32), jax.ShapeDtypeStruct((r_, cw), bf16)),
        grid=(cw // cb, cfg.B),
        in_specs=_conv_specs(cfg) + [pl.BlockSpec((w31.shape[0], cb), lambda c, e: (0, c)),
                                     pl.BlockSpec((w3.shape[0], cb), lambda c, e: (0, c))],
        out_specs=(ospec, ospec), scratch_shapes=[pltpu.VMEM((buf_rows, cb), f32)] * 2,
        compiler_params=_params(2, VMEM_LIMIT), name=name)(p, p, p, p, p, w31, w3)


def _conv_bwd(cfg, name, p, dhc, dz, w31, w3, comm=None):
    r_ = p.shape[0]
    t, lc, cb, cw = cfg.T, cfg.LC, cfg.CB, cfg.CW
    k31, k3 = cfg.K31, cfg.K3

    segs, buf_rows = _segments(cfg)
    ch = CONV_CHUNK

    def body(a_ref, g_ref, bg_ref, cg_ref, xs_ref, dhc_ref, dz_ref, w31_ref, w3_ref,
             da_ref, dg_ref, dbg_ref, dcg_ref, dxs_ref, dw31_ref, dw3_ref, hbuf, dbuf, ubuf, ebuf, acc31, acc3):
        e = pl.program_id(1)

        @pl.when(e == 0)
        def _():
            dw31_ref[...] = jnp.zeros_like(dw31_ref)
            dw3_ref[...] = jnp.zeros_like(dw3_ref)

        _fill_padded(hbuf, cfg, lambda lo, hi: a_ref[lo:hi, :].astype(f32) * jax.nn.sigmoid(g_ref[lo:hi, :].astype(f32)))
        _fill_padded(dbuf, cfg, lambda lo, hi: dhc_ref[lo:hi, :])
        _fill_padded(ubuf, cfg, lambda lo, hi: cg_ref[lo:hi, :].astype(f32) * xs_ref[lo:hi, :].astype(f32))
        _fill_padded(ebuf, cfg, lambda lo, hi: dz_ref[lo:hi, :].astype(f32) * bg_ref[lo:hi, :].astype(f32))
        acc31[...] = jnp.zeros_like(acc31)
        acc3[...] = jnp.zeros_like(acc3)
        for off, lo, n in segs:
            def chunk(c, carry, off=off, lo=lo):
                base = off + pl.multiple_of(c * ch, ch)
                here = pl.ds(base, ch)
                rows = pl.ds(pl.multiple_of(lo + c * ch, ch), ch)
                dy = dbuf[here, :]
                dhh = _taps(dbuf, w31_ref, base, k31, -1)
                _tap_grads(hbuf, dy, acc31, base, k31)
                acc31[SUBLANE * k31:SUBLANE * (k31 + 1), :] += _fold_tiles(dy)
                sg = jax.nn.sigmoid(g_ref[rows, :].astype(f32))
                da_ref[rows, :] = (dhh * sg).astype(bf16)
                dg_ref[rows, :] = (dhh * hbuf[here, :] * (1.0 - sg)).astype(bf16)

                duc = ebuf[here, :]
                dbg_ref[rows, :] = (dz_ref[rows, :].astype(f32) * _taps(ubuf, w3_ref, base, k3, 1)).astype(bf16)
                du = _taps(ebuf, w3_ref, base, k3, -1)
                _tap_grads(ubuf, duc, acc3, base, k3)
                dcg_ref[rows, :] = (du * xs_ref[rows, :].astype(f32)).astype(bf16)
                dxs_ref[rows, :] = (du * cg_ref[rows, :].astype(f32)).astype(bf16)
                return carry

            lax.fori_loop(0, n // ch, chunk, 0)
        for k in range(k31 + 1):
            dw31_ref[k:k + 1, :] += jnp.sum(acc31[SUBLANE * k:SUBLANE * (k + 1), :], axis=0, keepdims=True)
        for k in range(k3):
            dw3_ref[k:k + 1, :] += jnp.sum(acc3[SUBLANE * k:SUBLANE * (k + 1), :], axis=0, keepdims=True)

    ospec = pl.BlockSpec((t, cb), lambda c, e: (e, c))
    w31_spec = pl.BlockSpec((w31.shape[0], cb), lambda c, e: (0, c))
    w3_spec = pl.BlockSpec((w3.shape[0], cb), lambda c, e: (0, c))
    return _hosting_call(
        body, comm,
        out_shape=(jax.ShapeDtypeStruct((r_, cw), bf16),) * 5
        + (jax.ShapeDtypeStruct(w31.shape, f32), jax.ShapeDtypeStruct(w3.shape, f32)),
        grid=(cw // cb, cfg.B),
        in_specs=_conv_specs(cfg) + [ospec, ospec, w31_spec, w3_spec],
        out_specs=(ospec,) * 5 + (w31_spec, w3_spec),
        scratch_shapes=[pltpu.VMEM((buf_rows, cb), f32)] * 4
        + [pltpu.VMEM((SUBLANE * w31.shape[0], cb), f32), pltpu.VMEM((SUBLANE * w3.shape[0], cb), f32)],
        compiler_params=_params(2, VMEM_LIMIT), name=name, args=(p, p, p, p, p, dhc, dz, w31, w3))


def _ln_silu_fwd(cfg, name, hc, lnp):
    r_, cw = hc.shape

    def body(x_ref, p_ref, o_ref):
        x = x_ref[...]
        mu = jnp.mean(x, axis=-1, keepdims=True)
        xc = x - mu
        rs = lax.rsqrt(jnp.mean(xc * xc, axis=-1, keepdims=True) + EPS)
        hn = (xc * rs) * p_ref[0:1, :] + p_ref[1:2, :]
        o_ref[...] = (hn * jax.nn.sigmoid(hn)).astype(bf16)

    row = _row_spec(cfg, cw)
    return pl.pallas_call(body, out_shape=jax.ShapeDtypeStruct((r_, cw), bf16), grid=(r_ // cfg.TR,),
                          in_specs=[row, _const_spec(lnp)], out_specs=row,
                          compiler_params=_params(1), name=name)(hc, lnp)


def _ln_silu_bwd(cfg, name, hc, dhs, lnp):
    r_, cw = hc.shape

    def body(x_ref, d_ref, p_ref, dx_ref, dp_ref):
        i = pl.program_id(0)
        x = x_ref[...]
        mu = jnp.mean(x, axis=-1, keepdims=True)
        xc = x - mu
        rs = lax.rsqrt(jnp.mean(xc * xc, axis=-1, keepdims=True) + EPS)
        xh = xc * rs
        gain = p_ref[0:1, :]
        hn = xh * gain + p_ref[1:2, :]
        sg = jax.nn.sigmoid(hn)
        dhn = d_ref[...].astype(f32) * (sg * (1.0 + hn * (1.0 - sg)))
        dxh = dhn * gain
        dx_ref[...] = rs * (dxh - jnp.mean(dxh, axis=-1, keepdims=True)
                            - xh * jnp.mean(dxh * xh, axis=-1, keepdims=True))

        @pl.when(i == 0)
        def _():
            dp_ref[...] = jnp.zeros_like(dp_ref)

        dp_ref[0:1, :] += jnp.sum(dhn * xh, axis=0, keepdims=True)
        dp_ref[1:2, :] += jnp.sum(dhn, axis=0, keepdims=True)

    row = _row_spec(cfg, cw)
    return pl.pallas_call(
        body, out_shape=(jax.ShapeDtypeStruct((r_, cw), f32), jax.ShapeDtypeStruct((SUBLANE, cw), f32)),
        grid=(r_ // cfg.TR,), in_specs=[row, row, _const_spec(lnp)],
        out_specs=(row, pl.BlockSpec((SUBLANE, cw), lambda i: (0, 0))),
        compiler_params=_params(1), name=name)(hc, dhs, lnp)


def _merge_fwd(cfg, name, p, attn, hs, z, wao_t, wco_t, wso_t):
    r_ = attn.shape[0]
    d = wao_t.shape[0]
    gb = cfg.GB
    base = cfg.OFF_GATE // gb
    per = d // gb
    tr = _divisor(r_, GATE_ROWS, cfg.TR)

    def body(g0_ref, g1_ref, g2_ref, a_ref, h_ref, z_ref, wa_ref, wc_ref, ws_ref, m_ref, ya_ref, yc_ref, ys_ref):
        m = None
        for g_ref, x_ref, w_ref, y_ref in ((g0_ref, a_ref, wa_ref, ya_ref), (g1_ref, h_ref, wc_ref, yc_ref),
                                           (g2_ref, z_ref, ws_ref, ys_ref)):
            y = lax.dot_general(x_ref[...].astype(bf16), w_ref[...], NT_, preferred_element_type=f32)
            yb = y.astype(ACT)
            y_ref[...] = yb
            term = jax.nn.sigmoid(g_ref[...].astype(f32)) * yb.astype(f32)
            m = term if m is None else m + term
        m_ref[...] = m.astype(bf16)

    yspec = pl.BlockSpec((tr, gb), lambda i, c: (i, c))
    gspecs = [pl.BlockSpec((tr, gb), lambda i, c, k=k: (i, base + k * per + c)) for k in range(3)]
    xspecs = [pl.BlockSpec((tr, x.shape[1]), lambda i, c: (i, 0)) for x in (attn, hs, z)]
    wspecs = [pl.BlockSpec((gb, w.shape[1]), lambda i, c: (c, 0)) for w in (wao_t, wco_t, wso_t)]
    return pl.pallas_call(
        body, out_shape=(jax.ShapeDtypeStruct((r_, d), bf16),) + (jax.ShapeDtypeStruct((r_, d), ACT),) * 3,
        grid=(r_ // tr, per), in_specs=gspecs + xspecs + wspecs, out_specs=(yspec,) * 4,
        compiler_params=_params(2, VMEM_LIMIT), name=name)(p, p, p, attn, hs, z, wao_t, wco_t, wso_t)


def _gate_bwd(cfg, name, p, ya, yc, ys, dm):
    r_, d = ya.shape
    gb = cfg.GB
    base = cfg.OFF_GATE // gb
    per = d // gb

    tr = _divisor(r_, GATE_ROWS, cfg.TR)

    def body(g0_ref, g1_ref, g2_ref, ya_ref, yc_ref, ys_ref, dm_ref, da_ref, dc_ref, ds_ref, d0_ref, d1_ref, d2_ref):
        dmv = dm_ref[...].astype(f32)
        for g_ref, y_ref, dy_ref, dg_ref in ((g0_ref, ya_ref, da_ref, d0_ref), (g1_ref, yc_ref, dc_ref, d1_ref),
                                             (g2_ref, ys_ref, ds_ref, d2_ref)):
            sg = jax.nn.sigmoid(g_ref[...].astype(f32))
            dy_ref[...] = (dmv * sg).astype(bf16)
            dg_ref[...] = (dmv * y_ref[...].astype(f32) * sg * (1.0 - sg)).astype(bf16)

    yspec = pl.BlockSpec((tr, gb), lambda i, c: (i, c))
    gspecs = [pl.BlockSpec((tr, gb), lambda i, c, k=k: (i, base + k * per + c)) for k in range(3)]
    return pl.pallas_call(body, out_shape=(jax.ShapeDtypeStruct((r_, d), bf16),) * 6, grid=(r_ // tr, per),
                          in_specs=gspecs + [yspec] * 4, out_specs=(yspec,) * 6,
                          compiler_params=_params(2), name=name)(p, p, p, ya, yc, ys, dm)


def _swiglu_fwd(cfg, name, ff):
    r_, f2 = ff.shape
    fh = f2 // 2

    def body(a_ref, b_ref, o_ref):
        a = a_ref[...].astype(f32)
        o_ref[...] = (a * jax.nn.sigmoid(a) * b_ref[...].astype(f32)).astype(bf16)

    return pl.pallas_call(body, out_shape=jax.ShapeDtypeStruct((r_, fh), bf16), grid=(r_ // cfg.TR,),
                          in_specs=[_row_spec(cfg, fh, 0), _row_spec(cfg, fh, 1)], out_specs=_row_spec(cfg, fh),
                          compiler_params=_params(1, VMEM_LIMIT), name=name)(ff, ff)


def _swiglu_bwd(cfg, name, ff, ds):
    r_, f2 = ff.shape
    fh = f2 // 2

    def body(a_ref, b_ref, d_ref, o_ref):
        a = a_ref[...].astype(f32)
        sg = jax.nn.sigmoid(a)
        dsv = d_ref[...].astype(f32)
        o_ref[:, 0:fh] = (dsv * b_ref[...].astype(f32) * (sg * (1.0 + a * (1.0 - sg)))).astype(bf16)
        o_ref[:, fh:f2] = (dsv * a * sg).astype(bf16)

    return pl.pallas_call(body, out_shape=jax.ShapeDtypeStruct((r_, f2), bf16), grid=(r_ // cfg.TR,),
                          in_specs=[_row_spec(cfg, fh, 0), _row_spec(cfg, fh, 1), _row_spec(cfg, fh)],
                          out_specs=_row_spec(cfg, f2), compiler_params=_params(1, VMEM_LIMIT), name=name)(ff, ff, ds)


def _ada_fwd(name, call, w_ada, b_loc):
    nl, d, na = w_ada.shape
    nbp = call.shape[0]

    def body(c_ref, w_ref, b_ref, o_ref):
        cv = c_ref[...]
        a = (cv * jax.nn.sigmoid(cv)).astype(bf16)
        o_ref[0] = jnp.dot(a, w_ref[0].astype(bf16), preferred_element_type=f32) + b_ref[0]

    return pl.pallas_call(
        body, out_shape=jax.ShapeDtypeStruct((nl, nbp, na), f32), grid=(nl,),
        in_specs=[_const_spec(call), pl.BlockSpec((1, d, na), lambda l: (l, 0, 0)),
                  pl.BlockSpec((1, 1, na), lambda l: (l, 0, 0))],
        out_specs=pl.BlockSpec((1, nbp, na), lambda l: (l, 0, 0)),
        compiler_params=_params(1, VMEM_LIMIT), name=name)(call, w_ada, b_loc)


def _ada_bwd(name, call, w_ada, dmod, dmodc, cctx_row):
    nl, d, na = w_ada.shape
    nbp = call.shape[0]

    def body(c_ref, w_ref, dm_ref, dc_ref, gw_ref, gb_ref, gc_ref):
        l = pl.program_id(0)
        cv = c_ref[...]
        sg = jax.nn.sigmoid(cv)
        a = (cv * sg).astype(bf16)
        dctx = jnp.sum(dc_ref[0], axis=0, keepdims=True)
        rows = lax.broadcasted_iota(jnp.int32, (nbp, 1), 0)
        dm = jnp.where(rows == cctx_row, dctx, dm_ref[0])
        gw_ref[0] = lax.dot_general(a, dm.astype(bf16), TN, preferred_element_type=f32)
        gb_ref[0] = jnp.zeros((SUBLANE, na), f32)
        gb_ref[0, 0:1, :] = jnp.sum(dm, axis=0, keepdims=True)
        dc8 = jnp.broadcast_to(dctx, (SUBLANE, na)).astype(bf16)
        part = lax.dot_general(dc8, w_ref[0].astype(bf16), NT_, preferred_element_type=f32)
        cc = c_ref[cctx_row:cctx_row + 1, :]
        sc = jax.nn.sigmoid(cc)
        part = part * (sc * (1.0 + cc * (1.0 - sc)))

        @pl.when(l == 0)
        def _():
            gc_ref[...] = jnp.zeros_like(gc_ref)

        gc_ref[...] += part

    return pl.pallas_call(
        body,
        out_shape=(jax.ShapeDtypeStruct((nl, d, na), f32), jax.ShapeDtypeStruct((nl, SUBLANE, na), f32),
                   jax.ShapeDtypeStruct((SUBLANE, d), f32)),
        grid=(nl,),
        in_specs=[_const_spec(call), pl.BlockSpec((1, d, na), lambda l: (l, 0, 0)),
                  pl.BlockSpec((1, nbp, na), lambda l: (l, 0, 0)),
                  pl.BlockSpec((1, NDEV, na), lambda l: (l, 0, 0))],
        out_specs=(pl.BlockSpec((1, d, na), lambda l: (l, 0, 0)), pl.BlockSpec((1, SUBLANE, na), lambda l: (l, 0, 0)),
                   pl.BlockSpec((SUBLANE, d), lambda l: (0, 0))),
        compiler_params=_params(1, VMEM_LIMIT), name=name)(call, w_ada, dmod, dmodc)


def _adamw(name, w, g, m, v):
    rows, cols = w.shape
    tr = _divisor(rows, max(SUBLANE, (1 << 19) // cols), SUBLANE)
    c1 = 1.0 / (1.0 - ADAM_B1 ** ADAM_STEP)
    c2 = 1.0 / (1.0 - ADAM_B2 ** ADAM_STEP)

    def body(w_ref, g_ref, m_ref, v_ref, d_ref, nm_ref, nv_ref):
        gv = g_ref[...]
        nm = ADAM_B1 * m_ref[...] + (1.0 - ADAM_B1) * gv
        nv = ADAM_B2 * v_ref[...] + (1.0 - ADAM_B2) * (gv * gv)
        nm_ref[...] = nm
        nv_ref[...] = nv
        d_ref[...] = -ADAM_LR * ((nm * c1) / (jnp.sqrt(nv * c2) + ADAM_EPS) + ADAM_WD * w_ref[...])

    spec = pl.BlockSpec((tr, cols), lambda i: (i, 0))
    return pl.pallas_call(body, out_shape=(jax.ShapeDtypeStruct((rows, cols), f32),) * 3, grid=(rows // tr,),
                          in_specs=[spec] * 4, out_specs=(spec,) * 3,
                          compiler_params=_params(1, VMEM_LIMIT), name=name)(w, g, m, v)


def _sum_slots(name, x):
    nd, rows, cols = x.shape
    tr = _divisor(rows, max(SUBLANE, (1 << 18) // cols), SUBLANE)

    def body(x_ref, o_ref):
        acc = x_ref[0].astype(f32)
        for s in range(1, nd):
            acc = acc + x_ref[s].astype(f32)
        o_ref[...] = acc

    return pl.pallas_call(body, out_shape=jax.ShapeDtypeStruct((rows, cols), f32), grid=(rows // tr,),
                          in_specs=[pl.BlockSpec((nd, tr, cols), lambda i: (0, i, 0))],
                          out_specs=pl.BlockSpec((tr, cols), lambda i: (i, 0)),
                          compiler_params=_params(1, VMEM_LIMIT), name=name)(x)


def _peer(k):
    x, y, c = (lax.axis_index(a) for a in AXES)
    px = 1 - x if k & 4 else x
    py = 1 - y if k & 2 else y
    pc = 1 - c if k & 1 else c
    return (px, py, pc), 4 * px + 2 * py + pc


def _exchange_out_shapes(xs, scatter):
    return [jax.ShapeDtypeStruct(xa.shape if scatter else (NDEV,) + xa.shape, xa.dtype) for xa in xs]


def _exchange_scratch(n):
    nrel = NDEV - 1
    return [pltpu.SemaphoreType.DMA((n * nrel,)), pltpu.SemaphoreType.DMA((n * nrel,)), pltpu.SemaphoreType.DMA((n,))]


def _exchange_copies(ins, outs, sems, scatter, receiving):
    send_sems, recv_sems, _ = sems
    nrel = NDEV - 1
    x, y, c = (lax.axis_index(a) for a in AXES)
    me = 4 * x + 2 * y + c
    copies = []
    for a in range(len(ins)):
        for k in range(1, NDEV):
            peer, pidx = _peer(k)
            src = ins[a].at[pidx] if scatter else ins[a]
            copies.append(pltpu.make_async_remote_copy(
                src_ref=src, dst_ref=outs[a].at[pidx if receiving else me], send_sem=send_sems.at[a * nrel + k - 1],
                recv_sem=recv_sems.at[a * nrel + k - 1], device_id=peer, device_id_type=pl.DeviceIdType.MESH))
    return copies


def _exchange_local(ins, outs, sems, scatter):
    x, y, c = (lax.axis_index(a) for a in AXES)
    me = 4 * x + 2 * y + c
    return [pltpu.make_async_copy(ins[a].at[me] if scatter else ins[a], outs[a].at[me], sems[2].at[a])
            for a in range(len(ins))]


CHIP_RELATIONS = (2, 4, 6)
SIBLING = 1


def _gather_copy(ins, outs, sems, a, slot, src, block, to):
    nrel = NDEV - 1
    return pltpu.make_async_remote_copy(
        src_ref=src, dst_ref=outs[a].at[block], send_sem=sems[0].at[a * nrel + slot],
        recv_sem=sems[1].at[a * nrel + slot], device_id=to, device_id_type=pl.DeviceIdType.MESH)


def _exchange_start(ins, outs, sems, scatter):
    for cp in _exchange_local(ins, outs, sems, scatter):
        cp.start()
    if scatter:
        for cp in _exchange_copies(ins, outs, sems, scatter, False):
            cp.start()
        return
    x, y, c = (lax.axis_index(a) for a in AXES)
    me = 4 * x + 2 * y + c
    for a in range(len(ins)):
        for slot, k in enumerate((SIBLING,) + CHIP_RELATIONS):
            _gather_copy(ins, outs, sems, a, slot, ins[a], me, _peer(k)[0]).start()


def _exchange_wait(ins, outs, sems, scatter):
    if scatter:
        for cp in _exchange_copies(ins, outs, sems, scatter, True):
            cp.wait_recv()
            cp.wait_send()
    else:
        x, y, c = (lax.axis_index(a) for a in AXES)
        me = 4 * x + 2 * y + c
        sibling, sibling_idx = _peer(SIBLING)
        n = len(ins)
        passed = []
        for a in range(n):
            for j, k in enumerate(CHIP_RELATIONS):
                peer, pidx = _peer(k)
                _gather_copy(ins, outs, sems, a, 1 + j, ins[a], pidx, peer).wait_recv()
                fwd = _gather_copy(ins, outs, sems, a, 4 + j, outs[a].at[pidx], pidx, sibling)
                fwd.start()
                passed.append(fwd)
        for a in range(n):
            _gather_copy(ins, outs, sems, a, 0, ins[a], sibling_idx, sibling).wait_recv()
            for j, k in enumerate(CHIP_RELATIONS):
                _gather_copy(ins, outs, sems, a, 4 + j, ins[a], _peer(k | SIBLING)[1], sibling).wait_recv()
        for a in range(n):
            for slot, k in enumerate((SIBLING,) + CHIP_RELATIONS):
                _gather_copy(ins, outs, sems, a, slot, ins[a], me, _peer(k)[0]).wait_send()
        for fwd in passed:
            fwd.wait_send()
    for cp in _exchange_local(ins, outs, sems, scatter):
        cp.wait()


def _exchange(name, xs, scatter):
    n = len(xs)

    def body(*refs):
        ins, outs, sems = refs[:n], refs[n:2 * n], refs[2 * n:]
        _exchange_start(ins, outs, sems, scatter)
        _exchange_wait(ins, outs, sems, scatter)

    anyspec = pl.BlockSpec(memory_space=pl.ANY)
    outs = pl.pallas_call(
        body, out_shape=tuple(_exchange_out_shapes(xs, scatter)), in_specs=[anyspec] * n, out_specs=(anyspec,) * n,
        scratch_shapes=_exchange_scratch(n), name=name)(*xs)
    return list(outs)


def _all_gather(name, xs):
    return _exchange(name, xs, False)


def _all_to_all(name, xs):
    return _exchange(name, xs, True)


def _rope_table(cfg):
    s, lc = cfg.S, cfg.LC
    rows = s // GRID_W
    r_ids, c_ids = jnp.meshgrid(jnp.arange(rows), jnp.arange(GRID_W), indexing="ij")
    r_ids = r_ids.reshape(-1).astype(f32)
    c_ids = c_ids.reshape(-1).astype(f32)
    freqs = ROPE_THETA ** (-jnp.arange(0, AXIS_DIM, 2, dtype=f32) / AXIS_DIM)
    ang_r = r_ids[:, None] * freqs
    ang_c = c_ids[:, None] * freqs
    cos = jnp.concatenate([jnp.cos(ang_r), jnp.cos(ang_r), jnp.cos(ang_c), jnp.cos(ang_c)], axis=1)
    sin = jnp.concatenate([-jnp.sin(ang_r), jnp.sin(ang_r), -jnp.sin(ang_c), jnp.sin(ang_c)], axis=1)
    cos = jnp.concatenate([jnp.ones((lc, HEAD_DIM), f32), cos], axis=0)
    sin = jnp.concatenate([jnp.zeros((lc, HEAD_DIM), f32), sin], axis=0)
    reps = QKW // HEAD_DIM
    return jnp.stack([jnp.tile(cos, (1, reps)), jnp.tile(sin, (1, reps))])


def _block_diag_ones():
    idx = np.arange(QKW) // HEAD_DIM
    return jnp.asarray((idx[:, None] == idx[None, :]).astype(np.float32))


def _replicate_matrix():
    src = np.arange(KVW)
    dst = np.arange(AW)
    m = (src[:, None] // HEAD_DIM == dst[None, :] // GW) & (src[:, None] % HEAD_DIM == dst[None, :] % HEAD_DIM)
    return m.astype(np.float32)


def _pack_flat(parts):
    flat = jnp.concatenate([p.reshape(-1) for p in parts])
    n = flat.shape[0]
    unit = SUBLANE * LANE
    total = -(-n // unit) * unit
    flat = jnp.pad(flat, (0, total - n))
    return flat.reshape(total // LANE, LANE)


def _unpack_flat(flat, shapes):
    flat = flat.reshape(flat.shape[:-2] + (-1,))
    out, off = [], 0
    for shp in shapes:
        size = int(np.prod(shp))
        out.append(flat[..., off:off + size].reshape(flat.shape[:-1] + tuple(shp)))
        off += size
    return out


def kernel(x, c, ctx, c_ctx, w_ada, b_ada, w_in, q_norm, k_norm, w_attn_o, conf_dw_w, conf_dw_b, conf_ln_g, conf_ln_b, w_conf_out, sc_dw_w, w_sc_out, w_mix_out, w_ffn_in, w_ffn_out, loss_target, m_c_ctx, m_w_ada, m_b_ada, m_w_in, m_q_norm, m_k_norm, m_w_attn_o, m_conf_dw_w, m_conf_dw_b, m_conf_ln_g, m_conf_ln_b, m_w_conf_out, m_sc_dw_w, m_w_sc_out, m_w_mix_out, m_w_ffn_in, m_w_ffn_out, v_c_ctx, v_w_ada, v_b_ada, v_w_in, v_q_norm, v_k_norm, v_w_attn_o, v_conf_dw_w, v_conf_dw_b, v_conf_ln_g, v_conf_ln_b, v_w_conf_out, v_sc_dw_w, v_w_sc_out, v_w_mix_out, v_w_ffn_in, v_w_ffn_out):
    cfg = _Cfg()
    cfg.B, cfg.S, cfg.D = x.shape
    cfg.LC = ctx.shape[1]
    cfg.T = cfg.LC + cfg.S
    cfg.TR = min(256, cfg.LC)
    assert cfg.LC % cfg.TR == 0 and cfg.S % cfg.TR == 0 and cfg.S % GRID_W == 0
    cfg.NT, cfg.NC = cfg.T // cfg.TR, cfg.LC // cfg.TR
    cfg.R = cfg.B * cfg.T
    nl = w_in.shape[0]
    b, d = cfg.B, cfg.D
    cfg.CW = conf_dw_b.shape[1]
    cfg.K31, cfg.K3 = conf_dw_w.shape[1], sc_dw_w.shape[1]
    assert w_sc_out.shape[1] == cfg.CW and cfg.CW % LANE == 0
    assert cfg.K3 // 2 <= cfg.K31 // 2 <= CONV_PAD_ROWS and cfg.LC % CONV_CHUNK == 0 and cfg.S % CONV_CHUNK == 0
    cfg.CB = LANE
    cfg.OFF_CONF = QKVW
    cfg.OFF_SC = cfg.OFF_CONF + 2 * cfg.CW
    cfg.OFF_GATE = cfg.OFF_SC + 3 * cfg.CW
    n_in = w_in.shape[2] * NDEV
    assert n_in == cfg.OFF_GATE + 3 * d and w_attn_o.shape[1] == AW
    cfg.GB = math.gcd(cfg.OFF_GATE, d)
    assert cfg.GB % LANE == 0
    fh = w_ffn_out.shape[1] * NDEV
    na = w_ada.shape[2]
    cw8 = cfg.CW // NDEV

    xi, yi, ci = (lax.axis_index(a) for a in AXES)
    me = 4 * xi + 2 * yi + ci

    small_shapes = [c.shape, conf_dw_w.shape, sc_dw_w.shape]
    (g0,) = _all_gather("gather_small", [_pack_flat([c, conf_dw_w, sc_dw_w])])
    c_all, cw_all, sw_all = _unpack_flat(g0, small_shapes)
    nb = NDEV * b
    nbp = -(-(nb + 1) // SUBLANE) * SUBLANE
    call = jnp.concatenate([c_all.reshape(nb, d), c_ctx[None, :], jnp.zeros((nbp - nb - 1, d), f32)], axis=0)
    w31_full = jnp.moveaxis(cw_all, 0, 2).reshape(nl, cfg.K31, cfg.CW)
    w3_full = jnp.moveaxis(sw_all, 0, 2).reshape(nl, cfg.K3, cfg.CW)
    k31p = -(-(cfg.K31 + 1) // SUBLANE) * SUBLANE
    w31b = jnp.concatenate([w31_full, conf_dw_b[:, None, :], jnp.zeros((nl, k31p - cfg.K31 - 1, cfg.CW), f32)], axis=1)
    w3p = jnp.concatenate([w3_full, jnp.zeros((nl, SUBLANE - cfg.K3, cfg.CW), f32)], axis=1)

    b_loc = lax.dynamic_slice(b_ada, (0, me * na), (nl, na))[:, None, :]
    mod_loc = _ada_fwd("ada_fwd", call, w_ada, b_loc)
    (mod_g,) = _all_gather("gather_mod", [mod_loc.reshape(nl * nbp, na)])
    mod_full = jnp.transpose(mod_g.reshape(NDEV, nl, nbp, na), (1, 2, 0, 3)).reshape(nl, nbp, N_MOD, d)
    mod_lat = lax.dynamic_slice_in_dim(mod_full, me * b, b, axis=1)
    mod_ctx = jnp.broadcast_to(mod_full[:, nb][:, None], (nl, b, N_MOD, d))
    mod = jnp.stack([mod_ctx, mod_lat], axis=2)
    mod = jnp.pad(mod, ((0, 0), (0, 0), (0, 0), (0, MOD_ROWS - N_MOD), (0, 0)))

    cs = _rope_table(cfg)
    bd = _block_diag_ones().astype(bf16)
    rep_np = _replicate_matrix()
    rep = jnp.asarray(rep_np, dtype=bf16)
    rept = jnp.asarray(rep_np.T, dtype=bf16)
    reps_q, reps_k = AW // HEAD_DIM, KVW // HEAD_DIM

    xin = jnp.concatenate([ctx, x], axis=1).reshape(cfg.R, d)
    target = loss_target.reshape(b * cfg.S, d)
    saved = []
    xcur, ocur = xin, None

    def weight_shards(l):
        return [jnp.transpose(w_in[l]).astype(bf16), jnp.transpose(w_attn_o[l]).astype(bf16),
                jnp.transpose(w_conf_out[l]).astype(bf16), jnp.transpose(w_sc_out[l]).astype(bf16),
                w_mix_out[l].astype(bf16), jnp.transpose(w_ffn_in[l]).astype(bf16), w_ffn_out[l].astype(bf16)]

    def whole(g):
        return g.reshape(NDEV * g.shape[1], g.shape[2])

    shards0 = weight_shards(0)
    (w_first,) = _all_gather("gather_w0", shards0[:1])
    w_rest = None
    for l in range(nl):
        win_t = whole(w_first)
        gain = jnp.concatenate([jnp.tile(q_norm[l], reps_q), jnp.tile(k_norm[l], reps_k)])[None, :]
        lnp = jnp.concatenate([conf_ln_g[l][None], conf_ln_b[l][None], jnp.zeros((SUBLANE - 2, cfg.CW), f32)], axis=0)
        nxt = weight_shards(l + 1) if l + 1 < nl else None

        if l == 0:
            x0, h = _norm_fwd(cfg, f"norm_a{l}", xcur, None, None, 0, mod[l], 0)
            p, w_rest = _mm(h, win_t, "nt", ACT, f"mm_in{l}", comm=(shards0[1:], False))
        else:
            x0, h = _norm_fwd(cfg, f"norm_a{l}", xcur, ocur, mod[l - 1], 5, mod[l], 0)
            p = _mm(h, win_t, "nt", ACT, f"mm_in{l}")
        wao_t, wco_t, wso_t, wmix, wfi_t, wfo = [whole(g) for g in w_rest]
        q, krep, vrep = _qkv_fwd(cfg, f"qkv{l}", p, cs, gain, bd, rep)
        (attn, lse), w_rest = _attn_fwd(cfg, f"attn{l}", q, krep, vrep,
                                        comm=None if nxt is None else (nxt[1:], False))
        hc, z = _conv_fwd(cfg, f"conv{l}", p, w31b[l], w3p[l])
        hs = _ln_silu_fwd(cfg, f"lnsilu{l}", hc, lnp)
        merged, ya, yc, ys = _merge_fwd(cfg, f"merge{l}", p, attn, hs, z, wao_t, wco_t, wso_t)
        mixed = _mm(merged, wmix, "nn", f32, f"mm_mix{l}")
        x1, h2 = _norm_fwd(cfg, f"norm_b{l}", x0, mixed, mod[l], 2, mod[l], 3)
        if nxt is None:
            ff = _mm(h2, wfi_t, "nt", ACT, f"mm_fi{l}")
        else:
            ff, (w_first,) = _mm(h2, wfi_t, "nt", ACT, f"mm_fi{l}", comm=(nxt[:1], False))
        sw = _swiglu_fwd(cfg, f"swiglu{l}", ff)
        o = _mm(sw, wfo, "nn", f32, f"mm_fo{l}")
        saved.append(dict(x0=x0, h=h, p=p, q=q, krep=krep, vrep=vrep, attn=attn, lse=lse, hc=hc, z=z, hs=hs,
                          ya=ya, yc=yc, ys=ys, merged=merged, mixed=mixed, x1=x1, h2=h2, ff=ff, sw=sw, o=o,
                          oprev=ocur, gain=gain, lnp=lnp,
                          w=(win_t, wao_t, wco_t, wso_t, wmix, wfi_t, wfo)))
        xcur, ocur = x1, o

    lsum, dres, do, dm_loss = _loss_head(cfg, "loss", xcur, ocur, mod[nl - 1], 5, target)
    loss = lax.psum((0.5 / d) * jnp.sum(lsum), AXES)

    dmod_rows = [[None] * N_MOD for _ in range(nl)]
    dmod_rows[nl - 1][5] = dm_loss[:, :, 2]
    g_qn, g_kn, g_w31, g_w3, g_ln = [None] * nl, [None] * nl, [None] * nl, [None] * nl, [None] * nl
    gbig = [None] * nl
    pending = None
    pending_early = None

    def blocks(g):
        return g.reshape(NDEV, g.shape[0] // NDEV, g.shape[1])

    def sum_received(l, recv):
        gbig[l] = [_sum_slots(f"sum_g{l}_{i}", r) for i, r in enumerate(recv)]

    for l in reversed(range(nl)):
        sv = saved[l]
        win_t, wao_t, wco_t, wso_t, wmix, wfi_t, wfo = sv["w"]
        ds = _mm(do, wfo, "nt", ACT, f"mm_dsw{l}")
        g_wfo = _mm(sv["sw"], do, "tn", bf16, f"mm_gfo{l}")
        dff = _swiglu_bwd(cfg, f"swiglu_b{l}", sv["ff"], ds)
        dh2 = _mm(dff, wfi_t, "nn", GRAD, f"mm_dh2{l}")
        g_wfi = _mm(dff, sv["h2"], "tn", bf16, f"mm_gfi{l}")
        dx0p, dmixed, dm_b = _norm_bwd(cfg, f"norm_b_b{l}", sv["x1"], sv["mixed"], dh2, dres, mod[l], 2, mod[l], 3)
        dmod_rows[l][3], dmod_rows[l][4], dmod_rows[l][2] = dm_b[:, :, 0], dm_b[:, :, 1], dm_b[:, :, 2]
        dmerged = _mm(dmixed, wmix, "nt", GRAD, f"mm_dmg{l}")
        g_wmix = _mm(sv["merged"], dmixed, "tn", bf16, f"mm_gmix{l}")
        dya, dyc, dys, dg0, dg1, dg2 = _gate_bwd(cfg, f"gate_b{l}", sv["p"], sv["ya"], sv["yc"], sv["ys"], dmerged)
        dattn = _mm(dya, wao_t, "nn", GRAD, f"mm_dat{l}")
        g_wao = _mm(dya, sv["attn"], "tn", bf16, f"mm_gao{l}")
        dhs = _mm(dyc, wco_t, "nn", GRAD, f"mm_dhs{l}")
        g_wco = _mm(dyc, sv["hs"], "tn", bf16, f"mm_gco{l}")
        dz = _mm(dys, wso_t, "nn", GRAD, f"mm_dz{l}")
        g_wso = _mm(dys, sv["z"], "tn", bf16, f"mm_gso{l}")
        dhc, g_ln[l] = _ln_silu_bwd(cfg, f"lnsilu_b{l}", sv["hc"], dhs, sv["lnp"])
        (da, dg, dbg, dcg, dxs, g_w31[l], g_w3[l]), recv = _conv_bwd(
            cfg, f"conv_b{l}", sv["p"], dhc, dz, w31b[l], w3p[l], comm=None if pending is None else ([pending], True))
        if pending is not None:
            sum_received(l + 1, recv + pending_early)
        early = [blocks(g) for g in (g_wao, g_wco, g_wso, g_wmix, g_wfi, g_wfo)]
        (dq, dkr, dvr), pending_early = _attn_bwd(cfg, f"attn_b{l}", sv["q"], sv["krep"], sv["vrep"], sv["attn"],
                                                  sv["lse"], dattn, comm=(early, True))
        dpq, dgain = _qkv_bwd(cfg, f"qkv_b{l}", sv["p"], dq, dkr, dvr, cs, sv["gain"], bd, rept)
        g_qn[l] = jnp.sum(dgain[0, :AW].reshape(reps_q, HEAD_DIM), axis=0)
        g_kn[l] = jnp.sum(dgain[0, AW:].reshape(reps_k, HEAD_DIM), axis=0)
        dp = jnp.concatenate([dpq, da, dg, dbg, dcg, dxs, dg0, dg1, dg2], axis=1)
        g_win = _mm(dp, sv["h"], "tn", bf16, f"mm_gin{l}")
        pending = blocks(g_win)
        if l == 0:
            dh, recv = _mm(dp, win_t, "nn", GRAD, f"mm_dh{l}", comm=([pending], True))
            sum_received(0, recv + pending_early)
            dres, _, dm_a = _norm_bwd(cfg, f"norm_a_b{l}", sv["x0"], None, dh, dx0p, None, 0, mod[l], 0)
        else:
            dh = _mm(dp, win_t, "nn", GRAD, f"mm_dh{l}")
            dres, do, dm_a = _norm_bwd(cfg, f"norm_a_b{l}", sv["x0"], sv["oprev"], dh, dx0p, mod[l - 1], 5, mod[l], 0)
            dmod_rows[l - 1][5] = dm_a[:, :, 2]
        dmod_rows[l][0], dmod_rows[l][1] = dm_a[:, :, 0], dm_a[:, :, 1]

    grad_x = dres.reshape(b, cfg.T, d)[:, cfg.LC:, :]

    rb = -(-(b + 1) // SUBLANE) * SUBLANE
    dmod_l = jnp.stack([jnp.stack(rows, axis=2) for rows in dmod_rows])
    d_lat = dmod_l[:, :, 1].reshape(nl, b, NDEV, na)
    d_ctx = jnp.sum(dmod_l[:, :, 0], axis=1).reshape(nl, 1, NDEV, na)
    send = jnp.concatenate([d_lat, d_ctx, jnp.zeros((nl, rb - b - 1, NDEV, na), f32)], axis=1)
    send = jnp.transpose(send, (2, 0, 1, 3)).reshape(NDEV, nl * rb, na)
    (recv,) = _all_to_all("scatter_dmod", [send])
    recv = recv.reshape(NDEV, nl, rb, na)
    dmod_ex = jnp.transpose(recv[:, :, :b], (1, 0, 2, 3)).reshape(nl, nb, na)
    dmod_ex = jnp.pad(dmod_ex, ((0, 0), (0, nbp - nb), (0, 0)))
    dmodc = jnp.transpose(recv[:, :, b], (1, 0, 2))
    g_wada, g_bloc, g_cctx = _ada_bwd("ada_bwd", call, w_ada, dmod_ex, dmodc, nb)

    g_bada = lax.dynamic_update_slice(jnp.zeros((nl, N_MOD * d), f32), g_bloc[:, 0], (0, me * na))
    g_w31s = jnp.stack(g_w31)
    small_parts = [g_cctx[0], g_bada, jnp.stack(g_qn), jnp.stack(g_kn), g_w31s[:, :cfg.K31], g_w31s[:, cfg.K31],
                   jnp.stack([g[0] for g in g_ln]), jnp.stack([g[1] for g in g_ln]),
                   jnp.stack(g_w3)[:, :cfg.K3]]
    small_part_shapes = [p_.shape for p_ in small_parts]
    (gs,) = _all_gather("gather_gsmall", [_pack_flat(small_parts)])
    gsum = _sum_slots("sum_gsmall", gs)
    (gr_cctx, gr_bada, gr_qn, gr_kn, gr_w31, gr_b31, gr_lng, gr_lnb, gr_w3) = _unpack_flat(gsum, small_part_shapes)
    gr_w31 = lax.dynamic_slice_in_dim(gr_w31, me * cw8, cw8, axis=2)
    gr_w3 = lax.dynamic_slice_in_dim(gr_w3, me * cw8, cw8, axis=2)

    def big(i, transposed):
        g = jnp.stack([gbig[l][i] for l in range(nl)])
        return jnp.transpose(g, (0, 2, 1)) if transposed else g

    grads = {
        "c_ctx": gr_cctx, "w_ada": g_wada, "b_ada": gr_bada, "w_in": big(0, True), "q_norm": gr_qn, "k_norm": gr_kn,
        "w_attn_o": big(1, True), "conf_dw_w": gr_w31, "conf_dw_b": gr_b31, "conf_ln_g": gr_lng, "conf_ln_b": gr_lnb,
        "w_conf_out": big(2, True), "sc_dw_w": gr_w3, "w_sc_out": big(3, True), "w_mix_out": big(4, False),
        "w_ffn_in": big(5, True), "w_ffn_out": big(6, False)}
    weights = dict(c_ctx=c_ctx, w_ada=w_ada, b_ada=b_ada, w_in=w_in, q_norm=q_norm, k_norm=k_norm, w_attn_o=w_attn_o,
                   conf_dw_w=conf_dw_w, conf_dw_b=conf_dw_b, conf_ln_g=conf_ln_g, conf_ln_b=conf_ln_b,
                   w_conf_out=w_conf_out, sc_dw_w=sc_dw_w, w_sc_out=w_sc_out, w_mix_out=w_mix_out,
                   w_ffn_in=w_ffn_in, w_ffn_out=w_ffn_out)
    m_in = dict(c_ctx=m_c_ctx, w_ada=m_w_ada, b_ada=m_b_ada, w_in=m_w_in, q_norm=m_q_norm, k_norm=m_k_norm,
                w_attn_o=m_w_attn_o, conf_dw_w=m_conf_dw_w, conf_dw_b=m_conf_dw_b, conf_ln_g=m_conf_ln_g,
                conf_ln_b=m_conf_ln_b, w_conf_out=m_w_conf_out, sc_dw_w=m_sc_dw_w, w_sc_out=m_w_sc_out,
                w_mix_out=m_w_mix_out, w_ffn_in=m_w_ffn_in, w_ffn_out=m_w_ffn_out)
    v_in = dict(c_ctx=v_c_ctx, w_ada=v_w_ada, b_ada=v_b_ada, w_in=v_w_in, q_norm=v_q_norm, k_norm=v_k_norm,
                w_attn_o=v_w_attn_o, conf_dw_w=v_conf_dw_w, conf_dw_b=v_conf_dw_b, conf_ln_g=v_conf_ln_g,
                conf_ln_b=v_conf_ln_b, w_conf_out=v_w_conf_out, sc_dw_w=v_sc_dw_w, w_sc_out=v_w_sc_out,
                w_mix_out=v_w_mix_out, w_ffn_in=v_w_ffn_in, w_ffn_out=v_w_ffn_out)
    names = list(weights)
    big_names = ("w_ada", "w_in", "w_attn_o", "w_conf_out", "w_sc_out", "w_mix_out", "w_ffn_in", "w_ffn_out")
    small_names = [n for n in names if n not in big_names]
    delta, new_m, new_v = {}, {}, {}
    for n in big_names:
        shp = weights[n].shape
        two_d = (shp[0] * shp[1], shp[2])
        dl, nm, nv = _adamw(f"adamw_{n}", weights[n].reshape(two_d), grads[n].reshape(two_d),
                            m_in[n].reshape(two_d), v_in[n].reshape(two_d))
        delta[n], new_m[n], new_v[n] = dl.reshape(shp), nm.reshape(shp), nv.reshape(shp)
    sshapes = [weights[n].shape for n in small_names]
    dl, nm, nv = _adamw("adamw_small", _pack_flat([weights[n] for n in small_names]),
                        _pack_flat([grads[n] for n in small_names]), _pack_flat([m_in[n] for n in small_names]),
                        _pack_flat([v_in[n] for n in small_names]))
    for n, a_, b_, c_ in zip(small_names, _unpack_flat(dl, sshapes), _unpack_flat(nm, sshapes), _unpack_flat(nv, sshapes)):
        delta[n], new_m[n], new_v[n] = a_, b_, c_

    return (loss, grad_x, *[grads[n] for n in names], *[delta[n] for n in names],
            *[new_m[n] for n in names], *[new_v[n] for n in names])
```

```python
import functools

import numpy as np
import jax
import jax.numpy as jnp
from jax import lax
from jax.experimental import pallas as pl
from jax.experimental.pallas import tpu as pltpu

f32 = jnp.float32
bf16 = jnp.bfloat16
ACT = bf16
GRAD = bf16

NDEV = 8
AXES = ("x", "y", "c")
HEAD_DIM = 64
N_Q_HEADS = 8
N_KV_HEADS = 2
Q_GROUP = N_Q_HEADS // N_KV_HEADS
AW = N_Q_HEADS * HEAD_DIM
KVW = N_KV_HEADS * HEAD_DIM
GW = Q_GROUP * HEAD_DIM
QKW = AW + KVW
QKVW = AW + 2 * KVW
GRID_W = 64
AXIS_DIM = HEAD_DIM // 2
ROPE_THETA = 10000.0
ATTN_SCALE = HEAD_DIM ** -0.5
EPS = 1e-6
N_MOD = 6
N_BRANCHES = 3
CONV_PARTS = 5
MOD_ROWS = 8
CONV_PAD_ROWS = 16
GATE_ROWS = 1024
HEADS_PER_PASS_FWD = 1
HEADS_PER_PASS_BWD = 2
CONV_CHUNK = 64
LANE = 128
SUBLANE = 8
VMEM_LIMIT = 56 * 1024 * 1024
MM_VMEM_BUDGET = 40 * 1024 * 1024

ADAM_LR = 0.001
ADAM_B1 = 0.9
ADAM_B2 = 0.999
ADAM_EPS = 1e-08
ADAM_WD = 0.01
ADAM_STEP = 10

NN =(((1,), (0,)), ((), ()))
NT_ = (((1,), (1,)), ((), ()))
TN = (((0,), (0,)), ((), ()))


def _params(ndims, vmem=None):
    return pltpu.CompilerParams(dimension_semantics=("arbitrary",) * ndims, vmem_limit_bytes=vmem)


def _divisor(n, cap, mult):
    best = None
    for d in range(mult, min(n, cap) + 1, mult):
        if n % d == 0:
            best = d
    return n if best is None else best


def _const_spec(a):
    nd = a.ndim
    return pl.BlockSpec(a.shape, lambda *idx: (0,) * nd)


class _Cfg:
    pass


def _mm(a, b, mode, out_dtype, name, comm=None):
    def accumulate(prod, o_ref, scratch, nsteps):
        if nsteps == 1:
            o_ref[...] = prod.astype(out_dtype)
            return
        acc_ref = scratch[0]
        k = pl.program_id(2)

        @pl.when(k == 0)
        def _():
            acc_ref[...] = prod

        if nsteps > 2:
            @pl.when(jnp.logical_and(k > 0, k < nsteps - 1))
            def _():
                acc_ref[...] += prod

        @pl.when(k == nsteps - 1)
        def _():
            o_ref[...] = (acc_ref[...] + prod).astype(out_dtype)

    def fits(*tiles):
        return sum(2 * r * c * s for r, c, s in tiles[:-1]) + tiles[-1][0] * tiles[-1][1] * 4 <= MM_VMEM_BUDGET

    osz = jnp.dtype(out_dtype).itemsize
    if mode == "tn":
        m, ka = a.shape
        nb = b.shape[1]
        tka = _divisor(ka, 1408, LANE)
        tnb = _divisor(nb, 1024, LANE)
        tmr = SUBLANE
        for cand in range(SUBLANE, min(m, 2304) + 1, SUBLANE):
            if m % cand == 0 and fits((cand, tka, a.dtype.itemsize), (cand, tnb, b.dtype.itemsize),
                                      (tka, tnb, osz), (tka, tnb, 4)):
                tmr = cand
        nsteps = m // tmr

        def body(a_ref, b_ref, o_ref, *scratch):
            prod = lax.dot_general(a_ref[...].astype(bf16), b_ref[...].astype(bf16), TN, preferred_element_type=f32)
            accumulate(prod, o_ref, scratch, nsteps)

        return pl.pallas_call(
            body, out_shape=jax.ShapeDtypeStruct((ka, nb), out_dtype),
            grid=(ka // tka, nb // tnb, nsteps),
            in_specs=[pl.BlockSpec((tmr, tka), lambda i, j, k: (k, i)),
                      pl.BlockSpec((tmr, tnb), lambda i, j, k: (k, j))],
            out_specs=pl.BlockSpec((tka, tnb), lambda i, j, k: (i, j)),
            scratch_shapes=[pltpu.VMEM((tka, tnb), f32)] if nsteps > 1 else [],
            compiler_params=_params(3, VMEM_LIMIT), name=name)(a, b)

    m, kdim = a.shape
    n = b.shape[1] if mode == "nn" else b.shape[0]
    tm = _divisor(m, 1024 if a.dtype == bf16 else 512, SUBLANE)
    tn = _divisor(n, 1408, LANE)
    tk = LANE if kdim % LANE == 0 else kdim
    for cand in range(LANE, min(kdim, 3328) + 1, LANE):
        if kdim % cand == 0 and fits((tm, cand, a.dtype.itemsize), (cand, tn, b.dtype.itemsize), (tm, tn, osz),
                                     (tm, tn, 4)):
            tk = cand
    nsteps = kdim // tk
    dims = NN if mode == "nn" else NT_

    def body(a_ref, b_ref, o_ref, *scratch):
        prod = lax.dot_general(a_ref[...].astype(bf16), b_ref[...].astype(bf16), dims, preferred_element_type=f32)
        accumulate(prod, o_ref, scratch, nsteps)

    if mode == "nn":
        b_spec = pl.BlockSpec((tk, tn), lambda i, j, k: (k, j))
    else:
        b_spec = pl.BlockSpec((tn, tk), lambda i, j, k: (j, k))
    (out,), couts = _hosting_call(
        body, comm, out_shape=(jax.ShapeDtypeStruct((m, n), out_dtype),),
        grid=(m // tm, n // tn, nsteps),
        in_specs=[pl.BlockSpec((tm, tk), lambda i, j, k: (i, k)), b_spec],
        out_specs=(pl.BlockSpec((tm, tn), lambda i, j, k: (i, j)),),
        scratch_shapes=[pltpu.VMEM((tm, tn), f32)] if nsteps > 1 else [],
        compiler_params=_params(3, VMEM_LIMIT), name=name, args=(a, b))
    return out if comm is None else (out, couts)


def _row_spec(cfg, width, col=0):
    return pl.BlockSpec((cfg.TR, width), lambda i: (i, col))


def _mod_spec(cfg):
    nt, nc = cfg.NT, cfg.NC
    return pl.BlockSpec((1, 1, MOD_ROWS, cfg.D),
                        lambda i: (i // nt, ((i % nt) >= nc).astype(jnp.int32), 0, 0))


def _segment_start(cfg, i):
    j = i % cfg.NT
    return jnp.logical_or(j == 0, j == cfg.NC)


def _norm_fwd(cfg, name, xin, o, modg, gk, modn, sk):
    has_o = o is not None
    r_, d = xin.shape

    def body(*refs):
        if has_o:
            x_ref, o_ref, mg_ref, mn_ref, xn_ref, h_ref = refs
            x = x_ref[...] + mg_ref[0, 0, gk:gk + 1, :] * o_ref[...]
            xn_ref[...] = x
        else:
            x_ref, mn_ref, h_ref = refs
            x = x_ref[...]
        r = lax.rsqrt(jnp.mean(x * x, axis=-1, keepdims=True) + EPS)
        h = (x * r) * (1.0 + mn_ref[0, 0, sk + 1:sk + 2, :]) + mn_ref[0, 0, sk:sk + 1, :]
        h_ref[...] = h.astype(bf16)

    row = _row_spec(cfg, d)
    if has_o:
        ins, in_specs = (xin, o, modg, modn), [row, row, _mod_spec(cfg), _mod_spec(cfg)]
        out_shape = (jax.ShapeDtypeStruct((r_, d), f32), jax.ShapeDtypeStruct((r_, d), bf16))
        out_specs = (row, row)
    else:
        ins, in_specs = (xin, modn), [row, _mod_spec(cfg)]
        out_shape = jax.ShapeDtypeStruct((r_, d), bf16)
        out_specs = row
    res = pl.pallas_call(body, out_shape=out_shape, grid=(r_ // cfg.TR,), in_specs=in_specs,
                         out_specs=out_specs, compiler_params=_params(1), name=name)(*ins)
    return res if has_o else (xin, res)


def _norm_bwd(cfg, name, xnew, o, dh, dres, modg, gk, modn, sk):
    has_o = o is not None
    r_, d = xnew.shape

    def body(*refs):
        if has_o:
            xn_ref, o_ref, dh_ref, dr_ref, mg_ref, mn_ref, dx_ref, do_ref, dm_ref = refs
        else:
            xn_ref, dh_ref, dr_ref, mn_ref, dx_ref, dm_ref = refs
        i = pl.program_id(0)
        x = xn_ref[...]
        dhv = dh_ref[...].astype(f32)
        r = lax.rsqrt(jnp.mean(x * x, axis=-1, keepdims=True) + EPS)
        xh = x * r
        dxh = dhv * (1.0 + mn_ref[0, 0, sk + 1:sk + 2, :])
        dx = r * (dxh - xh * jnp.mean(dxh * xh, axis=-1, keepdims=True)) + dr_ref[...]
        dx_ref[...] = dx

        @pl.when(_segment_start(cfg, i))
        def _():
            dm_ref[...] = jnp.zeros_like(dm_ref)

        dm_ref[0, 0, 0:1, :] += jnp.sum(dhv, axis=0, keepdims=True)
        dm_ref[0, 0, 1:2, :] += jnp.sum(dhv * xh, axis=0, keepdims=True)
        if has_o:
            ov = o_ref[...]
            dm_ref[0, 0, 2:3, :] += jnp.sum(dx * ov, axis=0, keepdims=True)
            do_ref[...] = (dx * mg_ref[0, 0, gk:gk + 1, :]).astype(bf16)

    row = _row_spec(cfg, d)
    dm_shape = jax.ShapeDtypeStruct((cfg.B, 2, MOD_ROWS, d), f32)
    if has_o:
        ins = (xnew, o, dh, dres, modg, modn)
        in_specs = [row, row, row, row, _mod_spec(cfg), _mod_spec(cfg)]
        out_shape = (jax.ShapeDtypeStruct((r_, d), f32), jax.ShapeDtypeStruct((r_, d), bf16), dm_shape)
        out_specs = (row, row, _mod_spec(cfg))
    else:
        ins = (xnew, dh, dres, modn)
        in_specs = [row, row, row, _mod_spec(cfg)]
        out_shape = (jax.ShapeDtypeStruct((r_, d), f32), dm_shape)
        out_specs = (row, _mod_spec(cfg))
    res = pl.pallas_call(body, out_shape=out_shape, grid=(r_ // cfg.TR,), in_specs=in_specs,
                         out_specs=out_specs, compiler_params=_params(1), name=name)(*ins)
    if has_o:
        return res
    return res[0], None, res[1]


def _loss_head(cfg, name, x1, o, modg, gk, target):
    r_, d = x1.shape
    nt, nc = cfg.NT, cfg.NC
    nlat = nt - nc

    def body(x_ref, o_ref, mg_ref, t_ref, ls_ref, dy_ref, do_ref, dm_ref):
        i = pl.program_id(0)
        lat = (i % nt) >= nc
        gate = mg_ref[0, 0, gk:gk + 1, :]
        ov = o_ref[...]
        err = jnp.where(lat, x_ref[...] + gate * ov - t_ref[...], 0.0)
        ls_ref[...] = jnp.zeros_like(ls_ref)
        ls_ref[0, 0:1, :] = jnp.sum(err * err, axis=0, keepdims=True)
        dy = err * (1.0 / d)
        dy_ref[...] = dy
        do_ref[...] = (dy * gate).astype(bf16)

        @pl.when(_segment_start(cfg, i))
        def _():
            dm_ref[...] = jnp.zeros_like(dm_ref)

        dm_ref[0, 0, 2:3, :] += jnp.sum(dy * ov, axis=0, keepdims=True)

    row = _row_spec(cfg, d)
    t_spec = pl.BlockSpec((cfg.TR, d), lambda i: ((i // nt) * nlat + jnp.maximum((i % nt) - nc, 0), 0))
    return pl.pallas_call(
        body,
        out_shape=(jax.ShapeDtypeStruct((r_ // cfg.TR, SUBLANE, d), f32), jax.ShapeDtypeStruct((r_, d), f32),
                   jax.ShapeDtypeStruct((r_, d), bf16), jax.ShapeDtypeStruct((cfg.B, 2, MOD_ROWS, d), f32)),
        grid=(r_ // cfg.TR,),
        in_specs=[row, row, _mod_spec(cfg), t_spec],
        out_specs=(pl.BlockSpec((1, SUBLANE, d), lambda i: (i, 0, 0)), row, row, _mod_spec(cfg)),
        compiler_params=_params(1), name=name)(x1, o, modg, target)


def _swap16(y):
    w = y.shape[1]
    lane = lax.broadcasted_iota(jnp.int32, (1, w), 1)
    lo = jnp.bitwise_and(lane, 31) < 16
    return jnp.where(lo, pltpu.roll(y, w - 16, 1), pltpu.roll(y, 16, 1))


def _dot_split(v, m):
    hi = v.astype(bf16)
    lo = (v - hi.astype(f32)).astype(bf16)
    return jnp.dot(hi, m, preferred_element_type=f32) + jnp.dot(lo, m, preferred_element_type=f32)


def _head_sum(v, bd):
    return _dot_split(v, bd)


def _qkv_fwd(cfg, name, p, cs, gain, bd, rep):
    r_ = p.shape[0]
    nt = cfg.NT

    def body(p_ref, cs_ref, g_ref, bd_ref, rep_ref, q_ref, k_ref, v_ref):
        x = p_ref[:, 0:QKW].astype(f32)
        r = lax.rsqrt(_head_sum(x * x, bd_ref[...]) * (1.0 / HEAD_DIM) + EPS)
        y = (x * r) * g_ref[...]
        out = y * cs_ref[0] + _swap16(y) * cs_ref[1]
        q_ref[...] = out[:, 0:AW].astype(bf16)
        kr = out[:, AW:QKW].astype(bf16)
        k_ref[...] = jnp.dot(kr, rep_ref[...], preferred_element_type=f32).astype(bf16)
        vv = p_ref[:, QKW:QKVW].astype(bf16)
        v_ref[...] = jnp.dot(vv, rep_ref[...], preferred_element_type=f32).astype(bf16)

    row = pl.BlockSpec((cfg.TR, AW), lambda j, e: (e * nt + j, 0))
    return pl.pallas_call(
        body, out_shape=(jax.ShapeDtypeStruct((r_, AW), bf16),) * 3, grid=(nt, cfg.B),
        in_specs=[pl.BlockSpec((cfg.TR, QKVW), lambda j, e: (e * nt + j, 0)),
                  pl.BlockSpec((2, cfg.TR, QKW), lambda j, e: (0, j, 0)),
                  _const_spec(gain), _const_spec(bd), _const_spec(rep)],
        out_specs=(row, row, row), compiler_params=_params(2), name=name)(p, cs, gain, bd, rep)


def _qkv_bwd(cfg, name, p, dp, dq, dkr, dvr, cs, gain, bd, rept):
    nt = cfg.NT

    def body(p_ref, dp_in, dq_ref, dk_ref, dv_ref, cs_ref, g_ref, bd_ref, rt_ref, dp_ref, dg_ref):
        del dp_in
        first = jnp.logical_and(pl.program_id(0) == 0, pl.program_id(1) == 0)
        x = p_ref[:, 0:QKW].astype(f32)
        rt = rt_ref[...]
        dk = _dot_split(dk_ref[...], rt)
        g = jnp.concatenate([dq_ref[...].astype(f32), dk], axis=1)
        gy = g * cs_ref[0] + _swap16(g * cs_ref[1])
        bdv = bd_ref[...]
        r = lax.rsqrt(_head_sum(x * x, bdv) * (1.0 / HEAD_DIM) + EPS)
        xh = x * r
        dxh = gy * g_ref[...]
        dx = r * (dxh - xh * (_head_sum(dxh * xh, bdv) * (1.0 / HEAD_DIM)))
        dp_ref[:, 0:QKW] = dx.astype(bf16)
        dp_ref[:, QKW:QKVW] = _dot_split(dv_ref[...], rt).astype(bf16)

        @pl.when(first)
        def _():
            dg_ref[...] = jnp.zeros_like(dg_ref)

        dg_ref[0:1, :] += jnp.sum(gy * xh, axis=0, keepdims=True)

    row = pl.BlockSpec((cfg.TR, AW), lambda j, e: (e * nt + j, 0))
    wide = pl.BlockSpec((cfg.TR, QKVW), lambda j, e: (e * nt + j, 0))
    return pl.pallas_call(
        body, out_shape=(jax.ShapeDtypeStruct(dp.shape, dp.dtype), jax.ShapeDtypeStruct((SUBLANE, QKW), f32)),
        grid=(nt, cfg.B),
        in_specs=[wide, pl.BlockSpec(memory_space=pl.ANY), row, row, row,
                  pl.BlockSpec((2, cfg.TR, QKW), lambda j, e: (0, j, 0)),
                  _const_spec(gain), _const_spec(bd), _const_spec(rept)],
        out_specs=(wide, pl.BlockSpec((SUBLANE, QKW), lambda j, e: (0, 0))), input_output_aliases={1: 0},
        compiler_params=_params(2), name=name)(p, dp, dq, dkr, dvr, cs, gain, bd, rept)


def _head_masks():
    lane = lax.broadcasted_iota(jnp.int32, (1, GW), 1)
    return [jnp.logical_and(lane >= HEAD_DIM * h, lane < HEAD_DIM * (h + 1)) for h in range(Q_GROUP)]


def _attn_fwd(cfg, name, q, krep, vrep, comm=None):
    r_ = q.shape[0]
    tr, t, lc, nt, nc = cfg.TR, cfg.T, cfg.LC, cfg.NT, cfg.NC

    def body(q_ref, k_ref, v_ref, o_ref, l_ref):
        j = pl.program_id(2)
        masks = _head_masks()
        lane = lax.broadcasted_iota(jnp.int32, (1, LANE), 1)

        def run(nk):
            qv = q_ref[...]
            kv = k_ref[0:nk, :]
            vv = v_ref[0:nk, :]
            acc = jnp.zeros((tr, GW), f32)
            lse = jnp.zeros((tr, LANE), f32)
            for h0 in range(0, Q_GROUP, HEADS_PER_PASS_FWD):
                hs = range(h0, h0 + HEADS_PER_PASS_FWD)
                qs = jnp.concatenate([jnp.where(masks[h], qv, jnp.zeros_like(qv)) for h in hs], axis=0)
                s = lax.dot_general(qs, kv, NT_, preferred_element_type=f32) * ATTN_SCALE
                m = jnp.max(s, axis=1, keepdims=True)
                pr = jnp.exp(s - m)
                l = jnp.sum(pr, axis=1, keepdims=True)
                os_ = jnp.dot(pr.astype(bf16), vv, preferred_element_type=f32) / l
                ls = m + jnp.log(l)
                for i, h in enumerate(hs):
                    acc = acc + jnp.where(masks[h], os_[i * tr:(i + 1) * tr], 0.0)
                    lse = lse + jnp.where(lane == h, ls[i * tr:(i + 1) * tr], 0.0)
            o_ref[...] = acc
            l_ref[...] = lse

        @pl.when(j < nc)
        def _():
            run(lc)

        @pl.when(j >= nc)
        def _():
            run(t)

    qspec = pl.BlockSpec((tr, GW), lambda e, g, j: (e * nt + j, g))
    kspec = pl.BlockSpec((t, GW), lambda e, g, j: (e, g))
    return _hosting_call(
        body, comm, out_shape=(jax.ShapeDtypeStruct((r_, AW), f32), jax.ShapeDtypeStruct((r_, N_KV_HEADS * LANE), f32)),
        grid=(cfg.B, N_KV_HEADS, nt), in_specs=[qspec, kspec, kspec],
        out_specs=(qspec, pl.BlockSpec((tr, LANE), lambda e, g, j: (e * nt + j, g))),
        compiler_params=_params(3, VMEM_LIMIT), name=name, args=(q, krep, vrep))


def _hosting_call(body, comm, *, out_shape, grid, in_specs, out_specs, compiler_params, name, args, scratch_shapes=(),
                  aliases=None):
    aliases = aliases or {}
    if comm is None:
        return pl.pallas_call(body, out_shape=out_shape, grid=grid, in_specs=in_specs, out_specs=out_specs,
                              scratch_shapes=list(scratch_shapes), input_output_aliases=aliases,
                              compiler_params=compiler_params, name=name)(*args), None
    xs, scatter = comm
    n, n_in, n_out, n_scr = len(xs), len(args), len(out_shape), len(scratch_shapes)
    last = [g - 1 for g in grid]

    def hosted(*refs):
        ins, cins = refs[:n_in], refs[n_in:n_in + n]
        outs, couts = refs[n_in + n:n_in + n + n_out], refs[n_in + n + n_out:n_in + 2 * n + n_out]
        scratch = refs[n_in + 2 * n + n_out:n_in + 2 * n + n_out + n_scr]
        sems = refs[n_in + 2 * n + n_out + n_scr:]
        ids = [pl.program_id(a) for a in range(len(grid))]
        is_first = functools.reduce(jnp.logical_and, [i == 0 for i in ids])
        is_last = functools.reduce(jnp.logical_and, [i == l for i, l in zip(ids, last)])

        @pl.when(is_first)
        def _():
            _exchange_start(cins, couts, sems, scatter)

        body(*ins, *outs, *scratch)

        @pl.when(is_last)
        def _():
            _exchange_wait(cins, couts, sems, scatter)

    anyspec = pl.BlockSpec(memory_space=pl.ANY)
    res = pl.pallas_call(
        hosted, out_shape=tuple(out_shape) + tuple(_exchange_out_shapes(xs, scatter)), grid=grid,
        in_specs=list(in_specs) + [anyspec] * n, out_specs=tuple(out_specs) + (anyspec,) * n,
        scratch_shapes=list(scratch_shapes) + _exchange_scratch(n), input_output_aliases=aliases,
        compiler_params=compiler_params, name=name)(*args, *xs)
    return tuple(res[:n_out]), list(res[n_out:])


def _attn_bwd(cfg, name, q, krep, vrep, o, lse, do, comm=None):
    r_ = q.shape[0]
    tr, t, lc, nt, nc = cfg.TR, cfg.T, cfg.LC, cfg.NT, cfg.NC

    def body(q_ref, k_ref, v_ref, o_ref, l_ref, do_ref, dq_ref, dk_ref, dv_ref):
        j = pl.program_id(2)
        masks = _head_masks()
        lane = lax.broadcasted_iota(jnp.int32, (1, LANE), 1)

        @pl.when(j == 0)
        def _():
            dk_ref[...] = jnp.zeros_like(dk_ref)
            dv_ref[...] = jnp.zeros_like(dv_ref)

        def run(nk):
            qv = q_ref[...]
            kv = k_ref[0:nk, :]
            vv = v_ref[0:nk, :]
            ov = o_ref[...]
            dov = do_ref[...].astype(f32)
            lv = l_ref[...]
            dq = jnp.zeros((tr, GW), f32)
            dk = jnp.zeros((nk, GW), f32)
            dv = jnp.zeros((nk, GW), f32)
            for h0 in range(0, Q_GROUP, HEADS_PER_PASS_BWD):
                hs = range(h0, h0 + HEADS_PER_PASS_BWD)
                qs = jnp.concatenate([jnp.where(masks[h], qv, jnp.zeros_like(qv)) for h in hs], axis=0)
                dos = jnp.concatenate([jnp.where(masks[h], dov, 0.0) for h in hs], axis=0)
                dosb = dos.astype(bf16)
                delta = jnp.sum(dos * jnp.concatenate([ov] * len(hs), axis=0), axis=1, keepdims=True)
                lses = jnp.concatenate([jnp.sum(jnp.where(lane == h, lv, 0.0), axis=1, keepdims=True) for h in hs],
                                       axis=0)
                s = lax.dot_general(qs, kv, NT_, preferred_element_type=f32) * ATTN_SCALE
                pr = jnp.exp(s - lses)
                dpr = lax.dot_general(dosb, vv, NT_, preferred_element_type=f32)
                ds = (pr * (dpr - delta) * ATTN_SCALE).astype(bf16)
                dqs = jnp.dot(ds, kv, preferred_element_type=f32)
                for i, h in enumerate(hs):
                    dq = dq + jnp.where(masks[h], dqs[i * tr:(i + 1) * tr], 0.0)
                dk = dk + lax.dot_general(ds, qs, TN, preferred_element_type=f32)
                dv = dv + lax.dot_general(pr.astype(bf16), dosb, TN, preferred_element_type=f32)
            dq_ref[...] = dq.astype(dq_ref.dtype)
            dk_ref[0:nk, :] += dk
            dv_ref[0:nk, :] += dv

        @pl.when(j < nc)
        def _():
            run(lc)

        @pl.when(j >= nc)
        def _():
            run(t)

    qspec = pl.BlockSpec((tr, GW), lambda e, g, j: (e * nt + j, g))
    kspec = pl.BlockSpec((t, GW), lambda e, g, j: (e, g))
    lspec = pl.BlockSpec((tr, LANE), lambda e, g, j: (e * nt + j, g))
    return _hosting_call(
        body, comm, out_shape=(jax.ShapeDtypeStruct((r_, AW), GRAD),) + (jax.ShapeDtypeStruct((r_, AW), f32),) * 2,
        grid=(cfg.B, N_KV_HEADS, nt), in_specs=[qspec, kspec, kspec, qspec, lspec, qspec],
        out_specs=(qspec, kspec, kspec),
        compiler_params=_params(3, VMEM_LIMIT), name=name, args=(q, krep, vrep, o, lse, do))


def _segments(cfg):
    pad = CONV_PAD_ROWS
    return ((pad, 0, cfg.LC), (cfg.LC + 3 * pad, cfg.LC, cfg.S)), cfg.T + 4 * pad


def _fill_padded(buf, cfg, fn):
    buf[...] = jnp.zeros_like(buf)
    for off, lo, n in _segments(cfg)[0]:
        buf[off:off + n, :] = fn(lo, lo + n)


def _taps(buf, w_ref, base, taps, sign):
    acc = None
    for k in range(taps):
        term = w_ref[k:k + 1, :] * buf[pl.ds(base + sign * (k - taps // 2), CONV_CHUNK), :]
        acc = term if acc is None else acc + term
    return acc


def _fold_tiles(v):
    acc = v[0:SUBLANE]
    for i in range(1, v.shape[0] // SUBLANE):
        acc = acc + v[SUBLANE * i:SUBLANE * (i + 1)]
    return acc


def _tap_grads(buf, dy, acc_ref, base, taps):
    for k in range(taps):
        acc_ref[SUBLANE * k:SUBLANE * (k + 1), :] += _fold_tiles(
            buf[pl.ds(base + k - taps // 2, CONV_CHUNK), :] * dy)


def _conv_spec(cfg):
    width = CONV_PARTS * cfg.CB
    return pl.BlockSpec((cfg.T, width), lambda c, e: (e, cfg.OFF_CONV // width + c))


def _conv_views(p_ref, cb):
    return [lambda rows, q=q: p_ref[rows, q * cb:(q + 1) * cb].astype(f32) for q in range(CONV_PARTS)]


def _conv_fwd(cfg, name, p, w31, w3):
    r_ = p.shape[0]
    t, cb, cw = cfg.T, cfg.CB, cfg.CW
    k31, k3 = cfg.K31, cfg.K3

    segs, buf_rows = _segments(cfg)
    ch = CONV_CHUNK

    def body(p_ref, w31_ref, w3_ref, hc_ref, z_ref, hbuf, ubuf):
        a, g, bg, cg, xs = _conv_views(p_ref, cb)
        _fill_padded(hbuf, cfg, lambda lo, hi: a(slice(lo, hi)) * jax.nn.sigmoid(g(slice(lo, hi))))
        _fill_padded(ubuf, cfg, lambda lo, hi: cg(slice(lo, hi)) * xs(slice(lo, hi)))
        bias = w31_ref[k31:k31 + 1, :]
        for off, lo, n in segs:
            def chunk(c, carry, off=off, lo=lo):
                base = off + pl.multiple_of(c * ch, ch)
                rows = pl.ds(pl.multiple_of(lo + c * ch, ch), ch)
                hc_ref[rows, :] = _taps(hbuf, w31_ref, base, k31, 1) + bias
                z_ref[rows, :] = (bg(rows) * _taps(ubuf, w3_ref, base, k3, 1)).astype(bf16)
                return carry

            lax.fori_loop(0, n // ch, chunk, 0)

    ospec = pl.BlockSpec((t, cb), lambda c, e: (e, c))
    return pl.pallas_call(
        body, out_shape=(jax.ShapeDtypeStruct((r_, cw), f32), jax.ShapeDtypeStruct((r_, cw), bf16)),
        grid=(cw // cb, cfg.B),
        in_specs=[_conv_spec(cfg), pl.BlockSpec((w31.shape[0], cb), lambda c, e: (0, c)),
                  pl.BlockSpec((w3.shape[0], cb), lambda c, e: (0, c))],
        out_specs=(ospec, ospec), scratch_shapes=[pltpu.VMEM((buf_rows, cb), f32)] * 2,
        compiler_params=_params(2, VMEM_LIMIT), name=name)(p, w31, w3)


def _conv_bwd(cfg, name, p, dp, dhc, dz, w31, w3, comm=None):
    t, cb, cw = cfg.T, cfg.CB, cfg.CW
    k31, k3 = cfg.K31, cfg.K3

    segs, buf_rows = _segments(cfg)
    ch = CONV_CHUNK

    def body(p_ref, dp_in, dhc_ref, dz_ref, w31_ref, w3_ref, dp_ref, dw31_ref, dw3_ref,
             hbuf, dbuf, ubuf, ebuf, acc31, acc3):
        del dp_in
        e = pl.program_id(1)
        a, g, bg, cg, xs = _conv_views(p_ref, cb)
        da_ref, dg_ref, dbg_ref, dcg_ref, dxs_ref = [dp_ref.at[:, q * cb:(q + 1) * cb] for q in range(CONV_PARTS)]

        @pl.when(e == 0)
        def _():
            dw31_ref[...] = jnp.zeros_like(dw31_ref)
            dw3_ref[...] = jnp.zeros_like(dw3_ref)

        _fill_padded(hbuf, cfg, lambda lo, hi: a(slice(lo, hi)) * jax.nn.sigmoid(g(slice(lo, hi))))
        _fill_padded(dbuf, cfg, lambda lo, hi: dhc_ref[lo:hi, :])
        _fill_padded(ubuf, cfg, lambda lo, hi: cg(slice(lo, hi)) * xs(slice(lo, hi)))
        _fill_padded(ebuf, cfg, lambda lo, hi: dz_ref[lo:hi, :].astype(f32) * bg(slice(lo, hi)))
        acc31[...] = jnp.zeros_like(acc31)
        acc3[...] = jnp.zeros_like(acc3)
        for off, lo, n in segs:
            def chunk(c, carry, off=off, lo=lo):
                base = off + pl.multiple_of(c * ch, ch)
                here = pl.ds(base, ch)
                rows = pl.ds(pl.multiple_of(lo + c * ch, ch), ch)
                dy = dbuf[here, :]
                dhh = _taps(dbuf, w31_ref, base, k31, -1)
                _tap_grads(hbuf, dy, acc31, base, k31)
                acc31[SUBLANE * k31:SUBLANE * (k31 + 1), :] += _fold_tiles(dy)
                sg = jax.nn.sigmoid(g(rows))
                da_ref[rows, :] = (dhh * sg).astype(bf16)
                dg_ref[rows, :] = (dhh * hbuf[here, :] * (1.0 - sg)).astype(bf16)

                duc = ebuf[here, :]
                dbg_ref[rows, :] = (dz_ref[rows, :].astype(f32) * _taps(ubuf, w3_ref, base, k3, 1)).astype(bf16)
                du = _taps(ebuf, w3_ref, base, k3, -1)
                _tap_grads(ubuf, duc, acc3, base, k3)
                dcg_ref[rows, :] = (du * xs(rows)).astype(bf16)
                dxs_ref[rows, :] = (du * cg(rows)).astype(bf16)
                return carry

            lax.fori_loop(0, n // ch, chunk, 0)
        for k in range(k31 + 1):
            dw31_ref[k:k + 1, :] += jnp.sum(acc31[SUBLANE * k:SUBLANE * (k + 1), :], axis=0, keepdims=True)
        for k in range(k3):
            dw3_ref[k:k + 1, :] += jnp.sum(acc3[SUBLANE * k:SUBLANE * (k + 1), :], axis=0, keepdims=True)

    ospec = pl.BlockSpec((t, cb), lambda c, e: (e, c))
    w31_spec = pl.BlockSpec((w31.shape[0], cb), lambda c, e: (0, c))
    w3_spec = pl.BlockSpec((w3.shape[0], cb), lambda c, e: (0, c))
    return _hosting_call(
        body, comm,
        out_shape=(jax.ShapeDtypeStruct(dp.shape, dp.dtype), jax.ShapeDtypeStruct(w31.shape, f32),
                   jax.ShapeDtypeStruct(w3.shape, f32)),
        grid=(cw // cb, cfg.B),
        in_specs=[_conv_spec(cfg), pl.BlockSpec(memory_space=pl.ANY), ospec, ospec, w31_spec, w3_spec],
        out_specs=(_conv_spec(cfg), w31_spec, w3_spec), aliases={1: 0},
        scratch_shapes=[pltpu.VMEM((buf_rows, cb), f32)] * 4
        + [pltpu.VMEM((SUBLANE * w31.shape[0], cb), f32), pltpu.VMEM((SUBLANE * w3.shape[0], cb), f32)],
        compiler_params=_params(2, VMEM_LIMIT), name=name, args=(p, dp, dhc, dz, w31, w3))


def _ln_silu_fwd(cfg, name, hc, lnp):
    r_, cw = hc.shape

    def body(x_ref, p_ref, o_ref):
        x = x_ref[...]
        mu = jnp.mean(x, axis=-1, keepdims=True)
        xc = x - mu
        rs = lax.rsqrt(jnp.mean(xc * xc, axis=-1, keepdims=True) + EPS)
        hn = (xc * rs) * p_ref[0:1, :] + p_ref[1:2, :]
        o_ref[...] = (hn * jax.nn.sigmoid(hn)).astype(bf16)

    row = _row_spec(cfg, cw)
    return pl.pallas_call(body, out_shape=jax.ShapeDtypeStruct((r_, cw), bf16), grid=(r_ // cfg.TR,),
                          in_specs=[row, _const_spec(lnp)], out_specs=row,
                          compiler_params=_params(1), name=name)(hc, lnp)


def _ln_silu_bwd(cfg, name, hc, dhs, lnp):
    r_, cw = hc.shape

    def body(x_ref, d_ref, p_ref, dx_ref, dp_ref):
        i = pl.program_id(0)
        x = x_ref[...]
        mu = jnp.mean(x, axis=-1, keepdims=True)
        xc = x - mu
        rs = lax.rsqrt(jnp.mean(xc * xc, axis=-1, keepdims=True) + EPS)
        xh = xc * rs
        gain = p_ref[0:1, :]
        hn = xh * gain + p_ref[1:2, :]
        sg = jax.nn.sigmoid(hn)
        dhn = d_ref[...].astype(f32) * (sg * (1.0 + hn * (1.0 - sg)))
        dxh = dhn * gain
        dx_ref[...] = rs * (dxh - jnp.mean(dxh, axis=-1, keepdims=True)
                            - xh * jnp.mean(dxh * xh, axis=-1, keepdims=True))

        @pl.when(i == 0)
        def _():
            dp_ref[...] = jnp.zeros_like(dp_ref)

        dp_ref[0:1, :] += jnp.sum(dhn * xh, axis=0, keepdims=True)
        dp_ref[1:2, :] += jnp.sum(dhn, axis=0, keepdims=True)

    row = _row_spec(cfg, cw)
    return pl.pallas_call(
        body, out_shape=(jax.ShapeDtypeStruct((r_, cw), f32), jax.ShapeDtypeStruct((SUBLANE, cw), f32)),
        grid=(r_ // cfg.TR,), in_specs=[row, row, _const_spec(lnp)],
        out_specs=(row, pl.BlockSpec((SUBLANE, cw), lambda i: (0, 0))),
        compiler_params=_params(1), name=name)(hc, dhs, lnp)


def _merge_fwd(cfg, name, p, attn, hs, z, wao_t, wco_t, wso_t):
    r_ = attn.shape[0]
    d = wao_t.shape[0]
    gb = cfg.GB
    tr = _divisor(r_, GATE_ROWS, cfg.TR)

    def body(g_ref, a_ref, h_ref, z_ref, wa_ref, wc_ref, ws_ref, m_ref, ya_ref, yc_ref, ys_ref):
        m = None
        for k, (x_ref, w_ref, y_ref) in enumerate(((a_ref, wa_ref, ya_ref), (h_ref, wc_ref, yc_ref),
                                                   (z_ref, ws_ref, ys_ref))):
            y = lax.dot_general(x_ref[...].astype(bf16), w_ref[...], NT_, preferred_element_type=f32)
            yb = y.astype(ACT)
            y_ref[...] = yb
            term = jax.nn.sigmoid(g_ref[:, k * gb:(k + 1) * gb].astype(f32)) * yb.astype(f32)
            m = term if m is None else m + term
        m_ref[...] = m.astype(bf16)

    yspec = pl.BlockSpec((tr, gb), lambda i, c: (i, c))
    xspecs = [pl.BlockSpec((tr, x.shape[1]), lambda i, c: (i, 0)) for x in (attn, hs, z)]
    wspecs = [pl.BlockSpec((gb, w.shape[1]), lambda i, c: (c, 0)) for w in (wao_t, wco_t, wso_t)]
    return pl.pallas_call(
        body, out_shape=(jax.ShapeDtypeStruct((r_, d), bf16),) + (jax.ShapeDtypeStruct((r_, d), ACT),) * 3,
        grid=(r_ // tr, d // gb), in_specs=[_gate_spec(cfg, tr)] + xspecs + wspecs, out_specs=(yspec,) * 4,
        compiler_params=_params(2, VMEM_LIMIT), name=name)(p, attn, hs, z, wao_t, wco_t, wso_t)


def _gate_spec(cfg, tr):
    width = N_BRANCHES * cfg.GB
    return pl.BlockSpec((tr, width), lambda i, c: (i, cfg.OFF_GATE // width + c))


def _gate_bwd(cfg, name, p, ya, yc, ys, dm):
    r_, d = ya.shape
    gb = cfg.GB
    tr = _divisor(r_, GATE_ROWS, cfg.TR)

    def body(g_ref, ya_ref, yc_ref, ys_ref, dm_ref, da_ref, dc_ref, ds_ref, dp_ref):
        dmv = dm_ref[...].astype(f32)
        for k, (y_ref, dy_ref) in enumerate(((ya_ref, da_ref), (yc_ref, dc_ref), (ys_ref, ds_ref))):
            sg = jax.nn.sigmoid(g_ref[:, k * gb:(k + 1) * gb].astype(f32))
            dy_ref[...] = (dmv * sg).astype(bf16)
            dp_ref[:, k * gb:(k + 1) * gb] = (dmv * y_ref[...].astype(f32) * sg * (1.0 - sg)).astype(bf16)

    yspec = pl.BlockSpec((tr, gb), lambda i, c: (i, c))
    return pl.pallas_call(
        body, out_shape=(jax.ShapeDtypeStruct((r_, d), bf16),) * 3 + (jax.ShapeDtypeStruct(p.shape, bf16),),
        grid=(r_ // tr, d // gb), in_specs=[_gate_spec(cfg, tr)] + [yspec] * 4,
        out_specs=(yspec,) * 3 + (_gate_spec(cfg, tr),),
        compiler_params=_params(2), name=name)(p, ya, yc, ys, dm)


def _swiglu_fwd(cfg, name, ff):
    r_, f2 = ff.shape
    fh = f2 // 2

    def body(a_ref, b_ref, o_ref):
        a = a_ref[...].astype(f32)
        o_ref[...] = (a * jax.nn.sigmoid(a) * b_ref[...].astype(f32)).astype(bf16)

    return pl.pallas_call(body, out_shape=jax.ShapeDtypeStruct((r_, fh), bf16), grid=(r_ // cfg.TR,),
                          in_specs=[_row_spec(cfg, fh, 0), _row_spec(cfg, fh, 1)], out_specs=_row_spec(cfg, fh),
                          compiler_params=_params(1, VMEM_LIMIT), name=name)(ff, ff)


def _swiglu_bwd(cfg, name, ff, ds):
    r_, f2 = ff.shape
    fh = f2 // 2

    def body(a_ref, b_ref, d_ref, o_ref):
        a = a_ref[...].astype(f32)
        sg = jax.nn.sigmoid(a)
        dsv = d_ref[...].astype(f32)
        o_ref[:, 0:fh] = (dsv * b_ref[...].astype(f32) * (sg * (1.0 + a * (1.0 - sg)))).astype(bf16)
        o_ref[:, fh:f2] = (dsv * a * sg).astype(bf16)

    return pl.pallas_call(body, out_shape=jax.ShapeDtypeStruct((r_, f2), bf16), grid=(r_ // cfg.TR,),
                          in_specs=[_row_spec(cfg, fh, 0), _row_spec(cfg, fh, 1), _row_spec(cfg, fh)],
                          out_specs=_row_spec(cfg, f2), compiler_params=_params(1, VMEM_LIMIT), name=name)(ff, ff, ds)


def _ada_fwd(name, call, w_ada, b_loc):
    nl, d, na = w_ada.shape
    nbp = call.shape[0]

    def body(c_ref, w_ref, b_ref, o_ref):
        cv = c_ref[...]
        a = (cv * jax.nn.sigmoid(cv)).astype(bf16)
        o_ref[0] = jnp.dot(a, w_ref[0].astype(bf16), preferred_element_type=f32) + b_ref[0]

    return pl.pallas_call(
        body, out_shape=jax.ShapeDtypeStruct((nl, nbp, na), f32), grid=(nl,),
        in_specs=[_const_spec(call), pl.BlockSpec((1, d, na), lambda l: (l, 0, 0)),
                  pl.BlockSpec((1, 1, na), lambda l: (l, 0, 0))],
        out_specs=pl.BlockSpec((1, nbp, na), lambda l: (l, 0, 0)),
        compiler_params=_params(1, VMEM_LIMIT), name=name)(call, w_ada, b_loc)


def _ada_bwd(name, call, w_ada, dmod, dmodc, cctx_row):
    nl, d, na = w_ada.shape
    nbp = call.shape[0]

    def body(c_ref, w_ref, dm_ref, dc_ref, gw_ref, gb_ref, gc_ref):
        l = pl.program_id(0)
        cv = c_ref[...]
        sg = jax.nn.sigmoid(cv)
        a = (cv * sg).astype(bf16)
        dctx = jnp.sum(dc_ref[0], axis=0, keepdims=True)
        rows = lax.broadcasted_iota(jnp.int32, (nbp, 1), 0)
        dm = jnp.where(rows == cctx_row, dctx, dm_ref[0])
        gw_ref[0] = lax.dot_general(a, dm.astype(bf16), TN, preferred_element_type=f32)
        gb_ref[0] = jnp.zeros((SUBLANE, na), f32)
        gb_ref[0, 0:1, :] = jnp.sum(dm, axis=0, keepdims=True)
        dc8 = jnp.broadcast_to(dctx, (SUBLANE, na)).astype(bf16)
        part = lax.dot_general(dc8, w_ref[0].astype(bf16), NT_, preferred_element_type=f32)
        cc = c_ref[cctx_row:cctx_row + 1, :]
        sc = jax.nn.sigmoid(cc)
        part = part * (sc * (1.0 + cc * (1.0 - sc)))

        @pl.when(l == 0)
        def _():
            gc_ref[...] = jnp.zeros_like(gc_ref)

        gc_ref[...] += part

    return pl.pallas_call(
        body,
        out_shape=(jax.ShapeDtypeStruct((nl, d, na), f32), jax.ShapeDtypeStruct((nl, SUBLANE, na), f32),
                   jax.ShapeDtypeStruct((SUBLANE, d), f32)),
        grid=(nl,),
        in_specs=[_const_spec(call), pl.BlockSpec((1, d, na), lambda l: (l, 0, 0)),
                  pl.BlockSpec((1, nbp, na), lambda l: (l, 0, 0)),
                  pl.BlockSpec((1, NDEV, na), lambda l: (l, 0, 0))],
        out_specs=(pl.BlockSpec((1, d, na), lambda l: (l, 0, 0)), pl.BlockSpec((1, SUBLANE, na), lambda l: (l, 0, 0)),
                   pl.BlockSpec((SUBLANE, d), lambda l: (0, 0))),
        compiler_params=_params(1, VMEM_LIMIT), name=name)(call, w_ada, dmod, dmodc)


def _adamw(name, w, g, m, v):
    rows, cols = w.shape
    tr = _divisor(rows, max(SUBLANE, (1 << 19) // cols), SUBLANE)
    c1 = 1.0 / (1.0 - ADAM_B1 ** ADAM_STEP)
    c2 = 1.0 / (1.0 - ADAM_B2 ** ADAM_STEP)

    def body(w_ref, g_ref, m_ref, v_ref, d_ref, nm_ref, nv_ref):
        gv = g_ref[...]
        nm = ADAM_B1 * m_ref[...] + (1.0 - ADAM_B1) * gv
        nv = ADAM_B2 * v_ref[...] + (1.0 - ADAM_B2) * (gv * gv)
        nm_ref[...] = nm
        nv_ref[...] = nv
        d_ref[...] = -ADAM_LR * ((nm * c1) / (jnp.sqrt(nv * c2) + ADAM_EPS) + ADAM_WD * w_ref[...])

    spec = pl.BlockSpec((tr, cols), lambda i: (i, 0))
    return pl.pallas_call(body, out_shape=(jax.ShapeDtypeStruct((rows, cols), f32),) * 3, grid=(rows // tr,),
                          in_specs=[spec] * 4, out_specs=(spec,) * 3,
                          compiler_params=_params(1, VMEM_LIMIT), name=name)(w, g, m, v)


def _sum_slots(name, x):
    nd, rows, cols = x.shape
    tr = _divisor(rows, max(SUBLANE, (1 << 18) // cols), SUBLANE)

    def body(x_ref, o_ref):
        acc = x_ref[0].astype(f32)
        for s in range(1, nd):
            acc = acc + x_ref[s].astype(f32)
        o_ref[...] = acc

    return pl.pallas_call(body, out_shape=jax.ShapeDtypeStruct((rows, cols), f32), grid=(rows // tr,),
                          in_specs=[pl.BlockSpec((nd, tr, cols), lambda i: (0, i, 0))],
                          out_specs=pl.BlockSpec((tr, cols), lambda i: (i, 0)),
                          compiler_params=_params(1, VMEM_LIMIT), name=name)(x)


def _peer(k):
    x, y, c = (lax.axis_index(a) for a in AXES)
    px = 1 - x if k & 4 else x
    py = 1 - y if k & 2 else y
    pc = 1 - c if k & 1 else c
    return (px, py, pc), 4 * px + 2 * py + pc


def _exchange_out_shapes(xs, scatter):
    return [jax.ShapeDtypeStruct(xa.shape if scatter else (NDEV,) + xa.shape, xa.dtype) for xa in xs]


def _exchange_scratch(n):
    nrel = NDEV - 1
    return [pltpu.SemaphoreType.DMA((n * nrel,)), pltpu.SemaphoreType.DMA((n * nrel,)), pltpu.SemaphoreType.DMA((n,))]


def _exchange_copies(ins, outs, sems, scatter, receiving):
    send_sems, recv_sems, _ = sems
    nrel = NDEV - 1
    x, y, c = (lax.axis_index(a) for a in AXES)
    me = 4 * x + 2 * y + c
    copies = []
    for a in range(len(ins)):
        for k in range(1, NDEV):
            peer, pidx = _peer(k)
            src = ins[a].at[pidx] if scatter else ins[a]
            copies.append(pltpu.make_async_remote_copy(
                src_ref=src, dst_ref=outs[a].at[pidx if receiving else me], send_sem=send_sems.at[a * nrel + k - 1],
                recv_sem=recv_sems.at[a * nrel + k - 1], device_id=peer, device_id_type=pl.DeviceIdType.MESH))
    return copies


def _exchange_local(ins, outs, sems, scatter):
    x, y, c = (lax.axis_index(a) for a in AXES)
    me = 4 * x + 2 * y + c
    return [pltpu.make_async_copy(ins[a].at[me] if scatter else ins[a], outs[a].at[me], sems[2].at[a])
            for a in range(len(ins))]


CHIP_RELATIONS = (2, 4, 6)
SIBLING = 1


def _gather_copy(ins, outs, sems, a, slot, src, block, to):
    nrel = NDEV - 1
    return pltpu.make_async_remote_copy(
        src_ref=src, dst_ref=outs[a].at[block], send_sem=sems[0].at[a * nrel + slot],
        recv_sem=sems[1].at[a * nrel + slot], device_id=to, device_id_type=pl.DeviceIdType.MESH)


def _exchange_start(ins, outs, sems, scatter):
    for cp in _exchange_local(ins, outs, sems, scatter):
        cp.start()
    if scatter:
        for cp in _exchange_copies(ins, outs, sems, scatter, False):
            cp.start()
        return
    x, y, c = (lax.axis_index(a) for a in AXES)
    me = 4 * x + 2 * y + c
    for a in range(len(ins)):
        for slot, k in enumerate((SIBLING,) + CHIP_RELATIONS):
            _gather_copy(ins, outs, sems, a, slot, ins[a], me, _peer(k)[0]).start()


def _exchange_wait(ins, outs, sems, scatter):
    if scatter:
        for cp in _exchange_copies(ins, outs, sems, scatter, True):
            cp.wait_recv()
            cp.wait_send()
    else:
        x, y, c = (lax.axis_index(a) for a in AXES)
        me = 4 * x + 2 * y + c
        sibling, sibling_idx = _peer(SIBLING)
        n = len(ins)
        passed = []
        for a in range(n):
            for j, k in enumerate(CHIP_RELATIONS):
                peer, pidx = _peer(k)
                _gather_copy(ins, outs, sems, a, 1 + j, ins[a], pidx, peer).wait_recv()
                fwd = _gather_copy(ins, outs, sems, a, 4 + j, outs[a].at[pidx], pidx, sibling)
                fwd.start()
                passed.append(fwd)
        for a in range(n):
            _gather_copy(ins, outs, sems, a, 0, ins[a], sibling_idx, sibling).wait_recv()
            for j, k in enumerate(CHIP_RELATIONS):
                _gather_copy(ins, outs, sems, a, 4 + j, ins[a], _peer(k | SIBLING)[1], sibling).wait_recv()
        for a in range(n):
            for slot, k in enumerate((SIBLING,) + CHIP_RELATIONS):
                _gather_copy(ins, outs, sems, a, slot, ins[a], me, _peer(k)[0]).wait_send()
        for fwd in passed:
            fwd.wait_send()
    for cp in _exchange_local(ins, outs, sems, scatter):
        cp.wait()


def _exchange(name, xs, scatter):
    n = len(xs)

    def body(*refs):
        ins, outs, sems = refs[:n], refs[n:2 * n], refs[2 * n:]
        _exchange_start(ins, outs, sems, scatter)
        _exchange_wait(ins, outs, sems, scatter)

    anyspec = pl.BlockSpec(memory_space=pl.ANY)
    outs = pl.pallas_call(
        body, out_shape=tuple(_exchange_out_shapes(xs, scatter)), in_specs=[anyspec] * n, out_specs=(anyspec,) * n,
        scratch_shapes=_exchange_scratch(n), name=name)(*xs)
    return list(outs)


def _all_gather(name, xs):
    return _exchange(name, xs, False)


def _all_to_all(name, xs):
    return _exchange(name, xs, True)


def _rope_table(cfg):
    s, lc = cfg.S, cfg.LC
    rows = s // GRID_W
    r_ids, c_ids = jnp.meshgrid(jnp.arange(rows), jnp.arange(GRID_W), indexing="ij")
    r_ids = r_ids.reshape(-1).astype(f32)
    c_ids = c_ids.reshape(-1).astype(f32)
    freqs = ROPE_THETA ** (-jnp.arange(0, AXIS_DIM, 2, dtype=f32) / AXIS_DIM)
    ang_r = r_ids[:, None] * freqs
    ang_c = c_ids[:, None] * freqs
    cos = jnp.concatenate([jnp.cos(ang_r), jnp.cos(ang_r), jnp.cos(ang_c), jnp.cos(ang_c)], axis=1)
    sin = jnp.concatenate([-jnp.sin(ang_r), jnp.sin(ang_r), -jnp.sin(ang_c), jnp.sin(ang_c)], axis=1)
    cos = jnp.concatenate([jnp.ones((lc, HEAD_DIM), f32), cos], axis=0)
    sin = jnp.concatenate([jnp.zeros((lc, HEAD_DIM), f32), sin], axis=0)
    reps = QKW // HEAD_DIM
    return jnp.stack([jnp.tile(cos, (1, reps)), jnp.tile(sin, (1, reps))])


def _block_diag_ones():
    idx = np.arange(QKW) // HEAD_DIM
    return jnp.asarray((idx[:, None] == idx[None, :]).astype(np.float32))


def _replicate_matrix():
    src = np.arange(KVW)
    dst = np.arange(AW)
    m = (src[:, None] // HEAD_DIM == dst[None, :] // GW) & (src[:, None] % HEAD_DIM == dst[None, :] % HEAD_DIM)
    return m.astype(np.float32)


def _pack_flat(parts):
    flat = jnp.concatenate([p.reshape(-1) for p in parts])
    n = flat.shape[0]
    unit = SUBLANE * LANE
    total = -(-n // unit) * unit
    flat = jnp.pad(flat, (0, total - n))
    return flat.reshape(total // LANE, LANE)


def _unpack_flat(flat, shapes):
    flat = flat.reshape(flat.shape[:-2] + (-1,))
    out, off = [], 0
    for shp in shapes:
        size = int(np.prod(shp))
        out.append(flat[..., off:off + size].reshape(flat.shape[:-1] + tuple(shp)))
        off += size
    return out


def kernel(x, c, ctx, c_ctx, w_ada, b_ada, w_in, q_norm, k_norm, w_attn_o, conf_dw_w, conf_dw_b, conf_ln_g, conf_ln_b, w_conf_out, sc_dw_w, w_sc_out, w_mix_out, w_ffn_in, w_ffn_out, loss_target, m_c_ctx, m_w_ada, m_b_ada, m_w_in, m_q_norm, m_k_norm, m_w_attn_o, m_conf_dw_w, m_conf_dw_b, m_conf_ln_g, m_conf_ln_b, m_w_conf_out, m_sc_dw_w, m_w_sc_out, m_w_mix_out, m_w_ffn_in, m_w_ffn_out, v_c_ctx, v_w_ada, v_b_ada, v_w_in, v_q_norm, v_k_norm, v_w_attn_o, v_conf_dw_w, v_conf_dw_b, v_conf_ln_g, v_conf_ln_b, v_w_conf_out, v_sc_dw_w, v_w_sc_out, v_w_mix_out, v_w_ffn_in, v_w_ffn_out):
    cfg = _Cfg()
    cfg.B, cfg.S, cfg.D = x.shape
    cfg.LC = ctx.shape[1]
    cfg.T = cfg.LC + cfg.S
    cfg.TR = min(256, cfg.LC)
    assert cfg.LC % cfg.TR == 0 and cfg.S % cfg.TR == 0 and cfg.S % GRID_W == 0
    cfg.NT, cfg.NC = cfg.T // cfg.TR, cfg.LC // cfg.TR
    cfg.R = cfg.B * cfg.T
    nl = w_in.shape[0]
    b, d = cfg.B, cfg.D
    cfg.CW = conf_dw_b.shape[1]
    cfg.K31, cfg.K3 = conf_dw_w.shape[1], sc_dw_w.shape[1]
    assert w_sc_out.shape[1] == cfg.CW and cfg.CW % LANE == 0
    assert cfg.K3 // 2 <= cfg.K31 // 2 <= CONV_PAD_ROWS and cfg.LC % CONV_CHUNK == 0 and cfg.S % CONV_CHUNK == 0
    cfg.CB = LANE
    cfg.GB = 2 * LANE if d % (2 * LANE) == 0 else LANE
    cfg.OFF_GATE = QKVW
    cfg.OFF_CONV = QKVW + N_BRANCHES * d
    n_in = w_in.shape[2] * NDEV
    assert n_in == cfg.OFF_CONV + CONV_PARTS * cfg.CW and w_attn_o.shape[1] == AW
    assert cfg.OFF_GATE % (N_BRANCHES * cfg.GB) == 0 and cfg.OFF_CONV % (CONV_PARTS * cfg.CB) == 0

    def to_kernel_layout(w):
        conv = w[QKVW:QKVW + CONV_PARTS * cfg.CW].reshape((CONV_PARTS, cfg.CW // cfg.CB, cfg.CB) + w.shape[1:])
        gate = w[QKVW + CONV_PARTS * cfg.CW:].reshape((N_BRANCHES, d // cfg.GB, cfg.GB) + w.shape[1:])
        return jnp.concatenate([w[:QKVW], jnp.swapaxes(gate, 0, 1).reshape((-1,) + w.shape[1:]),
                                jnp.swapaxes(conv, 0, 1).reshape((-1,) + w.shape[1:])], axis=0)

    def from_kernel_layout(g):
        gate = g[cfg.OFF_GATE:cfg.OFF_CONV].reshape((d // cfg.GB, N_BRANCHES, cfg.GB) + g.shape[1:])
        conv = g[cfg.OFF_CONV:].reshape((cfg.CW // cfg.CB, CONV_PARTS, cfg.CB) + g.shape[1:])
        return jnp.concatenate([g[:QKVW], jnp.swapaxes(conv, 0, 1).reshape((-1,) + g.shape[1:]),
                                jnp.swapaxes(gate, 0, 1).reshape((-1,) + g.shape[1:])], axis=0)

    fh = w_ffn_out.shape[1] * NDEV
    na = w_ada.shape[2]
    cw8 = cfg.CW // NDEV

    xi, yi, ci = (lax.axis_index(a) for a in AXES)
    me = 4 * xi + 2 * yi + ci

    small_shapes = [c.shape, conf_dw_w.shape, sc_dw_w.shape]
    (g0,) = _all_gather("gather_small", [_pack_flat([c, conf_dw_w, sc_dw_w])])
    c_all, cw_all, sw_all = _unpack_flat(g0, small_shapes)
    nb = NDEV * b
    nbp = -(-(nb + 1) // SUBLANE) * SUBLANE
    call = jnp.concatenate([c_all.reshape(nb, d), c_ctx[None, :], jnp.zeros((nbp - nb - 1, d), f32)], axis=0)
    w31_full = jnp.moveaxis(cw_all, 0, 2).reshape(nl, cfg.K31, cfg.CW)
    w3_full = jnp.moveaxis(sw_all, 0, 2).reshape(nl, cfg.K3, cfg.CW)
    k31p = -(-(cfg.K31 + 1) // SUBLANE) * SUBLANE
    w31b = jnp.concatenate([w31_full, conf_dw_b[:, None, :], jnp.zeros((nl, k31p - cfg.K31 - 1, cfg.CW), f32)], axis=1)
    w3p = jnp.concatenate([w3_full, jnp.zeros((nl, SUBLANE - cfg.K3, cfg.CW), f32)], axis=1)

    b_loc = lax.dynamic_slice(b_ada, (0, me * na), (nl, na))[:, None, :]
    mod_loc = _ada_fwd("ada_fwd", call, w_ada, b_loc)
    (mod_g,) = _all_gather("gather_mod", [mod_loc.reshape(nl * nbp, na)])
    mod_full = jnp.transpose(mod_g.reshape(NDEV, nl, nbp, na), (1, 2, 0, 3)).reshape(nl, nbp, N_MOD, d)
    mod_lat = lax.dynamic_slice_in_dim(mod_full, me * b, b, axis=1)
    mod_ctx = jnp.broadcast_to(mod_full[:, nb][:, None], (nl, b, N_MOD, d))
    mod = jnp.stack([mod_ctx, mod_lat], axis=2)
    mod = jnp.pad(mod, ((0, 0), (0, 0), (0, 0), (0, MOD_ROWS - N_MOD), (0, 0)))

    cs = _rope_table(cfg)
    bd = _block_diag_ones().astype(bf16)
    rep_np = _replicate_matrix()
    rep = jnp.asarray(rep_np, dtype=bf16)
    rept = jnp.asarray(rep_np.T, dtype=bf16)
    reps_q, reps_k = AW // HEAD_DIM, KVW // HEAD_DIM

    xin = jnp.concatenate([ctx, x], axis=1).reshape(cfg.R, d)
    target = loss_target.reshape(b * cfg.S, d)
    saved = []
    xcur, ocur = xin, None

    def weight_shards(l):
        return [jnp.transpose(w_in[l]).astype(bf16), jnp.transpose(w_attn_o[l]).astype(bf16),
                jnp.transpose(w_conf_out[l]).astype(bf16), jnp.transpose(w_sc_out[l]).astype(bf16),
                w_mix_out[l].astype(bf16), jnp.transpose(w_ffn_in[l]).astype(bf16), w_ffn_out[l].astype(bf16)]

    def whole(g):
        return g.reshape(NDEV * g.shape[1], g.shape[2])

    shards0 = weight_shards(0)
    (w_first,) = _all_gather("gather_w0", shards0[:1])
    w_rest = None
    for l in range(nl):
        win_t = to_kernel_layout(whole(w_first))
        gain = jnp.concatenate([jnp.tile(q_norm[l], reps_q), jnp.tile(k_norm[l], reps_k)])[None, :]
        lnp = jnp.concatenate([conf_ln_g[l][None], conf_ln_b[l][None], jnp.zeros((SUBLANE - 2, cfg.CW), f32)], axis=0)
        nxt = weight_shards(l + 1) if l + 1 < nl else None

        if l == 0:
            x0, h = _norm_fwd(cfg, f"norm_a{l}", xcur, None, None, 0, mod[l], 0)
            p, w_rest = _mm(h, win_t, "nt", ACT, f"mm_in{l}", comm=(shards0[1:], False))
        else:
            x0, h = _norm_fwd(cfg, f"norm_a{l}", xcur, ocur, mod[l - 1], 5, mod[l], 0)
            p = _mm(h, win_t, "nt", ACT, f"mm_in{l}")
        wao_t, wco_t, wso_t, wmix, wfi_t, wfo = [whole(g) for g in w_rest]
        q, krep, vrep = _qkv_fwd(cfg, f"qkv{l}", p, cs, gain, bd, rep)
        (attn, lse), w_rest = _attn_fwd(cfg, f"attn{l}", q, krep, vrep,
                                        comm=None if nxt is None else (nxt[1:], False))
        hc, z = _conv_fwd(cfg, f"conv{l}", p, w31b[l], w3p[l])
        hs = _ln_silu_fwd(cfg, f"lnsilu{l}", hc, lnp)
        merged, ya, yc, ys = _merge_fwd(cfg, f"merge{l}", p, attn, hs, z, wao_t, wco_t, wso_t)
        mixed = _mm(merged, wmix, "nn", f32, f"mm_mix{l}")
        x1, h2 = _norm_fwd(cfg, f"norm_b{l}", x0, mixed, mod[l], 2, mod[l], 3)
        if nxt is None:
            ff = _mm(h2, wfi_t, "nt", ACT, f"mm_fi{l}")
        else:
            ff, (w_first,) = _mm(h2, wfi_t, "nt", ACT, f"mm_fi{l}", comm=(nxt[:1], False))
        sw = _swiglu_fwd(cfg, f"swiglu{l}", ff)
        o = _mm(sw, wfo, "nn", f32, f"mm_fo{l}")
        saved.append(dict(x0=x0, h=h, p=p, q=q, krep=krep, vrep=vrep, attn=attn, lse=lse, hc=hc, z=z, hs=hs,
                          ya=ya, yc=yc, ys=ys, merged=merged, mixed=mixed, x1=x1, h2=h2, ff=ff, sw=sw, o=o,
                          oprev=ocur, gain=gain, lnp=lnp,
                          w=(win_t, wao_t, wco_t, wso_t, wmix, wfi_t, wfo)))
        xcur, ocur = x1, o

    lsum, dres, do, dm_loss = _loss_head(cfg, "loss", xcur, ocur, mod[nl - 1], 5, target)
    loss = lax.psum((0.5 / d) * jnp.sum(lsum), AXES)

    dmod_rows = [[None] * N_MOD for _ in range(nl)]
    dmod_rows[nl - 1][5] = dm_loss[:, :, 2]
    g_qn, g_kn, g_w31, g_w3, g_ln = [None] * nl, [None] * nl, [None] * nl, [None] * nl, [None] * nl
    gbig = [None] * nl
    pending = None
    pending_early = None

    def blocks(g):
        return g.reshape(NDEV, g.shape[0] // NDEV, g.shape[1])

    def sum_received(l, recv):
        gbig[l] = [_sum_slots(f"sum_g{l}_{i}", r) for i, r in enumerate(recv)]

    for l in reversed(range(nl)):
        sv = saved[l]
        win_t, wao_t, wco_t, wso_t, wmix, wfi_t, wfo = sv["w"]
        ds = _mm(do, wfo, "nt", ACT, f"mm_dsw{l}")
        g_wfo = _mm(sv["sw"], do, "tn", bf16, f"mm_gfo{l}")
        dff = _swiglu_bwd(cfg, f"swiglu_b{l}", sv["ff"], ds)
        dh2 = _mm(dff, wfi_t, "nn", GRAD, f"mm_dh2{l}")
        g_wfi = _mm(dff, sv["h2"], "tn", bf16, f"mm_gfi{l}")
        dx0p, dmixed, dm_b = _norm_bwd(cfg, f"norm_b_b{l}", sv["x1"], sv["mixed"], dh2, dres, mod[l], 2, mod[l], 3)
        dmod_rows[l][3], dmod_rows[l][4], dmod_rows[l][2] = dm_b[:, :, 0], dm_b[:, :, 1], dm_b[:, :, 2]
        dmerged = _mm(dmixed, wmix, "nt", GRAD, f"mm_dmg{l}")
        g_wmix = _mm(sv["merged"], dmixed, "tn", bf16, f"mm_gmix{l}")
        dya, dyc, dys, dp = _gate_bwd(cfg, f"gate_b{l}", sv["p"], sv["ya"], sv["yc"], sv["ys"], dmerged)
        dattn = _mm(dya, wao_t, "nn", GRAD, f"mm_dat{l}")
        g_wao = _mm(dya, sv["attn"], "tn", bf16, f"mm_gao{l}")
        dhs = _mm(dyc, wco_t, "nn", GRAD, f"mm_dhs{l}")
        g_wco = _mm(dyc, sv["hs"], "tn", bf16, f"mm_gco{l}")
        dz = _mm(dys, wso_t, "nn", GRAD, f"mm_dz{l}")
        g_wso = _mm(dys, sv["z"], "tn", bf16, f"mm_gso{l}")
        dhc, g_ln[l] = _ln_silu_bwd(cfg, f"lnsilu_b{l}", sv["hc"], dhs, sv["lnp"])
        (dp, g_w31[l], g_w3[l]), recv = _conv_bwd(
            cfg, f"conv_b{l}", sv["p"], dp, dhc, dz, w31b[l], w3p[l],
            comm=None if pending is None else ([pending], True))
        if pending is not None:
            sum_received(l + 1, recv + pending_early)
        early = [blocks(g) for g in (g_wao, g_wco, g_wso, g_wmix, g_wfi, g_wfo)]
        (dq, dkr, dvr), pending_early = _attn_bwd(cfg, f"attn_b{l}", sv["q"], sv["krep"], sv["vrep"], sv["attn"],
                                                  sv["lse"], dattn, comm=(early, True))
        dp, dgain = _qkv_bwd(cfg, f"qkv_b{l}", sv["p"], dp, dq, dkr, dvr, cs, sv["gain"], bd, rept)
        g_qn[l] = jnp.sum(dgain[0, :AW].reshape(reps_q, HEAD_DIM), axis=0)
        g_kn[l] = jnp.sum(dgain[0, AW:].reshape(reps_k, HEAD_DIM), axis=0)
        g_win = _mm(dp, sv["h"], "tn", bf16, f"mm_gin{l}")
        pending = blocks(from_kernel_layout(g_win))
        if l == 0:
            dh, recv = _mm(dp, win_t, "nn", GRAD, f"mm_dh{l}", comm=([pending], True))
            sum_received(0, recv + pending_early)
            dres, _, dm_a = _norm_bwd(cfg, f"norm_a_b{l}", sv["x0"], None, dh, dx0p, None, 0, mod[l], 0)
        else:
            dh = _mm(dp, win_t, "nn", GRAD, f"mm_dh{l}")
            dres, do, dm_a = _norm_bwd(cfg, f"norm_a_b{l}", sv["x0"], sv["oprev"], dh, dx0p, mod[l - 1], 5, mod[l], 0)
            dmod_rows[l - 1][5] = dm_a[:, :, 2]
        dmod_rows[l][0], dmod_rows[l][1] = dm_a[:, :, 0], dm_a[:, :, 1]

    grad_x = dres.reshape(b, cfg.T, d)[:, cfg.LC:, :]

    rb = -(-(b + 1) // SUBLANE) * SUBLANE
    dmod_l = jnp.stack([jnp.stack(rows, axis=2) for rows in dmod_rows])
    d_lat = dmod_l[:, :, 1].reshape(nl, b, NDEV, na)
    d_ctx = jnp.sum(dmod_l[:, :, 0], axis=1).reshape(nl, 1, NDEV, na)
    send = jnp.concatenate([d_lat, d_ctx, jnp.zeros((nl, rb - b - 1, NDEV, na), f32)], axis=1)
    send = jnp.transpose(send, (2, 0, 1, 3)).reshape(NDEV, nl * rb, na)
    (recv,) = _all_to_all("scatter_dmod", [send])
    recv = recv.reshape(NDEV, nl, rb, na)
    dmod_ex = jnp.transpose(recv[:, :, :b], (1, 0, 2, 3)).reshape(nl, nb, na)
    dmod_ex = jnp.pad(dmod_ex, ((0, 0), (0, nbp - nb), (0, 0)))
    dmodc = jnp.transpose(recv[:, :, b], (1, 0, 2))
    g_wada, g_bloc, g_cctx = _ada_bwd("ada_bwd", call, w_ada, dmod_ex, dmodc, nb)

    g_bada = lax.dynamic_update_slice(jnp.zeros((nl, N_MOD * d), f32), g_bloc[:, 0], (0, me * na))
    g_w31s = jnp.stack(g_w31)
    small_parts = [g_cctx[0], g_bada, jnp.stack(g_qn), jnp.stack(g_kn), g_w31s[:, :cfg.K31], g_w31s[:, cfg.K31],
                   jnp.stack([g[0] for g in g_ln]), jnp.stack([g[1] for g in g_ln]),
                   jnp.stack(g_w3)[:, :cfg.K3]]
    small_part_shapes = [p_.shape for p_ in small_parts]
    (gs,) = _all_gather("gather_gsmall", [_pack_flat(small_parts)])
    gsum = _sum_slots("sum_gsmall", gs)
    (gr_cctx, gr_bada, gr_qn, gr_kn, gr_w31, gr_b31, gr_lng, gr_lnb, gr_w3) = _unpack_flat(gsum, small_part_shapes)
    gr_w31 = lax.dynamic_slice_in_dim(gr_w31, me * cw8, cw8, axis=2)
    gr_w3 = lax.dynamic_slice_in_dim(gr_w3, me * cw8, cw8, axis=2)

    def big(i, transposed):
        g = jnp.stack([gbig[l][i] for l in range(nl)])
        return jnp.transpose(g, (0, 2, 1)) if transposed else g

    grads = {
        "c_ctx": gr_cctx, "w_ada": g_wada, "b_ada": gr_bada, "w_in": big(0, True), "q_norm": gr_qn, "k_norm": gr_kn,
        "w_attn_o": big(1, True), "conf_dw_w": gr_w31, "conf_dw_b": gr_b31, "conf_ln_g": gr_lng, "conf_ln_b": gr_lnb,
        "w_conf_out": big(2, True), "sc_dw_w": gr_w3, "w_sc_out": big(3, True), "w_mix_out": big(4, False),
        "w_ffn_in": big(5, True), "w_ffn_out": big(6, False)}
    weights = dict(c_ctx=c_ctx, w_ada=w_ada, b_ada=b_ada, w_in=w_in, q_norm=q_norm, k_norm=k_norm, w_attn_o=w_attn_o,
                   conf_dw_w=conf_dw_w, conf_dw_b=conf_dw_b, conf_ln_g=conf_ln_g, conf_ln_b=conf_ln_b,
                   w_conf_out=w_conf_out, sc_dw_w=sc_dw_w, w_sc_out=w_sc_out, w_mix_out=w_mix_out,
                   w_ffn_in=w_ffn_in, w_ffn_out=w_ffn_out)
    m_in = dict(c_ctx=m_c_ctx, w_ada=m_w_ada, b_ada=m_b_ada, w_in=m_w_in, q_norm=m_q_norm, k_norm=m_k_norm,
                w_attn_o=m_w_attn_o, conf_dw_w=m_conf_dw_w, conf_dw_b=m_conf_dw_b, conf_ln_g=m_conf_ln_g,
                conf_ln_b=m_conf_ln_b, w_conf_out=m_w_conf_out, sc_dw_w=m_sc_dw_w, w_sc_out=m_w_sc_out,
                w_mix_out=m_w_mix_out, w_ffn_in=m_w_ffn_in, w_ffn_out=m_w_ffn_out)
    v_in = dict(c_ctx=v_c_ctx, w_ada=v_w_ada, b_ada=v_b_ada, w_in=v_w_in, q_norm=v_q_norm, k_norm=v_k_norm,
                w_attn_o=v_w_attn_o, conf_dw_w=v_conf_dw_w, conf_dw_b=v_conf_dw_b, conf_ln_g=v_conf_ln_g,
                conf_ln_b=v_conf_ln_b, w_conf_out=v_w_conf_out, sc_dw_w=v_sc_dw_w, w_sc_out=v_w_sc_out,
                w_mix_out=v_w_mix_out, w_ffn_in=v_w_ffn_in, w_ffn_out=v_w_ffn_out)
    names = list(weights)
    big_names = ("w_ada", "w_in", "w_attn_o", "w_conf_out", "w_sc_out", "w_mix_out", "w_ffn_in", "w_ffn_out")
    small_names = [n for n in names if n not in big_names]
    delta, new_m, new_v = {}, {}, {}
    for n in big_names:
        shp = weights[n].shape
        two_d = (shp[0] * shp[1], shp[2])
        dl, nm, nv = _adamw(f"adamw_{n}", weights[n].reshape(two_d), grads[n].reshape(two_d),
                            m_in[n].reshape(two_d), v_in[n].reshape(two_d))
        delta[n], new_m[n], new_v[n] = dl.reshape(shp), nm.reshape(shp), nv.reshape(shp)
    sshapes = [weights[n].shape for n in small_names]
    dl, nm, nv = _adamw("adamw_small", _pack_flat([weights[n] for n in small_names]),
                        _pack_flat([grads[n] for n in small_names]), _pack_flat([m_in[n] for n in small_names]),
                        _pack_flat([v_in[n] for n in small_names]))
    for n, a_, b_, c_ in zip(small_names, _unpack_flat(dl, sshapes), _unpack_flat(nm, sshapes), _unpack_flat(nv, sshapes)):
        delta[n], new_m[n], new_v[n] = a_, b_, c_

    return (loss, grad_x, *[grads[n] for n in names], *[delta[n] for n in names],
            *[new_m[n] for n in names], *[new_v[n] for n in names])
```

```python
import functools

import numpy as np
import jax
import jax.numpy as jnp
from jax import lax
from jax.experimental import pallas as pl
from jax.experimental.pallas import tpu as pltpu

f32 = jnp.float32
bf16 = jnp.bfloat16
ACT = bf16
GRAD = bf16

NDEV = 8
AXES = ("x", "y", "c")
HEAD_DIM = 64
N_Q_HEADS = 8
N_KV_HEADS = 2
Q_GROUP = N_Q_HEADS // N_KV_HEADS
AW = N_Q_HEADS * HEAD_DIM
KVW = N_KV_HEADS * HEAD_DIM
GW = Q_GROUP * HEAD_DIM
QKW = AW + KVW
QKVW = AW + 2 * KVW
GRID_W = 64
AXIS_DIM = HEAD_DIM // 2
ROPE_THETA = 10000.0
ATTN_SCALE = HEAD_DIM ** -0.5
EPS = 1e-6
N_MOD = 6
N_BRANCHES = 3
CONV_PARTS = 5
MOD_ROWS = 8
CONV_PAD_ROWS = 16
GATE_ROWS = 1024
HEADS_PER_PASS_FWD = 1
HEADS_PER_PASS_BWD = 2
CONV_CHUNK = 64
LANE = 128
SUBLANE = 8
VMEM_LIMIT = 56 * 1024 * 1024
MM_VMEM_BUDGET = 40 * 1024 * 1024

ADAM_LR = 0.001
ADAM_B1 = 0.9
ADAM_B2 = 0.999
ADAM_EPS = 1e-08
ADAM_WD = 0.01
ADAM_STEP = 10

NN =(((1,), (0,)), ((), ()))
NT_ = (((1,), (1,)), ((), ()))
TN = (((0,), (0,)), ((), ()))


def _params(ndims, vmem=None):
    return pltpu.CompilerParams(dimension_semantics=("arbitrary",) * ndims, vmem_limit_bytes=vmem)


def _divisor(n, cap, mult):
    best = None
    for d in range(mult, min(n, cap) + 1, mult):
        if n % d == 0:
            best = d
    return n if best is None else best


def _const_spec(a):
    nd = a.ndim
    return pl.BlockSpec(a.shape, lambda *idx: (0,) * nd)


class _Cfg:
    pass


def _mm(a, b, mode, out_dtype, name, comm=None):
    def accumulate(prod, o_ref, scratch, nsteps):
        if nsteps == 1:
            o_ref[...] = prod.astype(out_dtype)
            return
        acc_ref = scratch[0]
        k = pl.program_id(2)

        @pl.when(k == 0)
        def _():
            acc_ref[...] = prod

        if nsteps > 2:
            @pl.when(jnp.logical_and(k > 0, k < nsteps - 1))
            def _():
                acc_ref[...] += prod

        @pl.when(k == nsteps - 1)
        def _():
            o_ref[...] = (acc_ref[...] + prod).astype(out_dtype)

    def fits(*tiles):
        return sum(2 * r * c * s for r, c, s in tiles[:-1]) + tiles[-1][0] * tiles[-1][1] * 4 <= MM_VMEM_BUDGET

    osz = jnp.dtype(out_dtype).itemsize
    if mode == "tn":
        m, ka = a.shape
        nb = b.shape[1]
        tka = _divisor(ka, 1408, LANE)
        tnb = _divisor(nb, 1024, LANE)
        tmr = SUBLANE
        for cand in range(SUBLANE, min(m, 2304) + 1, SUBLANE):
            if m % cand == 0 and fits((cand, tka, a.dtype.itemsize), (cand, tnb, b.dtype.itemsize),
                                      (tka, tnb, osz), (tka, tnb, 4)):
                tmr = cand
        nsteps = m // tmr

        def body(a_ref, b_ref, o_ref, *scratch):
            prod = lax.dot_general(a_ref[...].astype(bf16), b_ref[...].astype(bf16), TN, preferred_element_type=f32)
            accumulate(prod, o_ref, scratch, nsteps)

        return pl.pallas_call(
            body, out_shape=jax.ShapeDtypeStruct((ka, nb), out_dtype),
            grid=(ka // tka, nb // tnb, nsteps),
            in_specs=[pl.BlockSpec((tmr, tka), lambda i, j, k: (k, i)),
                      pl.BlockSpec((tmr, tnb), lambda i, j, k: (k, j))],
            out_specs=pl.BlockSpec((tka, tnb), lambda i, j, k: (i, j)),
            scratch_shapes=[pltpu.VMEM((tka, tnb), f32)] if nsteps > 1 else [],
            compiler_params=_params(3, VMEM_LIMIT), name=name)(a, b)

    m, kdim = a.shape
    n = b.shape[1] if mode == "nn" else b.shape[0]
    tm = _divisor(m, 1024 if a.dtype == bf16 else 512, SUBLANE)
    tn = _divisor(n, 1408, LANE)
    tk = LANE if kdim % LANE == 0 else kdim
    for cand in range(LANE, min(kdim, 3328) + 1, LANE):
        if kdim % cand == 0 and fits((tm, cand, a.dtype.itemsize), (cand, tn, b.dtype.itemsize), (tm, tn, osz),
                                     (tm, tn, 4)):
            tk = cand
    nsteps = kdim // tk
    dims = NN if mode == "nn" else NT_

    def body(a_ref, b_ref, o_ref, *scratch):
        prod = lax.dot_general(a_ref[...].astype(bf16), b_ref[...].astype(bf16), dims, preferred_element_type=f32)
        accumulate(prod, o_ref, scratch, nsteps)

    if mode == "nn":
        b_spec = pl.BlockSpec((tk, tn), lambda i, j, k: (k, j))
    else:
        b_spec = pl.BlockSpec((tn, tk), lambda i, j, k: (j, k))
    (out,), couts = _hosting_call(
        body, comm, out_shape=(jax.ShapeDtypeStruct((m, n), out_dtype),),
        grid=(m // tm, n // tn, nsteps),
        in_specs=[pl.BlockSpec((tm, tk), lambda i, j, k: (i, k)), b_spec],
        out_specs=(pl.BlockSpec((tm, tn), lambda i, j, k: (i, j)),),
        scratch_shapes=[pltpu.VMEM((tm, tn), f32)] if nsteps > 1 else [],
        compiler_params=_params(3, VMEM_LIMIT), name=name, args=(a, b))
    return out if comm is None else (out, couts)


def _row_spec(cfg, width, col=0):
    return pl.BlockSpec((cfg.TR, width), lambda i: (i, col))


def _mod_spec(cfg):
    nt, nc = cfg.NT, cfg.NC
    return pl.BlockSpec((1, 1, MOD_ROWS, cfg.D),
                        lambda i: (i // nt, ((i % nt) >= nc).astype(jnp.int32), 0, 0))


def _segment_start(cfg, i):
    j = i % cfg.NT
    return jnp.logical_or(j == 0, j == cfg.NC)


def _norm_fwd(cfg, name, xin, o, modg, gk, modn, sk):
    has_o = o is not None
    r_, d = xin.shape

    def body(*refs):
        if has_o:
            x_ref, o_ref, mg_ref, mn_ref, xn_ref, h_ref = refs
            x = x_ref[...] + mg_ref[0, 0, gk:gk + 1, :] * o_ref[...]
            xn_ref[...] = x
        else:
            x_ref, mn_ref, h_ref = refs
            x = x_ref[...]
        r = lax.rsqrt(jnp.mean(x * x, axis=-1, keepdims=True) + EPS)
        h = (x * r) * (1.0 + mn_ref[0, 0, sk + 1:sk + 2, :]) + mn_ref[0, 0, sk:sk + 1, :]
        h_ref[...] = h.astype(bf16)

    row = _row_spec(cfg, d)
    if has_o:
        ins, in_specs = (xin, o, modg, modn), [row, row, _mod_spec(cfg), _mod_spec(cfg)]
        out_shape = (jax.ShapeDtypeStruct((r_, d), f32), jax.ShapeDtypeStruct((r_, d), bf16))
        out_specs = (row, row)
    else:
        ins, in_specs = (xin, modn), [row, _mod_spec(cfg)]
        out_shape = jax.ShapeDtypeStruct((r_, d), bf16)
        out_specs = row
    res = pl.pallas_call(body, out_shape=out_shape, grid=(r_ // cfg.TR,), in_specs=in_specs,
                         out_specs=out_specs, compiler_params=_params(1), name=name)(*ins)
    return res if has_o else (xin, res)


def _norm_bwd(cfg, name, xnew, o, dh, dres, modg, gk, modn, sk):
    has_o = o is not None
    r_, d = xnew.shape

    def body(*refs):
        if has_o:
            xn_ref, o_ref, dh_ref, dr_ref, mg_ref, mn_ref, dx_ref, do_ref, dm_ref = refs
        else:
            xn_ref, dh_ref, dr_ref, mn_ref, dx_ref, dm_ref = refs
        i = pl.program_id(0)
        x = xn_ref[...]
        dhv = dh_ref[...].astype(f32)
        r = lax.rsqrt(jnp.mean(x * x, axis=-1, keepdims=True) + EPS)
        xh = x * r
        dxh = dhv * (1.0 + mn_ref[0, 0, sk + 1:sk + 2, :])
        dx = r * (dxh - xh * jnp.mean(dxh * xh, axis=-1, keepdims=True)) + dr_ref[...]
        dx_ref[...] = dx

        @pl.when(_segment_start(cfg, i))
        def _():
            dm_ref[...] = jnp.zeros_like(dm_ref)

        dm_ref[0, 0, 0:1, :] += jnp.sum(dhv, axis=0, keepdims=True)
        dm_ref[0, 0, 1:2, :] += jnp.sum(dhv * xh, axis=0, keepdims=True)
        if has_o:
            ov = o_ref[...]
            dm_ref[0, 0, 2:3, :] += jnp.sum(dx * ov, axis=0, keepdims=True)
            do_ref[...] = (dx * mg_ref[0, 0, gk:gk + 1, :]).astype(bf16)

    row = _row_spec(cfg, d)
    dm_shape = jax.ShapeDtypeStruct((cfg.B, 2, MOD_ROWS, d), f32)
    if has_o:
        ins = (xnew, o, dh, dres, modg, modn)
        in_specs = [row, row, row, row, _mod_spec(cfg), _mod_spec(cfg)]
        out_shape = (jax.ShapeDtypeStruct((r_, d), f32), jax.ShapeDtypeStruct((r_, d), bf16), dm_shape)
        out_specs = (row, row, _mod_spec(cfg))
    else:
        ins = (xnew, dh, dres, modn)
        in_specs = [row, row, row, _mod_spec(cfg)]
        out_shape = (jax.ShapeDtypeStruct((r_, d), f32), dm_shape)
        out_specs = (row, _mod_spec(cfg))
    res = pl.pallas_call(body, out_shape=out_shape, grid=(r_ // cfg.TR,), in_specs=in_specs,
                         out_specs=out_specs, compiler_params=_params(1), name=name)(*ins)
    if has_o:
        return res
    return res[0], None, res[1]


def _loss_head(cfg, name, x1, o, modg, gk, target):
    r_, d = x1.shape
    nt, nc = cfg.NT, cfg.NC
    nlat = nt - nc

    def body(x_ref, o_ref, mg_ref, t_ref, ls_ref, dy_ref, do_ref, dm_ref):
        i = pl.program_id(0)
        lat = (i % nt) >= nc
        gate = mg_ref[0, 0, gk:gk + 1, :]
        ov = o_ref[...]
        err = jnp.where(lat, x_ref[...] + gate * ov - t_ref[...], 0.0)
        ls_ref[...] = jnp.zeros_like(ls_ref)
        ls_ref[0, 0:1, :] = jnp.sum(err * err, axis=0, keepdims=True)
        dy = err * (1.0 / d)
        dy_ref[...] = dy
        do_ref[...] = (dy * gate).astype(bf16)

        @pl.when(_segment_start(cfg, i))
        def _():
            dm_ref[...] = jnp.zeros_like(dm_ref)

        dm_ref[0, 0, 2:3, :] += jnp.sum(dy * ov, axis=0, keepdims=True)

    row = _row_spec(cfg, d)
    t_spec = pl.BlockSpec((cfg.TR, d), lambda i: ((i // nt) * nlat + jnp.maximum((i % nt) - nc, 0), 0))
    return pl.pallas_call(
        body,
        out_shape=(jax.ShapeDtypeStruct((r_ // cfg.TR, SUBLANE, d), f32), jax.ShapeDtypeStruct((r_, d), f32),
                   jax.ShapeDtypeStruct((r_, d), bf16), jax.ShapeDtypeStruct((cfg.B, 2, MOD_ROWS, d), f32)),
        grid=(r_ // cfg.TR,),
        in_specs=[row, row, _mod_spec(cfg), t_spec],
        out_specs=(pl.BlockSpec((1, SUBLANE, d), lambda i: (i, 0, 0)), row, row, _mod_spec(cfg)),
        compiler_params=_params(1), name=name)(x1, o, modg, target)


def _swap16(y):
    w = y.shape[1]
    lane = lax.broadcasted_iota(jnp.int32, (1, w), 1)
    lo = jnp.bitwise_and(lane, 31) < 16
    return jnp.where(lo, pltpu.roll(y, w - 16, 1), pltpu.roll(y, 16, 1))


def _dot_split(v, m):
    hi = v.astype(bf16)
    lo = (v - hi.astype(f32)).astype(bf16)
    return jnp.dot(hi, m, preferred_element_type=f32) + jnp.dot(lo, m, preferred_element_type=f32)


def _head_sum(v, bd):
    return _dot_split(v, bd)


def _qkv_fwd(cfg, name, p, cs, gain, bd, rep):
    r_ = p.shape[0]
    nt = cfg.NT

    def body(p_ref, cs_ref, g_ref, bd_ref, rep_ref, q_ref, k_ref, v_ref):
        x = p_ref[:, 0:QKW].astype(f32)
        r = lax.rsqrt(_head_sum(x * x, bd_ref[...]) * (1.0 / HEAD_DIM) + EPS)
        y = (x * r) * g_ref[...]
        out = y * cs_ref[0] + _swap16(y) * cs_ref[1]
        q_ref[...] = out[:, 0:AW].astype(bf16)
        kr = out[:, AW:QKW].astype(bf16)
        k_ref[...] = jnp.dot(kr, rep_ref[...], preferred_element_type=f32).astype(bf16)
        vv = p_ref[:, QKW:QKVW].astype(bf16)
        v_ref[...] = jnp.dot(vv, rep_ref[...], preferred_element_type=f32).astype(bf16)

    row = pl.BlockSpec((cfg.TR, AW), lambda j, e: (e * nt + j, 0))
    return pl.pallas_call(
        body, out_shape=(jax.ShapeDtypeStruct((r_, AW), bf16),) * 3, grid=(nt, cfg.B),
        in_specs=[pl.BlockSpec((cfg.TR, QKVW), lambda j, e: (e * nt + j, 0)),
                  pl.BlockSpec((2, cfg.TR, QKW), lambda j, e: (0, j, 0)),
                  _const_spec(gain), _const_spec(bd), _const_spec(rep)],
        out_specs=(row, row, row), compiler_params=_params(2), name=name)(p, cs, gain, bd, rep)


def _qkv_bwd(cfg, name, p, dp, dq, dkr, dvr, cs, gain, bd, rept):
    nt = cfg.NT

    def body(p_ref, dp_in, dq_ref, dk_ref, dv_ref, cs_ref, g_ref, bd_ref, rt_ref, dp_ref, dg_ref):
        del dp_in
        first = jnp.logical_and(pl.program_id(0) == 0, pl.program_id(1) == 0)
        x = p_ref[:, 0:QKW].astype(f32)
        rt = rt_ref[...]
        dk = _dot_split(dk_ref[...], rt)
        g = jnp.concatenate([dq_ref[...].astype(f32), dk], axis=1)
        gy = g * cs_ref[0] + _swap16(g * cs_ref[1])
        bdv = bd_ref[...]
        r = lax.rsqrt(_head_sum(x * x, bdv) * (1.0 / HEAD_DIM) + EPS)
        xh = x * r
        dxh = gy * g_ref[...]
        dx = r * (dxh - xh * (_head_sum(dxh * xh, bdv) * (1.0 / HEAD_DIM)))
        dp_ref[:, 0:QKW] = dx.astype(bf16)
        dp_ref[:, QKW:QKVW] = _dot_split(dv_ref[...], rt).astype(bf16)

        @pl.when(first)
        def _():
            dg_ref[...] = jnp.zeros_like(dg_ref)

        dg_ref[0:1, :] += jnp.sum(gy * xh, axis=0, keepdims=True)

    row = pl.BlockSpec((cfg.TR, AW), lambda j, e: (e * nt + j, 0))
    wide = pl.BlockSpec((cfg.TR, QKVW), lambda j, e: (e * nt + j, 0))
    return pl.pallas_call(
        body, out_shape=(jax.ShapeDtypeStruct(dp.shape, dp.dtype), jax.ShapeDtypeStruct((SUBLANE, QKW), f32)),
        grid=(nt, cfg.B),
        in_specs=[wide, pl.BlockSpec(memory_space=pl.ANY), row, row, row,
                  pl.BlockSpec((2, cfg.TR, QKW), lambda j, e: (0, j, 0)),
                  _const_spec(gain), _const_spec(bd), _const_spec(rept)],
        out_specs=(wide, pl.BlockSpec((SUBLANE, QKW), lambda j, e: (0, 0))), input_output_aliases={1: 0},
        compiler_params=_params(2), name=name)(p, dp, dq, dkr, dvr, cs, gain, bd, rept)


def _head_masks():
    lane = lax.broadcasted_iota(jnp.int32, (1, GW), 1)
    return [jnp.logical_and(lane >= HEAD_DIM * h, lane < HEAD_DIM * (h + 1)) for h in range(Q_GROUP)]


def _attn_fwd(cfg, name, q, krep, vrep, comm=None):
    r_ = q.shape[0]
    tr, t, lc, nt, nc = cfg.TR, cfg.T, cfg.LC, cfg.NT, cfg.NC

    def body(q_ref, k_ref, v_ref, o_ref, l_ref):
        j = pl.program_id(2)
        masks = _head_masks()
        lane = lax.broadcasted_iota(jnp.int32, (1, LANE), 1)

        def run(nk):
            qv = q_ref[...]
            kv = k_ref[0:nk, :]
            vv = v_ref[0:nk, :]
            acc = jnp.zeros((tr, GW), f32)
            lse = jnp.zeros((tr, LANE), f32)
            for h0 in range(0, Q_GROUP, HEADS_PER_PASS_FWD):
                hs = range(h0, h0 + HEADS_PER_PASS_FWD)
                qs = jnp.concatenate([jnp.where(masks[h], qv, jnp.zeros_like(qv)) for h in hs], axis=0)
                s = lax.dot_general(qs, kv, NT_, preferred_element_type=f32) * ATTN_SCALE
                m = jnp.max(s, axis=1, keepdims=True)
                pr = jnp.exp(s - m)
                l = jnp.sum(pr, axis=1, keepdims=True)
                os_ = jnp.dot(pr.astype(bf16), vv, preferred_element_type=f32) / l
                ls = m + jnp.log(l)
                for i, h in enumerate(hs):
                    acc = acc + jnp.where(masks[h], os_[i * tr:(i + 1) * tr], 0.0)
                    lse = lse + jnp.where(lane == h, ls[i * tr:(i + 1) * tr], 0.0)
            o_ref[...] = acc
            l_ref[...] = lse

        @pl.when(j < nc)
        def _():
            run(lc)

        @pl.when(j >= nc)
        def _():
            run(t)

    qspec = pl.BlockSpec((tr, GW), lambda e, g, j: (e * nt + j, g))
    kspec = pl.BlockSpec((t, GW), lambda e, g, j: (e, g))
    return _hosting_call(
        body, comm, out_shape=(jax.ShapeDtypeStruct((r_, AW), f32), jax.ShapeDtypeStruct((r_, N_KV_HEADS * LANE), f32)),
        grid=(cfg.B, N_KV_HEADS, nt), in_specs=[qspec, kspec, kspec],
        out_specs=(qspec, pl.BlockSpec((tr, LANE), lambda e, g, j: (e * nt + j, g))),
        compiler_params=_params(3, VMEM_LIMIT), name=name, args=(q, krep, vrep))


def _hosting_call(body, comm, *, out_shape, grid, in_specs, out_specs, compiler_params, name, args, scratch_shapes=(),
                  aliases=None):
    aliases = aliases or {}
    if comm is None:
        return pl.pallas_call(body, out_shape=out_shape, grid=grid, in_specs=in_specs, out_specs=out_specs,
                              scratch_shapes=list(scratch_shapes), input_output_aliases=aliases,
                              compiler_params=compiler_params, name=name)(*args), None
    xs, scatter = comm
    n, n_in, n_out, n_scr = len(xs), len(args), len(out_shape), len(scratch_shapes)
    last = [g - 1 for g in grid]

    def hosted(*refs):
        ins, cins = refs[:n_in], refs[n_in:n_in + n]
        outs, couts = refs[n_in + n:n_in + n + n_out], refs[n_in + n + n_out:n_in + 2 * n + n_out]
        scratch = refs[n_in + 2 * n + n_out:n_in + 2 * n + n_out + n_scr]
        sems = refs[n_in + 2 * n + n_out + n_scr:]
        ids = [pl.program_id(a) for a in range(len(grid))]
        is_first = functools.reduce(jnp.logical_and, [i == 0 for i in ids])
        is_last = functools.reduce(jnp.logical_and, [i == l for i, l in zip(ids, last)])

        @pl.when(is_first)
        def _():
            _exchange_start(cins, couts, sems, scatter)

        body(*ins, *outs, *scratch)

        @pl.when(is_last)
        def _():
            _exchange_wait(cins, couts, sems, scatter)

    anyspec = pl.BlockSpec(memory_space=pl.ANY)
    res = pl.pallas_call(
        hosted, out_shape=tuple(out_shape) + tuple(_exchange_out_shapes(xs, scatter)), grid=grid,
        in_specs=list(in_specs) + [anyspec] * n, out_specs=tuple(out_specs) + (anyspec,) * n,
        scratch_shapes=list(scratch_shapes) + _exchange_scratch(n), input_output_aliases=aliases,
        compiler_params=compiler_params, name=name)(*args, *xs)
    return tuple(res[:n_out]), list(res[n_out:])


def _attn_bwd(cfg, name, q, krep, vrep, o, lse, do, comm=None):
    r_ = q.shape[0]
    tr, t, lc, nt, nc = cfg.TR, cfg.T, cfg.LC, cfg.NT, cfg.NC

    def body(q_ref, k_ref, v_ref, o_ref, l_ref, do_ref, dq_ref, dk_ref, dv_ref):
        j = pl.program_id(2)
        masks = _head_masks()
        lane = lax.broadcasted_iota(jnp.int32, (1, LANE), 1)

        @pl.when(j == 0)
        def _():
            dk_ref[...] = jnp.zeros_like(dk_ref)
            dv_ref[...] = jnp.zeros_like(dv_ref)

        def run(nk):
            qv = q_ref[...]
            kv = k_ref[0:nk, :]
            vv = v_ref[0:nk, :]
            ov = o_ref[...]
            dov = do_ref[...].astype(f32)
            lv = l_ref[...]
            dq = jnp.zeros((tr, GW), f32)
            dk = jnp.zeros((nk, GW), f32)
            dv = jnp.zeros((nk, GW), f32)
            for h0 in range(0, Q_GROUP, HEADS_PER_PASS_BWD):
                hs = range(h0, h0 + HEADS_PER_PASS_BWD)
                qs = jnp.concatenate([jnp.where(masks[h], qv, jnp.zeros_like(qv)) for h in hs], axis=0)
                dos = jnp.concatenate([jnp.where(masks[h], dov, 0.0) for h in hs], axis=0)
                dosb = dos.astype(bf16)
                delta = jnp.sum(dos * jnp.concatenate([ov] * len(hs), axis=0), axis=1, keepdims=True)
                lses = jnp.concatenate([jnp.sum(jnp.where(lane == h, lv, 0.0), axis=1, keepdims=True) for h in hs],
                                       axis=0)
                s = lax.dot_general(qs, kv, NT_, preferred_element_type=f32) * ATTN_SCALE
                pr = jnp.exp(s - lses)
                dpr = lax.dot_general(dosb, vv, NT_, preferred_element_type=f32)
                ds = (pr * (dpr - delta) * ATTN_SCALE).astype(bf16)
                dqs = jnp.dot(ds, kv, preferred_element_type=f32)
                for i, h in enumerate(hs):
                    dq = dq + jnp.where(masks[h], dqs[i * tr:(i + 1) * tr], 0.0)
                dk = dk + lax.dot_general(ds, qs, TN, preferred_element_type=f32)
                dv = dv + lax.dot_general(pr.astype(bf16), dosb, TN, preferred_element_type=f32)
            dq_ref[...] = dq.astype(dq_ref.dtype)
            dk_ref[0:nk, :] += dk
            dv_ref[0:nk, :] += dv

        @pl.when(j < nc)
        def _():
            run(lc)

        @pl.when(j >= nc)
        def _():
            run(t)

    qspec = pl.BlockSpec((tr, GW), lambda e, g, j: (e * nt + j, g))
    kspec = pl.BlockSpec((t, GW), lambda e, g, j: (e, g))
    lspec = pl.BlockSpec((tr, LANE), lambda e, g, j: (e * nt + j, g))
    return _hosting_call(
        body, comm, out_shape=(jax.ShapeDtypeStruct((r_, AW), GRAD),) + (jax.ShapeDtypeStruct((r_, AW), f32),) * 2,
        grid=(cfg.B, N_KV_HEADS, nt), in_specs=[qspec, kspec, kspec, qspec, lspec, qspec],
        out_specs=(qspec, kspec, kspec),
        compiler_params=_params(3, VMEM_LIMIT), name=name, args=(q, krep, vrep, o, lse, do))


def _segments(cfg):
    pad = CONV_PAD_ROWS
    return ((pad, 0, cfg.LC), (cfg.LC + 3 * pad, cfg.LC, cfg.S)), cfg.T + 4 * pad


def _fill_padded(buf, cfg, fn):
    buf[...] = jnp.zeros_like(buf)
    for off, lo, n in _segments(cfg)[0]:
        buf[off:off + n, :] = fn(lo, lo + n)


def _taps(buf, w_ref, base, taps, sign):
    acc = None
    for k in range(taps):
        term = w_ref[k:k + 1, :] * buf[pl.ds(base + sign * (k - taps // 2), CONV_CHUNK), :]
        acc = term if acc is None else acc + term
    return acc


def _fold_tiles(v):
    acc = v[0:SUBLANE]
    for i in range(1, v.shape[0] // SUBLANE):
        acc = acc + v[SUBLANE * i:SUBLANE * (i + 1)]
    return acc


def _tap_grads(buf, dy, acc_ref, base, taps):
    for k in range(taps):
        acc_ref[SUBLANE * k:SUBLANE * (k + 1), :] += _fold_tiles(
            buf[pl.ds(base + k - taps // 2, CONV_CHUNK), :] * dy)


def _conv_spec(cfg):
    width = CONV_PARTS * cfg.CB
    return pl.BlockSpec((cfg.T, width), lambda c, e: (e, cfg.OFF_CONV // width + c))


def _conv_views(p_ref, cb):
    return [lambda rows, q=q: p_ref[rows, q * cb:(q + 1) * cb].astype(f32) for q in range(CONV_PARTS)]


def _conv_fwd(cfg, name, p, w31, w3):
    r_ = p.shape[0]
    t, cb, cw = cfg.T, cfg.CB, cfg.CW
    k31, k3 = cfg.K31, cfg.K3

    segs, buf_rows = _segments(cfg)
    ch = CONV_CHUNK

    def body(p_ref, w31_ref, w3_ref, hc_ref, z_ref, hbuf, ubuf):
        a, g, bg, cg, xs = _conv_views(p_ref, cb)
        _fill_padded(hbuf, cfg, lambda lo, hi: a(slice(lo, hi)) * jax.nn.sigmoid(g(slice(lo, hi))))
        _fill_padded(ubuf, cfg, lambda lo, hi: cg(slice(lo, hi)) * xs(slice(lo, hi)))
        bias = w31_ref[k31:k31 + 1, :]
        for off, lo, n in segs:
            def chunk(c, carry, off=off, lo=lo):
                base = off + pl.multiple_of(c * ch, ch)
                rows = pl.ds(pl.multiple_of(lo + c * ch, ch), ch)
                hc_ref[rows, :] = _taps(hbuf, w31_ref, base, k31, 1) + bias
                z_ref[rows, :] = (bg(rows) * _taps(ubuf, w3_ref, base, k3, 1)).astype(bf16)
                return carry

            lax.fori_loop(0, n // ch, chunk, 0)

    ospec = pl.BlockSpec((t, cb), lambda c, e: (e, c))
    return pl.pallas_call(
        body, out_shape=(jax.ShapeDtypeStruct((r_, cw), f32), jax.ShapeDtypeStruct((r_, cw), bf16)),
        grid=(cw // cb, cfg.B),
        in_specs=[_conv_spec(cfg), pl.BlockSpec((w31.shape[0], cb), lambda c, e: (0, c)),
                  pl.BlockSpec((w3.shape[0], cb), lambda c, e: (0, c))],
        out_specs=(ospec, ospec), scratch_shapes=[pltpu.VMEM((buf_rows, cb), f32)] * 2,
        compiler_params=_params(2, VMEM_LIMIT), name=name)(p, w31, w3)


def _conv_bwd(cfg, name, p, dp, dhc, dz, w31, w3, comm=None):
    t, cb, cw = cfg.T, cfg.CB, cfg.CW
    k31, k3 = cfg.K31, cfg.K3

    segs, buf_rows = _segments(cfg)
    ch = CONV_CHUNK

    def body(p_ref, dp_in, dhc_ref, dz_ref, w31_ref, w3_ref, dp_ref, dw31_ref, dw3_ref,
             hbuf, dbuf, ubuf, ebuf, acc31, acc3):
        del dp_in
        e = pl.program_id(1)
        a, g, bg, cg, xs = _conv_views(p_ref, cb)
        da_ref, dg_ref, dbg_ref, dcg_ref, dxs_ref = [dp_ref.at[:, q * cb:(q + 1) * cb] for q in range(CONV_PARTS)]

        @pl.when(e == 0)
        def _():
            dw31_ref[...] = jnp.zeros_like(dw31_ref)
            dw3_ref[...] = jnp.zeros_like(dw3_ref)

        _fill_padded(hbuf, cfg, lambda lo, hi: a(slice(lo, hi)) * jax.nn.sigmoid(g(slice(lo, hi))))
        _fill_padded(dbuf, cfg, lambda lo, hi: dhc_ref[lo:hi, :])
        _fill_padded(ubuf, cfg, lambda lo, hi: cg(slice(lo, hi)) * xs(slice(lo, hi)))
        _fill_padded(ebuf, cfg, lambda lo, hi: dz_ref[lo:hi, :].astype(f32) * bg(slice(lo, hi)))
        acc31[...] = jnp.zeros_like(acc31)
        acc3[...] = jnp.zeros_like(acc3)
        for off, lo, n in segs:
            def chunk(c, carry, off=off, lo=lo):
                base = off + pl.multiple_of(c * ch, ch)
                here = pl.ds(base, ch)
                rows = pl.ds(pl.multiple_of(lo + c * ch, ch), ch)
                dy = dbuf[here, :]
                dhh = _taps(dbuf, w31_ref, base, k31, -1)
                _tap_grads(hbuf, dy, acc31, base, k31)
                acc31[SUBLANE * k31:SUBLANE * (k31 + 1), :] += _fold_tiles(dy)
                sg = jax.nn.sigmoid(g(rows))
                da_ref[rows, :] = (dhh * sg).astype(bf16)
                dg_ref[rows, :] = (dhh * hbuf[here, :] * (1.0 - sg)).astype(bf16)

                duc = ebuf[here, :]
                dbg_ref[rows, :] = (dz_ref[rows, :].astype(f32) * _taps(ubuf, w3_ref, base, k3, 1)).astype(bf16)
                du = _taps(ebuf, w3_ref, base, k3, -1)
                _tap_grads(ubuf, duc, acc3, base, k3)
                dcg_ref[rows, :] = (du * xs(rows)).astype(bf16)
                dxs_ref[rows, :] = (du * cg(rows)).astype(bf16)
                return carry

            lax.fori_loop(0, n // ch, chunk, 0)
        for k in range(k31 + 1):
            dw31_ref[k:k + 1, :] += jnp.sum(acc31[SUBLANE * k:SUBLANE * (k + 1), :], axis=0, keepdims=True)
        for k in range(k3):
            dw3_ref[k:k + 1, :] += jnp.sum(acc3[SUBLANE * k:SUBLANE * (k + 1), :], axis=0, keepdims=True)

    ospec = pl.BlockSpec((t, cb), lambda c, e: (e, c))
    w31_spec = pl.BlockSpec((w31.shape[0], cb), lambda c, e: (0, c))
    w3_spec = pl.BlockSpec((w3.shape[0], cb), lambda c, e: (0, c))
    return _hosting_call(
        body, comm,
        out_shape=(jax.ShapeDtypeStruct(dp.shape, dp.dtype), jax.ShapeDtypeStruct(w31.shape, f32),
                   jax.ShapeDtypeStruct(w3.shape, f32)),
        grid=(cw // cb, cfg.B),
        in_specs=[_conv_spec(cfg), pl.BlockSpec(memory_space=pl.ANY), ospec, ospec, w31_spec, w3_spec],
        out_specs=(_conv_spec(cfg), w31_spec, w3_spec), aliases={1: 0},
        scratch_shapes=[pltpu.VMEM((buf_rows, cb), f32)] * 4
        + [pltpu.VMEM((SUBLANE * w31.shape[0], cb), f32), pltpu.VMEM((SUBLANE * w3.shape[0], cb), f32)],
        compiler_params=_params(2, VMEM_LIMIT), name=name, args=(p, dp, dhc, dz, w31, w3))


def _ln_silu_fwd(cfg, name, hc, lnp):
    r_, cw = hc.shape

    def body(x_ref, p_ref, o_ref):
        x = x_ref[...]
        mu = jnp.mean(x, axis=-1, keepdims=True)
        xc = x - mu
        rs = lax.rsqrt(jnp.mean(xc * xc, axis=-1, keepdims=True) + EPS)
        hn = (xc * rs) * p_ref[0:1, :] + p_ref[1:2, :]
        o_ref[...] = (hn * jax.nn.sigmoid(hn)).astype(bf16)

    row = _row_spec(cfg, cw)
    return pl.pallas_call(body, out_shape=jax.ShapeDtypeStruct((r_, cw), bf16), grid=(r_ // cfg.TR,),
                          in_specs=[row, _const_spec(lnp)], out_specs=row,
                          compiler_params=_params(1), name=name)(hc, lnp)


def _ln_silu_bwd(cfg, name, hc, dhs, lnp):
    r_, cw = hc.shape

    def body(x_ref, d_ref, p_ref, dx_ref, dp_ref):
        i = pl.program_id(0)
        x = x_ref[...]
        mu = jnp.mean(x, axis=-1, keepdims=True)
        xc = x - mu
        rs = lax.rsqrt(jnp.mean(xc * xc, axis=-1, keepdims=True) + EPS)
        xh = xc * rs
        gain = p_ref[0:1, :]
        hn = xh * gain + p_ref[1:2, :]
        sg = jax.nn.sigmoid(hn)
        dhn = d_ref[...].astype(f32) * (sg * (1.0 + hn * (1.0 - sg)))
        dxh = dhn * gain
        dx_ref[...] = rs * (dxh - jnp.mean(dxh, axis=-1, keepdims=True)
                            - xh * jnp.mean(dxh * xh, axis=-1, keepdims=True))

        @pl.when(i == 0)
        def _():
            dp_ref[...] = jnp.zeros_like(dp_ref)

        dp_ref[0:1, :] += jnp.sum(dhn * xh, axis=0, keepdims=True)
        dp_ref[1:2, :] += jnp.sum(dhn, axis=0, keepdims=True)

    row = _row_spec(cfg, cw)
    return pl.pallas_call(
        body, out_shape=(jax.ShapeDtypeStruct((r_, cw), f32), jax.ShapeDtypeStruct((SUBLANE, cw), f32)),
        grid=(r_ // cfg.TR,), in_specs=[row, row, _const_spec(lnp)],
        out_specs=(row, pl.BlockSpec((SUBLANE, cw), lambda i: (0, 0))),
        compiler_params=_params(1), name=name)(hc, dhs, lnp)


def _merge_fwd(cfg, name, p, attn, hs, z, wao_t, wco_t, wso_t):
    r_ = attn.shape[0]
    d = wao_t.shape[0]
    gb = cfg.GB
    tr = _divisor(r_, GATE_ROWS, cfg.TR)

    def body(g_ref, a_ref, h_ref, z_ref, wa_ref, wc_ref, ws_ref, m_ref, ya_ref, yc_ref, ys_ref):
        m = None
        for k, (x_ref, w_ref, y_ref) in enumerate(((a_ref, wa_ref, ya_ref), (h_ref, wc_ref, yc_ref),
                                                   (z_ref, ws_ref, ys_ref))):
            y = lax.dot_general(x_ref[...].astype(bf16), w_ref[...], NT_, preferred_element_type=f32)
            yb = y.astype(ACT)
            y_ref[...] = yb
            term = jax.nn.sigmoid(g_ref[:, k * gb:(k + 1) * gb].astype(f32)) * yb.astype(f32)
            m = term if m is None else m + term
        m_ref[...] = m.astype(bf16)

    yspec = pl.BlockSpec((tr, gb), lambda i, c: (i, c))
    xspecs = [pl.BlockSpec((tr, x.shape[1]), lambda i, c: (i, 0)) for x in (attn, hs, z)]
    wspecs = [pl.BlockSpec((gb, w.shape[1]), lambda i, c: (c, 0)) for w in (wao_t, wco_t, wso_t)]
    return pl.pallas_call(
        body, out_shape=(jax.ShapeDtypeStruct((r_, d), bf16),) + (jax.ShapeDtypeStruct((r_, d), ACT),) * 3,
        grid=(r_ // tr, d // gb), in_specs=[_gate_spec(cfg, tr)] + xspecs + wspecs, out_specs=(yspec,) * 4,
        compiler_params=_params(2, VMEM_LIMIT), name=name)(p, attn, hs, z, wao_t, wco_t, wso_t)


def _gate_spec(cfg, tr):
    width = N_BRANCHES * cfg.GB
    return pl.BlockSpec((tr, width), lambda i, c: (i, cfg.OFF_GATE // width + c))


def _relayout_rows(cfg, name, w, to_kernel):
    n_in, cols = w.shape
    nq = QKVW // LANE
    ncb = cfg.CW // cfg.CB
    gbb = cfg.GB // LANE
    dblk = cfg.D // LANE
    nconv, ngate = CONV_PARTS * ncb, N_BRANCHES * dblk

    def source(i):
        if to_kernel:
            j = i - nq
            gate_src = nq + nconv + ((j % (N_BRANCHES * gbb)) // gbb) * dblk + (j // (N_BRANCHES * gbb)) * gbb + j % gbb
            jc = j - ngate
            conv_src = nq + (jc % CONV_PARTS) * ncb + jc // CONV_PARTS
            return jnp.where(i < nq, i, jnp.where(j < ngate, gate_src, conv_src))
        j = i - nq
        conv_src = nq + ngate + (j % ncb) * CONV_PARTS + j // ncb
        jg = j - nconv
        rem = jg % dblk
        gate_src = nq + (rem // gbb) * (N_BRANCHES * gbb) + (jg // dblk) * gbb + rem % gbb
        return jnp.where(i < nq, i, jnp.where(j < nconv, conv_src, gate_src))

    def body(x_ref, o_ref):
        o_ref[...] = x_ref[...]

    return pl.pallas_call(
        body, out_shape=jax.ShapeDtypeStruct(w.shape, w.dtype), grid=(n_in // LANE,),
        in_specs=[pl.BlockSpec((LANE, cols), lambda i: (source(i), 0))],
        out_specs=pl.BlockSpec((LANE, cols), lambda i: (i, 0)), compiler_params=_params(1), name=name)(w)


def _gate_bwd(cfg, name, p, ya, yc, ys, dm):
    r_, d = ya.shape
    gb = cfg.GB
    tr = _divisor(r_, GATE_ROWS, cfg.TR)

    def body(g_ref, ya_ref, yc_ref, ys_ref, dm_ref, da_ref, dc_ref, ds_ref, dp_ref):
        dmv = dm_ref[...].astype(f32)
        for k, (y_ref, dy_ref) in enumerate(((ya_ref, da_ref), (yc_ref, dc_ref), (ys_ref, ds_ref))):
            sg = jax.nn.sigmoid(g_ref[:, k * gb:(k + 1) * gb].astype(f32))
            dy_ref[...] = (dmv * sg).astype(bf16)
            dp_ref[:, k * gb:(k + 1) * gb] = (dmv * y_ref[...].astype(f32) * sg * (1.0 - sg)).astype(bf16)

    yspec = pl.BlockSpec((tr, gb), lambda i, c: (i, c))
    return pl.pallas_call(
        body, out_shape=(jax.ShapeDtypeStruct((r_, d), bf16),) * 3 + (jax.ShapeDtypeStruct(p.shape, bf16),),
        grid=(r_ // tr, d // gb), in_specs=[_gate_spec(cfg, tr)] + [yspec] * 4,
        out_specs=(yspec,) * 3 + (_gate_spec(cfg, tr),),
        compiler_params=_params(2), name=name)(p, ya, yc, ys, dm)


def _swiglu_fwd(cfg, name, ff):
    r_, f2 = ff.shape
    fh = f2 // 2

    def body(a_ref, b_ref, o_ref):
        a = a_ref[...].astype(f32)
        o_ref[...] = (a * jax.nn.sigmoid(a) * b_ref[...].astype(f32)).astype(bf16)

    return pl.pallas_call(body, out_shape=jax.ShapeDtypeStruct((r_, fh), bf16), grid=(r_ // cfg.TR,),
                          in_specs=[_row_spec(cfg, fh, 0), _row_spec(cfg, fh, 1)], out_specs=_row_spec(cfg, fh),
                          compiler_params=_params(1, VMEM_LIMIT), name=name)(ff, ff)


def _swiglu_bwd(cfg, name, ff, ds):
    r_, f2 = ff.shape
    fh = f2 // 2

    def body(a_ref, b_ref, d_ref, o_ref):
        a = a_ref[...].astype(f32)
        sg = jax.nn.sigmoid(a)
        dsv = d_ref[...].astype(f32)
        o_ref[:, 0:fh] = (dsv * b_ref[...].astype(f32) * (sg * (1.0 + a * (1.0 - sg)))).astype(bf16)
        o_ref[:, fh:f2] = (dsv * a * sg).astype(bf16)

    return pl.pallas_call(body, out_shape=jax.ShapeDtypeStruct((r_, f2), bf16), grid=(r_ // cfg.TR,),
                          in_specs=[_row_spec(cfg, fh, 0), _row_spec(cfg, fh, 1), _row_spec(cfg, fh)],
                          out_specs=_row_spec(cfg, f2), compiler_params=_params(1, VMEM_LIMIT), name=name)(ff, ff, ds)


def _ada_fwd(name, call, w_ada, b_loc):
    nl, d, na = w_ada.shape
    nbp = call.shape[0]

    def body(c_ref, w_ref, b_ref, o_ref):
        cv = c_ref[...]
        a = (cv * jax.nn.sigmoid(cv)).astype(bf16)
        o_ref[0] = jnp.dot(a, w_ref[0].astype(bf16), preferred_element_type=f32) + b_ref[0]

    return pl.pallas_call(
        body, out_shape=jax.ShapeDtypeStruct((nl, nbp, na), f32), grid=(nl,),
        in_specs=[_const_spec(call), pl.BlockSpec((1, d, na), lambda l: (l, 0, 0)),
                  pl.BlockSpec((1, 1, na), lambda l: (l, 0, 0))],
        out_specs=pl.BlockSpec((1, nbp, na), lambda l: (l, 0, 0)),
        compiler_params=_params(1, VMEM_LIMIT), name=name)(call, w_ada, b_loc)


def _ada_bwd(name, call, w_ada, dmod, dmodc, cctx_row):
    nl, d, na = w_ada.shape
    nbp = call.shape[0]

    def body(c_ref, w_ref, dm_ref, dc_ref, gw_ref, gb_ref, gc_ref):
        l = pl.program_id(0)
        cv = c_ref[...]
        sg = jax.nn.sigmoid(cv)
        a = (cv * sg).astype(bf16)
        dctx = jnp.sum(dc_ref[0], axis=0, keepdims=True)
        rows = lax.broadcasted_iota(jnp.int32, (nbp, 1), 0)
        dm = jnp.where(rows == cctx_row, dctx, dm_ref[0])
        gw_ref[0] = lax.dot_general(a, dm.astype(bf16), TN, preferred_element_type=f32)
        gb_ref[0] = jnp.zeros((SUBLANE, na), f32)
        gb_ref[0, 0:1, :] = jnp.sum(dm, axis=0, keepdims=True)
        dc8 = jnp.broadcast_to(dctx, (SUBLANE, na)).astype(bf16)
        part = lax.dot_general(dc8, w_ref[0].astype(bf16), NT_, preferred_element_type=f32)
        cc = c_ref[cctx_row:cctx_row + 1, :]
        sc = jax.nn.sigmoid(cc)
        part = part * (sc * (1.0 + cc * (1.0 - sc)))

        @pl.when(l == 0)
        def _():
            gc_ref[...] = jnp.zeros_like(gc_ref)

        gc_ref[...] += part

    return pl.pallas_call(
        body,
        out_shape=(jax.ShapeDtypeStruct((nl, d, na), f32), jax.ShapeDtypeStruct((nl, SUBLANE, na), f32),
                   jax.ShapeDtypeStruct((SUBLANE, d), f32)),
        grid=(nl,),
        in_specs=[_const_spec(call), pl.BlockSpec((1, d, na), lambda l: (l, 0, 0)),
                  pl.BlockSpec((1, nbp, na), lambda l: (l, 0, 0)),
                  pl.BlockSpec((1, NDEV, na), lambda l: (l, 0, 0))],
        out_specs=(pl.BlockSpec((1, d, na), lambda l: (l, 0, 0)), pl.BlockSpec((1, SUBLANE, na), lambda l: (l, 0, 0)),
                   pl.BlockSpec((SUBLANE, d), lambda l: (0, 0))),
        compiler_params=_params(1, VMEM_LIMIT), name=name)(call, w_ada, dmod, dmodc)


def _adamw(name, w, g, m, v):
    rows, cols = w.shape
    tr = _divisor(rows, max(SUBLANE, (1 << 19) // cols), SUBLANE)
    c1 = 1.0 / (1.0 - ADAM_B1 ** ADAM_STEP)
    c2 = 1.0 / (1.0 - ADAM_B2 ** ADAM_STEP)

    def body(w_ref, g_ref, m_ref, v_ref, d_ref, nm_ref, nv_ref):
        gv = g_ref[...]
        nm = ADAM_B1 * m_ref[...] + (1.0 - ADAM_B1) * gv
        nv = ADAM_B2 * v_ref[...] + (1.0 - ADAM_B2) * (gv * gv)
        nm_ref[...] = nm
        nv_ref[...] = nv
        d_ref[...] = -ADAM_LR * ((nm * c1) / (jnp.sqrt(nv * c2) + ADAM_EPS) + ADAM_WD * w_ref[...])

    spec = pl.BlockSpec((tr, cols), lambda i: (i, 0))
    return pl.pallas_call(body, out_shape=(jax.ShapeDtypeStruct((rows, cols), f32),) * 3, grid=(rows // tr,),
                          in_specs=[spec] * 4, out_specs=(spec,) * 3,
                          compiler_params=_params(1, VMEM_LIMIT), name=name)(w, g, m, v)


def _sum_slots(name, x):
    nd, rows, cols = x.shape
    tr = _divisor(rows, max(SUBLANE, (1 << 18) // cols), SUBLANE)

    def body(x_ref, o_ref):
        acc = x_ref[0].astype(f32)
        for s in range(1, nd):
            acc = acc + x_ref[s].astype(f32)
        o_ref[...] = acc

    return pl.pallas_call(body, out_shape=jax.ShapeDtypeStruct((rows, cols), f32), grid=(rows // tr,),
                          in_specs=[pl.BlockSpec((nd, tr, cols), lambda i: (0, i, 0))],
                          out_specs=pl.BlockSpec((tr, cols), lambda i: (i, 0)),
                          compiler_params=_params(1, VMEM_LIMIT), name=name)(x)


def _peer(k):
    x, y, c = (lax.axis_index(a) for a in AXES)
    px = 1 - x if k & 4 else x
    py = 1 - y if k & 2 else y
    pc = 1 - c if k & 1 else c
    return (px, py, pc), 4 * px + 2 * py + pc


def _exchange_out_shapes(xs, scatter):
    return [jax.ShapeDtypeStruct(xa.shape if scatter else (NDEV,) + xa.shape, xa.dtype) for xa in xs]


def _exchange_scratch(n):
    nrel = NDEV - 1
    return [pltpu.SemaphoreType.DMA((n * nrel,)), pltpu.SemaphoreType.DMA((n * nrel,)), pltpu.SemaphoreType.DMA((n,))]


def _exchange_copies(ins, outs, sems, scatter, receiving):
    send_sems, recv_sems, _ = sems
    nrel = NDEV - 1
    x, y, c = (lax.axis_index(a) for a in AXES)
    me = 4 * x + 2 * y + c
    copies = []
    for a in range(len(ins)):
        for k in range(1, NDEV):
            peer, pidx = _peer(k)
            src = ins[a].at[pidx] if scatter else ins[a]
            copies.append(pltpu.make_async_remote_copy(
                src_ref=src, dst_ref=outs[a].at[pidx if receiving else me], send_sem=send_sems.at[a * nrel + k - 1],
                recv_sem=recv_sems.at[a * nrel + k - 1], device_id=peer, device_id_type=pl.DeviceIdType.MESH))
    return copies


def _exchange_local(ins, outs, sems, scatter):
    x, y, c = (lax.axis_index(a) for a in AXES)
    me = 4 * x + 2 * y + c
    return [pltpu.make_async_copy(ins[a].at[me] if scatter else ins[a], outs[a].at[me], sems[2].at[a])
            for a in range(len(ins))]


CHIP_RELATIONS = (2, 4, 6)
SIBLING = 1


def _gather_copy(ins, outs, sems, a, slot, src, block, to):
    nrel = NDEV - 1
    return pltpu.make_async_remote_copy(
        src_ref=src, dst_ref=outs[a].at[block], send_sem=sems[0].at[a * nrel + slot],
        recv_sem=sems[1].at[a * nrel + slot], device_id=to, device_id_type=pl.DeviceIdType.MESH)


def _exchange_start(ins, outs, sems, scatter):
    for cp in _exchange_local(ins, outs, sems, scatter):
        cp.start()
    if scatter:
        for cp in _exchange_copies(ins, outs, sems, scatter, False):
            cp.start()
        return
    x, y, c = (lax.axis_index(a) for a in AXES)
    me = 4 * x + 2 * y + c
    for a in range(len(ins)):
        for slot, k in enumerate((SIBLING,) + CHIP_RELATIONS):
            _gather_copy(ins, outs, sems, a, slot, ins[a], me, _peer(k)[0]).start()


def _exchange_wait(ins, outs, sems, scatter):
    if scatter:
        for cp in _exchange_copies(ins, outs, sems, scatter, True):
            cp.wait_recv()
            cp.wait_send()
    else:
        x, y, c = (lax.axis_index(a) for a in AXES)
        me = 4 * x + 2 * y + c
        sibling, sibling_idx = _peer(SIBLING)
        n = len(ins)
        passed = []
        for a in range(n):
            for j, k in enumerate(CHIP_RELATIONS):
                peer, pidx = _peer(k)
                _gather_copy(ins, outs, sems, a, 1 + j, ins[a], pidx, peer).wait_recv()
                fwd = _gather_copy(ins, outs, sems, a, 4 + j, outs[a].at[pidx], pidx, sibling)
                fwd.start()
                passed.append(fwd)
        for a in range(n):
            _gather_copy(ins, outs, sems, a, 0, ins[a], sibling_idx, sibling).wait_recv()
            for j, k in enumerate(CHIP_RELATIONS):
                _gather_copy(ins, outs, sems, a, 4 + j, ins[a], _peer(k | SIBLING)[1], sibling).wait_recv()
        for a in range(n):
            for slot, k in enumerate((SIBLING,) + CHIP_RELATIONS):
                _gather_copy(ins, outs, sems, a, slot, ins[a], me, _peer(k)[0]).wait_send()
        for fwd in passed:
            fwd.wait_send()
    for cp in _exchange_local(ins, outs, sems, scatter):
        cp.wait()


def _exchange(name, xs, scatter):
    n = len(xs)

    def body(*refs):
        ins, outs, sems = refs[:n], refs[n:2 * n], refs[2 * n:]
        _exchange_start(ins, outs, sems, scatter)
        _exchange_wait(ins, outs, sems, scatter)

    anyspec = pl.BlockSpec(memory_space=pl.ANY)
    outs = pl.pallas_call(
        body, out_shape=tuple(_exchange_out_shapes(xs, scatter)), in_specs=[anyspec] * n, out_specs=(anyspec,) * n,
        scratch_shapes=_exchange_scratch(n), name=name)(*xs)
    return list(outs)


def _all_gather(name, xs):
    return _exchange(name, xs, False)


def _all_to_all(name, xs):
    return _exchange(name, xs, True)


def _rope_table(cfg):
    s, lc = cfg.S, cfg.LC
    rows = s // GRID_W
    r_ids, c_ids = jnp.meshgrid(jnp.arange(rows), jnp.arange(GRID_W), indexing="ij")
    r_ids = r_ids.reshape(-1).astype(f32)
    c_ids = c_ids.reshape(-1).astype(f32)
    freqs = ROPE_THETA ** (-jnp.arange(0, AXIS_DIM, 2, dtype=f32) / AXIS_DIM)
    ang_r = r_ids[:, None] * freqs
    ang_c = c_ids[:, None] * freqs
    cos = jnp.concatenate([jnp.cos(ang_r), jnp.cos(ang_r), jnp.cos(ang_c), jnp.cos(ang_c)], axis=1)
    sin = jnp.concatenate([-jnp.sin(ang_r), jnp.sin(ang_r), -jnp.sin(ang_c), jnp.sin(ang_c)], axis=1)
    cos = jnp.concatenate([jnp.ones((lc, HEAD_DIM), f32), cos], axis=0)
    sin = jnp.concatenate([jnp.zeros((lc, HEAD_DIM), f32), sin], axis=0)
    reps = QKW // HEAD_DIM
    return jnp.stack([jnp.tile(cos, (1, reps)), jnp.tile(sin, (1, reps))])


def _block_diag_ones():
    idx = np.arange(QKW) // HEAD_DIM
    return jnp.asarray((idx[:, None] == idx[None, :]).astype(np.float32))


def _replicate_matrix():
    src = np.arange(KVW)
    dst = np.arange(AW)
    m = (src[:, None] // HEAD_DIM == dst[None, :] // GW) & (src[:, None] % HEAD_DIM == dst[None, :] % HEAD_DIM)
    return m.astype(np.float32)


def _pack_flat(parts):
    flat = jnp.concatenate([p.reshape(-1) for p in parts])
    n = flat.shape[0]
    unit = SUBLANE * LANE
    total = -(-n // unit) * unit
    flat = jnp.pad(flat, (0, total - n))
    return flat.reshape(total // LANE, LANE)


def _unpack_flat(flat, shapes):
    flat = flat.reshape(flat.shape[:-2] + (-1,))
    out, off = [], 0
    for shp in shapes:
        size = int(np.prod(shp))
        out.append(flat[..., off:off + size].reshape(flat.shape[:-1] + tuple(shp)))
        off += size
    return out


def kernel(x, c, ctx, c_ctx, w_ada, b_ada, w_in, q_norm, k_norm, w_attn_o, conf_dw_w, conf_dw_b, conf_ln_g, conf_ln_b, w_conf_out, sc_dw_w, w_sc_out, w_mix_out, w_ffn_in, w_ffn_out, loss_target, m_c_ctx, m_w_ada, m_b_ada, m_w_in, m_q_norm, m_k_norm, m_w_attn_o, m_conf_dw_w, m_conf_dw_b, m_conf_ln_g, m_conf_ln_b, m_w_conf_out, m_sc_dw_w, m_w_sc_out, m_w_mix_out, m_w_ffn_in, m_w_ffn_out, v_c_ctx, v_w_ada, v_b_ada, v_w_in, v_q_norm, v_k_norm, v_w_attn_o, v_conf_dw_w, v_conf_dw_b, v_conf_ln_g, v_conf_ln_b, v_w_conf_out, v_sc_dw_w, v_w_sc_out, v_w_mix_out, v_w_ffn_in, v_w_ffn_out):
    cfg = _Cfg()
    cfg.B, cfg.S, cfg.D = x.shape
    cfg.LC = ctx.shape[1]
    cfg.T = cfg.LC + cfg.S
    cfg.TR = min(256, cfg.LC)
    assert cfg.LC % cfg.TR == 0 and cfg.S % cfg.TR == 0 and cfg.S % GRID_W == 0
    cfg.NT, cfg.NC = cfg.T // cfg.TR, cfg.LC // cfg.TR
    cfg.R = cfg.B * cfg.T
    nl = w_in.shape[0]
    b, d = cfg.B, cfg.D
    cfg.CW = conf_dw_b.shape[1]
    cfg.K31, cfg.K3 = conf_dw_w.shape[1], sc_dw_w.shape[1]
    assert w_sc_out.shape[1] == cfg.CW and cfg.CW % LANE == 0
    assert cfg.K3 // 2 <= cfg.K31 // 2 <= CONV_PAD_ROWS and cfg.LC % CONV_CHUNK == 0 and cfg.S % CONV_CHUNK == 0
    cfg.CB = LANE
    cfg.GB = 2 * LANE if d % (2 * LANE) == 0 else LANE
    cfg.OFF_GATE = QKVW
    cfg.OFF_CONV = QKVW + N_BRANCHES * d
    n_in = w_in.shape[2] * NDEV
    assert n_in == cfg.OFF_CONV + CONV_PARTS * cfg.CW and w_attn_o.shape[1] == AW
    assert cfg.OFF_GATE % (N_BRANCHES * cfg.GB) == 0 and cfg.OFF_CONV % (CONV_PARTS * cfg.CB) == 0
    fh = w_ffn_out.shape[1] * NDEV
    na = w_ada.shape[2]
    cw8 = cfg.CW // NDEV

    xi, yi, ci = (lax.axis_index(a) for a in AXES)
    me = 4 * xi + 2 * yi + ci

    small_shapes = [c.shape, conf_dw_w.shape, sc_dw_w.shape]
    (g0,) = _all_gather("gather_small", [_pack_flat([c, conf_dw_w, sc_dw_w])])
    c_all, cw_all, sw_all = _unpack_flat(g0, small_shapes)
    nb = NDEV * b
    nbp = -(-(nb + 1) // SUBLANE) * SUBLANE
    call = jnp.concatenate([c_all.reshape(nb, d), c_ctx[None, :], jnp.zeros((nbp - nb - 1, d), f32)], axis=0)
    w31_full = jnp.moveaxis(cw_all, 0, 2).reshape(nl, cfg.K31, cfg.CW)
    w3_full = jnp.moveaxis(sw_all, 0, 2).reshape(nl, cfg.K3, cfg.CW)
    k31p = -(-(cfg.K31 + 1) // SUBLANE) * SUBLANE
    w31b = jnp.concatenate([w31_full, conf_dw_b[:, None, :], jnp.zeros((nl, k31p - cfg.K31 - 1, cfg.CW), f32)], axis=1)
    w3p = jnp.concatenate([w3_full, jnp.zeros((nl, SUBLANE - cfg.K3, cfg.CW), f32)], axis=1)

    b_loc = lax.dynamic_slice(b_ada, (0, me * na), (nl, na))[:, None, :]
    mod_loc = _ada_fwd("ada_fwd", call, w_ada, b_loc)
    (mod_g,) = _all_gather("gather_mod", [mod_loc.reshape(nl * nbp, na)])
    mod_full = jnp.transpose(mod_g.reshape(NDEV, nl, nbp, na), (1, 2, 0, 3)).reshape(nl, nbp, N_MOD, d)
    mod_lat = lax.dynamic_slice_in_dim(mod_full, me * b, b, axis=1)
    mod_ctx = jnp.broadcast_to(mod_full[:, nb][:, None], (nl, b, N_MOD, d))
    mod = jnp.stack([mod_ctx, mod_lat], axis=2)
    mod = jnp.pad(mod, ((0, 0), (0, 0), (0, 0), (0, MOD_ROWS - N_MOD), (0, 0)))

    cs = _rope_table(cfg)
    bd = _block_diag_ones().astype(bf16)
    rep_np = _replicate_matrix()
    rep = jnp.asarray(rep_np, dtype=bf16)
    rept = jnp.asarray(rep_np.T, dtype=bf16)
    reps_q, reps_k = AW // HEAD_DIM, KVW // HEAD_DIM

    xin = jnp.concatenate([ctx, x], axis=1).reshape(cfg.R, d)
    target = loss_target.reshape(b * cfg.S, d)
    saved = []
    xcur, ocur = xin, None

    def weight_shards(l):
        return [jnp.transpose(w_in[l]).astype(bf16), jnp.transpose(w_attn_o[l]).astype(bf16),
                jnp.transpose(w_conf_out[l]).astype(bf16), jnp.transpose(w_sc_out[l]).astype(bf16),
                w_mix_out[l].astype(bf16), jnp.transpose(w_ffn_in[l]).astype(bf16), w_ffn_out[l].astype(bf16)]

    def whole(g):
        return g.reshape(NDEV * g.shape[1], g.shape[2])

    shards0 = weight_shards(0)
    (w_first,) = _all_gather("gather_w0", shards0[:1])
    w_rest = None
    for l in range(nl):
        win_t = _relayout_rows(cfg, f"relayout_w{l}", whole(w_first), True)
        gain = jnp.concatenate([jnp.tile(q_norm[l], reps_q), jnp.tile(k_norm[l], reps_k)])[None, :]
        lnp = jnp.concatenate([conf_ln_g[l][None], conf_ln_b[l][None], jnp.zeros((SUBLANE - 2, cfg.CW), f32)], axis=0)
        nxt = weight_shards(l + 1) if l + 1 < nl else None

        if l == 0:
            x0, h = _norm_fwd(cfg, f"norm_a{l}", xcur, None, None, 0, mod[l], 0)
            p, w_rest = _mm(h, win_t, "nt", ACT, f"mm_in{l}", comm=(shards0[1:], False))
        else:
            x0, h = _norm_fwd(cfg, f"norm_a{l}", xcur, ocur, mod[l - 1], 5, mod[l], 0)
            p = _mm(h, win_t, "nt", ACT, f"mm_in{l}")
        wao_t, wco_t, wso_t, wmix, wfi_t, wfo = [whole(g) for g in w_rest]
        q, krep, vrep = _qkv_fwd(cfg, f"qkv{l}", p, cs, gain, bd, rep)
        (attn, lse), w_rest = _attn_fwd(cfg, f"attn{l}", q, krep, vrep,
                                        comm=None if nxt is None else (nxt[1:], False))
        hc, z = _conv_fwd(cfg, f"conv{l}", p, w31b[l], w3p[l])
        hs = _ln_silu_fwd(cfg, f"lnsilu{l}", hc, lnp)
        merged, ya, yc, ys = _merge_fwd(cfg, f"merge{l}", p, attn, hs, z, wao_t, wco_t, wso_t)
        mixed = _mm(merged, wmix, "nn", f32, f"mm_mix{l}")
        x1, h2 = _norm_fwd(cfg, f"norm_b{l}", x0, mixed, mod[l], 2, mod[l], 3)
        if nxt is None:
            ff = _mm(h2, wfi_t, "nt", ACT, f"mm_fi{l}")
        else:
            ff, (w_first,) = _mm(h2, wfi_t, "nt", ACT, f"mm_fi{l}", comm=(nxt[:1], False))
        sw = _swiglu_fwd(cfg, f"swiglu{l}", ff)
        o = _mm(sw, wfo, "nn", f32, f"mm_fo{l}")
        saved.append(dict(x0=x0, h=h, p=p, q=q, krep=krep, vrep=vrep, attn=attn, lse=lse, hc=hc, z=z, hs=hs,
                          ya=ya, yc=yc, ys=ys, merged=merged, mixed=mixed, x1=x1, h2=h2, ff=ff, sw=sw, o=o,
                          oprev=ocur, gain=gain, lnp=lnp,
                          w=(win_t, wao_t, wco_t, wso_t, wmix, wfi_t, wfo)))
        xcur, ocur = x1, o

    lsum, dres, do, dm_loss = _loss_head(cfg, "loss", xcur, ocur, mod[nl - 1], 5, target)
    loss = lax.psum((0.5 / d) * jnp.sum(lsum), AXES)

    dmod_rows = [[None] * N_MOD for _ in range(nl)]
    dmod_rows[nl - 1][5] = dm_loss[:, :, 2]
    g_qn, g_kn, g_w31, g_w3, g_ln = [None] * nl, [None] * nl, [None] * nl, [None] * nl, [None] * nl
    gbig = [None] * nl
    pending = None
    pending_early = None

    def blocks(g):
        return g.reshape(NDEV, g.shape[0] // NDEV, g.shape[1])

    def sum_received(l, recv):
        gbig[l] = [_sum_slots(f"sum_g{l}_{i}", r) for i, r in enumerate(recv)]

    for l in reversed(range(nl)):
        sv = saved[l]
        win_t, wao_t, wco_t, wso_t, wmix, wfi_t, wfo = sv["w"]
        ds = _mm(do, wfo, "nt", ACT, f"mm_dsw{l}")
        g_wfo = _mm(sv["sw"], do, "tn", bf16, f"mm_gfo{l}")
        dff = _swiglu_bwd(cfg, f"swiglu_b{l}", sv["ff"], ds)
        dh2 = _mm(dff, wfi_t, "nn", GRAD, f"mm_dh2{l}")
        g_wfi = _mm(dff, sv["h2"], "tn", bf16, f"mm_gfi{l}")
        dx0p, dmixed, dm_b = _norm_bwd(cfg, f"norm_b_b{l}", sv["x1"], sv["mixed"], dh2, dres, mod[l], 2, mod[l], 3)
        dmod_rows[l][3], dmod_rows[l][4], dmod_rows[l][2] = dm_b[:, :, 0], dm_b[:, :, 1], dm_b[:, :, 2]
        dmerged = _mm(dmixed, wmix, "nt", GRAD, f"mm_dmg{l}")
        g_wmix = _mm(sv["merged"], dmixed, "tn", bf16, f"mm_gmix{l}")
        dya, dyc, dys, dp = _gate_bwd(cfg, f"gate_b{l}", sv["p"], sv["ya"], sv["yc"], sv["ys"], dmerged)
        dattn = _mm(dya, wao_t, "nn", GRAD, f"mm_dat{l}")
        g_wao = _mm(dya, sv["attn"], "tn", bf16, f"mm_gao{l}")
        dhs = _mm(dyc, wco_t, "nn", GRAD, f"mm_dhs{l}")
        g_wco = _mm(dyc, sv["hs"], "tn", bf16, f"mm_gco{l}")
        dz = _mm(dys, wso_t, "nn", GRAD, f"mm_dz{l}")
        g_wso = _mm(dys, sv["z"], "tn", bf16, f"mm_gso{l}")
        dhc, g_ln[l] = _ln_silu_bwd(cfg, f"lnsilu_b{l}", sv["hc"], dhs, sv["lnp"])
        (dp, g_w31[l], g_w3[l]), recv = _conv_bwd(
            cfg, f"conv_b{l}", sv["p"], dp, dhc, dz, w31b[l], w3p[l],
            comm=None if pending is None else ([pending], True))
        if pending is not None:
            sum_received(l + 1, recv + pending_early)
        early = [blocks(g) for g in (g_wao, g_wco, g_wso, g_wmix, g_wfi, g_wfo)]
        (dq, dkr, dvr), pending_early = _attn_bwd(cfg, f"attn_b{l}", sv["q"], sv["krep"], sv["vrep"], sv["attn"],
                                                  sv["lse"], dattn, comm=(early, True))
        dp, dgain = _qkv_bwd(cfg, f"qkv_b{l}", sv["p"], dp, dq, dkr, dvr, cs, sv["gain"], bd, rept)
        g_qn[l] = jnp.sum(dgain[0, :AW].reshape(reps_q, HEAD_DIM), axis=0)
        g_kn[l] = jnp.sum(dgain[0, AW:].reshape(reps_k, HEAD_DIM), axis=0)
        g_win = _mm(dp, sv["h"], "tn", bf16, f"mm_gin{l}")
        pending = blocks(_relayout_rows(cfg, f"relayout_g{l}", g_win, False))
        if l == 0:
            dh, recv = _mm(dp, win_t, "nn", GRAD, f"mm_dh{l}", comm=([pending], True))
            sum_received(0, recv + pending_early)
            dres, _, dm_a = _norm_bwd(cfg, f"norm_a_b{l}", sv["x0"], None, dh, dx0p, None, 0, mod[l], 0)
        else:
            dh = _mm(dp, win_t, "nn", GRAD, f"mm_dh{l}")
            dres, do, dm_a = _norm_bwd(cfg, f"norm_a_b{l}", sv["x0"], sv["oprev"], dh, dx0p, mod[l - 1], 5, mod[l], 0)
            dmod_rows[l - 1][5] = dm_a[:, :, 2]
        dmod_rows[l][0], dmod_rows[l][1] = dm_a[:, :, 0], dm_a[:, :, 1]

    grad_x = dres.reshape(b, cfg.T, d)[:, cfg.LC:, :]

    rb = -(-(b + 1) // SUBLANE) * SUBLANE
    dmod_l = jnp.stack([jnp.stack(rows, axis=2) for rows in dmod_rows])
    d_lat = dmod_l[:, :, 1].reshape(nl, b, NDEV, na)
    d_ctx = jnp.sum(dmod_l[:, :, 0], axis=1).reshape(nl, 1, NDEV, na)
    send = jnp.concatenate([d_lat, d_ctx, jnp.zeros((nl, rb - b - 1, NDEV, na), f32)], axis=1)
    send = jnp.transpose(send, (2, 0, 1, 3)).reshape(NDEV, nl * rb, na)
    (recv,) = _all_to_all("scatter_dmod", [send])
    recv = recv.reshape(NDEV, nl, rb, na)
    dmod_ex = jnp.transpose(recv[:, :, :b], (1, 0, 2, 3)).reshape(nl, nb, na)
    dmod_ex = jnp.pad(dmod_ex, ((0, 0), (0, nbp - nb), (0, 0)))
    dmodc = jnp.transpose(recv[:, :, b], (1, 0, 2))
    g_wada, g_bloc, g_cctx = _ada_bwd("ada_bwd", call, w_ada, dmod_ex, dmodc, nb)

    g_bada = lax.dynamic_update_slice(jnp.zeros((nl, N_MOD * d), f32), g_bloc[:, 0], (0, me * na))
    g_w31s = jnp.stack(g_w31)
    small_parts = [g_cctx[0], g_bada, jnp.stack(g_qn), jnp.stack(g_kn), g_w31s[:, :cfg.K31], g_w31s[:, cfg.K31],
                   jnp.stack([g[0] for g in g_ln]), jnp.stack([g[1] for g in g_ln]),
                   jnp.stack(g_w3)[:, :cfg.K3]]
    small_part_shapes = [p_.shape for p_ in small_parts]
    (gs,) = _all_gather("gather_gsmall", [_pack_flat(small_parts)])
    gsum = _sum_slots("sum_gsmall", gs)
    (gr_cctx, gr_bada, gr_qn, gr_kn, gr_w31, gr_b31, gr_lng, gr_lnb, gr_w3) = _unpack_flat(gsum, small_part_shapes)
    gr_w31 = lax.dynamic_slice_in_dim(gr_w31, me * cw8, cw8, axis=2)
    gr_w3 = lax.dynamic_slice_in_dim(gr_w3, me * cw8, cw8, axis=2)

    def big(i, transposed):
        g = jnp.stack([gbig[l][i] for l in range(nl)])
        return jnp.transpose(g, (0, 2, 1)) if transposed else g

    grads = {
        "c_ctx": gr_cctx, "w_ada": g_wada, "b_ada": gr_bada, "w_in": big(0, True), "q_norm": gr_qn, "k_norm": gr_kn,
        "w_attn_o": big(1, True), "conf_dw_w": gr_w31, "conf_dw_b": gr_b31, "conf_ln_g": gr_lng, "conf_ln_b": gr_lnb,
        "w_conf_out": big(2, True), "sc_dw_w": gr_w3, "w_sc_out": big(3, True), "w_mix_out": big(4, False),
        "w_ffn_in": big(5, True), "w_ffn_out": big(6, False)}
    weights = dict(c_ctx=c_ctx, w_ada=w_ada, b_ada=b_ada, w_in=w_in, q_norm=q_norm, k_norm=k_norm, w_attn_o=w_attn_o,
                   conf_dw_w=conf_dw_w, conf_dw_b=conf_dw_b, conf_ln_g=conf_ln_g, conf_ln_b=conf_ln_b,
                   w_conf_out=w_conf_out, sc_dw_w=sc_dw_w, w_sc_out=w_sc_out, w_mix_out=w_mix_out,
                   w_ffn_in=w_ffn_in, w_ffn_out=w_ffn_out)
    m_in = dict(c_ctx=m_c_ctx, w_ada=m_w_ada, b_ada=m_b_ada, w_in=m_w_in, q_norm=m_q_norm, k_norm=m_k_norm,
                w_attn_o=m_w_attn_o, conf_dw_w=m_conf_dw_w, conf_dw_b=m_conf_dw_b, conf_ln_g=m_conf_ln_g,
                conf_ln_b=m_conf_ln_b, w_conf_out=m_w_conf_out, sc_dw_w=m_sc_dw_w, w_sc_out=m_w_sc_out,
                w_mix_out=m_w_mix_out, w_ffn_in=m_w_ffn_in, w_ffn_out=m_w_ffn_out)
    v_in = dict(c_ctx=v_c_ctx, w_ada=v_w_ada, b_ada=v_b_ada, w_in=v_w_in, q_norm=v_q_norm, k_norm=v_k_norm,
                w_attn_o=v_w_attn_o, conf_dw_w=v_conf_dw_w, conf_dw_b=v_conf_dw_b, conf_ln_g=v_conf_ln_g,
                conf_ln_b=v_conf_ln_b, w_conf_out=v_w_conf_out, sc_dw_w=v_sc_dw_w, w_sc_out=v_w_sc_out,
                w_mix_out=v_w_mix_out, w_ffn_in=v_w_ffn_in, w_ffn_out=v_w_ffn_out)
    names = list(weights)
    big_names = ("w_ada", "w_in", "w_attn_o", "w_conf_out", "w_sc_out", "w_mix_out", "w_ffn_in", "w_ffn_out")
    small_names = [n for n in names if n not in big_names]
    delta, new_m, new_v = {}, {}, {}
    for n in big_names:
        shp = weights[n].shape
        two_d = (shp[0] * shp[1], shp[2])
        dl, nm, nv = _adamw(f"adamw_{n}", weights[n].reshape(two_d), grads[n].reshape(two_d),
                            m_in[n].reshape(two_d), v_in[n].reshape(two_d))
        delta[n], new_m[n], new_v[n] = dl.reshape(shp), nm.reshape(shp), nv.reshape(shp)
    sshapes = [weights[n].shape for n in small_names]
    dl, nm, nv = _adamw("adamw_small", _pack_flat([weights[n] for n in small_names]),
                        _pack_flat([grads[n] for n in small_names]), _pack_flat([m_in[n] for n in small_names]),
                        _pack_flat([v_in[n] for n in small_names]))
    for n, a_, b_, c_ in zip(small_names, _unpack_flat(dl, sshapes), _unpack_flat(nm, sshapes), _unpack_flat(nv, sshapes)):
        delta[n], new_m[n], new_v[n] = a_, b_, c_

    return (loss, grad_x, *[grads[n] for n in names], *[delta[n] for n in names],
            *[new_m[n] for n in names], *[new_v[n] for n in names])
```
